```python
import math
import jax, jax.numpy as jnp
from jax import lax
import numpy as np

D_MODEL = 1024
BATCH = 16
SEQ = 2048
DEPTH = 2

HEAD_DIM = 64
MLA_HEADS = 6
MLA_Q_RANK = 384
MLA_KV_RANK = 256
MLA_NOPE = 64
MLA_ROPE = 32
MLA_V = 64
ROPE_THETA = 10000.0
FOX_HEADS = 6
FOX_FORGET_BIAS = 3.0
NSA_HEADS = 4
NSA_BRANCHES = 3
NSA_CMP_BLOCK = 32
NSA_CMP_STRIDE = 16
NSA_CMP_HIDDEN = 128
NSA_SEL_BLOCK = 64
NSA_SEL_TOPN = 8
NSA_WINDOW = 256
NSA_FORCE_SCORE = 1.0e4
Q_BLOCK = 128
D_FF = 2816
N_EXPERTS = 8
TOP_K = 2
D_FF_EXPERT = 1408
NORM_EPS = 1e-5
DEEPNORM_ALPHA = (2 * DEPTH) ** 0.25
DEEPNORM_BETA = (8 * DEPTH) ** -0.25
MLA_COLS = MLA_Q_RANK + MLA_KV_RANK + MLA_ROPE
FOX_COLS = 3 * FOX_HEADS * HEAD_DIM + FOX_HEADS
NSA_COLS = NSA_HEADS * HEAD_DIM + 2 * NSA_BRANCHES * HEAD_DIM + NSA_HEADS * NSA_BRANCHES
IN_COLS = MLA_COLS + FOX_COLS + NSA_COLS
MIX_WIDTH = MLA_HEADS * MLA_V + FOX_HEADS * HEAD_DIM + NSA_HEADS * HEAD_DIM
N_DENSE_LAYERS = (DEPTH + 1) // 2
N_MOE_LAYERS = DEPTH // 2

kernel_name = "hymba_mla_fox_nsa_deepnorm_moe"


def layer_norm(x, g, b):
    xf = x.astype(jnp.float32)
    mu = jnp.mean(xf, -1, keepdims=True)
    var = jnp.mean(jnp.square(xf - mu), -1, keepdims=True)
    return ((xf - mu) * lax.rsqrt(var + NORM_EPS) * g + b).astype(x.dtype)


def rms_norm(x, g):
    xf = x.astype(jnp.float32)
    return (xf * lax.rsqrt(jnp.mean(xf * xf, -1, keepdims=True) + NORM_EPS) * g).astype(x.dtype)


def masked_softmax(scores, mask):
    s = jnp.where(mask, scores, -1e30)
    m = jnp.max(s, -1, keepdims=True)
    p = jnp.where(mask, jnp.exp(s - m), 0.0)
    return p / jnp.maximum(jnp.sum(p, -1, keepdims=True), 1e-30)


def alibi_slopes(n):
    return jnp.asarray([2.0 ** (-8.0 * (i + 1) / n) for i in range(n)], jnp.float32)


def rope(x, pos):
    half = x.shape[-1] // 2
    freqs = ROPE_THETA ** (-jnp.arange(half, dtype=jnp.float32) / half)
    ang = pos.astype(jnp.float32)[:, None] * freqs[None, :]
    cos, sin = jnp.cos(ang), jnp.sin(ang)
    x1 = x[..., :half].astype(jnp.float32)
    x2 = x[..., half:].astype(jnp.float32)
    return jnp.concatenate([x1 * cos - x2 * sin, x2 * cos + x1 * sin], -1).astype(x.dtype)


def split_cols(z, sizes):
    offs = np.cumsum(np.asarray(sizes))[:-1].tolist()
    return jnp.split(z, offs, axis=-1)


def blocked_causal_attention(q, k, v, decay_cum=None):
    B, H, S, dk = q.shape
    nb = S // Q_BLOCK
    scale = dk ** -0.5
    k_pos = jnp.arange(S)
    q_blocks = jnp.moveaxis(q.reshape(B, H, nb, Q_BLOCK, dk), 2, 0)
    c_blocks = None if decay_cum is None else jnp.moveaxis(decay_cum.reshape(B, H, nb, Q_BLOCK), 2, 0)

    def one_block(args):
        i, q_blk, c_blk = args
        s = jnp.einsum("bhqd,bhkd->bhqk", q_blk, k).astype(jnp.float32) * scale
        if c_blk is not None:
            s = s + (c_blk[..., :, None] - decay_cum[..., None, :])
        q_pos = i * Q_BLOCK + jnp.arange(Q_BLOCK)
        p = masked_softmax(s, q_pos[:, None] >= k_pos[None, :])
        return jnp.einsum("bhqk,bhkd->bhqd", p.astype(v.dtype), v)

    o = lax.map(one_block, (jnp.arange(nb), q_blocks, c_blocks))
    return jnp.moveaxis(o, 0, 2).reshape(B, H, S, v.shape[-1])


def mla_mixer(cq_raw, ckv_raw, kr_raw, pos, g_cq, w_uq, g_ckv, w_ukv):
    B, S, _ = cq_raw.shape
    H = MLA_HEADS
    q = (rms_norm(cq_raw, g_cq) @ w_uq).reshape(B, S, H, MLA_NOPE + MLA_ROPE).transpose(0, 2, 1, 3)
    q = jnp.concatenate([q[..., :MLA_NOPE], rope(q[..., MLA_NOPE:], pos)], -1)
    kv = (rms_norm(ckv_raw, g_ckv) @ w_ukv).reshape(B, S, H, MLA_NOPE + MLA_V).transpose(0, 2, 1, 3)
    k_nope, v = kv[..., :MLA_NOPE], kv[..., MLA_NOPE:]
    k_rope = rope(kr_raw, pos)
    k = jnp.concatenate([k_nope, jnp.broadcast_to(k_rope[:, None], (B, H, S, MLA_ROPE))], -1)
    o = blocked_causal_attention(q, k, v)
    return o.transpose(0, 2, 1, 3).reshape(B, S, H * MLA_V)


def fox_mixer(qkv, f_logit, b_forget):
    B, S, _ = qkv.shape
    H = FOX_HEADS
    qkv = qkv.reshape(B, S, 3, H, HEAD_DIM).transpose(2, 0, 3, 1, 4)
    log_f = jax.nn.log_sigmoid(f_logit.astype(jnp.float32) + b_forget)
    c = jnp.cumsum(log_f, axis=1).transpose(0, 2, 1)
    o = blocked_causal_attention(qkv[0], qkv[1], qkv[2], c)
    return o.transpose(0, 2, 1, 3).reshape(B, S, H * HEAD_DIM)


def nsa_mixer(q, k_cmp_raw, v_cmp_raw, k_slc, v_slc, k_win, v_win, gate_logit,
              cmp_k_pos, cmp_k_w1, cmp_k_w2, cmp_v_pos, cmp_v_w1, cmp_v_w2):
    B, S, _ = q.shape
    H = NSA_HEADS
    f32 = jnp.float32
    q = q.reshape(B, S, H, HEAD_DIM).transpose(0, 2, 1, 3)
    scale = HEAD_DIM ** -0.5
    slopes = alibi_slopes(H)
    pos = jnp.arange(S)

    n_cmp = (S - NSA_CMP_BLOCK) // NSA_CMP_STRIDE + 1
    starts = jnp.arange(n_cmp) * NSA_CMP_STRIDE
    blk_idx = starts[:, None] + jnp.arange(NSA_CMP_BLOCK)[None, :]

    def compress(t, pos_emb, w1, w2):
        blocks = t[:, blk_idx] + pos_emb
        return jax.nn.silu(blocks.reshape(B, n_cmp, NSA_CMP_BLOCK * HEAD_DIM) @ w1) @ w2

    k_c = compress(k_cmp_raw, cmp_k_pos, cmp_k_w1, cmp_k_w2)
    v_c = compress(v_cmp_raw, cmp_v_pos, cmp_v_w1, cmp_v_w2)
    blk_end = starts + NSA_CMP_BLOCK - 1
    dist_c = (pos[:, None] - blk_end[None, :]).astype(f32)
    s_c = jnp.einsum("bhtd,bnd->bhtn", q, k_c).astype(f32) * scale - slopes[:, None, None] * dist_c
    p_c = masked_softmax(s_c, dist_c >= 0)
    o_c = jnp.einsum("bhtn,bnd->bhtd", p_c.astype(v_c.dtype), v_c)

    n_blk = S // NSA_SEL_BLOCK
    n_sel = min(NSA_SEL_TOPN, n_blk)
    sel_start = jnp.arange(n_blk) * NSA_SEL_BLOCK
    overlap = ((starts[:, None] < sel_start[None, :] + NSA_SEL_BLOCK)
               & (starts[:, None] + NSA_CMP_BLOCK > sel_start[None, :])).astype(f32)
    imp = jnp.einsum("bhtn,nj->btj", p_c, overlap)
    cur = pos // NSA_SEL_BLOCK
    blk_ids = jnp.arange(n_blk)
    forced = (blk_ids[None, :] == 0) | (blk_ids[None, :] == cur[:, None]) | (blk_ids[None, :] == cur[:, None] - 1)
    future = sel_start[None, :] > pos[:, None]
    imp = jnp.where(forced, NSA_FORCE_SCORE, jnp.where(future, -1.0, imp))

    nb = S // Q_BLOCK
    q_blocks = jnp.moveaxis(q.reshape(B, H, nb, Q_BLOCK, HEAD_DIM), 2, 0)
    imp_blocks = jnp.moveaxis(imp.reshape(B, nb, Q_BLOCK, n_blk), 1, 0)
    pad = jnp.zeros((B, NSA_WINDOW, HEAD_DIM), k_win.dtype)
    k_win_pad = jnp.concatenate([pad, k_win], 1)
    v_win_pad = jnp.concatenate([pad, v_win], 1)
    win_len = Q_BLOCK + NSA_WINDOW

    def sel_win_block(args):
        i, q_blk, imp_blk = args
        q_pos = i * Q_BLOCK + jnp.arange(Q_BLOCK)
        _, top = lax.top_k(imp_blk, n_sel)
        tok = (top[..., None] * NSA_SEL_BLOCK + jnp.arange(NSA_SEL_BLOCK)).reshape(B, Q_BLOCK, n_sel * NSA_SEL_BLOCK)
        k_g = jax.vmap(lambda kk, ii: kk[ii])(k_slc, tok)
        v_g = jax.vmap(lambda vv, ii: vv[ii])(v_slc, tok)
        dist_s = (q_pos[None, :, None] - tok).astype(f32)
        s_s = jnp.einsum("bhqd,bqkd->bhqk", q_blk, k_g).astype(f32) * scale - slopes[None, :, None, None] * dist_s[:, None]
        p_s = masked_softmax(s_s, (dist_s >= 0)[:, None])
        o_s = jnp.einsum("bhqk,bqkd->bhqd", p_s.astype(v_g.dtype), v_g)
        k_w = lax.dynamic_slice_in_dim(k_win_pad, i * Q_BLOCK, win_len, axis=1)
        v_w = lax.dynamic_slice_in_dim(v_win_pad, i * Q_BLOCK, win_len, axis=1)
        kp = i * Q_BLOCK - NSA_WINDOW + jnp.arange(win_len)
        d_w = q_pos[:, None] - kp[None, :]
        m_w = (d_w >= 0) & (d_w < NSA_WINDOW) & (kp >= 0)[None, :]
        s_w = jnp.einsum("bhqd,bkd->bhqk", q_blk, k_w).astype(f32) * scale - slopes[:, None, None] * d_w.astype(f32)
        p_w = masked_softmax(s_w, m_w)
        o_w = jnp.einsum("bhqk,bkd->bhqd", p_w.astype(v_w.dtype), v_w)
        return o_s, o_w

    o_s, o_w = lax.map(sel_win_block, (jnp.arange(nb), q_blocks, imp_blocks))
    o_s = jnp.moveaxis(o_s, 0, 2).reshape(B, H, S, HEAD_DIM)
    o_w = jnp.moveaxis(o_w, 0, 2).reshape(B, H, S, HEAD_DIM)

    g = jax.nn.sigmoid(gate_logit.astype(f32)).reshape(B, S, H, NSA_BRANCHES).transpose(0, 2, 1, 3)
    o = (g[..., 0:1] * o_c.astype(f32) + g[..., 1:2] * o_s.astype(f32) + g[..., 2:3] * o_w.astype(f32)).astype(q.dtype)
    return o.transpose(0, 2, 1, 3).reshape(B, S, H * HEAD_DIM)


def hybrid_mixer(x, pos, w_in, b_forget, g_cq, w_uq, g_ckv, w_ukv,
                 cmp_k_pos, cmp_k_w1, cmp_k_w2, cmp_v_pos, cmp_v_w1, cmp_v_w2, w_out):
    z = x @ w_in
    (cq, ckv, kr, fox_qkv, fox_f, nsa_q, k_cmp, v_cmp, k_slc, v_slc, k_win, v_win, nsa_g) = split_cols(
        z, [MLA_Q_RANK, MLA_KV_RANK, MLA_ROPE,
            3 * FOX_HEADS * HEAD_DIM, FOX_HEADS,
            NSA_HEADS * HEAD_DIM, HEAD_DIM, HEAD_DIM, HEAD_DIM, HEAD_DIM, HEAD_DIM, HEAD_DIM,
            NSA_HEADS * NSA_BRANCHES])
    o_mla = mla_mixer(cq, ckv, kr, pos, g_cq, w_uq, g_ckv, w_ukv)
    o_fox = fox_mixer(fox_qkv, fox_f, b_forget)
    o_nsa = nsa_mixer(nsa_q, k_cmp, v_cmp, k_slc, v_slc, k_win, v_win, nsa_g,
                      cmp_k_pos, cmp_k_w1, cmp_k_w2, cmp_v_pos, cmp_v_w1, cmp_v_w2)
    return jnp.concatenate([o_mla, o_fox, o_nsa], -1) @ w_out


def swiglu(h, w1, w3, w2):
    return (jax.nn.silu(h @ w1) * (h @ w3)) @ w2


def moe_swiglu(x, router_w, w1, w3, w2):
    B, S, D = x.shape
    h = x.reshape(B * S, D)
    probs = jax.nn.softmax((h @ router_w).astype(jnp.float32), -1)
    top_p, top_i = lax.top_k(probs, TOP_K)
    top_p = top_p / jnp.sum(top_p, -1, keepdims=True)
    gates = jnp.sum(jax.nn.one_hot(top_i, N_EXPERTS, dtype=jnp.float32) * top_p[..., None], axis=1)
    out = jnp.zeros_like(h)
    for e in range(N_EXPERTS):
        out = out + gates[:, e:e + 1].astype(h.dtype) * swiglu(h, w1[e], w3[e], w2[e])
    return out.reshape(B, S, D)


def setup_inputs(seed: int = 0) -> dict:
    key = jax.random.key(seed)
    ks = iter(jax.random.split(key, 32))
    L = DEPTH
    nrm = lambda shape, scale: jax.random.normal(next(ks), shape, jnp.float32) * scale
    return {
        "x": nrm((BATCH, SEQ, D_MODEL), 1.0),
        "w_in": nrm((L, D_MODEL, IN_COLS), D_MODEL ** -0.5),
        "b_forget": FOX_FORGET_BIAS + nrm((L, FOX_HEADS), 0.1),
        "g_cq": 1.0 + nrm((L, MLA_Q_RANK), 0.02),
        "w_uq": nrm((L, MLA_Q_RANK, MLA_HEADS * (MLA_NOPE + MLA_ROPE)), MLA_Q_RANK ** -0.5),
        "g_ckv": 1.0 + nrm((L, MLA_KV_RANK), 0.02),
        "w_ukv": nrm((L, MLA_KV_RANK, MLA_HEADS * (MLA_NOPE + MLA_V)), MLA_KV_RANK ** -0.5),
        "cmp_k_pos": nrm((L, NSA_CMP_BLOCK, HEAD_DIM), 0.02),
        "cmp_k_w1": nrm((L, NSA_CMP_BLOCK * HEAD_DIM, NSA_CMP_HIDDEN), (NSA_CMP_BLOCK * HEAD_DIM) ** -0.5),
        "cmp_k_w2": nrm((L, NSA_CMP_HIDDEN, HEAD_DIM), NSA_CMP_HIDDEN ** -0.5),
        "cmp_v_pos": nrm((L, NSA_CMP_BLOCK, HEAD_DIM), 0.02),
        "cmp_v_w1": nrm((L, NSA_CMP_BLOCK * HEAD_DIM, NSA_CMP_HIDDEN), (NSA_CMP_BLOCK * HEAD_DIM) ** -0.5),
        "cmp_v_w2": nrm((L, NSA_CMP_HIDDEN, HEAD_DIM), NSA_CMP_HIDDEN ** -0.5),
        "w_out": nrm((L, MIX_WIDTH, D_MODEL), MIX_WIDTH ** -0.5 * DEEPNORM_BETA),
        "ln1_g": 1.0 + nrm((L, D_MODEL), 0.02),
        "ln1_b": nrm((L, D_MODEL), 0.02),
        "ln2_g": 1.0 + nrm((L, D_MODEL), 0.02),
        "ln2_b": nrm((L, D_MODEL), 0.02),
        "ffn_w1": nrm((N_DENSE_LAYERS, D_MODEL, D_FF), D_MODEL ** -0.5),
        "ffn_w3": nrm((N_DENSE_LAYERS, D_MODEL, D_FF), D_MODEL ** -0.5),
        "ffn_w2": nrm((N_DENSE_LAYERS, D_FF, D_MODEL), D_FF ** -0.5 * DEEPNORM_BETA),
        "router_w": nrm((N_MOE_LAYERS, D_MODEL, N_EXPERTS), D_MODEL ** -0.5),
        "moe_w1": nrm((N_MOE_LAYERS, N_EXPERTS, D_MODEL, D_FF_EXPERT), D_MODEL ** -0.5),
        "moe_w3": nrm((N_MOE_LAYERS, N_EXPERTS, D_MODEL, D_FF_EXPERT), D_MODEL ** -0.5),
        "moe_w2": nrm((N_MOE_LAYERS, N_EXPERTS, D_FF_EXPERT, D_MODEL), D_FF_EXPERT ** -0.5 * DEEPNORM_BETA),
    }


def reference(x, w_in, b_forget, g_cq, w_uq, g_ckv, w_ukv,
              cmp_k_pos, cmp_k_w1, cmp_k_w2, cmp_v_pos, cmp_v_w1, cmp_v_w2,
              w_out, ln1_g, ln1_b, ln2_g, ln2_b,
              ffn_w1, ffn_w3, ffn_w2, router_w, moe_w1, moe_w3, moe_w2):
    pos = jnp.arange(x.shape[1])
    for layer in range(DEPTH):
        mix = hybrid_mixer(x, pos, w_in[layer], b_forget[layer], g_cq[layer], w_uq[layer], g_ckv[layer], w_ukv[layer],
                           cmp_k_pos[layer], cmp_k_w1[layer], cmp_k_w2[layer],
                           cmp_v_pos[layer], cmp_v_w1[layer], cmp_v_w2[layer], w_out[layer])
        x = layer_norm(DEEPNORM_ALPHA * x + mix, ln1_g[layer], ln1_b[layer])
        j = layer // 2
        if layer % 2 == 0:
            f = swiglu(x, ffn_w1[j], ffn_w3[j], ffn_w2[j])
        else:
            f = moe_swiglu(x, router_w[j], moe_w1[j], moe_w3[j], moe_w2[j])
        x = layer_norm(DEEPNORM_ALPHA * x + f, ln2_g[layer], ln2_b[layer])
    return x
```

```python
import functools

import numpy as np
import jax
import jax.numpy as jnp
from jax import lax
from jax.experimental import pallas as pl
from jax.experimental.pallas import tpu as pltpu

F32 = jnp.float32
BF16 = jnp.bfloat16

D_MODEL = 1024
HEAD_DIM = 64
LANES = 128
MLA_HEADS = 6
MLA_Q_RANK = 384
MLA_KV_RANK = 256
MLA_NOPE = 64
MLA_ROPE = 32
ROPE_THETA = 10000.0
FOX_HEADS = 6
NSA_HEADS = 4
NSA_CMP_BLOCK = 32
NSA_CMP_STRIDE = 16
NSA_CMP_HIDDEN = 128
NSA_SEL_BLOCK = 64
NSA_SEL_TOPN = 8
NSA_WINDOW = 256
NSA_FORCE_SCORE = 1.0e4
N_EXPERTS = 8
NORM_EPS = 1e-5
DEPTH = 2
DEEPNORM_ALPHA = (2 * DEPTH) ** 0.25
NEG = -1e30
VMEM_LIMIT = 56 * 1024 * 1024

KR_LANE = 64
FOXF_LANE = 96
NSAG_LANE = 102

ZA_W = 768
FQ_OFF = ZA_W
FK_OFF = FQ_OFF + 768
FV_OFF = FK_OFF + 768
NQ_OFF = FV_OFF + 384
NC_OFF = NQ_OFF + 512
NKV_OFF = NC_OFF + 128
IN_P = NKV_OFF + 256

ALIBI = tuple(2.0 ** (-8.0 * (i + 1) / NSA_HEADS) for i in range(NSA_HEADS))


def _dot(a, b, **kw):
    return jnp.dot(a, b, preferred_element_type=F32, **kw)


def _dot_nt(a, b):
    return lax.dot_general(a, b, (((1,), (1,)), ((), ())), preferred_element_type=F32)


def _iota(shape, dim):
    return lax.broadcasted_iota(jnp.int32, shape, dim)


def _cparams(sem):
    return pltpu.CompilerParams(dimension_semantics=sem, vmem_limit_bytes=VMEM_LIMIT)


def _in_proj_kernel(x_ref, w_ref, za_ref, fq_ref, fk_ref, fv_ref, nq_ref, nc_ref, nkv_ref):
    xb = x_ref[...].astype(BF16)

    def mm(a, b):
        return _dot(xb, w_ref[:, a:b])

    za_ref[...] = mm(0, ZA_W)
    fq_ref[...] = mm(FQ_OFF, FK_OFF).astype(BF16)
    fk_ref[...] = mm(FK_OFF, FV_OFF).astype(BF16)
    fv_ref[...] = mm(FV_OFF, NQ_OFF).astype(BF16)
    nq_ref[...] = mm(NQ_OFF, NC_OFF).astype(BF16)
    nc_ref[...] = mm(NC_OFF, NKV_OFF)
    nkv_ref[...] = mm(NKV_OFF, IN_P).astype(BF16)


def _in_proj(x2, w_p, tm=512):
    n = x2.shape[0]
    widths = [(ZA_W, F32), (768, BF16), (768, BF16), (384, BF16), (512, BF16), (128, F32), (256, BF16)]
    return pl.pallas_call(
        _in_proj_kernel,
        grid=(n // tm,),
        in_specs=[pl.BlockSpec((tm, D_MODEL), lambda i: (i, 0)),
                  pl.BlockSpec((D_MODEL, IN_P), lambda i: (0, 0))],
        out_specs=[pl.BlockSpec((tm, w), lambda i: (i, 0)) for w, _ in widths],
        out_shape=[jax.ShapeDtypeStruct((n, w), dt) for w, dt in widths],
        compiler_params=_cparams(("parallel",)),
        name="in_proj",
    )(x2, w_p)


def _prep_kernel(za_ref, fq_ref, fk_ref, fv_ref, gcq_ref, wuq_ref, gckv_ref, wuk_ref, wuv_ref,
                 bf_ref, cos_ref, sina_ref, sinb_ref, pq_ref, pk_ref, oneq_ref, onek_ref,
                 q_out, k_out, v_out, carry_ref, *, ts):
    @pl.when(pl.program_id(1) == 0)
    def _():
        carry_ref[...] = jnp.zeros_like(carry_ref)

    za = za_ref[...]
    cq = za[:, 0:MLA_Q_RANK]
    ckv = za[:, MLA_Q_RANK:MLA_Q_RANK + MLA_KV_RANK]
    small = za[:, MLA_Q_RANK + MLA_KV_RANK:ZA_W]

    cos = cos_ref[...]
    sina = sina_ref[...]
    sinb = sinb_ref[...]

    def rope(blk):
        return blk * cos + pltpu.roll(blk, LANES - 16, 1) * sina + pltpu.roll(blk, 16, 1) * sinb

    xn = cq * lax.rsqrt(jnp.mean(cq * cq, -1, keepdims=True) + NORM_EPS) * gcq_ref[...]
    q = _dot(xn.astype(BF16), wuq_ref[...])
    cn = ckv * lax.rsqrt(jnp.mean(ckv * ckv, -1, keepdims=True) + NORM_EPS) * gckv_ref[...]
    cnb = cn.astype(BF16)
    kn = _dot(cnb, wuk_ref[...])
    v = _dot(cnb, wuv_ref[...])
    kr = rope(small)
    mla_scale = (MLA_NOPE + MLA_ROPE) ** -0.5
    for h in range(MLA_HEADS):
        sl = slice(LANES * h, LANES * (h + 1))
        q_out[0, :, sl] = (rope(q[:, sl]) * mla_scale).astype(BF16)
        k_out[0, :, sl] = (kn[:, sl] + kr).astype(BF16)
    v_out[0, :, 0:MLA_HEADS * HEAD_DIM] = v.astype(BF16)

    lane = _iota((1, LANES), 1)
    fmask = (lane >= FOXF_LANE) & (lane < FOXF_LANE + FOX_HEADS)
    lf = jnp.where(fmask, jax.nn.log_sigmoid(small + bf_ref[...]), 0.0)
    tril = (_iota((ts, ts), 0) >= _iota((ts, ts), 1)).astype(F32)
    cs = _dot(tril, lf, precision=lax.Precision.HIGHEST) + carry_ref[...]
    carry_ref[...] = cs[ts - 1:ts, :]
    hi = cs.astype(BF16).astype(F32)
    r1 = cs - hi
    mid = r1.astype(BF16).astype(F32)
    lo = (r1 - mid).astype(BF16).astype(F32)
    c3 = (hi + pltpu.roll(mid, FOX_HEADS, 1) + pltpu.roll(lo, 2 * FOX_HEADS, 1)).astype(BF16)
    augq = _dot(c3, pq_ref[...]) + oneq_ref[...]
    augk = _dot(c3, pk_ref[...]) + onek_ref[...]
    off = MLA_HEADS * LANES
    q_out[0, :, off:off + FOX_HEADS * LANES] = (fq_ref[...].astype(F32) + augq).astype(BF16)
    k_out[0, :, off:off + FOX_HEADS * LANES] = (fk_ref[...].astype(F32) + augk).astype(BF16)
    v_out[0, :, MLA_HEADS * HEAD_DIM:] = fv_ref[...]


def _prep(za, fq, fk, fv, wl, tabs, b, s, ts=512):
    ns = s // ts
    row = lambda w: pl.BlockSpec((ts, w), lambda bi, si: (bi * ns + si, 0))
    full = lambda a: pl.BlockSpec(a.shape, lambda bi, si: (0,) * a.ndim)
    tab = pl.BlockSpec((ts, LANES), lambda bi, si: (si, 0))
    consts = [wl["g_cq"], wl["w_uq"], wl["g_ckv"], wl["w_uk"], wl["w_uv"], wl["b_forget"]]
    tail = [tabs["pq"], tabs["pk"], tabs["oneq"], tabs["onek"]]
    nh = MLA_HEADS + FOX_HEADS
    return pl.pallas_call(
        functools.partial(_prep_kernel, ts=ts),
        grid=(b, ns),
        in_specs=[row(ZA_W), row(768), row(768), row(384)] + [full(a) for a in consts]
                 + [tab, tab, tab] + [full(a) for a in tail],
        out_specs=[pl.BlockSpec((1, ts, nh * LANES), lambda bi, si: (bi, si, 0)),
                   pl.BlockSpec((1, ts, nh * LANES), lambda bi, si: (bi, si, 0)),
                   pl.BlockSpec((1, ts, nh * HEAD_DIM), lambda bi, si: (bi, si, 0))],
        out_shape=[jax.ShapeDtypeStruct((b, s, nh * LANES), BF16),
                   jax.ShapeDtypeStruct((b, s, nh * LANES), BF16),
                   jax.ShapeDtypeStruct((b, s, nh * HEAD_DIM), BF16)],
        scratch_shapes=[pltpu.VMEM((1, LANES), F32)],
        compiler_params=_cparams(("arbitrary", "arbitrary")),
        name="head_prep",
    )(za, fq, fk, fv, *consts, tabs["cos"], tabs["sina"], tabs["sinb"], *tail)


def _flash_kernel(q_ref, k_ref, v_ref, o_ref, m_ref, l_ref, acc_ref, *, tq):
    qi = pl.program_id(2)
    outs = []
    for h in range(2):
        sl = slice(LANES * h, LANES * (h + 1))
        q = q_ref[0, :, sl]
        m_ref[...] = jnp.full_like(m_ref, NEG)
        l_ref[...] = jnp.zeros_like(l_ref)
        acc_ref[...] = jnp.zeros_like(acc_ref)

        def step(j, masked):
            k0 = pl.multiple_of(j * tq, tq)
            k = k_ref[0, pl.ds(k0, tq), sl]
            v = v_ref[0, pl.ds(k0, tq), :]
            s = _dot_nt(q, k)
            if masked:
                keep = _iota((tq, tq), 0) >= _iota((tq, tq), 1)
                s = jnp.where(keep, s, NEG)
            m_prev = m_ref[...]
            m_new = jnp.maximum(m_prev, jnp.max(s, -1, keepdims=True))
            alpha = jnp.exp(m_prev - m_new)
            p = jnp.exp(s - m_new)
            if masked:
                p = jnp.where(keep, p, 0.0)
            l_ref[...] = alpha * l_ref[...] + jnp.sum(p, -1, keepdims=True)
            acc_ref[...] = alpha * acc_ref[...] + _dot(p.astype(BF16), v)
            m_ref[...] = m_new

        def body(j, carry):
            step(j, False)
            return carry

        lax.fori_loop(0, qi, body, 0)
        step(qi, True)
        outs.append(acc_ref[...] / l_ref[...])
    lane = _iota((1, LANES), 1)
    o_ref[0] = jnp.where(lane < HEAD_DIM, outs[0], outs[1]).astype(o_ref.dtype)


def _flash(q_all, k_all, v_all, tq=256):
    b, s, _ = q_all.shape
    npair = (MLA_HEADS + FOX_HEADS) // 2
    return pl.pallas_call(
        functools.partial(_flash_kernel, tq=tq),
        grid=(b, npair, s // tq),
        in_specs=[pl.BlockSpec((1, tq, 2 * LANES), lambda bi, p, qi: (bi, qi, p)),
                  pl.BlockSpec((1, s, 2 * LANES), lambda bi, p, qi: (bi, 0, p)),
                  pl.BlockSpec((1, s, LANES), lambda bi, p, qi: (bi, 0, p))],
        out_specs=pl.BlockSpec((1, tq, LANES), lambda bi, p, qi: (bi, qi, p)),
        out_shape=jax.ShapeDtypeStruct((b, s, npair * LANES), BF16),
        scratch_shapes=[pltpu.VMEM((tq, 1), F32), pltpu.VMEM((tq, 1), F32), pltpu.VMEM((tq, LANES), F32)],
        compiler_params=_cparams(("parallel", "parallel", "arbitrary")),
        name="flash_attn",
    )(q_all, k_all, v_all)


def _cmp_kernel(tc_ref, posa_ref, posb_ref, wa_ref, wb_ref, w2_ref, out_ref):
    tc = tc_ref[0]
    a = _dot((tc + posa_ref[...]).astype(BF16), wa_ref[...])
    b = _dot((tc + posb_ref[...]).astype(BF16), wb_ref[...])
    pre = a + pltpu.roll(b, b.shape[0] - 1, 0)
    hid = jax.nn.silu(pre)
    out_ref[0] = _dot(hid.astype(BF16), w2_ref[...]).astype(out_ref.dtype)


def _compress(tc, wl):
    b, nc, w = tc.shape
    full = lambda a: pl.BlockSpec(a.shape, lambda bi: (0,) * a.ndim)
    consts = [wl["cmp_posa"], wl["cmp_posb"], wl["cmp_wa"], wl["cmp_wb"], wl["cmp_w2"]]
    return pl.pallas_call(
        _cmp_kernel,
        grid=(b,),
        in_specs=[pl.BlockSpec((1, nc, w), lambda bi: (bi, 0, 0))] + [full(a) for a in consts],
        out_specs=pl.BlockSpec((1, nc, LANES), lambda bi: (bi, 0, 0)),
        out_shape=jax.ShapeDtypeStruct((b, nc, LANES), BF16),
        compiler_params=_cparams(("parallel",)),
        name="nsa_compress",
    )(tc, *consts)


def _masked_softmax(s, mask):
    s = jnp.where(mask, s, NEG)
    m = jnp.max(s, -1, keepdims=True)
    p = jnp.where(mask, jnp.exp(s - m), 0.0)
    return p / jnp.maximum(jnp.sum(p, -1, keepdims=True), 1e-30)


def _nsa_kernel(nq_ref, kc_ref, ksl_ref, kwin_ref, g_ref, o_ref, m_ref, l_ref, acc_ref, *, tq, n_cmp):
    qi = pl.program_id(1)
    t0 = pl.multiple_of(qi * tq, tq)
    rpos = t0 + _iota((tq, 1), 0)
    lane = _iota((1, LANES), 1)
    qs = [nq_ref[0, :, LANES * h:LANES * (h + 1)] for h in range(NSA_HEADS)]

    dist_i = rpos - (NSA_CMP_STRIDE * lane + NSA_CMP_BLOCK - 1)
    valid_c = (dist_i >= 0) & (lane < n_cmp)
    dist_c = dist_i.astype(F32)
    kc = kc_ref[0]
    psum = jnp.zeros((tq, LANES), F32)
    o_cmp = []
    for h in range(NSA_HEADS):
        p = _masked_softmax(_dot_nt(qs[h], kc) - ALIBI[h] * dist_c, valid_c)
        psum = psum + p
        o_cmp.append(_dot(p.astype(BF16), kc))

    n_i = _iota((LANES, LANES), 0)
    j_i = _iota((LANES, LANES), 1)
    ov = ((NSA_CMP_STRIDE * n_i < NSA_SEL_BLOCK * (j_i + 1))
          & (NSA_CMP_STRIDE * n_i + NSA_CMP_BLOCK > NSA_SEL_BLOCK * j_i)
          & (n_i < n_cmp)).astype(F32)
    imp = _dot(psum, ov, precision=lax.Precision.HIGHEST)
    cur = jnp.right_shift(rpos, 6)
    forced = (lane == 0) | (lane == cur) | (lane == cur - 1)
    future = lane * NSA_SEL_BLOCK > rpos
    n_blk = kwin_ref.shape[1] // NSA_SEL_BLOCK - NSA_WINDOW // NSA_SEL_BLOCK
    work = jnp.where(forced, NSA_FORCE_SCORE, jnp.where(future, -1.0, imp))
    work = jnp.where(lane < n_blk, work, -jnp.inf)
    sel = jnp.zeros((tq, LANES), jnp.bool_)
    for _ in range(NSA_SEL_TOPN):
        mx = jnp.max(work, -1, keepdims=True)
        idx = jnp.min(jnp.where(work == mx, lane, LANES), -1, keepdims=True)
        pick = lane == idx
        sel = sel | pick
        work = jnp.where(pick, -jnp.inf, work)
    selb = jnp.where(sel, 1.0, 0.0).astype(BF16)

    m_ref[...] = jnp.full_like(m_ref, NEG)
    l_ref[...] = jnp.zeros_like(l_ref)
    acc_ref[...] = jnp.zeros_like(acc_ref)

    def sel_chunk(c, carry):
        k0 = pl.multiple_of(c * tq, tq)
        kv = ksl_ref[0, pl.ds(k0, tq), :]
        kpos = k0 + _iota((1, tq), 1)
        expand = jnp.where(_iota((LANES, tq), 0) == jnp.right_shift(k0 + _iota((LANES, tq), 1), 6),
                           1.0, 0.0).astype(BF16)
        keep = (_dot(selb, expand) > 0.5) & (kpos <= rpos)
        dist = (rpos - kpos).astype(F32)
        for h in range(NSA_HEADS):
            s = jnp.where(keep, _dot_nt(qs[h], kv) - ALIBI[h] * dist, NEG)
            m_prev = m_ref[h]
            m_new = jnp.maximum(m_prev, jnp.max(s, -1, keepdims=True))
            alpha = jnp.exp(m_prev - m_new)
            p = jnp.where(keep, jnp.exp(s - m_new), 0.0)
            l_ref[h] = alpha * l_ref[h] + jnp.sum(p, -1, keepdims=True)
            acc_ref[h] = alpha * acc_ref[h] + _dot(p.astype(BF16), kv)
            m_ref[h] = m_new
        return carry

    lax.fori_loop(0, qi + 1, sel_chunk, 0)

    wlen = tq + NSA_WINDOW
    kw = kwin_ref[0, pl.ds(t0, wlen), :]
    kp = t0 - NSA_WINDOW + _iota((1, wlen), 1)
    d_w = rpos - kp
    keep_w = (d_w >= 0) & (d_w < NSA_WINDOW) & (kp >= 0)
    d_wf = d_w.astype(F32)

    gates = jax.nn.sigmoid(g_ref[...])
    for h in range(NSA_HEADS):
        p_w = _masked_softmax(_dot_nt(qs[h], kw) - ALIBI[h] * d_wf, keep_w)
        o_win = _dot(p_w.astype(BF16), kw)
        o_sel = acc_ref[h] / jnp.maximum(l_ref[h], 1e-30)
        g0 = NSAG_LANE + 3 * h
        o = (gates[:, g0:g0 + 1] * o_cmp[h] + gates[:, g0 + 1:g0 + 2] * o_sel
             + gates[:, g0 + 2:g0 + 3] * o_win)
        o_ref[0, :, LANES * h:LANES * (h + 1)] = jnp.where(lane >= HEAD_DIM, o, 0.0).astype(o_ref.dtype)


def _nsa(nq, kcvc, ksl, kwin_pad, za, tq=256):
    b, s, _ = nq.shape
    nq_t = s // tq
    n_cmp = (s - NSA_CMP_BLOCK) // NSA_CMP_STRIDE + 1
    return pl.pallas_call(
        functools.partial(_nsa_kernel, tq=tq, n_cmp=n_cmp),
        grid=(b, nq_t),
        in_specs=[pl.BlockSpec((1, tq, NSA_HEADS * LANES), lambda bi, qi: (bi, qi, 0)),
                  pl.BlockSpec((1, kcvc.shape[1], LANES), lambda bi, qi: (bi, 0, 0)),
                  pl.BlockSpec((1, s, LANES), lambda bi, qi: (bi, 0, 0)),
                  pl.BlockSpec((1, s + NSA_WINDOW, LANES), lambda bi, qi: (bi, 0, 0)),
                  pl.BlockSpec((tq, LANES), lambda bi, qi: (bi * nq_t + qi, (ZA_W - LANES) // LANES))],
        out_specs=pl.BlockSpec((1, tq, NSA_HEADS * LANES), lambda bi, qi: (bi, qi, 0)),
        out_shape=jax.ShapeDtypeStruct((b, s, NSA_HEADS * LANES), BF16),
        scratch_shapes=[pltpu.VMEM((NSA_HEADS, tq, 1), F32), pltpu.VMEM((NSA_HEADS, tq, 1), F32),
                        pltpu.VMEM((NSA_HEADS, tq, LANES), F32)],
        compiler_params=_cparams(("parallel", "arbitrary")),
        name="nsa_attn",
    )(nq, kcvc, ksl, kwin_pad, za)


def _layer_norm(y, g, b):
    mu = jnp.mean(y, -1, keepdims=True)
    yc = y - mu
    var = jnp.mean(yc * yc, -1, keepdims=True)
    return yc * lax.rsqrt(var + NORM_EPS) * g + b


def _out_proj_kernel(oa_ref, on_ref, x_ref, wa_ref, wb_ref, g_ref, b_ref, o_ref):
    mix = _dot(oa_ref[...], wa_ref[...]) + _dot(on_ref[...], wb_ref[...])
    o_ref[...] = _layer_norm(DEEPNORM_ALPHA * x_ref[...] + mix, g_ref[...], b_ref[...])


def _out_proj(o_attn, o_nsa, x2, wl, tm=512):
    n = x2.shape[0]
    full = lambda a: pl.BlockSpec(a.shape, lambda i: (0,) * a.ndim)
    row = lambda w: pl.BlockSpec((tm, w), lambda i: (i, 0))
    consts = [wl["w_out_a"], wl["w_out_b"], wl["ln1_g"], wl["ln1_b"]]
    return pl.pallas_call(
        _out_proj_kernel,
        grid=(n // tm,),
        in_specs=[row(o_attn.shape[1]), row(o_nsa.shape[1]), row(D_MODEL)] + [full(a) for a in consts],
        out_specs=row(D_MODEL),
        out_shape=jax.ShapeDtypeStruct((n, D_MODEL), F32),
        compiler_params=_cparams(("parallel",)),
        name="out_proj_ln",
    )(o_attn, o_nsa, x2, *consts)


def _ffn_kernel(*refs, routed):
    if routed:
        x_ref, w1_ref, w3_ref, w2_ref, g_ref, b_ref, rw_ref, o_ref, acc_ref, xb_ref, gate_ref = refs
    else:
        x_ref, w1_ref, w3_ref, w2_ref, g_ref, b_ref, o_ref, acc_ref, xb_ref = refs
    c = pl.program_id(1)
    lane = _iota((1, LANES), 1)

    @pl.when(c == 0)
    def _():
        x = x_ref[...]
        acc_ref[...] = jnp.zeros_like(acc_ref)
        xb_ref[...] = x.astype(BF16)
        if routed:
            logits = jnp.where(lane < N_EXPERTS, _dot(x, rw_ref[...], precision=lax.Precision.HIGHEST), NEG)
            e = jnp.exp(logits - jnp.max(logits, -1, keepdims=True))
            probs = e / jnp.sum(e, -1, keepdims=True)
            p1 = jnp.max(probs, -1, keepdims=True)
            i1 = jnp.min(jnp.where(probs == p1, lane, LANES), -1, keepdims=True)
            rest = jnp.where(lane == i1, -1.0, probs)
            p2 = jnp.max(rest, -1, keepdims=True)
            i2 = jnp.min(jnp.where(rest == p2, lane, LANES), -1, keepdims=True)
            tot = p1 + p2
            gate_ref[...] = jnp.where(lane == i1, p1 / tot, jnp.where(lane == i2, p2 / tot, 0.0))

    xb = xb_ref[...]
    a = jax.nn.silu(_dot(xb, w1_ref[0])) * _dot(xb, w3_ref[0])
    y = _dot(a.astype(BF16), w2_ref[0])
    if routed:
        y = y * jnp.sum(jnp.where(lane == c, gate_ref[...], 0.0), -1, keepdims=True)
    acc_ref[...] += y

    @pl.when(c == pl.num_programs(1) - 1)
    def _():
        o_ref[...] = _layer_norm(DEEPNORM_ALPHA * x_ref[...] + acc_ref[...], g_ref[...], b_ref[...])


def _ffn(x2, w1, w3, w2, ln_g, ln_b, router=None, tm=512):
    n = x2.shape[0]
    ne, _, tf = w1.shape
    routed = router is not None
    full = lambda a: pl.BlockSpec(a.shape, lambda i, c: (0,) * a.ndim)
    in_specs = [pl.BlockSpec((tm, D_MODEL), lambda i, c: (i, 0)),
                pl.BlockSpec((1, D_MODEL, tf), lambda i, c: (c, 0, 0)),
                pl.BlockSpec((1, D_MODEL, tf), lambda i, c: (c, 0, 0)),
                pl.BlockSpec((1, tf, D_MODEL), lambda i, c: (c, 0, 0)),
                full(ln_g), full(ln_b)]
    args = [x2, w1, w3, w2, ln_g, ln_b]
    scratch = [pltpu.VMEM((tm, D_MODEL), F32), pltpu.VMEM((tm, D_MODEL), BF16)]
    if routed:
        in_specs.append(full(router))
        args.append(router)
        scratch.append(pltpu.VMEM((tm, LANES), F32))
    return pl.pallas_call(
        functools.partial(_ffn_kernel, routed=routed),
        grid=(n // tm, ne),
        in_specs=in_specs,
        out_specs=pl.BlockSpec((tm, D_MODEL), lambda i, c: (i, 0)),
        out_shape=jax.ShapeDtypeStruct((n, D_MODEL), F32),
        scratch_shapes=scratch,
        compiler_params=_cparams(("parallel", "arbitrary")),
        name="moe_ffn_ln" if routed else "ffn_ln",
    )(*args)


def _in_proj_columns():
    src = np.full((IN_P,), -1, np.int64)
    scale = np.ones((IN_P,), np.float32)
    o_cq, o_ckv, o_kr = 0, MLA_Q_RANK, MLA_Q_RANK + MLA_KV_RANK
    o_fox = o_kr + MLA_ROPE
    o_foxf = o_fox + 3 * FOX_HEADS * HEAD_DIM
    o_nq = o_foxf + FOX_HEADS
    o_nkv = o_nq + NSA_HEADS * HEAD_DIM
    o_ng = o_nkv + 6 * HEAD_DIM
    src[0:o_kr] = np.arange(o_kr)
    small = o_kr
    src[small + KR_LANE:small + KR_LANE + MLA_ROPE] = o_kr + np.arange(MLA_ROPE)
    src[small + FOXF_LANE:small + FOXF_LANE + FOX_HEADS] = o_foxf + np.arange(FOX_HEADS)
    src[small + NSAG_LANE:small + NSAG_LANE + 3 * NSA_HEADS] = o_ng + np.arange(3 * NSA_HEADS)
    d = np.arange(HEAD_DIM)
    for h in range(FOX_HEADS):
        src[FQ_OFF + LANES * h + d] = o_fox + HEAD_DIM * h + d
        scale[FQ_OFF + LANES * h + d] = HEAD_DIM ** -0.5
        src[FK_OFF + LANES * h + d] = o_fox + FOX_HEADS * HEAD_DIM + HEAD_DIM * h + d
    src[FV_OFF:FV_OFF + FOX_HEADS * HEAD_DIM] = o_fox + 2 * FOX_HEADS * HEAD_DIM + np.arange(FOX_HEADS * HEAD_DIM)
    for h in range(NSA_HEADS):
        src[NQ_OFF + LANES * h + d] = o_nq + HEAD_DIM * h + d
        scale[NQ_OFF + LANES * h + d] = HEAD_DIM ** -0.5
    src[NC_OFF:NC_OFF + 6 * HEAD_DIM] = o_nkv + np.arange(6 * HEAD_DIM)
    return src, scale


def _gather_cols(w, src, scale=None):
    out = jnp.where(jnp.asarray(src >= 0), jnp.take(w, jnp.asarray(np.maximum(src, 0)), axis=-1), 0.0)
    return out if scale is None else out * jnp.asarray(scale)


def _tables(s):
    half = MLA_ROPE // 2
    freqs = ROPE_THETA ** (-jnp.arange(half, dtype=F32) / half)
    ang = jnp.arange(s).astype(F32)[:, None] * freqs[None, :]
    cos, sin = jnp.cos(ang), jnp.sin(ang)
    z = lambda w: jnp.zeros((s, w), F32)
    tabs = {
        "cos": jnp.concatenate([jnp.ones((s, MLA_NOPE), F32), cos, cos, z(LANES - MLA_NOPE - MLA_ROPE)], 1),
        "sina": jnp.concatenate([z(MLA_NOPE), -sin, z(half), z(LANES - MLA_NOPE - MLA_ROPE)], 1),
        "sinb": jnp.concatenate([z(MLA_NOPE), z(half), sin, z(LANES - MLA_NOPE - MLA_ROPE)], 1),
    }
    pq = np.zeros((LANES, FOX_HEADS * LANES), np.float32)
    pk = np.zeros((LANES, FOX_HEADS * LANES), np.float32)
    oneq = np.zeros((1, FOX_HEADS * LANES), np.float32)
    onek = np.zeros((1, FOX_HEADS * LANES), np.float32)
    for h in range(FOX_HEADS):
        for t in range(3):
            pq[FOXF_LANE + FOX_HEADS * t + h, LANES * h + HEAD_DIM + t] = 1.0
            pk[FOXF_LANE + FOX_HEADS * t + h, LANES * h + HEAD_DIM + 3 + t] = -1.0
            oneq[0, LANES * h + HEAD_DIM + 3 + t] = 1.0
            onek[0, LANES * h + HEAD_DIM + t] = 1.0
    tabs.update(pq=jnp.asarray(pq, BF16), pk=jnp.asarray(pk, BF16), oneq=jnp.asarray(oneq), onek=jnp.asarray(onek))
    return tabs


def _layer_weights(p, l):
    qd = MLA_NOPE + MLA_ROPE
    src_q = np.full((MLA_HEADS * LANES,), -1, np.int64)
    src_k = np.full((MLA_HEADS * LANES,), -1, np.int64)
    src_v = np.zeros((MLA_HEADS * HEAD_DIM,), np.int64)
    for h in range(MLA_HEADS):
        src_q[LANES * h + np.arange(qd)] = qd * h + np.arange(qd)
        src_k[LANES * h + np.arange(MLA_NOPE)] = 2 * HEAD_DIM * h + np.arange(MLA_NOPE)
        src_v[HEAD_DIM * h + np.arange(HEAD_DIM)] = 2 * HEAD_DIM * h + MLA_NOPE + np.arange(HEAD_DIM)
    w_out = p["w_out"][l]
    n_attn = (MLA_HEADS + FOX_HEADS) * HEAD_DIM
    wb = w_out[n_attn:].reshape(NSA_HEADS, HEAD_DIM, D_MODEL)
    wb = jnp.concatenate([jnp.zeros_like(wb), wb], axis=1).reshape(NSA_HEADS * LANES, D_MODEL)
    bf = jnp.zeros((1, LANES), F32).at[0, FOXF_LANE:FOXF_LANE + FOX_HEADS].set(p["b_forget"][l])

    half = NSA_CMP_BLOCK // 2
    kpos, vpos = p["cmp_k_pos"][l], p["cmp_v_pos"][l]
    posa = jnp.concatenate([kpos[:half], vpos[:half]], -1).reshape(1, half * LANES)
    posb = jnp.concatenate([kpos[half:], vpos[half:]], -1).reshape(1, half * LANES)

    def w1_half(lo):
        k = p["cmp_k_w1"][l].reshape(NSA_CMP_BLOCK, HEAD_DIM, NSA_CMP_HIDDEN)[lo:lo + half]
        v = p["cmp_v_w1"][l].reshape(NSA_CMP_BLOCK, HEAD_DIM, NSA_CMP_HIDDEN)[lo:lo + half]
        zk = jnp.zeros_like(k)
        top = jnp.concatenate([k, zk], -1)
        bot = jnp.concatenate([zk, v], -1)
        return jnp.concatenate([top, bot], 1).reshape(half * LANES, 2 * NSA_CMP_HIDDEN).astype(BF16)

    zw2 = jnp.zeros((NSA_CMP_HIDDEN, HEAD_DIM), F32)
    w2 = jnp.concatenate([jnp.concatenate([p["cmp_k_w2"][l], zw2], 1),
                          jnp.concatenate([zw2, p["cmp_v_w2"][l]], 1)], 0).astype(BF16)
    return {
        "g_cq": p["g_cq"][l][None, :], "g_ckv": p["g_ckv"][l][None, :],
        "w_uq": _gather_cols(p["w_uq"][l], src_q).astype(BF16),
        "w_uk": _gather_cols(p["w_ukv"][l], src_k).astype(BF16),
        "w_uv": _gather_cols(p["w_ukv"][l], src_v).astype(BF16),
        "b_forget": bf,
        "cmp_posa": posa, "cmp_posb": posb, "cmp_wa": w1_half(0), "cmp_wb": w1_half(half), "cmp_w2": w2,
        "w_out_a": w_out[:n_attn].astype(BF16), "w_out_b": wb.astype(BF16),
        "ln1_g": p["ln1_g"][l][None, :], "ln1_b": p["ln1_b"][l][None, :],
        "ln2_g": p["ln2_g"][l][None, :], "ln2_b": p["ln2_b"][l][None, :],
    }


def kernel(x, w_in, b_forget, g_cq, w_uq, g_ckv, w_ukv, cmp_k_pos, cmp_k_w1, cmp_k_w2, cmp_v_pos, cmp_v_w1,
           cmp_v_w2, w_out, ln1_g, ln1_b, ln2_g, ln2_b, ffn_w1, ffn_w3, ffn_w2, router_w, moe_w1, moe_w3,
           moe_w2):
    b, s, d = x.shape
    assert d == D_MODEL and s % 512 == 0 and s // NSA_CMP_STRIDE == LANES, (b, s, d)
    p = dict(b_forget=b_forget, g_cq=g_cq, w_uq=w_uq, g_ckv=g_ckv, w_ukv=w_ukv, cmp_k_pos=cmp_k_pos,
             cmp_k_w1=cmp_k_w1, cmp_k_w2=cmp_k_w2, cmp_v_pos=cmp_v_pos, cmp_v_w1=cmp_v_w1, cmp_v_w2=cmp_v_w2,
             w_out=w_out, ln1_g=ln1_g, ln1_b=ln1_b, ln2_g=ln2_g, ln2_b=ln2_b)
    src, scale = _in_proj_columns()
    w_in_p = _gather_cols(w_in, src, scale).astype(BF16)
    tabs = _tables(s)
    n = b * s
    x2 = x.reshape(n, d)
    for l in range(DEPTH):
        wl = _layer_weights(p, l)
        za, fq, fk, fv, nq, nc, nkv = _in_proj(x2, w_in_p[l])
        q_all, k_all, v_all = _prep(za, fq, fk, fv, wl, tabs, b, s)
        o_attn = _flash(q_all, k_all, v_all)
        kcvc = _compress(nc.reshape(b, s // NSA_CMP_STRIDE, NSA_CMP_STRIDE * LANES), wl)
        nkv3 = nkv.reshape(b, s, 2 * LANES)
        kwin_pad = jnp.pad(nkv3[:, :, LANES:], ((0, 0), (NSA_WINDOW, 0), (0, 0)))
        o_nsa = _nsa(nq.reshape(b, s, NSA_HEADS * LANES), kcvc, nkv3[:, :, :LANES], kwin_pad, za)
        x2 = _out_proj(o_attn.reshape(n, -1), o_nsa.reshape(n, -1), x2, wl)
        j = l // 2
        if l % 2 == 0:
            d_ff = ffn_w1.shape[-1]
            nchunk = 2
            tf = d_ff // nchunk
            w1 = ffn_w1[j].reshape(d, nchunk, tf).transpose(1, 0, 2).astype(BF16)
            w3 = ffn_w3[j].reshape(d, nchunk, tf).transpose(1, 0, 2).astype(BF16)
            w2 = ffn_w2[j].reshape(nchunk, tf, d).astype(BF16)
            x2 = _ffn(x2, w1, w3, w2, wl["ln2_g"], wl["ln2_b"])
        else:
            rw = jnp.pad(router_w[j], ((0, 0), (0, LANES - N_EXPERTS)))
            x2 = _ffn(x2, moe_w1[j].astype(BF16), moe_w3[j].astype(BF16), moe_w2[j].astype(BF16),
                      wl["ln2_g"], wl["ln2_b"], router=rw)
    return x2.reshape(b, s, d)
```

```python
import functools

import numpy as np
import jax
import jax.numpy as jnp
from jax import lax
from jax.experimental import pallas as pl
from jax.experimental.pallas import tpu as pltpu

F32 = jnp.float32
BF16 = jnp.bfloat16

D_MODEL = 1024
HEAD_DIM = 64
LANES = 128
MLA_HEADS = 6
MLA_Q_RANK = 384
MLA_KV_RANK = 256
MLA_NOPE = 64
MLA_ROPE = 32
ROPE_THETA = 10000.0
FOX_HEADS = 6
NSA_HEADS = 4
NSA_CMP_BLOCK = 32
NSA_CMP_STRIDE = 16
NSA_CMP_HIDDEN = 128
NSA_SEL_BLOCK = 64
NSA_SEL_TOPN = 8
NSA_WINDOW = 256
NSA_FORCE_SCORE = 1.0e4
N_EXPERTS = 8
NORM_EPS = 1e-5
DEPTH = 2
DEEPNORM_ALPHA = (2 * DEPTH) ** 0.25
NEG = -1e30
VMEM_LIMIT = 56 * 1024 * 1024

KR_LANE = 64
FOXF_LANE = 96
NSAG_LANE = 102

ZA_W = 768
FQ_OFF = ZA_W
FK_OFF = FQ_OFF + 768
FV_OFF = FK_OFF + 768
NQ_OFF = FV_OFF + 384
NC_OFF = NQ_OFF + 512
NKV_OFF = NC_OFF + 128
IN_P = NKV_OFF + 256

ALIBI = tuple(2.0 ** (-8.0 * (i + 1) / NSA_HEADS) for i in range(NSA_HEADS))


def _dot(a, b, **kw):
    return jnp.dot(a, b, preferred_element_type=F32, **kw)


def _dot_nt(a, b):
    return lax.dot_general(a, b, (((1,), (1,)), ((), ())), preferred_element_type=F32)


def _iota(shape, dim):
    return lax.broadcasted_iota(jnp.int32, shape, dim)


def _cparams(sem):
    return pltpu.CompilerParams(dimension_semantics=sem, vmem_limit_bytes=VMEM_LIMIT)


def _in_proj_kernel(x_ref, w_ref, za_ref, fq_ref, fk_ref, fv_ref, nq_ref, nc_ref, nkv_ref):
    xb = x_ref[...].astype(BF16)

    def mm(a, b):
        return _dot(xb, w_ref[:, a:b])

    za_ref[...] = mm(0, ZA_W)
    fq_ref[...] = mm(FQ_OFF, FK_OFF).astype(BF16)
    fk_ref[...] = mm(FK_OFF, FV_OFF).astype(BF16)
    fv_ref[...] = mm(FV_OFF, NQ_OFF).astype(BF16)
    nq_ref[...] = mm(NQ_OFF, NC_OFF).astype(BF16)
    nc_ref[...] = mm(NC_OFF, NKV_OFF)
    nkv_ref[...] = mm(NKV_OFF, IN_P).astype(BF16)


def _in_proj(x2, w_p, tm=512):
    n = x2.shape[0]
    widths = [(ZA_W, F32), (768, BF16), (768, BF16), (384, BF16), (512, BF16), (128, F32), (256, BF16)]
    return pl.pallas_call(
        _in_proj_kernel,
        grid=(n // tm,),
        in_specs=[pl.BlockSpec((tm, D_MODEL), lambda i: (i, 0)),
                  pl.BlockSpec((D_MODEL, IN_P), lambda i: (0, 0))],
        out_specs=[pl.BlockSpec((tm, w), lambda i: (i, 0)) for w, _ in widths],
        out_shape=[jax.ShapeDtypeStruct((n, w), dt) for w, dt in widths],
        compiler_params=_cparams(("parallel",)),
        name="in_proj",
    )(x2, w_p)


def _prep_kernel(za_ref, fq_ref, fk_ref, fv_ref, gcq_ref, wuq_ref, gckv_ref, wuk_ref, wuv_ref,
                 bf_ref, cos_ref, sina_ref, sinb_ref, pq_ref, pk_ref, oneq_ref, onek_ref,
                 q_out, k_out, v_out, carry_ref, *, ts):
    @pl.when(pl.program_id(1) == 0)
    def _():
        carry_ref[...] = jnp.zeros_like(carry_ref)

    za = za_ref[...]
    cq = za[:, 0:MLA_Q_RANK]
    ckv = za[:, MLA_Q_RANK:MLA_Q_RANK + MLA_KV_RANK]
    small = za[:, MLA_Q_RANK + MLA_KV_RANK:ZA_W]

    cos = cos_ref[...]
    sina = sina_ref[...]
    sinb = sinb_ref[...]

    def rope(blk):
        return blk * cos + pltpu.roll(blk, LANES - 16, 1) * sina + pltpu.roll(blk, 16, 1) * sinb

    xn = cq * lax.rsqrt(jnp.mean(cq * cq, -1, keepdims=True) + NORM_EPS) * gcq_ref[...]
    q = _dot(xn.astype(BF16), wuq_ref[...])
    cn = ckv * lax.rsqrt(jnp.mean(ckv * ckv, -1, keepdims=True) + NORM_EPS) * gckv_ref[...]
    cnb = cn.astype(BF16)
    kn = _dot(cnb, wuk_ref[...])
    v = _dot(cnb, wuv_ref[...])
    kr = rope(small)
    mla_scale = (MLA_NOPE + MLA_ROPE) ** -0.5
    lane = _iota((1, LANES), 1)

    def pad_v(blk):
        return jnp.where(lane < HEAD_DIM, blk, jnp.where(lane == HEAD_DIM, 1.0, 0.0)).astype(BF16)

    for h in range(MLA_HEADS):
        sl = slice(LANES * h, LANES * (h + 1))
        q_out[0, :, sl] = (rope(q[:, sl]) * mla_scale).astype(BF16)
        k_out[0, :, sl] = (kn[:, sl] + kr).astype(BF16)
        v_out[0, :, sl] = pad_v(v[:, sl])

    fmask = (lane >= FOXF_LANE) & (lane < FOXF_LANE + FOX_HEADS)
    lf = jnp.where(fmask, jax.nn.log_sigmoid(small + bf_ref[...]), 0.0)
    tril = (_iota((ts, ts), 0) >= _iota((ts, ts), 1)).astype(F32)
    cs = _dot(tril, lf, precision=lax.Precision.HIGHEST) + carry_ref[...]
    carry_ref[...] = cs[ts - 1:ts, :]
    hi = cs.astype(BF16).astype(F32)
    r1 = cs - hi
    mid = r1.astype(BF16).astype(F32)
    lo = (r1 - mid).astype(BF16).astype(F32)
    c3 = (hi + pltpu.roll(mid, FOX_HEADS, 1) + pltpu.roll(lo, 2 * FOX_HEADS, 1)).astype(BF16)
    augq = _dot(c3, pq_ref[...]) + oneq_ref[...]
    augk = _dot(c3, pk_ref[...]) + onek_ref[...]
    off = MLA_HEADS * LANES
    q_out[0, :, off:off + FOX_HEADS * LANES] = (fq_ref[...].astype(F32) + augq).astype(BF16)
    k_out[0, :, off:off + FOX_HEADS * LANES] = (fk_ref[...].astype(F32) + augk).astype(BF16)
    for pr in range(FOX_HEADS // 2):
        blk = fv_ref[:, LANES * pr:LANES * (pr + 1)].astype(F32)
        v_out[0, :, off + 2 * pr * LANES:off + (2 * pr + 1) * LANES] = pad_v(blk)
        v_out[0, :, off + (2 * pr + 1) * LANES:off + (2 * pr + 2) * LANES] = pad_v(pltpu.roll(blk, HEAD_DIM, 1))


def _prep(za, fq, fk, fv, wl, tabs, b, s, ts=512):
    ns = s // ts
    row = lambda w: pl.BlockSpec((ts, w), lambda bi, si: (bi * ns + si, 0))
    full = lambda a: pl.BlockSpec(a.shape, lambda bi, si: (0,) * a.ndim)
    tab = pl.BlockSpec((ts, LANES), lambda bi, si: (si, 0))
    consts = [wl["g_cq"], wl["w_uq"], wl["g_ckv"], wl["w_uk"], wl["w_uv"], wl["b_forget"]]
    tail = [tabs["pq"], tabs["pk"], tabs["oneq"], tabs["onek"]]
    nh = MLA_HEADS + FOX_HEADS
    return pl.pallas_call(
        functools.partial(_prep_kernel, ts=ts),
        grid=(b, ns),
        in_specs=[row(ZA_W), row(768), row(768), row(384)] + [full(a) for a in consts]
                 + [tab, tab, tab] + [full(a) for a in tail],
        out_specs=[pl.BlockSpec((1, ts, nh * LANES), lambda bi, si: (bi, si, 0)),
                   pl.BlockSpec((1, ts, nh * LANES), lambda bi, si: (bi, si, 0)),
                   pl.BlockSpec((1, ts, nh * LANES), lambda bi, si: (bi, si, 0))],
        out_shape=[jax.ShapeDtypeStruct((b, s, nh * LANES), BF16),
                   jax.ShapeDtypeStruct((b, s, nh * LANES), BF16),
                   jax.ShapeDtypeStruct((b, s, nh * LANES), BF16)],
        scratch_shapes=[pltpu.VMEM((1, LANES), F32)],
        compiler_params=_cparams(("arbitrary", "arbitrary")),
        name="head_prep",
    )(za, fq, fk, fv, *consts, tabs["cos"], tabs["sina"], tabs["sinb"], *tail)


def _flash_kernel(q_ref, k_ref, v_ref, o_ref, m_ref, acc_ref, s_ref, *, tq, tk):
    qi = pl.program_id(2)
    m_ref[...] = jnp.full_like(m_ref, NEG)
    acc_ref[...] = jnp.zeros_like(acc_ref)
    nsub = tq // tk
    nfull = qi * nsub
    heads = [slice(LANES * h, LANES * (h + 1)) for h in range(2)]

    def scores(j, r0, sl):
        k0 = pl.multiple_of(j * tk, tk)
        return _dot_nt(q_ref[0, r0:tq, sl], k_ref[0, pl.ds(k0, tk), sl])

    def consume(s, h, j, r0, masked):
        k0 = pl.multiple_of(j * tk, tk)
        if masked:
            s = jnp.where(_iota((tq - r0, tk), 0) >= _iota((tq - r0, tk), 1), s, NEG)
        chunks = [s[:, LANES * c:LANES * (c + 1)] for c in range(tk // LANES)]
        m_prev = m_ref[h, r0:tq, :]
        m_new = jnp.maximum(m_prev, jnp.max(functools.reduce(jnp.maximum, chunks), -1, keepdims=True))
        p = jnp.concatenate([jnp.exp(c - m_new) for c in chunks], 1).astype(BF16)
        acc_ref[h, r0:tq, :] = (jnp.exp(m_prev - m_new) * acc_ref[h, r0:tq, :]
                                + _dot(p, v_ref[0, pl.ds(k0, tk), heads[h]]))
        m_ref[h, r0:tq, :] = m_new

    assert nsub == 2
    for h in range(2):
        s_ref[0, h] = scores(0, 0, heads[h])

    def body(i, carry):
        j = 2 * i
        for b in range(2):
            for h in range(2):
                s_ref[1 - b, h] = scores(j + b + 1, 0, heads[h])
            for h in range(2):
                consume(s_ref[b, h], h, j + b, 0, False)
        return carry

    lax.fori_loop(0, qi, body, 0)
    for h in range(2):
        consume(s_ref[0, h], h, nfull, 0, True)
    for d in range(1, nsub):
        for h in range(2):
            consume(scores(nfull + d, d * tk, heads[h]), h, nfull + d, d * tk, True)

    lane = _iota((1, LANES), 1)
    o0 = acc_ref[0]
    o1 = acc_ref[1]
    o0 = o0 / o0[:, HEAD_DIM:HEAD_DIM + 1]
    o1 = o1 / o1[:, HEAD_DIM:HEAD_DIM + 1]
    o_ref[0] = jnp.where(lane < HEAD_DIM, o0, pltpu.roll(o1, HEAD_DIM, 1)).astype(o_ref.dtype)


def _flash(q_all, k_all, v_all, tq=512, tk=256):
    b, s, _ = q_all.shape
    npair = (MLA_HEADS + FOX_HEADS) // 2
    return pl.pallas_call(
        functools.partial(_flash_kernel, tq=tq, tk=tk),
        grid=(b, npair, s // tq),
        in_specs=[pl.BlockSpec((1, tq, 2 * LANES), lambda bi, p, qi: (bi, qi, p)),
                  pl.BlockSpec((1, s, 2 * LANES), lambda bi, p, qi: (bi, 0, p)),
                  pl.BlockSpec((1, s, 2 * LANES), lambda bi, p, qi: (bi, 0, p))],
        out_specs=pl.BlockSpec((1, tq, LANES), lambda bi, p, qi: (bi, qi, p)),
        out_shape=jax.ShapeDtypeStruct((b, s, npair * LANES), BF16),
        scratch_shapes=[pltpu.VMEM((2, tq, LANES), F32), pltpu.VMEM((2, tq, LANES), F32),
                        pltpu.VMEM((2, 2, tq, tk), F32)],
        compiler_params=_cparams(("parallel", "parallel", "arbitrary")),
        name="flash_attn",
    )(q_all, k_all, v_all)


def _cmp_kernel(tc_ref, posa_ref, posb_ref, wa_ref, wb_ref, w2_ref, out_ref):
    tc = tc_ref[0]
    a = _dot((tc + posa_ref[...]).astype(BF16), wa_ref[...])
    b = _dot((tc + posb_ref[...]).astype(BF16), wb_ref[...])
    pre = a + pltpu.roll(b, b.shape[0] - 1, 0)
    hid = jax.nn.silu(pre)
    out_ref[0] = _dot(hid.astype(BF16), w2_ref[...]).astype(out_ref.dtype)


def _compress(tc, wl):
    b, nc, w = tc.shape
    full = lambda a: pl.BlockSpec(a.shape, lambda bi: (0,) * a.ndim)
    consts = [wl["cmp_posa"], wl["cmp_posb"], wl["cmp_wa"], wl["cmp_wb"], wl["cmp_w2"]]
    return pl.pallas_call(
        _cmp_kernel,
        grid=(b,),
        in_specs=[pl.BlockSpec((1, nc, w), lambda bi: (bi, 0, 0))] + [full(a) for a in consts],
        out_specs=pl.BlockSpec((1, nc, LANES), lambda bi: (bi, 0, 0)),
        out_shape=jax.ShapeDtypeStruct((b, nc, LANES), BF16),
        compiler_params=_cparams(("parallel",)),
        name="nsa_compress",
    )(tc, *consts)


def _masked_softmax(s, mask):
    s = jnp.where(mask, s, NEG)
    m = jnp.max(s, -1, keepdims=True)
    p = jnp.where(mask, jnp.exp(s - m), 0.0)
    return p / jnp.maximum(jnp.sum(p, -1, keepdims=True), 1e-30)


def _nsa_kernel(nq_ref, kc_ref, ksl_ref, kwin_ref, g_ref, o_ref, m_ref, l_ref, acc_ref, *, tq, n_cmp):
    qi = pl.program_id(1)
    t0 = pl.multiple_of(qi * tq, tq)
    rpos = t0 + _iota((tq, 1), 0)
    lane = _iota((1, LANES), 1)
    qs = [nq_ref[0, :, LANES * h:LANES * (h + 1)] for h in range(NSA_HEADS)]

    dist_i = rpos - (NSA_CMP_STRIDE * lane + NSA_CMP_BLOCK - 1)
    valid_c = (dist_i >= 0) & (lane < n_cmp)
    dist_c = dist_i.astype(F32)
    kc = kc_ref[0]
    psum = jnp.zeros((tq, LANES), F32)
    o_cmp = []
    for h in range(NSA_HEADS):
        p = _masked_softmax(_dot_nt(qs[h], kc) - ALIBI[h] * dist_c, valid_c)
        psum = psum + p
        o_cmp.append(_dot(p.astype(BF16), kc))

    n_i = _iota((LANES, LANES), 0)
    j_i = _iota((LANES, LANES), 1)
    ov = ((NSA_CMP_STRIDE * n_i < NSA_SEL_BLOCK * (j_i + 1))
          & (NSA_CMP_STRIDE * n_i + NSA_CMP_BLOCK > NSA_SEL_BLOCK * j_i)
          & (n_i < n_cmp)).astype(F32)
    imp = _dot(psum, ov, precision=lax.Precision.HIGHEST)
    cur = jnp.right_shift(rpos, 6)
    forced = (lane == 0) | (lane == cur) | (lane == cur - 1)
    future = lane * NSA_SEL_BLOCK > rpos
    n_blk = kwin_ref.shape[1] // NSA_SEL_BLOCK - NSA_WINDOW // NSA_SEL_BLOCK
    work = jnp.where(forced, NSA_FORCE_SCORE, jnp.where(future, -1.0, imp))
    work = jnp.where(lane < n_blk, work, -jnp.inf)
    sel = jnp.zeros((tq, LANES), jnp.bool_)
    for _ in range(NSA_SEL_TOPN):
        mx = jnp.max(work, -1, keepdims=True)
        idx = jnp.min(jnp.where(work == mx, lane, LANES), -1, keepdims=True)
        pick = lane == idx
        sel = sel | pick
        work = jnp.where(pick, -jnp.inf, work)
    selb = jnp.where(sel, 1.0, 0.0).astype(BF16)

    m_ref[...] = jnp.full_like(m_ref, NEG)
    l_ref[...] = jnp.zeros_like(l_ref)
    acc_ref[...] = jnp.zeros_like(acc_ref)

    def sel_chunk(c, carry):
        k0 = pl.multiple_of(c * tq, tq)
        kv = ksl_ref[0, pl.ds(k0, tq), :]
        kpos = k0 + _iota((1, tq), 1)
        expand = jnp.where(_iota((LANES, tq), 0) == jnp.right_shift(k0 + _iota((LANES, tq), 1), 6),
                           1.0, 0.0).astype(BF16)
        keep = (_dot(selb, expand) > 0.5) & (kpos <= rpos)
        dist = (rpos - kpos).astype(F32)
        for h in range(NSA_HEADS):
            s = jnp.where(keep, _dot_nt(qs[h], kv) - ALIBI[h] * dist, NEG)
            chunks = [s[:, LANES * i:LANES * (i + 1)] for i in range(tq // LANES)]
            m_prev = m_ref[h]
            m_new = jnp.maximum(m_prev, jnp.max(functools.reduce(jnp.maximum, chunks), -1, keepdims=True))
            alpha = jnp.exp(m_prev - m_new)
            ps = [jnp.exp(ch - m_new) for ch in chunks]
            l_ref[h] = alpha * l_ref[h] + jnp.sum(functools.reduce(jnp.add, ps), -1, keepdims=True)
            acc_ref[h] = alpha * acc_ref[h] + _dot(jnp.concatenate(ps, 1).astype(BF16), kv)
            m_ref[h] = m_new
        return carry

    lax.fori_loop(0, qi + 1, sel_chunk, 0)

    wlen = tq + NSA_WINDOW
    kw = kwin_ref[0, pl.ds(t0, wlen), :]
    kp = t0 - NSA_WINDOW + _iota((1, wlen), 1)
    d_w = rpos - kp
    keep_w = (d_w >= 0) & (d_w < NSA_WINDOW) & (kp >= 0)
    d_wf = d_w.astype(F32)

    gates = jax.nn.sigmoid(g_ref[...])
    for h in range(NSA_HEADS):
        p_w = _masked_softmax(_dot_nt(qs[h], kw) - ALIBI[h] * d_wf, keep_w)
        o_win = _dot(p_w.astype(BF16), kw)
        o_sel = acc_ref[h] / jnp.maximum(l_ref[h], 1e-30)
        g0 = NSAG_LANE + 3 * h
        o = (gates[:, g0:g0 + 1] * o_cmp[h] + gates[:, g0 + 1:g0 + 2] * o_sel
             + gates[:, g0 + 2:g0 + 3] * o_win)
        o_ref[0, :, LANES * h:LANES * (h + 1)] = jnp.where(lane >= HEAD_DIM, o, 0.0).astype(o_ref.dtype)


def _nsa(nq, kcvc, ksl, kwin_pad, za, tq=256):
    b, s, _ = nq.shape
    nq_t = s // tq
    n_cmp = (s - NSA_CMP_BLOCK) // NSA_CMP_STRIDE + 1
    return pl.pallas_call(
        functools.partial(_nsa_kernel, tq=tq, n_cmp=n_cmp),
        grid=(b, nq_t),
        in_specs=[pl.BlockSpec((1, tq, NSA_HEADS * LANES), lambda bi, qi: (bi, qi, 0)),
                  pl.BlockSpec((1, kcvc.shape[1], LANES), lambda bi, qi: (bi, 0, 0)),
                  pl.BlockSpec((1, s, LANES), lambda bi, qi: (bi, 0, 0)),
                  pl.BlockSpec((1, s + NSA_WINDOW, LANES), lambda bi, qi: (bi, 0, 0)),
                  pl.BlockSpec((tq, LANES), lambda bi, qi: (bi * nq_t + qi, (ZA_W - LANES) // LANES))],
        out_specs=pl.BlockSpec((1, tq, NSA_HEADS * LANES), lambda bi, qi: (bi, qi, 0)),
        out_shape=jax.ShapeDtypeStruct((b, s, NSA_HEADS * LANES), BF16),
        scratch_shapes=[pltpu.VMEM((NSA_HEADS, tq, LANES), F32) for _ in range(3)],
        compiler_params=_cparams(("parallel", "arbitrary")),
        name="nsa_attn",
    )(nq, kcvc, ksl, kwin_pad, za)


def _layer_norm(y, g, b):
    mu = jnp.mean(y, -1, keepdims=True)
    yc = y - mu
    var = jnp.mean(yc * yc, -1, keepdims=True)
    return yc * lax.rsqrt(var + NORM_EPS) * g + b


def _out_proj_kernel(oa_ref, on_ref, x_ref, wa_ref, wb_ref, g_ref, b_ref, o_ref):
    mix = _dot(oa_ref[...], wa_ref[...]) + _dot(on_ref[...], wb_ref[...])
    o_ref[...] = _layer_norm(DEEPNORM_ALPHA * x_ref[...] + mix, g_ref[...], b_ref[...])


def _out_proj(o_attn, o_nsa, x2, wl, tm=512):
    n = x2.shape[0]
    full = lambda a: pl.BlockSpec(a.shape, lambda i: (0,) * a.ndim)
    row = lambda w: pl.BlockSpec((tm, w), lambda i: (i, 0))
    consts = [wl["w_out_a"], wl["w_out_b"], wl["ln1_g"], wl["ln1_b"]]
    return pl.pallas_call(
        _out_proj_kernel,
        grid=(n // tm,),
        in_specs=[row(o_attn.shape[1]), row(o_nsa.shape[1]), row(D_MODEL)] + [full(a) for a in consts],
        out_specs=row(D_MODEL),
        out_shape=jax.ShapeDtypeStruct((n, D_MODEL), F32),
        compiler_params=_cparams(("parallel",)),
        name="out_proj_ln",
    )(o_attn, o_nsa, x2, *consts)


def _ffn_kernel(*refs, routed):
    if routed:
        x_ref, w1_ref, w3_ref, w2_ref, g_ref, b_ref, rw_ref, o_ref, acc_ref, xb_ref, gate_ref = refs
    else:
        x_ref, w1_ref, w3_ref, w2_ref, g_ref, b_ref, o_ref, acc_ref, xb_ref = refs
    c = pl.program_id(1)
    lane = _iota((1, LANES), 1)

    @pl.when(c == 0)
    def _():
        x = x_ref[...]
        acc_ref[...] = jnp.zeros_like(acc_ref)
        xb_ref[...] = x.astype(BF16)
        if routed:
            logits = jnp.where(lane < N_EXPERTS, _dot(x, rw_ref[...], precision=lax.Precision.HIGHEST), NEG)
            e = jnp.exp(logits - jnp.max(logits, -1, keepdims=True))
            probs = e / jnp.sum(e, -1, keepdims=True)
            p1 = jnp.max(probs, -1, keepdims=True)
            i1 = jnp.min(jnp.where(probs == p1, lane, LANES), -1, keepdims=True)
            rest = jnp.where(lane == i1, -1.0, probs)
            p2 = jnp.max(rest, -1, keepdims=True)
            i2 = jnp.min(jnp.where(rest == p2, lane, LANES), -1, keepdims=True)
            tot = p1 + p2
            gate_ref[...] = jnp.where(lane == i1, p1 / tot, jnp.where(lane == i2, p2 / tot, 0.0))

    xb = xb_ref[...]
    a = jax.nn.silu(_dot(xb, w1_ref[0])) * _dot(xb, w3_ref[0])
    y = _dot(a.astype(BF16), w2_ref[0])
    if routed:
        y = y * jnp.sum(jnp.where(lane == c, gate_ref[...], 0.0), -1, keepdims=True)
    acc_ref[...] += y

    @pl.when(c == pl.num_programs(1) - 1)
    def _():
        o_ref[...] = _layer_norm(DEEPNORM_ALPHA * x_ref[...] + acc_ref[...], g_ref[...], b_ref[...])


def _ffn(x2, w1, w3, w2, ln_g, ln_b, router=None, tm=512):
    n = x2.shape[0]
    ne, _, tf = w1.shape
    routed = router is not None
    full = lambda a: pl.BlockSpec(a.shape, lambda i, c: (0,) * a.ndim)
    in_specs = [pl.BlockSpec((tm, D_MODEL), lambda i, c: (i, 0)),
                pl.BlockSpec((1, D_MODEL, tf), lambda i, c: (c, 0, 0)),
                pl.BlockSpec((1, D_MODEL, tf), lambda i, c: (c, 0, 0)),
                pl.BlockSpec((1, tf, D_MODEL), lambda i, c: (c, 0, 0)),
                full(ln_g), full(ln_b)]
    args = [x2, w1, w3, w2, ln_g, ln_b]
    scratch = [pltpu.VMEM((tm, D_MODEL), F32), pltpu.VMEM((tm, D_MODEL), BF16)]
    if routed:
        in_specs.append(full(router))
        args.append(router)
        scratch.append(pltpu.VMEM((tm, LANES), F32))
    return pl.pallas_call(
        functools.partial(_ffn_kernel, routed=routed),
        grid=(n // tm, ne),
        in_specs=in_specs,
        out_specs=pl.BlockSpec((tm, D_MODEL), lambda i, c: (i, 0)),
        out_shape=jax.ShapeDtypeStruct((n, D_MODEL), F32),
        scratch_shapes=scratch,
        compiler_params=_cparams(("parallel", "arbitrary")),
        name="moe_ffn_ln" if routed else "ffn_ln",
    )(*args)


def _in_proj_columns():
    src = np.full((IN_P,), -1, np.int64)
    scale = np.ones((IN_P,), np.float32)
    o_cq, o_ckv, o_kr = 0, MLA_Q_RANK, MLA_Q_RANK + MLA_KV_RANK
    o_fox = o_kr + MLA_ROPE
    o_foxf = o_fox + 3 * FOX_HEADS * HEAD_DIM
    o_nq = o_foxf + FOX_HEADS
    o_nkv = o_nq + NSA_HEADS * HEAD_DIM
    o_ng = o_nkv + 6 * HEAD_DIM
    src[0:o_kr] = np.arange(o_kr)
    small = o_kr
    src[small + KR_LANE:small + KR_LANE + MLA_ROPE] = o_kr + np.arange(MLA_ROPE)
    src[small + FOXF_LANE:small + FOXF_LANE + FOX_HEADS] = o_foxf + np.arange(FOX_HEADS)
    src[small + NSAG_LANE:small + NSAG_LANE + 3 * NSA_HEADS] = o_ng + np.arange(3 * NSA_HEADS)
    d = np.arange(HEAD_DIM)
    for h in range(FOX_HEADS):
        src[FQ_OFF + LANES * h + d] = o_fox + HEAD_DIM * h + d
        scale[FQ_OFF + LANES * h + d] = HEAD_DIM ** -0.5
        src[FK_OFF + LANES * h + d] = o_fox + FOX_HEADS * HEAD_DIM + HEAD_DIM * h + d
    src[FV_OFF:FV_OFF + FOX_HEADS * HEAD_DIM] = o_fox + 2 * FOX_HEADS * HEAD_DIM + np.arange(FOX_HEADS * HEAD_DIM)
    for h in range(NSA_HEADS):
        src[NQ_OFF + LANES * h + d] = o_nq + HEAD_DIM * h + d
        scale[NQ_OFF + LANES * h + d] = HEAD_DIM ** -0.5
    src[NC_OFF:NC_OFF + 6 * HEAD_DIM] = o_nkv + np.arange(6 * HEAD_DIM)
    return src, scale


def _gather_cols(w, src, scale=None):
    scale = np.ones(src.shape, np.float32) if scale is None else scale
    parts, i, n = [], 0, len(src)
    while i < n:
        j = i + 1
        while j < n and scale[j] == scale[i] and (src[j] == src[j - 1] + 1 if src[i] >= 0 else src[j] < 0):
            j += 1
        if src[i] < 0:
            parts.append(jnp.zeros(w.shape[:-1] + (j - i,), w.dtype))
        else:
            seg = w[..., int(src[i]):int(src[i]) + (j - i)]
            parts.append(seg if scale[i] == 1.0 else seg * float(scale[i]))
        i = j
    return jnp.concatenate(parts, axis=-1)


def _tables(s):
    half = MLA_ROPE // 2
    freqs = ROPE_THETA ** (-jnp.arange(half, dtype=F32) / half)
    ang = jnp.arange(s).astype(F32)[:, None] * freqs[None, :]
    cos, sin = jnp.cos(ang), jnp.sin(ang)
    z = lambda w: jnp.zeros((s, w), F32)
    tabs = {
        "cos": jnp.concatenate([jnp.ones((s, MLA_NOPE), F32), cos, cos, z(LANES - MLA_NOPE - MLA_ROPE)], 1),
        "sina": jnp.concatenate([z(MLA_NOPE), -sin, z(half), z(LANES - MLA_NOPE - MLA_ROPE)], 1),
        "sinb": jnp.concatenate([z(MLA_NOPE), z(half), sin, z(LANES - MLA_NOPE - MLA_ROPE)], 1),
    }
    pq = np.zeros((LANES, FOX_HEADS * LANES), np.float32)
    pk = np.zeros((LANES, FOX_HEADS * LANES), np.float32)
    oneq = np.zeros((1, FOX_HEADS * LANES), np.float32)
    onek = np.zeros((1, FOX_HEADS * LANES), np.float32)
    for h in range(FOX_HEADS):
        for t in range(3):
            pq[FOXF_LANE + FOX_HEADS * t + h, LANES * h + HEAD_DIM + t] = 1.0
            pk[FOXF_LANE + FOX_HEADS * t + h, LANES * h + HEAD_DIM + 3 + t] = -1.0
            oneq[0, LANES * h + HEAD_DIM + 3 + t] = 1.0
            onek[0, LANES * h + HEAD_DIM + t] = 1.0
    tabs.update(pq=jnp.asarray(pq, BF16), pk=jnp.asarray(pk, BF16), oneq=jnp.asarray(oneq), onek=jnp.asarray(onek))
    return tabs


def _layer_weights(p, l):
    qd = MLA_NOPE + MLA_ROPE
    src_q = np.full((MLA_HEADS * LANES,), -1, np.int64)
    src_k = np.full((MLA_HEADS * LANES,), -1, np.int64)
    src_v = np.full((MLA_HEADS * LANES,), -1, np.int64)
    for h in range(MLA_HEADS):
        src_q[LANES * h + np.arange(qd)] = qd * h + np.arange(qd)
        src_k[LANES * h + np.arange(MLA_NOPE)] = 2 * HEAD_DIM * h + np.arange(MLA_NOPE)
        src_v[LANES * h + np.arange(HEAD_DIM)] = 2 * HEAD_DIM * h + MLA_NOPE + np.arange(HEAD_DIM)
    w_out = p["w_out"][l]
    n_attn = (MLA_HEADS + FOX_HEADS) * HEAD_DIM
    wb = w_out[n_attn:].reshape(NSA_HEADS, HEAD_DIM, D_MODEL)
    wb = jnp.concatenate([jnp.zeros_like(wb), wb], axis=1).reshape(NSA_HEADS * LANES, D_MODEL)
    bf = jnp.zeros((1, LANES), F32).at[0, FOXF_LANE:FOXF_LANE + FOX_HEADS].set(p["b_forget"][l])

    half = NSA_CMP_BLOCK // 2
    kpos, vpos = p["cmp_k_pos"][l], p["cmp_v_pos"][l]
    posa = jnp.concatenate([kpos[:half], vpos[:half]], -1).reshape(1, half * LANES)
    posb = jnp.concatenate([kpos[half:], vpos[half:]], -1).reshape(1, half * LANES)

    def w1_half(lo):
        k = p["cmp_k_w1"][l].reshape(NSA_CMP_BLOCK, HEAD_DIM, NSA_CMP_HIDDEN)[lo:lo + half]
        v = p["cmp_v_w1"][l].reshape(NSA_CMP_BLOCK, HEAD_DIM, NSA_CMP_HIDDEN)[lo:lo + half]
        zk = jnp.zeros_like(k)
        top = jnp.concatenate([k, zk], -1)
        bot = jnp.concatenate([zk, v], -1)
        return jnp.concatenate([top, bot], 1).reshape(half * LANES, 2 * NSA_CMP_HIDDEN).astype(BF16)

    zw2 = jnp.zeros((NSA_CMP_HIDDEN, HEAD_DIM), F32)
    w2 = jnp.concatenate([jnp.concatenate([p["cmp_k_w2"][l], zw2], 1),
                          jnp.concatenate([zw2, p["cmp_v_w2"][l]], 1)], 0).astype(BF16)
    return {
        "g_cq": p["g_cq"][l][None, :], "g_ckv": p["g_ckv"][l][None, :],
        "w_uq": _gather_cols(p["w_uq"][l], src_q).astype(BF16),
        "w_uk": _gather_cols(p["w_ukv"][l], src_k).astype(BF16),
        "w_uv": _gather_cols(p["w_ukv"][l], src_v).astype(BF16),
        "b_forget": bf,
        "cmp_posa": posa, "cmp_posb": posb, "cmp_wa": w1_half(0), "cmp_wb": w1_half(half), "cmp_w2": w2,
        "w_out_a": w_out[:n_attn].astype(BF16), "w_out_b": wb.astype(BF16),
        "ln1_g": p["ln1_g"][l][None, :], "ln1_b": p["ln1_b"][l][None, :],
        "ln2_g": p["ln2_g"][l][None, :], "ln2_b": p["ln2_b"][l][None, :],
    }


def kernel(x, w_in, b_forget, g_cq, w_uq, g_ckv, w_ukv, cmp_k_pos, cmp_k_w1, cmp_k_w2, cmp_v_pos, cmp_v_w1,
           cmp_v_w2, w_out, ln1_g, ln1_b, ln2_g, ln2_b, ffn_w1, ffn_w3, ffn_w2, router_w, moe_w1, moe_w3,
           moe_w2):
    b, s, d = x.shape
    assert d == D_MODEL and s % 512 == 0 and s // NSA_CMP_STRIDE == LANES, (b, s, d)
    p = dict(b_forget=b_forget, g_cq=g_cq, w_uq=w_uq, g_ckv=g_ckv, w_ukv=w_ukv, cmp_k_pos=cmp_k_pos,
             cmp_k_w1=cmp_k_w1, cmp_k_w2=cmp_k_w2, cmp_v_pos=cmp_v_pos, cmp_v_w1=cmp_v_w1, cmp_v_w2=cmp_v_w2,
             w_out=w_out, ln1_g=ln1_g, ln1_b=ln1_b, ln2_g=ln2_g, ln2_b=ln2_b)
    src, scale = _in_proj_columns()
    w_in_p = _gather_cols(w_in, src, scale).astype(BF16)
    tabs = _tables(s)
    n = b * s
    x2 = x.reshape(n, d)
    for l in range(DEPTH):
        wl = _layer_weights(p, l)
        za, fq, fk, fv, nq, nc, nkv = _in_proj(x2, w_in_p[l])
        q_all, k_all, v_all = _prep(za, fq, fk, fv, wl, tabs, b, s)
        o_attn = _flash(q_all, k_all, v_all)
        kcvc = _compress(nc.reshape(b, s // NSA_CMP_STRIDE, NSA_CMP_STRIDE * LANES), wl)
        nkv3 = nkv.reshape(b, s, 2 * LANES)
        kwin_pad = jnp.pad(nkv3[:, :, LANES:], ((0, 0), (NSA_WINDOW, 0), (0, 0)))
        o_nsa = _nsa(nq.reshape(b, s, NSA_HEADS * LANES), kcvc, nkv3[:, :, :LANES], kwin_pad, za)
        x2 = _out_proj(o_attn.reshape(n, -1), o_nsa.reshape(n, -1), x2, wl)
        j = l // 2
        if l % 2 == 0:
            d_ff = ffn_w1.shape[-1]
            nchunk = 2
            tf = d_ff // nchunk
            w1 = ffn_w1[j].reshape(d, nchunk, tf).transpose(1, 0, 2).astype(BF16)
            w3 = ffn_w3[j].reshape(d, nchunk, tf).transpose(1, 0, 2).astype(BF16)
            w2 = ffn_w2[j].reshape(nchunk, tf, d).astype(BF16)
            x2 = _ffn(x2, w1, w3, w2, wl["ln2_g"], wl["ln2_b"])
        else:
            rw = jnp.pad(router_w[j], ((0, 0), (0, LANES - N_EXPERTS)))
            x2 = _ffn(x2, moe_w1[j].astype(BF16), moe_w3[j].astype(BF16), moe_w2[j].astype(BF16),
                      wl["ln2_g"], wl["ln2_b"], router=rw)
    return x2.reshape(b, s, d)
```

```python
import functools

import numpy as np
import jax
import jax.numpy as jnp
from jax import lax
from jax.experimental import pallas as pl
from jax.experimental.pallas import tpu as pltpu

F32 = jnp.float32
BF16 = jnp.bfloat16

D_MODEL = 1024
HEAD_DIM = 64
LANES = 128
MLA_HEADS = 6
MLA_Q_RANK = 384
MLA_KV_RANK = 256
MLA_NOPE = 64
MLA_ROPE = 32
ROPE_THETA = 10000.0
FOX_HEADS = 6
NSA_HEADS = 4
NSA_CMP_BLOCK = 32
NSA_CMP_STRIDE = 16
NSA_CMP_HIDDEN = 128
NSA_SEL_BLOCK = 64
NSA_SEL_TOPN = 8
NSA_WINDOW = 256
NSA_FORCE_SCORE = 1.0e4
N_EXPERTS = 8
NORM_EPS = 1e-5
DEPTH = 2
DEEPNORM_ALPHA = (2 * DEPTH) ** 0.25
NEG = -1e30
VMEM_LIMIT = 56 * 1024 * 1024

KR_LANE = 64
FOXF_LANE = 96
NSAG_LANE = 102

ZA_W = 768
FQ_OFF = ZA_W
FK_OFF = FQ_OFF + 768
FV_OFF = FK_OFF + 768
NQ_OFF = FV_OFF + 384
NC_OFF = NQ_OFF + 512
NKV_OFF = NC_OFF + 128
IN_P = NKV_OFF + 256

ALIBI = tuple(2.0 ** (-8.0 * (i + 1) / NSA_HEADS) for i in range(NSA_HEADS))


def _dot(a, b, **kw):
    return jnp.dot(a, b, preferred_element_type=F32, **kw)


def _dot_nt(a, b):
    return lax.dot_general(a, b, (((1,), (1,)), ((), ())), preferred_element_type=F32)


def _iota(shape, dim):
    return lax.broadcasted_iota(jnp.int32, shape, dim)


def _cparams(sem):
    return pltpu.CompilerParams(dimension_semantics=sem, vmem_limit_bytes=VMEM_LIMIT)


def _in_proj_kernel(x_ref, w_ref, za_ref, fq_ref, fk_ref, fv_ref, nq_ref, nc_ref, nkv_ref):
    xb = x_ref[...].astype(BF16)

    def mm(a, b):
        return _dot(xb, w_ref[:, a:b])

    za_ref[...] = mm(0, ZA_W)
    fq_ref[...] = mm(FQ_OFF, FK_OFF).astype(BF16)
    fk_ref[...] = mm(FK_OFF, FV_OFF).astype(BF16)
    fv_ref[...] = mm(FV_OFF, NQ_OFF).astype(BF16)
    nq_ref[...] = mm(NQ_OFF, NC_OFF).astype(BF16)
    nc_ref[...] = mm(NC_OFF, NKV_OFF)
    nkv_ref[...] = mm(NKV_OFF, IN_P).astype(BF16)


def _in_proj(x2, w_p, tm=512):
    n = x2.shape[0]
    widths = [(ZA_W, F32), (768, BF16), (768, BF16), (384, BF16), (512, BF16), (128, F32), (256, BF16)]
    return pl.pallas_call(
        _in_proj_kernel,
        grid=(n // tm,),
        in_specs=[pl.BlockSpec((tm, D_MODEL), lambda i: (i, 0)),
                  pl.BlockSpec((D_MODEL, IN_P), lambda i: (0, 0))],
        out_specs=[pl.BlockSpec((tm, w), lambda i: (i, 0)) for w, _ in widths],
        out_shape=[jax.ShapeDtypeStruct((n, w), dt) for w, dt in widths],
        compiler_params=_cparams(("parallel",)),
        name="in_proj",
    )(x2, w_p)


def _prep_kernel(za_ref, fq_ref, fk_ref, fv_ref, gcq_ref, wuq_ref, gckv_ref, wuk_ref, wuv_ref,
                 bf_ref, cos_ref, sina_ref, sinb_ref, pq_ref, pk_ref, oneq_ref, onek_ref,
                 q_out, k_out, v_out, carry_ref, *, ts):
    @pl.when(pl.program_id(1) == 0)
    def _():
        carry_ref[...] = jnp.zeros_like(carry_ref)

    za = za_ref[...]
    cq = za[:, 0:MLA_Q_RANK]
    ckv = za[:, MLA_Q_RANK:MLA_Q_RANK + MLA_KV_RANK]
    small = za[:, MLA_Q_RANK + MLA_KV_RANK:ZA_W]

    cos = cos_ref[...]
    sina = sina_ref[...]
    sinb = sinb_ref[...]

    def rope(blk):
        return blk * cos + pltpu.roll(blk, LANES - 16, 1) * sina + pltpu.roll(blk, 16, 1) * sinb

    xn = cq * lax.rsqrt(jnp.mean(cq * cq, -1, keepdims=True) + NORM_EPS) * gcq_ref[...]
    q = _dot(xn.astype(BF16), wuq_ref[...])
    cn = ckv * lax.rsqrt(jnp.mean(ckv * ckv, -1, keepdims=True) + NORM_EPS) * gckv_ref[...]
    cnb = cn.astype(BF16)
    kn = _dot(cnb, wuk_ref[...])
    v = _dot(cnb, wuv_ref[...])
    kr = rope(small)
    mla_scale = (MLA_NOPE + MLA_ROPE) ** -0.5
    lane = _iota((1, LANES), 1)

    def pad_v(blk):
        return jnp.where(lane < HEAD_DIM, blk, jnp.where(lane == HEAD_DIM, 1.0, 0.0)).astype(BF16)

    for h in range(MLA_HEADS):
        sl = slice(LANES * h, LANES * (h + 1))
        q_out[0, :, sl] = (rope(q[:, sl]) * mla_scale).astype(BF16)
        k_out[0, :, sl] = (kn[:, sl] + kr).astype(BF16)
        v_out[0, :, sl] = pad_v(v[:, sl])

    fmask = (lane >= FOXF_LANE) & (lane < FOXF_LANE + FOX_HEADS)
    lf = jnp.where(fmask, jax.nn.log_sigmoid(small + bf_ref[...]), 0.0)
    tril = (_iota((ts, ts), 0) >= _iota((ts, ts), 1)).astype(F32)
    cs = _dot(tril, lf, precision=lax.Precision.HIGHEST) + carry_ref[...]
    carry_ref[...] = cs[ts - 1:ts, :]
    hi = cs.astype(BF16).astype(F32)
    r1 = cs - hi
    mid = r1.astype(BF16).astype(F32)
    lo = (r1 - mid).astype(BF16).astype(F32)
    c3 = (hi + pltpu.roll(mid, FOX_HEADS, 1) + pltpu.roll(lo, 2 * FOX_HEADS, 1)).astype(BF16)
    augq = _dot(c3, pq_ref[...]) + oneq_ref[...]
    augk = _dot(c3, pk_ref[...]) + onek_ref[...]
    off = MLA_HEADS * LANES
    q_out[0, :, off:off + FOX_HEADS * LANES] = (fq_ref[...].astype(F32) + augq).astype(BF16)
    k_out[0, :, off:off + FOX_HEADS * LANES] = (fk_ref[...].astype(F32) + augk).astype(BF16)
    for pr in range(FOX_HEADS // 2):
        blk = fv_ref[:, LANES * pr:LANES * (pr + 1)].astype(F32)
        v_out[0, :, off + 2 * pr * LANES:off + (2 * pr + 1) * LANES] = pad_v(blk)
        v_out[0, :, off + (2 * pr + 1) * LANES:off + (2 * pr + 2) * LANES] = pad_v(pltpu.roll(blk, HEAD_DIM, 1))


def _prep(za, fq, fk, fv, wl, tabs, b, s, ts=512):
    ns = s // ts
    row = lambda w: pl.BlockSpec((ts, w), lambda bi, si: (bi * ns + si, 0))
    full = lambda a: pl.BlockSpec(a.shape, lambda bi, si: (0,) * a.ndim)
    tab = pl.BlockSpec((ts, LANES), lambda bi, si: (si, 0))
    consts = [wl["g_cq"], wl["w_uq"], wl["g_ckv"], wl["w_uk"], wl["w_uv"], wl["b_forget"]]
    tail = [tabs["pq"], tabs["pk"], tabs["oneq"], tabs["onek"]]
    nh = MLA_HEADS + FOX_HEADS
    return pl.pallas_call(
        functools.partial(_prep_kernel, ts=ts),
        grid=(b, ns),
        in_specs=[row(ZA_W), row(768), row(768), row(384)] + [full(a) for a in consts]
                 + [tab, tab, tab] + [full(a) for a in tail],
        out_specs=[pl.BlockSpec((1, ts, nh * LANES), lambda bi, si: (bi, si, 0)),
                   pl.BlockSpec((1, ts, nh * LANES), lambda bi, si: (bi, si, 0)),
                   pl.BlockSpec((1, ts, nh * LANES), lambda bi, si: (bi, si, 0))],
        out_shape=[jax.ShapeDtypeStruct((b, s, nh * LANES), BF16),
                   jax.ShapeDtypeStruct((b, s, nh * LANES), BF16),
                   jax.ShapeDtypeStruct((b, s, nh * LANES), BF16)],
        scratch_shapes=[pltpu.VMEM((1, LANES), F32)],
        compiler_params=_cparams(("arbitrary", "arbitrary")),
        name="head_prep",
    )(za, fq, fk, fv, *consts, tabs["cos"], tabs["sina"], tabs["sinb"], *tail)


def _flash_kernel(q_ref, k_ref, v_ref, o_ref, m_ref, acc_ref, s_ref, *, tq, tk):
    qi = pl.program_id(2)
    m_ref[...] = jnp.full_like(m_ref, NEG)
    acc_ref[...] = jnp.zeros_like(acc_ref)
    nsub = tq // tk
    nfull = qi * nsub
    heads = [slice(LANES * h, LANES * (h + 1)) for h in range(2)]

    def scores(j, r0, sl):
        k0 = pl.multiple_of(j * tk, tk)
        return _dot_nt(q_ref[0, r0:tq, sl], k_ref[0, pl.ds(k0, tk), sl])

    def consume(s, h, j, r0, masked):
        k0 = pl.multiple_of(j * tk, tk)
        if masked:
            s = jnp.where(_iota((tq - r0, tk), 0) >= _iota((tq - r0, tk), 1), s, NEG)
        chunks = [s[:, LANES * c:LANES * (c + 1)] for c in range(tk // LANES)]
        m_prev = m_ref[h, r0:tq, :]
        m_new = jnp.maximum(m_prev, jnp.max(functools.reduce(jnp.maximum, chunks), -1, keepdims=True))
        p = jnp.concatenate([jnp.exp(c - m_new) for c in chunks], 1).astype(BF16)
        acc_ref[h, r0:tq, :] = (jnp.exp(m_prev - m_new) * acc_ref[h, r0:tq, :]
                                + _dot(p, v_ref[0, pl.ds(k0, tk), heads[h]]))
        m_ref[h, r0:tq, :] = m_new

    assert nsub == 2
    for h in range(2):
        s_ref[0, h] = scores(0, 0, heads[h])

    def body(i, carry):
        j = 2 * i
        for b in range(2):
            for h in range(2):
                s_ref[1 - b, h] = scores(j + b + 1, 0, heads[h])
            for h in range(2):
                consume(s_ref[b, h], h, j + b, 0, False)
        return carry

    lax.fori_loop(0, qi, body, 0)
    for h in range(2):
        consume(s_ref[0, h], h, nfull, 0, True)
    for d in range(1, nsub):
        for h in range(2):
            consume(scores(nfull + d, d * tk, heads[h]), h, nfull + d, d * tk, True)

    lane = _iota((1, LANES), 1)
    o0 = acc_ref[0]
    o1 = acc_ref[1]
    o0 = o0 / o0[:, HEAD_DIM:HEAD_DIM + 1]
    o1 = o1 / o1[:, HEAD_DIM:HEAD_DIM + 1]
    o_ref[0] = jnp.where(lane < HEAD_DIM, o0, pltpu.roll(o1, HEAD_DIM, 1)).astype(o_ref.dtype)


def _flash(q_all, k_all, v_all, tq=1024, tk=512):
    b, s, _ = q_all.shape
    npair = (MLA_HEADS + FOX_HEADS) // 2
    return pl.pallas_call(
        functools.partial(_flash_kernel, tq=tq, tk=tk),
        grid=(b, npair, s // tq),
        in_specs=[pl.BlockSpec((1, tq, 2 * LANES), lambda bi, p, qi: (bi, qi, p)),
                  pl.BlockSpec((1, s, 2 * LANES), lambda bi, p, qi: (bi, 0, p)),
                  pl.BlockSpec((1, s, 2 * LANES), lambda bi, p, qi: (bi, 0, p))],
        out_specs=pl.BlockSpec((1, tq, LANES), lambda bi, p, qi: (bi, qi, p)),
        out_shape=jax.ShapeDtypeStruct((b, s, npair * LANES), BF16),
        scratch_shapes=[pltpu.VMEM((2, tq, LANES), F32), pltpu.VMEM((2, tq, LANES), F32),
                        pltpu.VMEM((2, 2, tq, tk), F32)],
        compiler_params=_cparams(("parallel", "parallel", "arbitrary")),
        name="flash_attn",
    )(q_all, k_all, v_all)


def _cmp_kernel(tc_ref, posa_ref, posb_ref, wa_ref, wb_ref, w2_ref, out_ref):
    tc = tc_ref[0]
    a = _dot((tc + posa_ref[...]).astype(BF16), wa_ref[...])
    b = _dot((tc + posb_ref[...]).astype(BF16), wb_ref[...])
    pre = a + pltpu.roll(b, b.shape[0] - 1, 0)
    hid = jax.nn.silu(pre)
    out_ref[0] = _dot(hid.astype(BF16), w2_ref[...]).astype(out_ref.dtype)


def _compress(tc, wl):
    b, nc, w = tc.shape
    full = lambda a: pl.BlockSpec(a.shape, lambda bi: (0,) * a.ndim)
    consts = [wl["cmp_posa"], wl["cmp_posb"], wl["cmp_wa"], wl["cmp_wb"], wl["cmp_w2"]]
    return pl.pallas_call(
        _cmp_kernel,
        grid=(b,),
        in_specs=[pl.BlockSpec((1, nc, w), lambda bi: (bi, 0, 0))] + [full(a) for a in consts],
        out_specs=pl.BlockSpec((1, nc, LANES), lambda bi: (bi, 0, 0)),
        out_shape=jax.ShapeDtypeStruct((b, nc, LANES), BF16),
        compiler_params=_cparams(("parallel",)),
        name="nsa_compress",
    )(tc, *consts)


def _masked_softmax(s, mask):
    s = jnp.where(mask, s, NEG)
    m = jnp.max(s, -1, keepdims=True)
    p = jnp.where(mask, jnp.exp(s - m), 0.0)
    return p / jnp.maximum(jnp.sum(p, -1, keepdims=True), 1e-30)


def _nsa_kernel(nq_ref, kc_ref, ksl_ref, kwin_ref, g_ref, o_ref, m_ref, l_ref, acc_ref, *, tq, n_cmp):
    qi = pl.program_id(1)
    t0 = pl.multiple_of(qi * tq, tq)
    rpos = t0 + _iota((tq, 1), 0)
    lane = _iota((1, LANES), 1)
    qs = [nq_ref[0, :, LANES * h:LANES * (h + 1)] for h in range(NSA_HEADS)]

    dist_i = rpos - (NSA_CMP_STRIDE * lane + NSA_CMP_BLOCK - 1)
    valid_c = (dist_i >= 0) & (lane < n_cmp)
    dist_c = dist_i.astype(F32)
    kc = kc_ref[0]
    psum = jnp.zeros((tq, LANES), F32)
    o_cmp = []
    for h in range(NSA_HEADS):
        p = _masked_softmax(_dot_nt(qs[h], kc) - ALIBI[h] * dist_c, valid_c)
        psum = psum + p
        o_cmp.append(_dot(p.astype(BF16), kc))

    n_i = _iota((LANES, LANES), 0)
    j_i = _iota((LANES, LANES), 1)
    ov = ((NSA_CMP_STRIDE * n_i < NSA_SEL_BLOCK * (j_i + 1))
          & (NSA_CMP_STRIDE * n_i + NSA_CMP_BLOCK > NSA_SEL_BLOCK * j_i)
          & (n_i < n_cmp)).astype(F32)
    imp = _dot(psum, ov, precision=lax.Precision.HIGHEST)
    cur = jnp.right_shift(rpos, 6)
    forced = (lane == 0) | (lane == cur) | (lane == cur - 1)
    future = lane * NSA_SEL_BLOCK > rpos
    n_blk = kwin_ref.shape[1] // NSA_SEL_BLOCK - NSA_WINDOW // NSA_SEL_BLOCK
    work = jnp.where(forced, NSA_FORCE_SCORE, jnp.where(future, -1.0, imp))
    work = jnp.where(lane < n_blk, work, -jnp.inf)
    sel = jnp.zeros((tq, LANES), jnp.bool_)
    for _ in range(NSA_SEL_TOPN):
        mx = jnp.max(work, -1, keepdims=True)
        idx = jnp.min(jnp.where(work == mx, lane, LANES), -1, keepdims=True)
        pick = lane == idx
        sel = sel | pick
        work = jnp.where(pick, -jnp.inf, work)
    selb = jnp.where(sel, 1.0, 0.0).astype(BF16)

    m_ref[...] = jnp.full_like(m_ref, NEG)
    l_ref[...] = jnp.zeros_like(l_ref)
    acc_ref[...] = jnp.zeros_like(acc_ref)

    def sel_chunk(c, carry):
        k0 = pl.multiple_of(c * tq, tq)
        kv = ksl_ref[0, pl.ds(k0, tq), :]
        kpos = k0 + _iota((1, tq), 1)
        expand = jnp.where(_iota((LANES, tq), 0) == jnp.right_shift(k0 + _iota((LANES, tq), 1), 6),
                           1.0, 0.0).astype(BF16)
        keep = (_dot(selb, expand) > 0.5) & (kpos <= rpos)
        dist = (rpos - kpos).astype(F32)
        for h in range(NSA_HEADS):
            s = jnp.where(keep, _dot_nt(qs[h], kv) - ALIBI[h] * dist, NEG)
            chunks = [s[:, LANES * i:LANES * (i + 1)] for i in range(tq // LANES)]
            m_prev = m_ref[h]
            m_new = jnp.maximum(m_prev, jnp.max(functools.reduce(jnp.maximum, chunks), -1, keepdims=True))
            alpha = jnp.exp(m_prev - m_new)
            ps = [jnp.exp(ch - m_new) for ch in chunks]
            l_ref[h] = alpha * l_ref[h] + jnp.sum(functools.reduce(jnp.add, ps), -1, keepdims=True)
            acc_ref[h] = alpha * acc_ref[h] + _dot(jnp.concatenate(ps, 1).astype(BF16), kv)
            m_ref[h] = m_new
        return carry

    lax.fori_loop(0, qi + 1, sel_chunk, 0)

    wlen = tq + NSA_WINDOW
    kw = kwin_ref[0, pl.ds(t0, wlen), :]
    kp = t0 - NSA_WINDOW + _iota((1, wlen), 1)
    d_w = rpos - kp
    keep_w = (d_w >= 0) & (d_w < NSA_WINDOW) & (kp >= 0)
    d_wf = d_w.astype(F32)

    gates = jax.nn.sigmoid(g_ref[...])
    for h in range(NSA_HEADS):
        p_w = _masked_softmax(_dot_nt(qs[h], kw) - ALIBI[h] * d_wf, keep_w)
        o_win = _dot(p_w.astype(BF16), kw)
        o_sel = acc_ref[h] / jnp.maximum(l_ref[h], 1e-30)
        g0 = NSAG_LANE + 3 * h
        o = (gates[:, g0:g0 + 1] * o_cmp[h] + gates[:, g0 + 1:g0 + 2] * o_sel
             + gates[:, g0 + 2:g0 + 3] * o_win)
        o_ref[0, :, LANES * h:LANES * (h + 1)] = jnp.where(lane >= HEAD_DIM, o, 0.0).astype(o_ref.dtype)


def _nsa(nq, kcvc, ksl, kwin_pad, za, tq=256):
    b, s, _ = nq.shape
    nq_t = s // tq
    n_cmp = (s - NSA_CMP_BLOCK) // NSA_CMP_STRIDE + 1
    return pl.pallas_call(
        functools.partial(_nsa_kernel, tq=tq, n_cmp=n_cmp),
        grid=(b, nq_t),
        in_specs=[pl.BlockSpec((1, tq, NSA_HEADS * LANES), lambda bi, qi: (bi, qi, 0)),
                  pl.BlockSpec((1, kcvc.shape[1], LANES), lambda bi, qi: (bi, 0, 0)),
                  pl.BlockSpec((1, s, LANES), lambda bi, qi: (bi, 0, 0)),
                  pl.BlockSpec((1, s + NSA_WINDOW, LANES), lambda bi, qi: (bi, 0, 0)),
                  pl.BlockSpec((tq, LANES), lambda bi, qi: (bi * nq_t + qi, (ZA_W - LANES) // LANES))],
        out_specs=pl.BlockSpec((1, tq, NSA_HEADS * LANES), lambda bi, qi: (bi, qi, 0)),
        out_shape=jax.ShapeDtypeStruct((b, s, NSA_HEADS * LANES), BF16),
        scratch_shapes=[pltpu.VMEM((NSA_HEADS, tq, LANES), F32) for _ in range(3)],
        compiler_params=_cparams(("parallel", "arbitrary")),
        name="nsa_attn",
    )(nq, kcvc, ksl, kwin_pad, za)


def _layer_norm(y, g, b):
    mu = jnp.mean(y, -1, keepdims=True)
    yc = y - mu
    var = jnp.mean(yc * yc, -1, keepdims=True)
    return yc * lax.rsqrt(var + NORM_EPS) * g + b


ROUTE_ROWS = 512


def _out_proj_kernel(*refs, routed):
    if routed:
        (oa_ref, on_ref, x_ref, wa_ref, wb_ref, g_ref, b_ref, rw_ref,
         o_ref, gate_ref, pos_ref, post_ref, cnt_ref) = refs
    else:
        oa_ref, on_ref, x_ref, wa_ref, wb_ref, g_ref, b_ref, o_ref = refs
    mix = _dot(oa_ref[...], wa_ref[...]) + _dot(on_ref[...], wb_ref[...])
    x1 = _layer_norm(DEEPNORM_ALPHA * x_ref[...] + mix, g_ref[...], b_ref[...])
    o_ref[...] = x1
    if routed:
        tm = x1.shape[0]
        lane = _iota((1, LANES), 1)
        logits = jnp.where(lane < N_EXPERTS, _dot(x1, rw_ref[...], precision=lax.Precision.HIGHEST), NEG)
        ex = jnp.exp(logits - jnp.max(logits, -1, keepdims=True))
        probs = ex / jnp.sum(ex, -1, keepdims=True)
        p1 = jnp.max(probs, -1, keepdims=True)
        i1 = jnp.min(jnp.where(probs == p1, lane, LANES), -1, keepdims=True)
        rest = jnp.where(lane == i1, -1.0, probs)
        p2 = jnp.max(rest, -1, keepdims=True)
        i2 = jnp.min(jnp.where(rest == p2, lane, LANES), -1, keepdims=True)
        tot = p1 + p2
        gate_ref[...] = jnp.where(lane == i1, p1 / tot, jnp.where(lane == i2, p2 / tot, 0.0))
        chosen = (lane == i1) | (lane == i2)
        before = (_iota((tm, tm), 0) > _iota((tm, tm), 1)).astype(BF16)
        onehot = jnp.where(chosen, 1.0, 0.0)
        slot = _dot(before, onehot.astype(BF16))
        posm = jnp.where(chosen, slot, -1.0)
        pos_ref[...] = posm
        post_ref[...] = posm.T[0:N_EXPERTS, :]
        cnt_ref[...] = jnp.broadcast_to(jnp.sum(onehot, 0, keepdims=True), (8, LANES)).astype(jnp.int32)


def _out_proj(o_attn, o_nsa, x2, wl, router=None, tm=ROUTE_ROWS):
    n = x2.shape[0]
    routed = router is not None
    full = lambda a: pl.BlockSpec(a.shape, lambda i: (0,) * a.ndim)
    row = lambda w: pl.BlockSpec((tm, w), lambda i: (i, 0))
    consts = [wl["w_out_a"], wl["w_out_b"], wl["ln1_g"], wl["ln1_b"]] + ([router] if routed else [])
    out_specs = [row(D_MODEL)]
    out_shape = [jax.ShapeDtypeStruct((n, D_MODEL), F32)]
    if routed:
        out_specs += [row(LANES), row(LANES), pl.BlockSpec((N_EXPERTS, tm), lambda i: (i, 0)),
                      pl.BlockSpec((8, LANES), lambda i: (i, 0))]
        out_shape += [jax.ShapeDtypeStruct((n, LANES), F32), jax.ShapeDtypeStruct((n, LANES), F32),
                      jax.ShapeDtypeStruct((n // tm * N_EXPERTS, tm), F32),
                      jax.ShapeDtypeStruct((n // tm * 8, LANES), jnp.int32)]
    outs = pl.pallas_call(
        functools.partial(_out_proj_kernel, routed=routed),
        grid=(n // tm,),
        in_specs=[row(o_attn.shape[1]), row(o_nsa.shape[1]), row(D_MODEL)] + [full(a) for a in consts],
        out_specs=out_specs,
        out_shape=out_shape,
        compiler_params=_cparams(("parallel",)),
        name="out_proj_route_ln" if routed else "out_proj_ln",
    )(o_attn, o_nsa, x2, *consts)
    return outs if routed else outs[0]


def _ffn_kernel(x_ref, w1_ref, w3_ref, w2_ref, g_ref, b_ref, o_ref, acc_ref, xb_ref):
    c = pl.program_id(1)

    @pl.when(c == 0)
    def _():
        acc_ref[...] = jnp.zeros_like(acc_ref)
        xb_ref[...] = x_ref[...].astype(BF16)

    xb = xb_ref[...]
    a = jax.nn.silu(_dot(xb, w1_ref[0])) * _dot(xb, w3_ref[0])
    acc_ref[...] += _dot(a.astype(BF16), w2_ref[0])

    @pl.when(c == pl.num_programs(1) - 1)
    def _():
        o_ref[...] = _layer_norm(DEEPNORM_ALPHA * x_ref[...] + acc_ref[...], g_ref[...], b_ref[...])


def _ffn(x2, w1, w3, w2, ln_g, ln_b, tm=512):
    n = x2.shape[0]
    ne, _, tf = w1.shape
    full = lambda a: pl.BlockSpec(a.shape, lambda i, c: (0,) * a.ndim)
    return pl.pallas_call(
        _ffn_kernel,
        grid=(n // tm, ne),
        in_specs=[pl.BlockSpec((tm, D_MODEL), lambda i, c: (i, 0)),
                  pl.BlockSpec((1, D_MODEL, tf), lambda i, c: (c, 0, 0)),
                  pl.BlockSpec((1, D_MODEL, tf), lambda i, c: (c, 0, 0)),
                  pl.BlockSpec((1, tf, D_MODEL), lambda i, c: (c, 0, 0)),
                  full(ln_g), full(ln_b)],
        out_specs=pl.BlockSpec((tm, D_MODEL), lambda i, c: (i, 0)),
        out_shape=jax.ShapeDtypeStruct((n, D_MODEL), F32),
        scratch_shapes=[pltpu.VMEM((tm, D_MODEL), F32), pltpu.VMEM((tm, D_MODEL), BF16)],
        compiler_params=_cparams(("parallel", "arbitrary")),
        name="ffn_ln",
    )(x2, w1, w3, w2, ln_g, ln_b)


MOE_CAP = 192
MOE_CAP_PAD = -(-MOE_CAP // LANES) * LANES
MOE_CHUNKS = -(-ROUTE_ROWS // MOE_CAP)


def _moe_kernel(cnt_ref, x_ref, gate_ref, pos_ref, *rest, groups):
    post_refs = rest[:groups]
    w1_ref, w3_ref, w2_ref, g_ref, b_ref, o_ref, acc_ref, xb_ref = rest[groups:]
    i = pl.program_id(0)
    e = pl.program_id(1)
    lane = _iota((1, LANES), 1)

    @pl.when(e == 0)
    def _():
        acc_ref[...] = jnp.zeros_like(acc_ref)
        xb_ref[...] = x_ref[...].astype(BF16)

    for gi in range(groups):
        rows = slice(ROUTE_ROWS * gi, ROUTE_ROWS * (gi + 1))
        cnt = cnt_ref[(i * groups + gi) * N_EXPERTS + e]
        gate_e = jnp.sum(jnp.where(lane == e, gate_ref[rows, :], 0.0), -1, keepdims=True)
        slot_col = jnp.sum(jnp.where(lane == e, pos_ref[rows, :], 0.0), -1, keepdims=True)
        slot_row = post_refs[gi][0]
        for k in range(MOE_CHUNKS):
            @pl.when(cnt > k * MOE_CAP)
            def _():
                want = (_iota((MOE_CAP, 1), 0) + k * MOE_CAP).astype(F32)
                pick = jnp.where(slot_row == want, 1.0, 0.0).astype(BF16)
                xg = _dot(pick, xb_ref[rows, :]).astype(BF16)
                a = jax.nn.silu(_dot(xg, w1_ref[0])) * _dot(xg, w3_ref[0])
                y = _dot(a.astype(BF16), w2_ref[0])
                col = _iota((1, MOE_CAP_PAD), 1)
                put = jnp.where((slot_col == (col + k * MOE_CAP).astype(F32)) & (col < MOE_CAP),
                                1.0, 0.0).astype(BF16)
                y_hi = y.astype(BF16)
                y_lo = (y - y_hi.astype(F32)).astype(BF16)
                if MOE_CAP_PAD > MOE_CAP:
                    zpad = jnp.zeros((MOE_CAP_PAD - MOE_CAP, D_MODEL), BF16)
                    y_hi = jnp.concatenate([y_hi, zpad], 0)
                    y_lo = jnp.concatenate([y_lo, zpad], 0)
                acc_ref[rows, :] += gate_e * (_dot(put, y_hi) + _dot(put, y_lo))

    @pl.when(e == pl.num_programs(1) - 1)
    def _():
        o_ref[...] = _layer_norm(DEEPNORM_ALPHA * x_ref[...] + acc_ref[...], g_ref[...], b_ref[...])


def _moe(x2, gates, pos, post, cnt, w1, w3, w2, ln_g, ln_b, groups=2):
    n = x2.shape[0]
    ne, _, tf = w1.shape
    tm = groups * ROUTE_ROWS
    post3 = post.reshape(n // ROUTE_ROWS * N_EXPERTS, 1, ROUTE_ROWS)
    full = lambda a: pl.BlockSpec(a.shape, lambda i, e, c: (0,) * a.ndim)
    row = lambda w: pl.BlockSpec((tm, w), lambda i, e, c: (i, 0))
    post_spec = lambda gi: pl.BlockSpec((1, 1, ROUTE_ROWS),
                                        lambda i, e, c: ((i * groups + gi) * N_EXPERTS + e, 0, 0))
    grid_spec = pltpu.PrefetchScalarGridSpec(
        num_scalar_prefetch=1,
        grid=(n // tm, ne),
        in_specs=[row(D_MODEL), row(LANES), row(LANES)] + [post_spec(gi) for gi in range(groups)]
                 + [pl.BlockSpec((1, D_MODEL, tf), lambda i, e, c: (e, 0, 0)),
                    pl.BlockSpec((1, D_MODEL, tf), lambda i, e, c: (e, 0, 0)),
                    pl.BlockSpec((1, tf, D_MODEL), lambda i, e, c: (e, 0, 0)),
                    full(ln_g), full(ln_b)],
        out_specs=row(D_MODEL),
        scratch_shapes=[pltpu.VMEM((tm, D_MODEL), F32), pltpu.VMEM((tm, D_MODEL), BF16)],
    )
    return pl.pallas_call(
        functools.partial(_moe_kernel, groups=groups),
        grid_spec=grid_spec,
        out_shape=jax.ShapeDtypeStruct((n, D_MODEL), F32),
        compiler_params=_cparams(("parallel", "arbitrary")),
        name="moe_top2_ln",
    )(cnt, x2, gates, pos, *([post3] * groups), w1, w3, w2, ln_g, ln_b)


def _in_proj_columns():
    src = np.full((IN_P,), -1, np.int64)
    scale = np.ones((IN_P,), np.float32)
    o_cq, o_ckv, o_kr = 0, MLA_Q_RANK, MLA_Q_RANK + MLA_KV_RANK
    o_fox = o_kr + MLA_ROPE
    o_foxf = o_fox + 3 * FOX_HEADS * HEAD_DIM
    o_nq = o_foxf + FOX_HEADS
    o_nkv = o_nq + NSA_HEADS * HEAD_DIM
    o_ng = o_nkv + 6 * HEAD_DIM
    src[0:o_kr] = np.arange(o_kr)
    small = o_kr
    src[small + KR_LANE:small + KR_LANE + MLA_ROPE] = o_kr + np.arange(MLA_ROPE)
    src[small + FOXF_LANE:small + FOXF_LANE + FOX_HEADS] = o_foxf + np.arange(FOX_HEADS)
    src[small + NSAG_LANE:small + NSAG_LANE + 3 * NSA_HEADS] = o_ng + np.arange(3 * NSA_HEADS)
    d = np.arange(HEAD_DIM)
    for h in range(FOX_HEADS):
        src[FQ_OFF + LANES * h + d] = o_fox + HEAD_DIM * h + d
        scale[FQ_OFF + LANES * h + d] = HEAD_DIM ** -0.5
        src[FK_OFF + LANES * h + d] = o_fox + FOX_HEADS * HEAD_DIM + HEAD_DIM * h + d
    src[FV_OFF:FV_OFF + FOX_HEADS * HEAD_DIM] = o_fox + 2 * FOX_HEADS * HEAD_DIM + np.arange(FOX_HEADS * HEAD_DIM)
    for h in range(NSA_HEADS):
        src[NQ_OFF + LANES * h + d] = o_nq + HEAD_DIM * h + d
        scale[NQ_OFF + LANES * h + d] = HEAD_DIM ** -0.5
    src[NC_OFF:NC_OFF + 6 * HEAD_DIM] = o_nkv + np.arange(6 * HEAD_DIM)
    return src, scale


def _gather_cols(w, src, scale=None):
    scale = np.ones(src.shape, np.float32) if scale is None else scale
    parts, i, n = [], 0, len(src)
    while i < n:
        j = i + 1
        while j < n and scale[j] == scale[i] and (src[j] == src[j - 1] + 1 if src[i] >= 0 else src[j] < 0):
            j += 1
        if src[i] < 0:
            parts.append(jnp.zeros(w.shape[:-1] + (j - i,), w.dtype))
        else:
            seg = w[..., int(src[i]):int(src[i]) + (j - i)]
            parts.append(seg if scale[i] == 1.0 else seg * float(scale[i]))
        i = j
    return jnp.concatenate(parts, axis=-1)


def _tables(s):
    half = MLA_ROPE // 2
    freqs = ROPE_THETA ** (-jnp.arange(half, dtype=F32) / half)
    ang = jnp.arange(s).astype(F32)[:, None] * freqs[None, :]
    cos, sin = jnp.cos(ang), jnp.sin(ang)
    z = lambda w: jnp.zeros((s, w), F32)
    tabs = {
        "cos": jnp.concatenate([jnp.ones((s, MLA_NOPE), F32), cos, cos, z(LANES - MLA_NOPE - MLA_ROPE)], 1),
        "sina": jnp.concatenate([z(MLA_NOPE), -sin, z(half), z(LANES - MLA_NOPE - MLA_ROPE)], 1),
        "sinb": jnp.concatenate([z(MLA_NOPE), z(half), sin, z(LANES - MLA_NOPE - MLA_ROPE)], 1),
    }
    pq = np.zeros((LANES, FOX_HEADS * LANES), np.float32)
    pk = np.zeros((LANES, FOX_HEADS * LANES), np.float32)
    oneq = np.zeros((1, FOX_HEADS * LANES), np.float32)
    onek = np.zeros((1, FOX_HEADS * LANES), np.float32)
    for h in range(FOX_HEADS):
        for t in range(3):
            pq[FOXF_LANE + FOX_HEADS * t + h, LANES * h + HEAD_DIM + t] = 1.0
            pk[FOXF_LANE + FOX_HEADS * t + h, LANES * h + HEAD_DIM + 3 + t] = -1.0
            oneq[0, LANES * h + HEAD_DIM + 3 + t] = 1.0
            onek[0, LANES * h + HEAD_DIM + t] = 1.0
    tabs.update(pq=jnp.asarray(pq, BF16), pk=jnp.asarray(pk, BF16), oneq=jnp.asarray(oneq), onek=jnp.asarray(onek))
    return tabs


def _layer_weights(p, l):
    qd = MLA_NOPE + MLA_ROPE
    src_q = np.full((MLA_HEADS * LANES,), -1, np.int64)
    src_k = np.full((MLA_HEADS * LANES,), -1, np.int64)
    src_v = np.full((MLA_HEADS * LANES,), -1, np.int64)
    for h in range(MLA_HEADS):
        src_q[LANES * h + np.arange(qd)] = qd * h + np.arange(qd)
        src_k[LANES * h + np.arange(MLA_NOPE)] = 2 * HEAD_DIM * h + np.arange(MLA_NOPE)
        src_v[LANES * h + np.arange(HEAD_DIM)] = 2 * HEAD_DIM * h + MLA_NOPE + np.arange(HEAD_DIM)
    w_out = p["w_out"][l]
    n_attn = (MLA_HEADS + FOX_HEADS) * HEAD_DIM
    wb = w_out[n_attn:].reshape(NSA_HEADS, HEAD_DIM, D_MODEL)
    wb = jnp.concatenate([jnp.zeros_like(wb), wb], axis=1).reshape(NSA_HEADS * LANES, D_MODEL)
    bf = jnp.zeros((1, LANES), F32).at[0, FOXF_LANE:FOXF_LANE + FOX_HEADS].set(p["b_forget"][l])

    half = NSA_CMP_BLOCK // 2
    kpos, vpos = p["cmp_k_pos"][l], p["cmp_v_pos"][l]
    posa = jnp.concatenate([kpos[:half], vpos[:half]], -1).reshape(1, half * LANES)
    posb = jnp.concatenate([kpos[half:], vpos[half:]], -1).reshape(1, half * LANES)

    def w1_half(lo):
        k = p["cmp_k_w1"][l].reshape(NSA_CMP_BLOCK, HEAD_DIM, NSA_CMP_HIDDEN)[lo:lo + half]
        v = p["cmp_v_w1"][l].reshape(NSA_CMP_BLOCK, HEAD_DIM, NSA_CMP_HIDDEN)[lo:lo + half]
        zk = jnp.zeros_like(k)
        top = jnp.concatenate([k, zk], -1)
        bot = jnp.concatenate([zk, v], -1)
        return jnp.concatenate([top, bot], 1).reshape(half * LANES, 2 * NSA_CMP_HIDDEN).astype(BF16)

    zw2 = jnp.zeros((NSA_CMP_HIDDEN, HEAD_DIM), F32)
    w2 = jnp.concatenate([jnp.concatenate([p["cmp_k_w2"][l], zw2], 1),
                          jnp.concatenate([zw2, p["cmp_v_w2"][l]], 1)], 0).astype(BF16)
    return {
        "g_cq": p["g_cq"][l][None, :], "g_ckv": p["g_ckv"][l][None, :],
        "w_uq": _gather_cols(p["w_uq"][l], src_q).astype(BF16),
        "w_uk": _gather_cols(p["w_ukv"][l], src_k).astype(BF16),
        "w_uv": _gather_cols(p["w_ukv"][l], src_v).astype(BF16),
        "b_forget": bf,
        "cmp_posa": posa, "cmp_posb": posb, "cmp_wa": w1_half(0), "cmp_wb": w1_half(half), "cmp_w2": w2,
        "w_out_a": w_out[:n_attn].astype(BF16), "w_out_b": wb.astype(BF16),
        "ln1_g": p["ln1_g"][l][None, :], "ln1_b": p["ln1_b"][l][None, :],
        "ln2_g": p["ln2_g"][l][None, :], "ln2_b": p["ln2_b"][l][None, :],
    }


def kernel(x, w_in, b_forget, g_cq, w_uq, g_ckv, w_ukv, cmp_k_pos, cmp_k_w1, cmp_k_w2, cmp_v_pos, cmp_v_w1,
           cmp_v_w2, w_out, ln1_g, ln1_b, ln2_g, ln2_b, ffn_w1, ffn_w3, ffn_w2, router_w, moe_w1, moe_w3,
           moe_w2):
    b, s, d = x.shape
    assert d == D_MODEL and s % 512 == 0 and s // NSA_CMP_STRIDE == LANES, (b, s, d)
    p = dict(b_forget=b_forget, g_cq=g_cq, w_uq=w_uq, g_ckv=g_ckv, w_ukv=w_ukv, cmp_k_pos=cmp_k_pos,
             cmp_k_w1=cmp_k_w1, cmp_k_w2=cmp_k_w2, cmp_v_pos=cmp_v_pos, cmp_v_w1=cmp_v_w1, cmp_v_w2=cmp_v_w2,
             w_out=w_out, ln1_g=ln1_g, ln1_b=ln1_b, ln2_g=ln2_g, ln2_b=ln2_b)
    src, scale = _in_proj_columns()
    w_in_p = _gather_cols(w_in, src, scale).astype(BF16)
    tabs = _tables(s)
    n = b * s
    x2 = x.reshape(n, d)
    for l in range(DEPTH):
        wl = _layer_weights(p, l)
        za, fq, fk, fv, nq, nc, nkv = _in_proj(x2, w_in_p[l])
        q_all, k_all, v_all = _prep(za, fq, fk, fv, wl, tabs, b, s)
        o_attn = _flash(q_all, k_all, v_all)
        kcvc = _compress(nc.reshape(b, s // NSA_CMP_STRIDE, NSA_CMP_STRIDE * LANES), wl)
        nkv3 = nkv.reshape(b, s, 2 * LANES)
        kwin_pad = jnp.pad(nkv3[:, :, LANES:], ((0, 0), (NSA_WINDOW, 0), (0, 0)))
        o_nsa = _nsa(nq.reshape(b, s, NSA_HEADS * LANES), kcvc, nkv3[:, :, :LANES], kwin_pad, za)
        j = l // 2
        if l % 2 == 0:
            x2 = _out_proj(o_attn.reshape(n, -1), o_nsa.reshape(n, -1), x2, wl)
            d_ff = ffn_w1.shape[-1]
            nchunk = 2
            tf = d_ff // nchunk
            w1 = ffn_w1[j].reshape(d, nchunk, tf).transpose(1, 0, 2).astype(BF16)
            w3 = ffn_w3[j].reshape(d, nchunk, tf).transpose(1, 0, 2).astype(BF16)
            w2 = ffn_w2[j].reshape(nchunk, tf, d).astype(BF16)
            x2 = _ffn(x2, w1, w3, w2, wl["ln2_g"], wl["ln2_b"])
        else:
            rw = jnp.pad(router_w[j], ((0, 0), (0, LANES - N_EXPERTS)))
            x2, gates, pos, post, cnt = _out_proj(o_attn.reshape(n, -1), o_nsa.reshape(n, -1), x2, wl, router=rw)
            cnt = cnt[::8, :N_EXPERTS].reshape(-1)
            x2 = _moe(x2, gates, pos, post, cnt, moe_w1[j].astype(BF16), moe_w3[j].astype(BF16),
                      moe_w2[j].astype(BF16), wl["ln2_g"], wl["ln2_b"])
    return x2.reshape(b, s, d)
```

```python
import functools

import numpy as np
import jax
import jax.numpy as jnp
from jax import lax
from jax.experimental import pallas as pl
from jax.experimental.pallas import tpu as pltpu

F32 = jnp.float32
BF16 = jnp.bfloat16

D_MODEL = 1024
HEAD_DIM = 64
LANES = 128
MLA_HEADS = 6
MLA_Q_RANK = 384
MLA_KV_RANK = 256
MLA_NOPE = 64
MLA_ROPE = 32
ROPE_THETA = 10000.0
FOX_HEADS = 6
NSA_HEADS = 4
NSA_CMP_BLOCK = 32
NSA_CMP_STRIDE = 16
NSA_CMP_HIDDEN = 128
NSA_SEL_BLOCK = 64
NSA_SEL_TOPN = 8
NSA_WINDOW = 256
NSA_FORCE_SCORE = 1.0e4
N_EXPERTS = 8
NORM_EPS = 1e-5
DEPTH = 2
DEEPNORM_ALPHA = (2 * DEPTH) ** 0.25
NEG = -1e30
VMEM_LIMIT = 56 * 1024 * 1024

KR_LANE = 64
FOXF_LANE = 96
NSAG_LANE = 102

ZA_W = 768
FQ_OFF = ZA_W
FK_OFF = FQ_OFF + 768
FV_OFF = FK_OFF + 768
NQ_OFF = FV_OFF + 384
NC_OFF = NQ_OFF + 512
NKV_OFF = NC_OFF + 128
IN_P = NKV_OFF + 512

NSA_AUX_LANE = HEAD_DIM
NSA_SEL_LANE = HEAD_DIM + 2
NSA_MASK_BIG = 2.0 ** 126

ALIBI = tuple(2.0 ** (-8.0 * (i + 1) / NSA_HEADS) for i in range(NSA_HEADS))


def _dot(a, b, **kw):
    return jnp.dot(a, b, preferred_element_type=F32, **kw)


def _dot_nt(a, b):
    return lax.dot_general(a, b, (((1,), (1,)), ((), ())), preferred_element_type=F32)


def _iota(shape, dim):
    return lax.broadcasted_iota(jnp.int32, shape, dim)


def _split3(x):
    hi = x.astype(BF16)
    r = x - hi.astype(F32)
    mid = r.astype(BF16)
    return hi, mid, (r - mid.astype(F32)).astype(BF16)


def _dot_exact_rhs(x, w01):
    hi, mid, lo = _split3(x)
    return _dot(jnp.concatenate([hi, mid, lo], 1), jnp.concatenate([w01, w01, w01], 0))


def _dot_split(x, w_stack):
    hi = x.astype(BF16)
    lo = (x - hi.astype(F32)).astype(BF16)
    r = _dot(jnp.concatenate([hi, lo], 1), w_stack)
    half = r.shape[1] // 2
    return r[:, :half] + r[:, half:]


def _cparams(sem):
    return pltpu.CompilerParams(dimension_semantics=sem, vmem_limit_bytes=VMEM_LIMIT)


def _in_proj_kernel(x_ref, w_ref, qaux_ref, kaux_ref, za_ref, fq_ref, fk_ref, fv_ref, nq_ref, nc_ref,
                    nsl_ref, nwin_ref):
    xb = x_ref[...].astype(BF16)

    def mm(a, b):
        return _dot(xb, w_ref[:, a:b])

    za_ref[...] = mm(0, ZA_W)
    fq_ref[...] = mm(FQ_OFF, FK_OFF).astype(BF16)
    fk_ref[...] = mm(FK_OFF, FV_OFF).astype(BF16)
    fv_ref[...] = mm(FV_OFF, NQ_OFF).astype(BF16)
    nq_ref[...] = (mm(NQ_OFF, NC_OFF) + qaux_ref[...]).astype(BF16)
    nc_ref[...] = mm(NC_OFF, NKV_OFF)
    ones_lane = jnp.where(_iota((1, LANES), 1) == HEAD_DIM, 1.0, 0.0)
    kaux = kaux_ref[...]
    for out_ref, off in ((nsl_ref, NKV_OFF), (nwin_ref, NKV_OFF + 2 * LANES)):
        out_ref[:, 0:LANES] = (mm(off, off + LANES) + kaux).astype(BF16)
        out_ref[:, LANES:2 * LANES] = (mm(off + LANES, off + 2 * LANES) + ones_lane).astype(BF16)


def _in_proj(x2, w_p, tabs, s, tm=512):
    n = x2.shape[0]
    widths = [(ZA_W, F32), (768, BF16), (768, BF16), (384, BF16), (512, BF16), (128, F32), (256, BF16),
              (256, BF16)]
    return pl.pallas_call(
        _in_proj_kernel,
        grid=(n // tm,),
        in_specs=[pl.BlockSpec((tm, D_MODEL), lambda i: (i, 0)),
                  pl.BlockSpec((D_MODEL, IN_P), lambda i: (0, 0)),
                  pl.BlockSpec((1, NSA_HEADS * LANES), lambda i: (0, 0)),
                  pl.BlockSpec((tm, LANES), lambda i: (i % (s // tm), 0))],
        out_specs=[pl.BlockSpec((tm, w), lambda i: (i, 0)) for w, _ in widths],
        out_shape=[jax.ShapeDtypeStruct((n, w), dt) for w, dt in widths],
        compiler_params=_cparams(("parallel",)),
        name="in_proj",
    )(x2, w_p, tabs["nsa_qaux"], tabs["nsa_kaux"])


def _prep_kernel(za_ref, fq_ref, fk_ref, fv_ref, gcq_ref, wuq_ref, gckv_ref, wuk_ref, wuv_ref,
                 bf_ref, cos_ref, sina_ref, sinb_ref, pq_ref, pk_ref, oneq_ref, onek_ref,
                 q_out, k_out, v_out, carry_ref, *, ts):
    @pl.when(pl.program_id(1) == 0)
    def _():
        carry_ref[...] = jnp.zeros_like(carry_ref)

    za = za_ref[...]
    cq = za[:, 0:MLA_Q_RANK]
    ckv = za[:, MLA_Q_RANK:MLA_Q_RANK + MLA_KV_RANK]
    small = za[:, MLA_Q_RANK + MLA_KV_RANK:ZA_W]

    cos = cos_ref[...]
    sina = sina_ref[...]
    sinb = sinb_ref[...]

    def rope(blk):
        return blk * cos + pltpu.roll(blk, LANES - 16, 1) * sina + pltpu.roll(blk, 16, 1) * sinb

    xn = cq * lax.rsqrt(jnp.mean(cq * cq, -1, keepdims=True) + NORM_EPS) * gcq_ref[...]
    q = _dot(xn.astype(BF16), wuq_ref[...])
    cn = ckv * lax.rsqrt(jnp.mean(ckv * ckv, -1, keepdims=True) + NORM_EPS) * gckv_ref[...]
    cnb = cn.astype(BF16)
    kn = _dot(cnb, wuk_ref[...])
    v = _dot(cnb, wuv_ref[...])
    kr = rope(small)
    mla_scale = (MLA_NOPE + MLA_ROPE) ** -0.5
    lane = _iota((1, LANES), 1)

    def pad_v(blk):
        return jnp.where(lane < HEAD_DIM, blk, jnp.where(lane == HEAD_DIM, 1.0, 0.0)).astype(BF16)

    for h in range(MLA_HEADS):
        sl = slice(LANES * h, LANES * (h + 1))
        q_out[0, :, sl] = (rope(q[:, sl]) * mla_scale).astype(BF16)
        k_out[0, :, sl] = (kn[:, sl] + kr).astype(BF16)
        v_out[0, :, sl] = pad_v(v[:, sl])

    fmask = (lane >= FOXF_LANE) & (lane < FOXF_LANE + FOX_HEADS)
    lf = jnp.where(fmask, jax.nn.log_sigmoid(small + bf_ref[...]), 0.0)
    tril = jnp.where(_iota((ts, ts), 0) >= _iota((ts, ts), 1), 1.0, 0.0).astype(BF16)
    lf_hi, lf_mid, lf_lo = _split3(lf)
    part = _dot(tril, jnp.concatenate([lf_hi, lf_mid], 1))
    cs = part[:, :LANES] + part[:, LANES:] + _dot(tril, lf_lo) + carry_ref[...]
    carry_ref[...] = cs[ts - 1:ts, :]
    hi, mid, lo = (t.astype(F32) for t in _split3(cs))
    c3 = (hi + pltpu.roll(mid, FOX_HEADS, 1) + pltpu.roll(lo, 2 * FOX_HEADS, 1)).astype(BF16)
    augq = _dot(c3, pq_ref[...]) + oneq_ref[...]
    augk = _dot(c3, pk_ref[...]) + onek_ref[...]
    off = MLA_HEADS * LANES
    q_out[0, :, off:off + FOX_HEADS * LANES] = (fq_ref[...].astype(F32) + augq).astype(BF16)
    k_out[0, :, off:off + FOX_HEADS * LANES] = (fk_ref[...].astype(F32) + augk).astype(BF16)
    for pr in range(FOX_HEADS // 2):
        blk = fv_ref[:, LANES * pr:LANES * (pr + 1)].astype(F32)
        v_out[0, :, off + 2 * pr * LANES:off + (2 * pr + 1) * LANES] = pad_v(blk)
        v_out[0, :, off + (2 * pr + 1) * LANES:off + (2 * pr + 2) * LANES] = pad_v(pltpu.roll(blk, HEAD_DIM, 1))


def _prep(za, fq, fk, fv, wl, tabs, b, s, ts=512):
    ns = s // ts
    row = lambda w: pl.BlockSpec((ts, w), lambda bi, si: (bi * ns + si, 0))
    full = lambda a: pl.BlockSpec(a.shape, lambda bi, si: (0,) * a.ndim)
    tab = pl.BlockSpec((ts, LANES), lambda bi, si: (si, 0))
    consts = [wl["g_cq"], wl["w_uq"], wl["g_ckv"], wl["w_uk"], wl["w_uv"], wl["b_forget"]]
    tail = [tabs["pq"], tabs["pk"], tabs["oneq"], tabs["onek"]]
    nh = MLA_HEADS + FOX_HEADS
    return pl.pallas_call(
        functools.partial(_prep_kernel, ts=ts),
        grid=(b, ns),
        in_specs=[row(ZA_W), row(768), row(768), row(384)] + [full(a) for a in consts]
                 + [tab, tab, tab] + [full(a) for a in tail],
        out_specs=[pl.BlockSpec((1, ts, nh * LANES), lambda bi, si: (bi, si, 0)),
                   pl.BlockSpec((1, ts, nh * LANES), lambda bi, si: (bi, si, 0)),
                   pl.BlockSpec((1, ts, nh * LANES), lambda bi, si: (bi, si, 0))],
        out_shape=[jax.ShapeDtypeStruct((b, s, nh * LANES), BF16),
                   jax.ShapeDtypeStruct((b, s, nh * LANES), BF16),
                   jax.ShapeDtypeStruct((b, s, nh * LANES), BF16)],
        scratch_shapes=[pltpu.VMEM((1, LANES), F32)],
        compiler_params=_cparams(("arbitrary", "arbitrary")),
        name="head_prep",
    )(za, fq, fk, fv, *consts, tabs["cos"], tabs["sina"], tabs["sinb"], *tail)


def _flash_kernel(q_ref, k_ref, v_ref, o_ref, m_ref, acc_ref, s_ref, *, tq, tk):
    qi = pl.program_id(2)
    m_ref[...] = jnp.full_like(m_ref, NEG)
    acc_ref[...] = jnp.zeros_like(acc_ref)
    nsub = tq // tk
    nfull = qi * nsub
    heads = [slice(LANES * h, LANES * (h + 1)) for h in range(2)]

    def scores(j, r0, sl):
        k0 = pl.multiple_of(j * tk, tk)
        return _dot_nt(q_ref[0, r0:tq, sl], k_ref[0, pl.ds(k0, tk), sl])

    def consume(s, h, j, r0, masked):
        k0 = pl.multiple_of(j * tk, tk)
        if masked:
            s = jnp.where(_iota((tq - r0, tk), 0) >= _iota((tq - r0, tk), 1), s, NEG)
        chunks = [s[:, LANES * c:LANES * (c + 1)] for c in range(tk // LANES)]
        m_prev = m_ref[h, r0:tq, :]
        m_new = jnp.maximum(m_prev, jnp.max(functools.reduce(jnp.maximum, chunks), -1, keepdims=True))
        p = jnp.concatenate([jnp.exp(c - m_new) for c in chunks], 1).astype(BF16)
        acc_ref[h, r0:tq, :] = (jnp.exp(m_prev - m_new) * acc_ref[h, r0:tq, :]
                                + _dot(p, v_ref[0, pl.ds(k0, tk), heads[h]]))
        m_ref[h, r0:tq, :] = m_new

    assert nsub == 2
    for h in range(2):
        s_ref[0, h] = scores(0, 0, heads[h])

    def body(i, carry):
        j = 2 * i
        for b in range(2):
            for h in range(2):
                s_ref[1 - b, h] = scores(j + b + 1, 0, heads[h])
            for h in range(2):
                consume(s_ref[b, h], h, j + b, 0, False)
        return carry

    lax.fori_loop(0, qi, body, 0)
    for h in range(2):
        consume(s_ref[0, h], h, nfull, 0, True)
    for d in range(1, nsub):
        for h in range(2):
            consume(scores(nfull + d, d * tk, heads[h]), h, nfull + d, d * tk, True)

    lane = _iota((1, LANES), 1)
    o0 = acc_ref[0]
    o1 = acc_ref[1]
    o0 = o0 / o0[:, HEAD_DIM:HEAD_DIM + 1]
    o1 = o1 / o1[:, HEAD_DIM:HEAD_DIM + 1]
    o_ref[0] = jnp.where(lane < HEAD_DIM, o0, pltpu.roll(o1, HEAD_DIM, 1)).astype(o_ref.dtype)


def _flash(q_all, k_all, v_all, tq=1024, tk=512):
    b, s, _ = q_all.shape
    npair = (MLA_HEADS + FOX_HEADS) // 2
    return pl.pallas_call(
        functools.partial(_flash_kernel, tq=tq, tk=tk),
        grid=(b, npair, s // tq),
        in_specs=[pl.BlockSpec((1, tq, 2 * LANES), lambda bi, p, qi: (bi, qi, p)),
                  pl.BlockSpec((1, s, 2 * LANES), lambda bi, p, qi: (bi, 0, p)),
                  pl.BlockSpec((1, s, 2 * LANES), lambda bi, p, qi: (bi, 0, p))],
        out_specs=pl.BlockSpec((1, tq, LANES), lambda bi, p, qi: (bi, qi, p)),
        out_shape=jax.ShapeDtypeStruct((b, s, npair * LANES), BF16),
        scratch_shapes=[pltpu.VMEM((2, tq, LANES), F32), pltpu.VMEM((2, tq, LANES), F32),
                        pltpu.VMEM((2, 2, tq, tk), F32)],
        compiler_params=_cparams(("parallel", "parallel", "arbitrary")),
        name="flash_attn",
    )(q_all, k_all, v_all)


def _cmp_kernel(tc_ref, posa_ref, posb_ref, wa_ref, wb_ref, w2_ref, out_ref):
    tc = tc_ref[0]
    a = _dot((tc + posa_ref[...]).astype(BF16), wa_ref[...])
    b = _dot((tc + posb_ref[...]).astype(BF16), wb_ref[...])
    pre = a + pltpu.roll(b, b.shape[0] - 1, 0)
    hid = jax.nn.silu(pre)
    out_ref[0] = _dot(hid.astype(BF16), w2_ref[...]).astype(out_ref.dtype)


def _compress(tc, wl):
    b, nc, w = tc.shape
    full = lambda a: pl.BlockSpec(a.shape, lambda bi: (0,) * a.ndim)
    consts = [wl["cmp_posa"], wl["cmp_posb"], wl["cmp_wa"], wl["cmp_wb"], wl["cmp_w2"]]
    return pl.pallas_call(
        _cmp_kernel,
        grid=(b,),
        in_specs=[pl.BlockSpec((1, nc, w), lambda bi: (bi, 0, 0))] + [full(a) for a in consts],
        out_specs=pl.BlockSpec((1, nc, LANES), lambda bi: (bi, 0, 0)),
        out_shape=jax.ShapeDtypeStruct((b, nc, LANES), BF16),
        compiler_params=_cparams(("parallel",)),
        name="nsa_compress",
    )(tc, *consts)


def _masked_softmax(s, mask):
    s = jnp.where(mask, s, NEG)
    m = jnp.max(s, -1, keepdims=True)
    p = jnp.where(mask, jnp.exp(s - m), 0.0)
    return p / jnp.maximum(jnp.sum(p, -1, keepdims=True), 1e-30)


def _nsa_kernel(nq_ref, kc_ref, ksl_ref, kwin_ref, g_ref, o_ref, m_ref, acc_ref, qst_ref, s_ref, *, tq, n_cmp):
    qi = pl.program_id(1)
    t0 = pl.multiple_of(qi * tq, tq)
    rpos = t0 + _iota((tq, 1), 0)
    lane = _iota((1, LANES), 1)
    qs = [nq_ref[0, :, LANES * h:LANES * (h + 1)] for h in range(NSA_HEADS)]

    dist_i = rpos - (NSA_CMP_STRIDE * lane + NSA_CMP_BLOCK - 1)
    valid_c = (dist_i >= 0) & (lane < n_cmp)
    dist_c = dist_i.astype(F32)
    kc = kc_ref[0]
    psum = jnp.zeros((tq, LANES), F32)
    o_cmp = []
    for h in range(NSA_HEADS):
        q_head = jnp.where(lane < HEAD_DIM, qs[h], jnp.zeros_like(qs[h]))
        p = _masked_softmax(_dot_nt(q_head, kc) - ALIBI[h] * dist_c, valid_c)
        psum = psum + p
        o_cmp.append(_dot(p.astype(BF16), kc))

    n_i = _iota((LANES, LANES), 0)
    j_i = _iota((LANES, LANES), 1)
    ov = ((NSA_CMP_STRIDE * n_i < NSA_SEL_BLOCK * (j_i + 1))
          & (NSA_CMP_STRIDE * n_i + NSA_CMP_BLOCK > NSA_SEL_BLOCK * j_i)
          & (n_i < n_cmp))
    imp = _dot_exact_rhs(psum, jnp.where(ov, 1.0, 0.0).astype(BF16))
    cur = jnp.right_shift(rpos, 6)
    forced = (lane == 0) | (lane == cur) | (lane == cur - 1)
    future = lane * NSA_SEL_BLOCK > rpos
    n_blk = kwin_ref.shape[1] // NSA_SEL_BLOCK - NSA_WINDOW // NSA_SEL_BLOCK
    work = jnp.where(forced, NSA_FORCE_SCORE, jnp.where(future, -1.0, imp))
    work = jnp.where(lane < n_blk, work, -jnp.inf)
    lane_f = lane.astype(F32)
    sel = jnp.zeros((tq, LANES), jnp.bool_)
    for _ in range(NSA_SEL_TOPN):
        mx = jnp.max(work, -1, keepdims=True)
        idx = jnp.min(jnp.where(work == mx, lane_f, float(LANES)), -1, keepdims=True)
        pick = lane_f == idx
        sel = sel | pick
        work = jnp.where(pick, -jnp.inf, work)
    sel_lanes = pltpu.roll(jnp.where(sel, 1.0, 0.0), NSA_SEL_LANE, 1)
    in_sel = (lane >= NSA_SEL_LANE) & (lane < NSA_SEL_LANE + n_blk)
    sel_bias = jnp.where(in_sel, (sel_lanes - 1.0) * NSA_MASK_BIG, 0.0)
    for h in range(NSA_HEADS):
        qst_ref[h * tq:(h + 1) * tq, :] = (qs[h].astype(F32) + sel_bias).astype(BF16)
    m_ref[...] = jnp.full_like(m_ref, NEG)
    acc_ref[...] = jnp.zeros_like(acc_ref)

    def online(s, v):
        chunks = [s[:, LANES * i:LANES * (i + 1)] for i in range(s.shape[1] // LANES)]
        m_prev = m_ref[...]
        m_new = jnp.maximum(m_prev, jnp.max(functools.reduce(jnp.maximum, chunks), -1, keepdims=True))
        p = jnp.concatenate([jnp.exp(ch - m_new) for ch in chunks], 1).astype(BF16)
        acc_ref[...] = jnp.exp(m_prev - m_new) * acc_ref[...] + _dot(p, v)
        m_ref[...] = m_new

    def scores(c):
        return _dot_nt(qst_ref[...], ksl_ref[0, pl.ds(pl.multiple_of(c * tq, tq), tq), 0:LANES])

    def values(c):
        return ksl_ref[0, pl.ds(pl.multiple_of(c * tq, tq), tq), LANES:2 * LANES]

    s_ref[0] = scores(0)

    def chunk_pair(i, carry):
        j = 2 * i
        s_ref[1] = scores(j + 1)
        online(s_ref[0], values(j))
        s_ref[0] = scores(j + 2)
        online(s_ref[1], values(j + 1))
        return carry

    lax.fori_loop(0, jnp.right_shift(qi, 1), chunk_pair, 0)

    @pl.when((qi & 1) == 1)
    def _():
        online(s_ref[0], values(qi - 1))
        s_ref[0] = scores(qi)

    nrow = NSA_HEADS * tq
    causal = (_iota((nrow, tq), 0) & (tq - 1)) >= _iota((nrow, tq), 1)
    online(jnp.where(causal, s_ref[0], NEG), values(qi))

    wlen = tq + NSA_WINDOW
    k_w = kwin_ref[0, pl.ds(t0, wlen), 0:LANES]
    v_w = kwin_ref[0, pl.ds(t0, wlen), LANES:2 * LANES]
    d_w = _iota((tq, wlen), 0) + NSA_WINDOW - _iota((tq, wlen), 1)
    keep_w = (d_w >= 0) & (d_w < NSA_WINDOW) & (t0 - NSA_WINDOW + _iota((1, wlen), 1) >= 0)

    gates = jax.nn.sigmoid(g_ref[...])
    for h in range(NSA_HEADS):
        s_w = jnp.where(keep_w, _dot_nt(qs[h], k_w), NEG)
        p_w = jnp.exp(s_w - jnp.max(s_w, -1, keepdims=True))
        o_win = _dot(p_w.astype(BF16), v_w)
        o_win = o_win / o_win[:, HEAD_DIM:HEAD_DIM + 1]
        o_sel = acc_ref[h * tq:(h + 1) * tq, :]
        o_sel = o_sel / o_sel[:, HEAD_DIM:HEAD_DIM + 1]
        g0 = NSAG_LANE + 3 * h
        o = (gates[:, g0:g0 + 1] * pltpu.roll(o_cmp[h], HEAD_DIM, 1) + gates[:, g0 + 1:g0 + 2] * o_sel
             + gates[:, g0 + 2:g0 + 3] * o_win)
        o_ref[0, :, LANES * h:LANES * (h + 1)] = jnp.where(lane < HEAD_DIM, o, 0.0).astype(o_ref.dtype)


def _nsa(nq, kcvc, nsl, nwin_pad, za, tq=256):
    b, s, _ = nq.shape
    nq_t = s // tq
    n_cmp = (s - NSA_CMP_BLOCK) // NSA_CMP_STRIDE + 1
    return pl.pallas_call(
        functools.partial(_nsa_kernel, tq=tq, n_cmp=n_cmp),
        grid=(b, nq_t),
        in_specs=[pl.BlockSpec((1, tq, NSA_HEADS * LANES), lambda bi, qi: (bi, qi, 0)),
                  pl.BlockSpec((1, kcvc.shape[1], LANES), lambda bi, qi: (bi, 0, 0)),
                  pl.BlockSpec((1, s, 2 * LANES), lambda bi, qi: (bi, 0, 0)),
                  pl.BlockSpec((1, s + NSA_WINDOW, 2 * LANES), lambda bi, qi: (bi, 0, 0)),
                  pl.BlockSpec((tq, LANES), lambda bi, qi: (bi * nq_t + qi, (ZA_W - LANES) // LANES))],
        out_specs=pl.BlockSpec((1, tq, NSA_HEADS * LANES), lambda bi, qi: (bi, qi, 0)),
        out_shape=jax.ShapeDtypeStruct((b, s, NSA_HEADS * LANES), BF16),
        scratch_shapes=[pltpu.VMEM((NSA_HEADS * tq, LANES), F32), pltpu.VMEM((NSA_HEADS * tq, LANES), F32),
                        pltpu.VMEM((NSA_HEADS * tq, LANES), BF16), pltpu.VMEM((2, NSA_HEADS * tq, tq), F32)],
        compiler_params=_cparams(("parallel", "arbitrary")),
        name="nsa_attn",
    )(nq, kcvc, nsl, nwin_pad, za)


def _layer_norm(y, g, b):
    mu = jnp.mean(y, -1, keepdims=True)
    yc = y - mu
    var = jnp.mean(yc * yc, -1, keepdims=True)
    return yc * lax.rsqrt(var + NORM_EPS) * g + b


ROUTE_ROWS = 512


def _out_proj_kernel(*refs, routed):
    if routed:
        (oa_ref, on_ref, x_ref, wa_ref, wb_ref, g_ref, b_ref, rw_ref,
         o_ref, gate_ref, pos_ref, post_ref, cnt_ref) = refs
    else:
        oa_ref, on_ref, x_ref, wa_ref, wb_ref, g_ref, b_ref, o_ref = refs
    mix = _dot(oa_ref[...], wa_ref[...]) + _dot(on_ref[...], wb_ref[...])
    x1 = _layer_norm(DEEPNORM_ALPHA * x_ref[...] + mix, g_ref[...], b_ref[...])
    o_ref[...] = x1
    if routed:
        tm = x1.shape[0]
        lane = _iota((1, LANES), 1)
        lane_f = lane.astype(F32)
        logits = jnp.where(lane < N_EXPERTS, _dot_split(x1, rw_ref[...]), NEG)
        ex = jnp.exp(logits - jnp.max(logits, -1, keepdims=True))
        probs = ex / jnp.sum(ex, -1, keepdims=True)
        p1 = jnp.max(probs, -1, keepdims=True)
        i1 = jnp.min(jnp.where(probs == p1, lane_f, float(LANES)), -1, keepdims=True)
        rest = jnp.where(lane_f == i1, -1.0, probs)
        p2 = jnp.max(rest, -1, keepdims=True)
        i2 = jnp.min(jnp.where(rest == p2, lane_f, float(LANES)), -1, keepdims=True)
        tot = p1 + p2
        gate_ref[...] = jnp.where(lane_f == i1, p1 / tot, jnp.where(lane_f == i2, p2 / tot, 0.0))
        chosen = (lane_f == i1) | (lane_f == i2)
        before = (_iota((tm, tm), 0) > _iota((tm, tm), 1)).astype(BF16)
        onehot = jnp.where(chosen, 1.0, 0.0)
        slot = _dot(before, onehot.astype(BF16))
        posm = jnp.where(chosen, slot, -1.0)
        pos_ref[...] = posm
        post_ref[...] = posm.T[0:N_EXPERTS, :]
        cnt_ref[...] = jnp.broadcast_to(jnp.sum(onehot, 0, keepdims=True), (8, LANES)).astype(jnp.int32)


def _out_proj(o_attn, o_nsa, x2, wl, router=None, tm=ROUTE_ROWS):
    n = x2.shape[0]
    routed = router is not None
    full = lambda a: pl.BlockSpec(a.shape, lambda i: (0,) * a.ndim)
    row = lambda w: pl.BlockSpec((tm, w), lambda i: (i, 0))
    consts = [wl["w_out_a"], wl["w_out_b"], wl["ln1_g"], wl["ln1_b"]] + ([router] if routed else [])
    out_specs = [row(D_MODEL)]
    out_shape = [jax.ShapeDtypeStruct((n, D_MODEL), F32)]
    if routed:
        out_specs += [row(LANES), row(LANES), pl.BlockSpec((N_EXPERTS, tm), lambda i: (i, 0)),
                      pl.BlockSpec((8, LANES), lambda i: (i, 0))]
        out_shape += [jax.ShapeDtypeStruct((n, LANES), F32), jax.ShapeDtypeStruct((n, LANES), F32),
                      jax.ShapeDtypeStruct((n // tm * N_EXPERTS, tm), F32),
                      jax.ShapeDtypeStruct((n // tm * 8, LANES), jnp.int32)]
    outs = pl.pallas_call(
        functools.partial(_out_proj_kernel, routed=routed),
        grid=(n // tm,),
        in_specs=[row(o_attn.shape[1]), row(o_nsa.shape[1]), row(D_MODEL)] + [full(a) for a in consts],
        out_specs=out_specs,
        out_shape=out_shape,
        compiler_params=_cparams(("parallel",)),
        name="out_proj_route_ln" if routed else "out_proj_ln",
    )(o_attn, o_nsa, x2, *consts)
    return outs if routed else outs[0]


def _ffn_kernel(x_ref, w1_ref, w3_ref, w2_ref, g_ref, b_ref, o_ref, acc_ref, xb_ref):
    c = pl.program_id(1)

    @pl.when(c == 0)
    def _():
        acc_ref[...] = jnp.zeros_like(acc_ref)
        xb_ref[...] = x_ref[...].astype(BF16)

    xb = xb_ref[...]
    a = jax.nn.silu(_dot(xb, w1_ref[0])) * _dot(xb, w3_ref[0])
    acc_ref[...] += _dot(a.astype(BF16), w2_ref[0])

    @pl.when(c == pl.num_programs(1) - 1)
    def _():
        o_ref[...] = _layer_norm(DEEPNORM_ALPHA * x_ref[...] + acc_ref[...], g_ref[...], b_ref[...])


def _ffn(x2, w1, w3, w2, ln_g, ln_b, tm=512):
    n = x2.shape[0]
    ne, _, tf = w1.shape
    full = lambda a: pl.BlockSpec(a.shape, lambda i, c: (0,) * a.ndim)
    return pl.pallas_call(
        _ffn_kernel,
        grid=(n // tm, ne),
        in_specs=[pl.BlockSpec((tm, D_MODEL), lambda i, c: (i, 0)),
                  pl.BlockSpec((1, D_MODEL, tf), lambda i, c: (c, 0, 0)),
                  pl.BlockSpec((1, D_MODEL, tf), lambda i, c: (c, 0, 0)),
                  pl.BlockSpec((1, tf, D_MODEL), lambda i, c: (c, 0, 0)),
                  full(ln_g), full(ln_b)],
        out_specs=pl.BlockSpec((tm, D_MODEL), lambda i, c: (i, 0)),
        out_shape=jax.ShapeDtypeStruct((n, D_MODEL), F32),
        scratch_shapes=[pltpu.VMEM((tm, D_MODEL), F32), pltpu.VMEM((tm, D_MODEL), BF16)],
        compiler_params=_cparams(("parallel", "arbitrary")),
        name="ffn_ln",
    )(x2, w1, w3, w2, ln_g, ln_b)


MOE_CAP = 160
MOE_CAP_PAD = -(-MOE_CAP // LANES) * LANES
MOE_CHUNKS = -(-ROUTE_ROWS // MOE_CAP)


def _moe_kernel(cnt_ref, x_ref, gate_ref, pos_ref, *rest, groups):
    post_refs = rest[:groups]
    w1_ref, w3_ref, w2_ref, g_ref, b_ref, o_ref, acc_ref, xb_ref = rest[groups:]
    i = pl.program_id(0)
    e = pl.program_id(1)
    lane = _iota((1, LANES), 1)

    @pl.when(e == 0)
    def _():
        acc_ref[...] = jnp.zeros_like(acc_ref)
        xb_ref[...] = x_ref[...].astype(BF16)

    for gi in range(groups):
        rows = slice(ROUTE_ROWS * gi, ROUTE_ROWS * (gi + 1))
        cnt = cnt_ref[(i * groups + gi) * N_EXPERTS + e]
        gate_e = jnp.sum(jnp.where(lane == e, gate_ref[rows, :], 0.0), -1, keepdims=True)
        slot_col = jnp.sum(jnp.where(lane == e, pos_ref[rows, :], 0.0), -1, keepdims=True)
        slot_row = post_refs[gi][0]
        for k in range(MOE_CHUNKS):
            @pl.when(cnt > k * MOE_CAP)
            def _():
                want = (_iota((MOE_CAP, 1), 0) + k * MOE_CAP).astype(F32)
                pick = jnp.where(slot_row == want, 1.0, 0.0).astype(BF16)
                xg = _dot(pick, xb_ref[rows, :]).astype(BF16)
                a = jax.nn.silu(_dot(xg, w1_ref[0])) * _dot(xg, w3_ref[0])
                y = _dot(a.astype(BF16), w2_ref[0])
                col = _iota((1, MOE_CAP_PAD), 1)
                put = jnp.where((slot_col == (col + k * MOE_CAP).astype(F32)) & (col < MOE_CAP),
                                1.0, 0.0).astype(BF16)
                y_hi = y.astype(BF16)
                y_lo = (y - y_hi.astype(F32)).astype(BF16)
                if MOE_CAP_PAD > MOE_CAP:
                    zpad = jnp.zeros((MOE_CAP_PAD - MOE_CAP, D_MODEL), BF16)
                    y_hi = jnp.concatenate([y_hi, zpad], 0)
                    y_lo = jnp.concatenate([y_lo, zpad], 0)
                acc_ref[rows, :] += gate_e * (_dot(put, y_hi) + _dot(put, y_lo))

    @pl.when(e == pl.num_programs(1) - 1)
    def _():
        o_ref[...] = _layer_norm(DEEPNORM_ALPHA * x_ref[...] + acc_ref[...], g_ref[...], b_ref[...])


def _moe(x2, gates, pos, post, cnt, w1, w3, w2, ln_g, ln_b, groups=2):
    n = x2.shape[0]
    ne, _, tf = w1.shape
    tm = groups * ROUTE_ROWS
    post3 = post.reshape(n // ROUTE_ROWS * N_EXPERTS, 1, ROUTE_ROWS)
    full = lambda a: pl.BlockSpec(a.shape, lambda i, e, c: (0,) * a.ndim)
    row = lambda w: pl.BlockSpec((tm, w), lambda i, e, c: (i, 0))
    post_spec = lambda gi: pl.BlockSpec((1, 1, ROUTE_ROWS),
                                        lambda i, e, c: ((i * groups + gi) * N_EXPERTS + e, 0, 0))
    grid_spec = pltpu.PrefetchScalarGridSpec(
        num_scalar_prefetch=1,
        grid=(n // tm, ne),
        in_specs=[row(D_MODEL), row(LANES), row(LANES)] + [post_spec(gi) for gi in range(groups)]
                 + [pl.BlockSpec((1, D_MODEL, tf), lambda i, e, c: (e, 0, 0)),
                    pl.BlockSpec((1, D_MODEL, tf), lambda i, e, c: (e, 0, 0)),
                    pl.BlockSpec((1, tf, D_MODEL), lambda i, e, c: (e, 0, 0)),
                    full(ln_g), full(ln_b)],
        out_specs=row(D_MODEL),
        scratch_shapes=[pltpu.VMEM((tm, D_MODEL), F32), pltpu.VMEM((tm, D_MODEL), BF16)],
    )
    return pl.pallas_call(
        functools.partial(_moe_kernel, groups=groups),
        grid_spec=grid_spec,
        out_shape=jax.ShapeDtypeStruct((n, D_MODEL), F32),
        compiler_params=_cparams(("parallel", "arbitrary")),
        name="moe_top2_ln",
    )(cnt, x2, gates, pos, *([post3] * groups), w1, w3, w2, ln_g, ln_b)


def _in_proj_columns():
    src = np.full((IN_P,), -1, np.int64)
    scale = np.ones((IN_P,), np.float32)
    o_cq, o_ckv, o_kr = 0, MLA_Q_RANK, MLA_Q_RANK + MLA_KV_RANK
    o_fox = o_kr + MLA_ROPE
    o_foxf = o_fox + 3 * FOX_HEADS * HEAD_DIM
    o_nq = o_foxf + FOX_HEADS
    o_nkv = o_nq + NSA_HEADS * HEAD_DIM
    o_ng = o_nkv + 6 * HEAD_DIM
    src[0:o_kr] = np.arange(o_kr)
    small = o_kr
    src[small + KR_LANE:small + KR_LANE + MLA_ROPE] = o_kr + np.arange(MLA_ROPE)
    src[small + FOXF_LANE:small + FOXF_LANE + FOX_HEADS] = o_foxf + np.arange(FOX_HEADS)
    src[small + NSAG_LANE:small + NSAG_LANE + 3 * NSA_HEADS] = o_ng + np.arange(3 * NSA_HEADS)
    d = np.arange(HEAD_DIM)
    for h in range(FOX_HEADS):
        src[FQ_OFF + LANES * h + d] = o_fox + HEAD_DIM * h + d
        scale[FQ_OFF + LANES * h + d] = HEAD_DIM ** -0.5
        src[FK_OFF + LANES * h + d] = o_fox + FOX_HEADS * HEAD_DIM + HEAD_DIM * h + d
    src[FV_OFF:FV_OFF + FOX_HEADS * HEAD_DIM] = o_fox + 2 * FOX_HEADS * HEAD_DIM + np.arange(FOX_HEADS * HEAD_DIM)
    for h in range(NSA_HEADS):
        src[NQ_OFF + LANES * h + d] = o_nq + HEAD_DIM * h + d
        scale[NQ_OFF + LANES * h + d] = HEAD_DIM ** -0.5
    src[NC_OFF:NC_OFF + 2 * HEAD_DIM] = o_nkv + np.arange(2 * HEAD_DIM)
    for blk in range(4):
        src[NKV_OFF + LANES * blk + d] = o_nkv + HEAD_DIM * (2 + blk) + d
    return src, scale


def _gather_cols(w, src, scale=None):
    scale = np.ones(src.shape, np.float32) if scale is None else scale
    parts, i, n = [], 0, len(src)
    while i < n:
        j = i + 1
        while j < n and scale[j] == scale[i] and (src[j] == src[j - 1] + 1 if src[i] >= 0 else src[j] < 0):
            j += 1
        if src[i] < 0:
            parts.append(jnp.zeros(w.shape[:-1] + (j - i,), w.dtype))
        else:
            seg = w[..., int(src[i]):int(src[i]) + (j - i)]
            parts.append(seg if scale[i] == 1.0 else seg * float(scale[i]))
        i = j
    return jnp.concatenate(parts, axis=-1)


def _tables(s):
    half = MLA_ROPE // 2
    freqs = ROPE_THETA ** (-jnp.arange(half, dtype=F32) / half)
    ang = jnp.arange(s).astype(F32)[:, None] * freqs[None, :]
    cos, sin = jnp.cos(ang), jnp.sin(ang)
    z = lambda w: jnp.zeros((s, w), F32)
    tabs = {
        "cos": jnp.concatenate([jnp.ones((s, MLA_NOPE), F32), cos, cos, z(LANES - MLA_NOPE - MLA_ROPE)], 1),
        "sina": jnp.concatenate([z(MLA_NOPE), -sin, z(half), z(LANES - MLA_NOPE - MLA_ROPE)], 1),
        "sinb": jnp.concatenate([z(MLA_NOPE), z(half), sin, z(LANES - MLA_NOPE - MLA_ROPE)], 1),
    }
    pq = np.zeros((LANES, FOX_HEADS * LANES), np.float32)
    pk = np.zeros((LANES, FOX_HEADS * LANES), np.float32)
    oneq = np.zeros((1, FOX_HEADS * LANES), np.float32)
    onek = np.zeros((1, FOX_HEADS * LANES), np.float32)
    for h in range(FOX_HEADS):
        for t in range(3):
            pq[FOXF_LANE + FOX_HEADS * t + h, LANES * h + HEAD_DIM + t] = 1.0
            pk[FOXF_LANE + FOX_HEADS * t + h, LANES * h + HEAD_DIM + 3 + t] = -1.0
            oneq[0, LANES * h + HEAD_DIM + 3 + t] = 1.0
            onek[0, LANES * h + HEAD_DIM + t] = 1.0
    tabs.update(pq=jnp.asarray(pq, BF16), pk=jnp.asarray(pk, BF16), oneq=jnp.asarray(oneq), onek=jnp.asarray(onek))
    qaux = np.zeros((1, NSA_HEADS * LANES), np.float32)
    for h in range(NSA_HEADS):
        qaux[0, LANES * h + NSA_AUX_LANE] = ALIBI[h] * NSA_SEL_BLOCK
        qaux[0, LANES * h + NSA_AUX_LANE + 1] = ALIBI[h]
    kaux = np.zeros((s, LANES), np.float32)
    pos = np.arange(s)
    kaux[:, NSA_AUX_LANE] = pos // NSA_SEL_BLOCK
    kaux[:, NSA_AUX_LANE + 1] = pos % NSA_SEL_BLOCK
    kaux[pos, NSA_SEL_LANE + pos // NSA_SEL_BLOCK] = 1.0
    tabs.update(nsa_qaux=jnp.asarray(qaux), nsa_kaux=jnp.asarray(kaux))
    return tabs


def _layer_weights(p, l):
    qd = MLA_NOPE + MLA_ROPE
    src_q = np.full((MLA_HEADS * LANES,), -1, np.int64)
    src_k = np.full((MLA_HEADS * LANES,), -1, np.int64)
    src_v = np.full((MLA_HEADS * LANES,), -1, np.int64)
    for h in range(MLA_HEADS):
        src_q[LANES * h + np.arange(qd)] = qd * h + np.arange(qd)
        src_k[LANES * h + np.arange(MLA_NOPE)] = 2 * HEAD_DIM * h + np.arange(MLA_NOPE)
        src_v[LANES * h + np.arange(HEAD_DIM)] = 2 * HEAD_DIM * h + MLA_NOPE + np.arange(HEAD_DIM)
    w_out = p["w_out"][l]
    n_attn = (MLA_HEADS + FOX_HEADS) * HEAD_DIM
    wb = w_out[n_attn:].reshape(NSA_HEADS, HEAD_DIM, D_MODEL)
    wb = jnp.concatenate([wb, jnp.zeros_like(wb)], axis=1).reshape(NSA_HEADS * LANES, D_MODEL)
    bf = jnp.zeros((1, LANES), F32).at[0, FOXF_LANE:FOXF_LANE + FOX_HEADS].set(p["b_forget"][l])

    half = NSA_CMP_BLOCK // 2
    kpos, vpos = p["cmp_k_pos"][l], p["cmp_v_pos"][l]
    posa = jnp.concatenate([kpos[:half], vpos[:half]], -1).reshape(1, half * LANES)
    posb = jnp.concatenate([kpos[half:], vpos[half:]], -1).reshape(1, half * LANES)

    def w1_half(lo):
        k = p["cmp_k_w1"][l].reshape(NSA_CMP_BLOCK, HEAD_DIM, NSA_CMP_HIDDEN)[lo:lo + half]
        v = p["cmp_v_w1"][l].reshape(NSA_CMP_BLOCK, HEAD_DIM, NSA_CMP_HIDDEN)[lo:lo + half]
        zk = jnp.zeros_like(k)
        top = jnp.concatenate([k, zk], -1)
        bot = jnp.concatenate([zk, v], -1)
        return jnp.concatenate([top, bot], 1).reshape(half * LANES, 2 * NSA_CMP_HIDDEN).astype(BF16)

    zw2 = jnp.zeros((NSA_CMP_HIDDEN, HEAD_DIM), F32)
    w2 = jnp.concatenate([jnp.concatenate([p["cmp_k_w2"][l], zw2], 1),
                          jnp.concatenate([zw2, p["cmp_v_w2"][l]], 1)], 0).astype(BF16)
    return {
        "g_cq": p["g_cq"][l][None, :], "g_ckv": p["g_ckv"][l][None, :],
        "w_uq": _gather_cols(p["w_uq"][l], src_q).astype(BF16),
        "w_uk": _gather_cols(p["w_ukv"][l], src_k).astype(BF16),
        "w_uv": _gather_cols(p["w_ukv"][l], src_v).astype(BF16),
        "b_forget": bf,
        "cmp_posa": posa, "cmp_posb": posb, "cmp_wa": w1_half(0), "cmp_wb": w1_half(half), "cmp_w2": w2,
        "w_out_a": w_out[:n_attn].astype(BF16), "w_out_b": wb.astype(BF16),
        "ln1_g": p["ln1_g"][l][None, :], "ln1_b": p["ln1_b"][l][None, :],
        "ln2_g": p["ln2_g"][l][None, :], "ln2_b": p["ln2_b"][l][None, :],
    }


def kernel(x, w_in, b_forget, g_cq, w_uq, g_ckv, w_ukv, cmp_k_pos, cmp_k_w1, cmp_k_w2, cmp_v_pos, cmp_v_w1,
           cmp_v_w2, w_out, ln1_g, ln1_b, ln2_g, ln2_b, ffn_w1, ffn_w3, ffn_w2, router_w, moe_w1, moe_w3,
           moe_w2):
    b, s, d = x.shape
    assert d == D_MODEL and s % 512 == 0 and s // NSA_CMP_STRIDE == LANES, (b, s, d)
    p = dict(b_forget=b_forget, g_cq=g_cq, w_uq=w_uq, g_ckv=g_ckv, w_ukv=w_ukv, cmp_k_pos=cmp_k_pos,
             cmp_k_w1=cmp_k_w1, cmp_k_w2=cmp_k_w2, cmp_v_pos=cmp_v_pos, cmp_v_w1=cmp_v_w1, cmp_v_w2=cmp_v_w2,
             w_out=w_out, ln1_g=ln1_g, ln1_b=ln1_b, ln2_g=ln2_g, ln2_b=ln2_b)
    src, scale = _in_proj_columns()
    w_in_p = _gather_cols(w_in, src, scale).astype(BF16)
    tabs = _tables(s)
    n = b * s
    x2 = x.reshape(n, d)
    for l in range(DEPTH):
        wl = _layer_weights(p, l)
        za, fq, fk, fv, nq, nc, nsl, nwin = _in_proj(x2, w_in_p[l], tabs, s)
        q_all, k_all, v_all = _prep(za, fq, fk, fv, wl, tabs, b, s)
        o_attn = _flash(q_all, k_all, v_all)
        kcvc = _compress(nc.reshape(b, s // NSA_CMP_STRIDE, NSA_CMP_STRIDE * LANES), wl)
        nwin_pad = jnp.pad(nwin.reshape(b, s, 2 * LANES), ((0, 0), (NSA_WINDOW, 0), (0, 0)))
        o_nsa = _nsa(nq.reshape(b, s, NSA_HEADS * LANES), kcvc, nsl.reshape(b, s, 2 * LANES), nwin_pad, za)
        j = l // 2
        if l % 2 == 0:
            x2 = _out_proj(o_attn.reshape(n, -1), o_nsa.reshape(n, -1), x2, wl)
            d_ff = ffn_w1.shape[-1]
            nchunk = 2
            tf = d_ff // nchunk
            w1 = ffn_w1[j].reshape(d, nchunk, tf).transpose(1, 0, 2).astype(BF16)
            w3 = ffn_w3[j].reshape(d, nchunk, tf).transpose(1, 0, 2).astype(BF16)
            w2 = ffn_w2[j].reshape(nchunk, tf, d).astype(BF16)
            x2 = _ffn(x2, w1, w3, w2, wl["ln2_g"], wl["ln2_b"])
        else:
            rw = jnp.pad(router_w[j], ((0, 0), (0, LANES - N_EXPERTS)))
            rw_hi = rw.astype(BF16)
            rw_lo = (rw - rw_hi.astype(F32)).astype(BF16)
            rw = jnp.concatenate([jnp.concatenate([rw_hi, rw_lo], 1),
                                  jnp.concatenate([rw_hi, jnp.zeros_like(rw_lo)], 1)], 0)
            x2, gates, pos, post, cnt = _out_proj(o_attn.reshape(n, -1), o_nsa.reshape(n, -1), x2, wl, router=rw)
            cnt = cnt[::8, :N_EXPERTS].reshape(-1)
            x2 = _moe(x2, gates, pos, post, cnt, moe_w1[j].astype(BF16), moe_w3[j].astype(BF16),
                      moe_w2[j].astype(BF16), wl["ln2_g"], wl["ln2_b"])
    return x2.reshape(b, s, d)
```

```python
import functools

import numpy as np
import jax
import jax.numpy as jnp
from jax import lax
from jax.experimental import pallas as pl
from jax.experimental.pallas import tpu as pltpu

F32 = jnp.float32
BF16 = jnp.bfloat16

D_MODEL = 1024
HEAD_DIM = 64
LANES = 128
MLA_HEADS = 6
MLA_Q_RANK = 384
MLA_KV_RANK = 256
MLA_NOPE = 64
MLA_ROPE = 32
ROPE_THETA = 10000.0
FOX_HEADS = 6
NSA_HEADS = 4
NSA_CMP_BLOCK = 32
NSA_CMP_STRIDE = 16
NSA_CMP_HIDDEN = 128
NSA_SEL_BLOCK = 64
NSA_SEL_TOPN = 8
NSA_WINDOW = 256
NSA_FORCE_SCORE = 1.0e4
N_EXPERTS = 8
NORM_EPS = 1e-5
DEPTH = 2
DEEPNORM_ALPHA = (2 * DEPTH) ** 0.25
NEG = -1e30
VMEM_LIMIT = 56 * 1024 * 1024

KR_LANE = 64
FOXF_LANE = 96
NSAG_LANE = 102

ZA_W = 768
FQ_OFF = ZA_W
FK_OFF = FQ_OFF + 384
FV_OFF = FK_OFF + 384
NQ_OFF = FV_OFF + 384
NC_OFF = NQ_OFF + 256
NKV_OFF = NC_OFF + 128
IN_P = NKV_OFF + 256

NSA_AUX_LANE = HEAD_DIM
NSA_SEL_LANE = HEAD_DIM + 2
NSA_MASK_BIG = 2.0 ** 126

ALIBI = tuple(2.0 ** (-8.0 * (i + 1) / NSA_HEADS) for i in range(NSA_HEADS))


def _dot(a, b, **kw):
    return jnp.dot(a, b, preferred_element_type=F32, **kw)


def _dot_nt(a, b):
    return lax.dot_general(a, b, (((1,), (1,)), ((), ())), preferred_element_type=F32)


def _iota(shape, dim):
    return lax.broadcasted_iota(jnp.int32, shape, dim)


def _split3(x):
    hi = x.astype(BF16)
    r = x - hi.astype(F32)
    mid = r.astype(BF16)
    return hi, mid, (r - mid.astype(F32)).astype(BF16)


def _dot_exact_rhs(x, w01):
    hi, mid, lo = _split3(x)
    return _dot(jnp.concatenate([hi, mid, lo], 1), jnp.concatenate([w01, w01, w01], 0))


def _dot_split(x, w_stack):
    hi = x.astype(BF16)
    lo = (x - hi.astype(F32)).astype(BF16)
    r = _dot(jnp.concatenate([hi, lo], 1), w_stack)
    half = r.shape[1] // 2
    return r[:, :half] + r[:, half:]


def _cparams(sem):
    return pltpu.CompilerParams(dimension_semantics=sem, vmem_limit_bytes=VMEM_LIMIT)


def _in_proj_kernel(x_ref, w_ref, qaux_ref, kaux_ref, za_ref, fq_ref, fk_ref, fv_ref, nq_ref, nc_ref,
                    nsl_ref, nwin_ref):
    xb = x_ref[...].astype(BF16)

    def mm(a, b):
        return _dot(xb, w_ref[:, a:b])

    lane = _iota((1, LANES), 1)

    def halves(pair):
        return (jnp.where(lane < HEAD_DIM, pair, 0.0),
                jnp.where(lane < HEAD_DIM, pltpu.roll(pair, HEAD_DIM, 1), 0.0))

    def expand(out_ref, z, add=None):
        for pr in range(z.shape[1] // LANES):
            for i, blk in enumerate(halves(z[:, LANES * pr:LANES * (pr + 1)])):
                sl = slice(LANES * (2 * pr + i), LANES * (2 * pr + i + 1))
                out_ref[:, sl] = (blk if add is None else blk + add[:, sl]).astype(BF16)

    za_ref[...] = mm(0, ZA_W)
    expand(fq_ref, mm(FQ_OFF, FK_OFF))
    expand(fk_ref, mm(FK_OFF, FV_OFF))
    fv_ref[...] = mm(FV_OFF, NQ_OFF).astype(BF16)
    expand(nq_ref, mm(NQ_OFF, NC_OFF), qaux_ref[...])
    nc_ref[...] = mm(NC_OFF, NKV_OFF)
    ones_lane = jnp.where(lane == HEAD_DIM, 1.0, 0.0)
    kaux = kaux_ref[...]
    nkv = mm(NKV_OFF, IN_P)
    for i, out_ref in enumerate((nsl_ref, nwin_ref)):
        k_blk, v_blk = halves(nkv[:, LANES * i:LANES * (i + 1)])
        out_ref[:, 0:LANES] = (k_blk + kaux).astype(BF16)
        out_ref[:, LANES:2 * LANES] = (v_blk + ones_lane).astype(BF16)


def _in_proj(x2, w_p, tabs, s, tm=512):
    n = x2.shape[0]
    widths = [(ZA_W, F32), (768, BF16), (768, BF16), (384, BF16), (512, BF16), (128, F32), (256, BF16),
              (256, BF16)]
    return pl.pallas_call(
        _in_proj_kernel,
        grid=(n // tm,),
        in_specs=[pl.BlockSpec((tm, D_MODEL), lambda i: (i, 0)),
                  pl.BlockSpec((D_MODEL, IN_P), lambda i: (0, 0)),
                  pl.BlockSpec((1, NSA_HEADS * LANES), lambda i: (0, 0)),
                  pl.BlockSpec((tm, LANES), lambda i: (i % (s // tm), 0))],
        out_specs=[pl.BlockSpec((tm, w), lambda i: (i, 0)) for w, _ in widths],
        out_shape=[jax.ShapeDtypeStruct((n, w), dt) for w, dt in widths],
        compiler_params=_cparams(("parallel",)),
        name="in_proj",
    )(x2, w_p, tabs["nsa_qaux"], tabs["nsa_kaux"])


def _prep_kernel(za_ref, fq_ref, fk_ref, fv_ref, gcq_ref, wuq_ref, gckv_ref, wuk_ref, wuv_ref,
                 bf_ref, cos_ref, sina_ref, sinb_ref, pq_ref, pk_ref, oneq_ref, onek_ref,
                 q_out, k_out, v_out, carry_ref, *, ts):
    @pl.when(pl.program_id(1) == 0)
    def _():
        carry_ref[...] = jnp.zeros_like(carry_ref)

    za = za_ref[...]
    cq = za[:, 0:MLA_Q_RANK]
    ckv = za[:, MLA_Q_RANK:MLA_Q_RANK + MLA_KV_RANK]
    small = za[:, MLA_Q_RANK + MLA_KV_RANK:ZA_W]

    cos = cos_ref[...]
    sina = sina_ref[...]
    sinb = sinb_ref[...]

    def rope(blk):
        return blk * cos + pltpu.roll(blk, LANES - 16, 1) * sina + pltpu.roll(blk, 16, 1) * sinb

    xn = cq * lax.rsqrt(jnp.mean(cq * cq, -1, keepdims=True) + NORM_EPS) * gcq_ref[...]
    q = _dot(xn.astype(BF16), wuq_ref[...])
    cn = ckv * lax.rsqrt(jnp.mean(ckv * ckv, -1, keepdims=True) + NORM_EPS) * gckv_ref[...]
    cnb = cn.astype(BF16)
    kn = _dot(cnb, wuk_ref[...])
    v = _dot(cnb, wuv_ref[...])
    kr = rope(small)
    mla_scale = (MLA_NOPE + MLA_ROPE) ** -0.5
    lane = _iota((1, LANES), 1)

    def pad_v(blk):
        return jnp.where(lane < HEAD_DIM, blk, jnp.where(lane == HEAD_DIM, 1.0, 0.0)).astype(BF16)

    for h in range(MLA_HEADS):
        sl = slice(LANES * h, LANES * (h + 1))
        q_out[0, :, sl] = (rope(q[:, sl]) * mla_scale).astype(BF16)
        k_out[0, :, sl] = (kn[:, sl] + kr).astype(BF16)
        v_out[0, :, sl] = pad_v(v[:, sl])

    fmask = (lane >= FOXF_LANE) & (lane < FOXF_LANE + FOX_HEADS)
    lf = jnp.where(fmask, jax.nn.log_sigmoid(small + bf_ref[...]), 0.0)
    tril = jnp.where(_iota((ts, ts), 0) >= _iota((ts, ts), 1), 1.0, 0.0).astype(BF16)
    lf_hi, lf_mid, lf_lo = _split3(lf)
    part = _dot(tril, jnp.concatenate([lf_hi, lf_mid], 1))
    cs = part[:, :LANES] + part[:, LANES:] + _dot(tril, lf_lo) + carry_ref[...]
    carry_ref[...] = cs[ts - 1:ts, :]
    hi, mid, lo = (t.astype(F32) for t in _split3(cs))
    c3 = (hi + pltpu.roll(mid, FOX_HEADS, 1) + pltpu.roll(lo, 2 * FOX_HEADS, 1)).astype(BF16)
    augq = _dot(c3, pq_ref[...]) + oneq_ref[...]
    augk = _dot(c3, pk_ref[...]) + onek_ref[...]
    off = MLA_HEADS * LANES
    q_out[0, :, off:off + FOX_HEADS * LANES] = (fq_ref[...].astype(F32) + augq).astype(BF16)
    k_out[0, :, off:off + FOX_HEADS * LANES] = (fk_ref[...].astype(F32) + augk).astype(BF16)
    for pr in range(FOX_HEADS // 2):
        blk = fv_ref[:, LANES * pr:LANES * (pr + 1)].astype(F32)
        v_out[0, :, off + 2 * pr * LANES:off + (2 * pr + 1) * LANES] = pad_v(blk)
        v_out[0, :, off + (2 * pr + 1) * LANES:off + (2 * pr + 2) * LANES] = pad_v(pltpu.roll(blk, HEAD_DIM, 1))


def _prep(za, fq, fk, fv, wl, tabs, b, s, ts=512):
    ns = s // ts
    row = lambda w: pl.BlockSpec((ts, w), lambda bi, si: (bi * ns + si, 0))
    full = lambda a: pl.BlockSpec(a.shape, lambda bi, si: (0,) * a.ndim)
    tab = pl.BlockSpec((ts, LANES), lambda bi, si: (si, 0))
    consts = [wl["g_cq"], wl["w_uq"], wl["g_ckv"], wl["w_uk"], wl["w_uv"], wl["b_forget"]]
    tail = [tabs["pq"], tabs["pk"], tabs["oneq"], tabs["onek"]]
    nh = MLA_HEADS + FOX_HEADS
    return pl.pallas_call(
        functools.partial(_prep_kernel, ts=ts),
        grid=(b, ns),
        in_specs=[row(ZA_W), row(768), row(768), row(384)] + [full(a) for a in consts]
                 + [tab, tab, tab] + [full(a) for a in tail],
        out_specs=[pl.BlockSpec((1, ts, nh * LANES), lambda bi, si: (bi, si, 0)),
                   pl.BlockSpec((1, ts, nh * LANES), lambda bi, si: (bi, si, 0)),
                   pl.BlockSpec((1, ts, nh * LANES), lambda bi, si: (bi, si, 0))],
        out_shape=[jax.ShapeDtypeStruct((b, s, nh * LANES), BF16),
                   jax.ShapeDtypeStruct((b, s, nh * LANES), BF16),
                   jax.ShapeDtypeStruct((b, s, nh * LANES), BF16)],
        scratch_shapes=[pltpu.VMEM((1, LANES), F32)],
        compiler_params=_cparams(("arbitrary", "arbitrary")),
        name="head_prep",
    )(za, fq, fk, fv, *consts, tabs["cos"], tabs["sina"], tabs["sinb"], *tail)


def _flash_kernel(q_ref, k_ref, v_ref, o_ref, m_ref, acc_ref, s_ref, *, tq, tk):
    qi = pl.program_id(2)
    m_ref[...] = jnp.full_like(m_ref, NEG)
    acc_ref[...] = jnp.zeros_like(acc_ref)
    nsub = tq // tk
    nfull = qi * nsub
    heads = [slice(LANES * h, LANES * (h + 1)) for h in range(2)]

    def scores(j, r0, sl):
        k0 = pl.multiple_of(j * tk, tk)
        return _dot_nt(q_ref[0, r0:tq, sl], k_ref[0, pl.ds(k0, tk), sl])

    def consume(s, h, j, r0, masked):
        k0 = pl.multiple_of(j * tk, tk)
        if masked:
            s = jnp.where(_iota((tq - r0, tk), 0) >= _iota((tq - r0, tk), 1), s, NEG)
        chunks = [s[:, LANES * c:LANES * (c + 1)] for c in range(tk // LANES)]
        m_prev = m_ref[h, r0:tq, :]
        m_new = jnp.maximum(m_prev, jnp.max(functools.reduce(jnp.maximum, chunks), -1, keepdims=True))
        p = jnp.concatenate([jnp.exp(c - m_new) for c in chunks], 1).astype(BF16)
        acc_ref[h, r0:tq, :] = (jnp.exp(m_prev - m_new) * acc_ref[h, r0:tq, :]
                                + _dot(p, v_ref[0, pl.ds(k0, tk), heads[h]]))
        m_ref[h, r0:tq, :] = m_new

    assert nsub == 2
    for h in range(2):
        s_ref[0, h] = scores(0, 0, heads[h])

    def body(i, carry):
        j = 2 * i
        for b in range(2):
            for h in range(2):
                s_ref[1 - b, h] = scores(j + b + 1, 0, heads[h])
            for h in range(2):
                consume(s_ref[b, h], h, j + b, 0, False)
        return carry

    lax.fori_loop(0, qi, body, 0)
    for h in range(2):
        consume(s_ref[0, h], h, nfull, 0, True)
    for d in range(1, nsub):
        for h in range(2):
            consume(scores(nfull + d, d * tk, heads[h]), h, nfull + d, d * tk, True)

    lane = _iota((1, LANES), 1)
    o0 = acc_ref[0]
    o1 = acc_ref[1]
    o0 = o0 / o0[:, HEAD_DIM:HEAD_DIM + 1]
    o1 = o1 / o1[:, HEAD_DIM:HEAD_DIM + 1]
    o_ref[0] = jnp.where(lane < HEAD_DIM, o0, pltpu.roll(o1, HEAD_DIM, 1)).astype(o_ref.dtype)


def _flash(q_all, k_all, v_all, tq=1024, tk=512):
    b, s, _ = q_all.shape
    npair = (MLA_HEADS + FOX_HEADS) // 2
    return pl.pallas_call(
        functools.partial(_flash_kernel, tq=tq, tk=tk),
        grid=(b, npair, s // tq),
        in_specs=[pl.BlockSpec((1, tq, 2 * LANES), lambda bi, p, qi: (bi, qi, p)),
                  pl.BlockSpec((1, s, 2 * LANES), lambda bi, p, qi: (bi, 0, p)),
                  pl.BlockSpec((1, s, 2 * LANES), lambda bi, p, qi: (bi, 0, p))],
        out_specs=pl.BlockSpec((1, tq, LANES), lambda bi, p, qi: (bi, qi, p)),
        out_shape=jax.ShapeDtypeStruct((b, s, npair * LANES), BF16),
        scratch_shapes=[pltpu.VMEM((2, tq, LANES), F32), pltpu.VMEM((2, tq, LANES), F32),
                        pltpu.VMEM((2, 2, tq, tk), F32)],
        compiler_params=_cparams(("parallel", "parallel", "arbitrary")),
        name="flash_attn",
    )(q_all, k_all, v_all)


def _cmp_kernel(tc_ref, posa_ref, posb_ref, wa_ref, wb_ref, w2_ref, out_ref):
    tc = tc_ref[0]
    a = _dot((tc + posa_ref[...]).astype(BF16), wa_ref[...])
    b = _dot((tc + posb_ref[...]).astype(BF16), wb_ref[...])
    pre = a + pltpu.roll(b, b.shape[0] - 1, 0)
    hid = jax.nn.silu(pre)
    out_ref[0] = _dot(hid.astype(BF16), w2_ref[...]).astype(out_ref.dtype)


def _compress(tc, wl):
    b, nc, w = tc.shape
    full = lambda a: pl.BlockSpec(a.shape, lambda bi: (0,) * a.ndim)
    consts = [wl["cmp_posa"], wl["cmp_posb"], wl["cmp_wa"], wl["cmp_wb"], wl["cmp_w2"]]
    return pl.pallas_call(
        _cmp_kernel,
        grid=(b,),
        in_specs=[pl.BlockSpec((1, nc, w), lambda bi: (bi, 0, 0))] + [full(a) for a in consts],
        out_specs=pl.BlockSpec((1, nc, LANES), lambda bi: (bi, 0, 0)),
        out_shape=jax.ShapeDtypeStruct((b, nc, LANES), BF16),
        compiler_params=_cparams(("parallel",)),
        name="nsa_compress",
    )(tc, *consts)


def _masked_softmax(s, mask):
    s = jnp.where(mask, s, NEG)
    m = jnp.max(s, -1, keepdims=True)
    p = jnp.where(mask, jnp.exp(s - m), 0.0)
    return p / jnp.maximum(jnp.sum(p, -1, keepdims=True), 1e-30)


def _nsa_kernel(nq_ref, kc_ref, ksl_ref, kwin_ref, g_ref, o_ref, m_ref, acc_ref, qst_ref, s_ref, *, tq, n_cmp):
    qi = pl.program_id(1)
    t0 = pl.multiple_of(qi * tq, tq)
    rpos = t0 + _iota((tq, 1), 0)
    lane = _iota((1, LANES), 1)
    qs = [nq_ref[0, :, LANES * h:LANES * (h + 1)] for h in range(NSA_HEADS)]

    dist_i = rpos - (NSA_CMP_STRIDE * lane + NSA_CMP_BLOCK - 1)
    valid_c = (dist_i >= 0) & (lane < n_cmp)
    dist_c = dist_i.astype(F32)
    kc = kc_ref[0]
    psum = jnp.zeros((tq, LANES), F32)
    o_cmp = []
    for h in range(NSA_HEADS):
        q_head = jnp.where(lane < HEAD_DIM, qs[h], jnp.zeros_like(qs[h]))
        p = _masked_softmax(_dot_nt(q_head, kc) - ALIBI[h] * dist_c, valid_c)
        psum = psum + p
        o_cmp.append(_dot(p.astype(BF16), kc))

    n_i = _iota((LANES, LANES), 0)
    j_i = _iota((LANES, LANES), 1)
    ov = ((NSA_CMP_STRIDE * n_i < NSA_SEL_BLOCK * (j_i + 1))
          & (NSA_CMP_STRIDE * n_i + NSA_CMP_BLOCK > NSA_SEL_BLOCK * j_i)
          & (n_i < n_cmp))
    imp = _dot_exact_rhs(psum, jnp.where(ov, 1.0, 0.0).astype(BF16))
    cur = jnp.right_shift(rpos, 6)
    forced = (lane == 0) | (lane == cur) | (lane == cur - 1)
    future = lane * NSA_SEL_BLOCK > rpos
    n_blk = kwin_ref.shape[1] // NSA_SEL_BLOCK - NSA_WINDOW // NSA_SEL_BLOCK
    work = jnp.where(forced, NSA_FORCE_SCORE, jnp.where(future, -1.0, imp))
    work = jnp.where(lane < n_blk, work, -jnp.inf)
    lane_f = lane.astype(F32)
    sel = jnp.zeros((tq, LANES), jnp.bool_)
    for _ in range(NSA_SEL_TOPN):
        mx = jnp.max(work, -1, keepdims=True)
        idx = jnp.min(jnp.where(work == mx, lane_f, float(LANES)), -1, keepdims=True)
        pick = lane_f == idx
        sel = sel | pick
        work = jnp.where(pick, -jnp.inf, work)
    sel_lanes = pltpu.roll(jnp.where(sel, 1.0, 0.0), NSA_SEL_LANE, 1)
    in_sel = (lane >= NSA_SEL_LANE) & (lane < NSA_SEL_LANE + n_blk)
    sel_bias = jnp.where(in_sel, (sel_lanes - 1.0) * NSA_MASK_BIG, 0.0)
    for h in range(NSA_HEADS):
        qst_ref[h * tq:(h + 1) * tq, :] = (qs[h].astype(F32) + sel_bias).astype(BF16)
    m_ref[...] = jnp.full_like(m_ref, NEG)
    acc_ref[...] = jnp.zeros_like(acc_ref)

    def online(s, v):
        chunks = [s[:, LANES * i:LANES * (i + 1)] for i in range(s.shape[1] // LANES)]
        m_prev = m_ref[...]
        m_new = jnp.maximum(m_prev, jnp.max(functools.reduce(jnp.maximum, chunks), -1, keepdims=True))
        p = jnp.concatenate([jnp.exp(ch - m_new) for ch in chunks], 1).astype(BF16)
        acc_ref[...] = jnp.exp(m_prev - m_new) * acc_ref[...] + _dot(p, v)
        m_ref[...] = m_new

    def scores(c):
        return _dot_nt(qst_ref[...], ksl_ref[0, pl.ds(pl.multiple_of(c * tq, tq), tq), 0:LANES])

    def values(c):
        return ksl_ref[0, pl.ds(pl.multiple_of(c * tq, tq), tq), LANES:2 * LANES]

    s_ref[0] = scores(0)

    def chunk_pair(i, carry):
        j = 2 * i
        s_ref[1] = scores(j + 1)
        online(s_ref[0], values(j))
        s_ref[0] = scores(j + 2)
        online(s_ref[1], values(j + 1))
        return carry

    lax.fori_loop(0, jnp.right_shift(qi, 1), chunk_pair, 0)

    @pl.when((qi & 1) == 1)
    def _():
        online(s_ref[0], values(qi - 1))
        s_ref[0] = scores(qi)

    nrow = NSA_HEADS * tq
    causal = (_iota((nrow, tq), 0) & (tq - 1)) >= _iota((nrow, tq), 1)
    online(jnp.where(causal, s_ref[0], NEG), values(qi))

    wlen = tq + NSA_WINDOW
    k_w = kwin_ref[0, pl.ds(t0, wlen), 0:LANES]
    v_w = kwin_ref[0, pl.ds(t0, wlen), LANES:2 * LANES]
    d_w = _iota((tq, wlen), 0) + NSA_WINDOW - _iota((tq, wlen), 1)
    keep_w = (d_w >= 0) & (d_w < NSA_WINDOW) & (t0 - NSA_WINDOW + _iota((1, wlen), 1) >= 0)

    gates = jax.nn.sigmoid(g_ref[...])
    for h in range(NSA_HEADS):
        s_w = jnp.where(keep_w, _dot_nt(qs[h], k_w), NEG)
        p_w = jnp.exp(s_w - jnp.max(s_w, -1, keepdims=True))
        o_win = _dot(p_w.astype(BF16), v_w)
        o_win = o_win / o_win[:, HEAD_DIM:HEAD_DIM + 1]
        o_sel = acc_ref[h * tq:(h + 1) * tq, :]
        o_sel = o_sel / o_sel[:, HEAD_DIM:HEAD_DIM + 1]
        g0 = NSAG_LANE + 3 * h
        o = (gates[:, g0:g0 + 1] * pltpu.roll(o_cmp[h], HEAD_DIM, 1) + gates[:, g0 + 1:g0 + 2] * o_sel
             + gates[:, g0 + 2:g0 + 3] * o_win)
        o_ref[0, :, LANES * h:LANES * (h + 1)] = jnp.where(lane < HEAD_DIM, o, 0.0).astype(o_ref.dtype)


def _nsa(nq, kcvc, nsl, nwin_pad, za, tq=256):
    b, s, _ = nq.shape
    nq_t = s // tq
    n_cmp = (s - NSA_CMP_BLOCK) // NSA_CMP_STRIDE + 1
    return pl.pallas_call(
        functools.partial(_nsa_kernel, tq=tq, n_cmp=n_cmp),
        grid=(b, nq_t),
        in_specs=[pl.BlockSpec((1, tq, NSA_HEADS * LANES), lambda bi, qi: (bi, qi, 0)),
                  pl.BlockSpec((1, kcvc.shape[1], LANES), lambda bi, qi: (bi, 0, 0)),
                  pl.BlockSpec((1, s, 2 * LANES), lambda bi, qi: (bi, 0, 0)),
                  pl.BlockSpec((1, s + NSA_WINDOW, 2 * LANES), lambda bi, qi: (bi, 0, 0)),
                  pl.BlockSpec((tq, LANES), lambda bi, qi: (bi * nq_t + qi, (ZA_W - LANES) // LANES))],
        out_specs=pl.BlockSpec((1, tq, NSA_HEADS * LANES), lambda bi, qi: (bi, qi, 0)),
        out_shape=jax.ShapeDtypeStruct((b, s, NSA_HEADS * LANES), BF16),
        scratch_shapes=[pltpu.VMEM((NSA_HEADS * tq, LANES), F32), pltpu.VMEM((NSA_HEADS * tq, LANES), F32),
                        pltpu.VMEM((NSA_HEADS * tq, LANES), BF16), pltpu.VMEM((2, NSA_HEADS * tq, tq), F32)],
        compiler_params=_cparams(("parallel", "arbitrary")),
        name="nsa_attn",
    )(nq, kcvc, nsl, nwin_pad, za)


def _layer_norm(y, g, b):
    mu = jnp.mean(y, -1, keepdims=True)
    yc = y - mu
    var = jnp.mean(yc * yc, -1, keepdims=True)
    return yc * lax.rsqrt(var + NORM_EPS) * g + b


ROUTE_ROWS = 512


def _out_proj_kernel(*refs, routed):
    if routed:
        (oa_ref, on_ref, x_ref, wa_ref, wb_ref, g_ref, b_ref, rw_ref,
         o_ref, gate_ref, pos_ref, post_ref, cnt_ref) = refs
    else:
        oa_ref, on_ref, x_ref, wa_ref, wb_ref, g_ref, b_ref, o_ref = refs
    mix = _dot(oa_ref[...], wa_ref[...]) + _dot(on_ref[...], wb_ref[...])
    x1 = _layer_norm(DEEPNORM_ALPHA * x_ref[...] + mix, g_ref[...], b_ref[...])
    o_ref[...] = x1
    if routed:
        tm = x1.shape[0]
        lane = _iota((1, LANES), 1)
        lane_f = lane.astype(F32)
        logits = jnp.where(lane < N_EXPERTS, _dot_split(x1, rw_ref[...]), NEG)
        ex = jnp.exp(logits - jnp.max(logits, -1, keepdims=True))
        probs = ex / jnp.sum(ex, -1, keepdims=True)
        p1 = jnp.max(probs, -1, keepdims=True)
        i1 = jnp.min(jnp.where(probs == p1, lane_f, float(LANES)), -1, keepdims=True)
        rest = jnp.where(lane_f == i1, -1.0, probs)
        p2 = jnp.max(rest, -1, keepdims=True)
        i2 = jnp.min(jnp.where(rest == p2, lane_f, float(LANES)), -1, keepdims=True)
        tot = p1 + p2
        gate_ref[...] = jnp.where(lane_f == i1, p1 / tot, jnp.where(lane_f == i2, p2 / tot, 0.0))
        chosen = (lane_f == i1) | (lane_f == i2)
        before = (_iota((tm, tm), 0) > _iota((tm, tm), 1)).astype(BF16)
        onehot = jnp.where(chosen, 1.0, 0.0)
        slot = _dot(before, onehot.astype(BF16))
        posm = jnp.where(chosen, slot, -1.0)
        pos_ref[...] = posm
        post_ref[...] = posm.T[0:N_EXPERTS, :]
        cnt_ref[...] = jnp.broadcast_to(jnp.sum(onehot, 0, keepdims=True), (8, LANES)).astype(jnp.int32)


def _out_proj(o_attn, o_nsa, x2, wl, router=None, tm=ROUTE_ROWS):
    n = x2.shape[0]
    routed = router is not None
    full = lambda a: pl.BlockSpec(a.shape, lambda i: (0,) * a.ndim)
    row = lambda w: pl.BlockSpec((tm, w), lambda i: (i, 0))
    consts = [wl["w_out_a"], wl["w_out_b"], wl["ln1_g"], wl["ln1_b"]] + ([router] if routed else [])
    out_specs = [row(D_MODEL)]
    out_shape = [jax.ShapeDtypeStruct((n, D_MODEL), F32)]
    if routed:
        out_specs += [row(LANES), row(LANES), pl.BlockSpec((N_EXPERTS, tm), lambda i: (i, 0)),
                      pl.BlockSpec((8, LANES), lambda i: (i, 0))]
        out_shape += [jax.ShapeDtypeStruct((n, LANES), F32), jax.ShapeDtypeStruct((n, LANES), F32),
                      jax.ShapeDtypeStruct((n // tm * N_EXPERTS, tm), F32),
                      jax.ShapeDtypeStruct((n // tm * 8, LANES), jnp.int32)]
    outs = pl.pallas_call(
        functools.partial(_out_proj_kernel, routed=routed),
        grid=(n // tm,),
        in_specs=[row(o_attn.shape[1]), row(o_nsa.shape[1]), row(D_MODEL)] + [full(a) for a in consts],
        out_specs=out_specs,
        out_shape=out_shape,
        compiler_params=_cparams(("parallel",)),
        name="out_proj_route_ln" if routed else "out_proj_ln",
    )(o_attn, o_nsa, x2, *consts)
    return outs if routed else outs[0]


def _ffn_kernel(x_ref, w1_ref, w3_ref, w2_ref, g_ref, b_ref, o_ref, acc_ref, xb_ref):
    c = pl.program_id(1)

    @pl.when(c == 0)
    def _():
        acc_ref[...] = jnp.zeros_like(acc_ref)
        xb_ref[...] = x_ref[...].astype(BF16)

    xb = xb_ref[...]
    a = jax.nn.silu(_dot(xb, w1_ref[0])) * _dot(xb, w3_ref[0])
    acc_ref[...] += _dot(a.astype(BF16), w2_ref[0])

    @pl.when(c == pl.num_programs(1) - 1)
    def _():
        o_ref[...] = _layer_norm(DEEPNORM_ALPHA * x_ref[...] + acc_ref[...], g_ref[...], b_ref[...])


def _ffn(x2, w1, w3, w2, ln_g, ln_b, tm=512):
    n = x2.shape[0]
    ne, _, tf = w1.shape
    full = lambda a: pl.BlockSpec(a.shape, lambda i, c: (0,) * a.ndim)
    return pl.pallas_call(
        _ffn_kernel,
        grid=(n // tm, ne),
        in_specs=[pl.BlockSpec((tm, D_MODEL), lambda i, c: (i, 0)),
                  pl.BlockSpec((1, D_MODEL, tf), lambda i, c: (c, 0, 0)),
                  pl.BlockSpec((1, D_MODEL, tf), lambda i, c: (c, 0, 0)),
                  pl.BlockSpec((1, tf, D_MODEL), lambda i, c: (c, 0, 0)),
                  full(ln_g), full(ln_b)],
        out_specs=pl.BlockSpec((tm, D_MODEL), lambda i, c: (i, 0)),
        out_shape=jax.ShapeDtypeStruct((n, D_MODEL), F32),
        scratch_shapes=[pltpu.VMEM((tm, D_MODEL), F32), pltpu.VMEM((tm, D_MODEL), BF16)],
        compiler_params=_cparams(("parallel", "arbitrary")),
        name="ffn_ln",
    )(x2, w1, w3, w2, ln_g, ln_b)


MOE_CAP = 160
MOE_CAP_PAD = -(-MOE_CAP // LANES) * LANES
MOE_CHUNKS = -(-ROUTE_ROWS // MOE_CAP)


def _moe_kernel(cnt_ref, x_ref, gate_ref, pos_ref, *rest, groups):
    post_refs = rest[:groups]
    w1_ref, w3_ref, w2_ref, g_ref, b_ref, o_ref, acc_ref, xb_ref = rest[groups:]
    i = pl.program_id(0)
    e = pl.program_id(1)
    lane = _iota((1, LANES), 1)

    @pl.when(e == 0)
    def _():
        acc_ref[...] = jnp.zeros_like(acc_ref)
        xb_ref[...] = x_ref[...].astype(BF16)

    for gi in range(groups):
        rows = slice(ROUTE_ROWS * gi, ROUTE_ROWS * (gi + 1))
        cnt = cnt_ref[(i * groups + gi) * N_EXPERTS + e]
        gate_e = jnp.sum(jnp.where(lane == e, gate_ref[rows, :], 0.0), -1, keepdims=True)
        slot_col = jnp.sum(jnp.where(lane == e, pos_ref[rows, :], 0.0), -1, keepdims=True)
        slot_row = post_refs[gi][0]
        for k in range(MOE_CHUNKS):
            @pl.when(cnt > k * MOE_CAP)
            def _():
                want = (_iota((MOE_CAP, 1), 0) + k * MOE_CAP).astype(F32)
                pick = jnp.where(slot_row == want, 1.0, 0.0).astype(BF16)
                xg = _dot(pick, xb_ref[rows, :]).astype(BF16)
                a = jax.nn.silu(_dot(xg, w1_ref[0])) * _dot(xg, w3_ref[0])
                y = _dot(a.astype(BF16), w2_ref[0])
                col = _iota((1, MOE_CAP_PAD), 1)
                put = jnp.where((slot_col == (col + k * MOE_CAP).astype(F32)) & (col < MOE_CAP),
                                1.0, 0.0).astype(BF16)
                yb = y.astype(BF16)
                if MOE_CAP_PAD > MOE_CAP:
                    yb = jnp.concatenate([yb, jnp.zeros((MOE_CAP_PAD - MOE_CAP, D_MODEL), BF16)], 0)
                acc_ref[rows, :] += gate_e * _dot(put, yb)

    @pl.when(e == pl.num_programs(1) - 1)
    def _():
        o_ref[...] = _layer_norm(DEEPNORM_ALPHA * x_ref[...] + acc_ref[...], g_ref[...], b_ref[...])


def _moe(x2, gates, pos, post, cnt, w1, w3, w2, ln_g, ln_b, groups=2):
    n = x2.shape[0]
    ne, _, tf = w1.shape
    tm = groups * ROUTE_ROWS
    post3 = post.reshape(n // ROUTE_ROWS * N_EXPERTS, 1, ROUTE_ROWS)
    full = lambda a: pl.BlockSpec(a.shape, lambda i, e, c: (0,) * a.ndim)
    row = lambda w: pl.BlockSpec((tm, w), lambda i, e, c: (i, 0))
    post_spec = lambda gi: pl.BlockSpec((1, 1, ROUTE_ROWS),
                                        lambda i, e, c: ((i * groups + gi) * N_EXPERTS + e, 0, 0))
    grid_spec = pltpu.PrefetchScalarGridSpec(
        num_scalar_prefetch=1,
        grid=(n // tm, ne),
        in_specs=[row(D_MODEL), row(LANES), row(LANES)] + [post_spec(gi) for gi in range(groups)]
                 + [pl.BlockSpec((1, D_MODEL, tf), lambda i, e, c: (e, 0, 0)),
                    pl.BlockSpec((1, D_MODEL, tf), lambda i, e, c: (e, 0, 0)),
                    pl.BlockSpec((1, tf, D_MODEL), lambda i, e, c: (e, 0, 0)),
                    full(ln_g), full(ln_b)],
        out_specs=row(D_MODEL),
        scratch_shapes=[pltpu.VMEM((tm, D_MODEL), F32), pltpu.VMEM((tm, D_MODEL), BF16)],
    )
    return pl.pallas_call(
        functools.partial(_moe_kernel, groups=groups),
        grid_spec=grid_spec,
        out_shape=jax.ShapeDtypeStruct((n, D_MODEL), F32),
        compiler_params=_cparams(("parallel", "arbitrary")),
        name="moe_top2_ln",
    )(cnt, x2, gates, pos, *([post3] * groups), w1, w3, w2, ln_g, ln_b)


def _in_proj_columns():
    src = np.full((IN_P,), -1, np.int64)
    scale = np.ones((IN_P,), np.float32)
    o_cq, o_ckv, o_kr = 0, MLA_Q_RANK, MLA_Q_RANK + MLA_KV_RANK
    o_fox = o_kr + MLA_ROPE
    o_foxf = o_fox + 3 * FOX_HEADS * HEAD_DIM
    o_nq = o_foxf + FOX_HEADS
    o_nkv = o_nq + NSA_HEADS * HEAD_DIM
    o_ng = o_nkv + 6 * HEAD_DIM
    src[0:o_kr] = np.arange(o_kr)
    small = o_kr
    src[small + KR_LANE:small + KR_LANE + MLA_ROPE] = o_kr + np.arange(MLA_ROPE)
    src[small + FOXF_LANE:small + FOXF_LANE + FOX_HEADS] = o_foxf + np.arange(FOX_HEADS)
    src[small + NSAG_LANE:small + NSAG_LANE + 3 * NSA_HEADS] = o_ng + np.arange(3 * NSA_HEADS)
    n_fox = FOX_HEADS * HEAD_DIM
    src[FQ_OFF:FQ_OFF + 3 * n_fox] = o_fox + np.arange(3 * n_fox)
    scale[FQ_OFF:FQ_OFF + n_fox] = HEAD_DIM ** -0.5
    src[NQ_OFF:NQ_OFF + NSA_HEADS * HEAD_DIM] = o_nq + np.arange(NSA_HEADS * HEAD_DIM)
    scale[NQ_OFF:NQ_OFF + NSA_HEADS * HEAD_DIM] = HEAD_DIM ** -0.5
    src[NC_OFF:NC_OFF + 6 * HEAD_DIM] = o_nkv + np.arange(6 * HEAD_DIM)
    return src, scale


def _gather_cols(w, src, scale=None):
    scale = np.ones(src.shape, np.float32) if scale is None else scale
    parts, i, n = [], 0, len(src)
    while i < n:
        j = i + 1
        while j < n and scale[j] == scale[i] and (src[j] == src[j - 1] + 1 if src[i] >= 0 else src[j] < 0):
            j += 1
        if src[i] < 0:
            parts.append(jnp.zeros(w.shape[:-1] + (j - i,), w.dtype))
        else:
            seg = w[..., int(src[i]):int(src[i]) + (j - i)]
            parts.append(seg if scale[i] == 1.0 else seg * float(scale[i]))
        i = j
    return jnp.concatenate(parts, axis=-1)


def _tables(s):
    half = MLA_ROPE // 2
    freqs = ROPE_THETA ** (-jnp.arange(half, dtype=F32) / half)
    ang = jnp.arange(s).astype(F32)[:, None] * freqs[None, :]
    cos, sin = jnp.cos(ang), jnp.sin(ang)
    z = lambda w: jnp.zeros((s, w), F32)
    tabs = {
        "cos": jnp.concatenate([jnp.ones((s, MLA_NOPE), F32), cos, cos, z(LANES - MLA_NOPE - MLA_ROPE)], 1),
        "sina": jnp.concatenate([z(MLA_NOPE), -sin, z(half), z(LANES - MLA_NOPE - MLA_ROPE)], 1),
        "sinb": jnp.concatenate([z(MLA_NOPE), z(half), sin, z(LANES - MLA_NOPE - MLA_ROPE)], 1),
    }
    pq = np.zeros((LANES, FOX_HEADS * LANES), np.float32)
    pk = np.zeros((LANES, FOX_HEADS * LANES), np.float32)
    oneq = np.zeros((1, FOX_HEADS * LANES), np.float32)
    onek = np.zeros((1, FOX_HEADS * LANES), np.float32)
    for h in range(FOX_HEADS):
        for t in range(3):
            pq[FOXF_LANE + FOX_HEADS * t + h, LANES * h + HEAD_DIM + t] = 1.0
            pk[FOXF_LANE + FOX_HEADS * t + h, LANES * h + HEAD_DIM + 3 + t] = -1.0
            oneq[0, LANES * h + HEAD_DIM + 3 + t] = 1.0
            onek[0, LANES * h + HEAD_DIM + t] = 1.0
    tabs.update(pq=jnp.asarray(pq, BF16), pk=jnp.asarray(pk, BF16), oneq=jnp.asarray(oneq), onek=jnp.asarray(onek))
    qaux = np.zeros((1, NSA_HEADS * LANES), np.float32)
    for h in range(NSA_HEADS):
        qaux[0, LANES * h + NSA_AUX_LANE] = ALIBI[h] * NSA_SEL_BLOCK
        qaux[0, LANES * h + NSA_AUX_LANE + 1] = ALIBI[h]
    kaux = np.zeros((s, LANES), np.float32)
    pos = np.arange(s)
    kaux[:, NSA_AUX_LANE] = pos // NSA_SEL_BLOCK
    kaux[:, NSA_AUX_LANE + 1] = pos % NSA_SEL_BLOCK
    kaux[pos, NSA_SEL_LANE + pos // NSA_SEL_BLOCK] = 1.0
    tabs.update(nsa_qaux=jnp.asarray(qaux), nsa_kaux=jnp.asarray(kaux))
    return tabs


def _layer_weights(p, l):
    qd = MLA_NOPE + MLA_ROPE
    src_q = np.full((MLA_HEADS * LANES,), -1, np.int64)
    src_k = np.full((MLA_HEADS * LANES,), -1, np.int64)
    src_v = np.full((MLA_HEADS * LANES,), -1, np.int64)
    for h in range(MLA_HEADS):
        src_q[LANES * h + np.arange(qd)] = qd * h + np.arange(qd)
        src_k[LANES * h + np.arange(MLA_NOPE)] = 2 * HEAD_DIM * h + np.arange(MLA_NOPE)
        src_v[LANES * h + np.arange(HEAD_DIM)] = 2 * HEAD_DIM * h + MLA_NOPE + np.arange(HEAD_DIM)
    w_out = p["w_out"][l]
    n_attn = (MLA_HEADS + FOX_HEADS) * HEAD_DIM
    wb = w_out[n_attn:].reshape(NSA_HEADS, HEAD_DIM, D_MODEL)
    wb = jnp.concatenate([wb, jnp.zeros_like(wb)], axis=1).reshape(NSA_HEADS * LANES, D_MODEL)
    bf = jnp.zeros((1, LANES), F32).at[0, FOXF_LANE:FOXF_LANE + FOX_HEADS].set(p["b_forget"][l])

    half = NSA_CMP_BLOCK // 2
    kpos, vpos = p["cmp_k_pos"][l], p["cmp_v_pos"][l]
    posa = jnp.concatenate([kpos[:half], vpos[:half]], -1).reshape(1, half * LANES)
    posb = jnp.concatenate([kpos[half:], vpos[half:]], -1).reshape(1, half * LANES)

    def w1_half(lo):
        k = p["cmp_k_w1"][l].reshape(NSA_CMP_BLOCK, HEAD_DIM, NSA_CMP_HIDDEN)[lo:lo + half]
        v = p["cmp_v_w1"][l].reshape(NSA_CMP_BLOCK, HEAD_DIM, NSA_CMP_HIDDEN)[lo:lo + half]
        zk = jnp.zeros_like(k)
        top = jnp.concatenate([k, zk], -1)
        bot = jnp.concatenate([zk, v], -1)
        return jnp.concatenate([top, bot], 1).reshape(half * LANES, 2 * NSA_CMP_HIDDEN).astype(BF16)

    zw2 = jnp.zeros((NSA_CMP_HIDDEN, HEAD_DIM), F32)
    w2 = jnp.concatenate([jnp.concatenate([p["cmp_k_w2"][l], zw2], 1),
                          jnp.concatenate([zw2, p["cmp_v_w2"][l]], 1)], 0).astype(BF16)
    return {
        "g_cq": p["g_cq"][l][None, :], "g_ckv": p["g_ckv"][l][None, :],
        "w_uq": _gather_cols(p["w_uq"][l], src_q).astype(BF16),
        "w_uk": _gather_cols(p["w_ukv"][l], src_k).astype(BF16),
        "w_uv": _gather_cols(p["w_ukv"][l], src_v).astype(BF16),
        "b_forget": bf,
        "cmp_posa": posa, "cmp_posb": posb, "cmp_wa": w1_half(0), "cmp_wb": w1_half(half), "cmp_w2": w2,
        "w_out_a": w_out[:n_attn].astype(BF16), "w_out_b": wb.astype(BF16),
        "ln1_g": p["ln1_g"][l][None, :], "ln1_b": p["ln1_b"][l][None, :],
        "ln2_g": p["ln2_g"][l][None, :], "ln2_b": p["ln2_b"][l][None, :],
    }


def kernel(x, w_in, b_forget, g_cq, w_uq, g_ckv, w_ukv, cmp_k_pos, cmp_k_w1, cmp_k_w2, cmp_v_pos, cmp_v_w1,
           cmp_v_w2, w_out, ln1_g, ln1_b, ln2_g, ln2_b, ffn_w1, ffn_w3, ffn_w2, router_w, moe_w1, moe_w3,
           moe_w2):
    b, s, d = x.shape
    assert d == D_MODEL and s % 512 == 0 and s // NSA_CMP_STRIDE == LANES, (b, s, d)
    p = dict(b_forget=b_forget, g_cq=g_cq, w_uq=w_uq, g_ckv=g_ckv, w_ukv=w_ukv, cmp_k_pos=cmp_k_pos,
             cmp_k_w1=cmp_k_w1, cmp_k_w2=cmp_k_w2, cmp_v_pos=cmp_v_pos, cmp_v_w1=cmp_v_w1, cmp_v_w2=cmp_v_w2,
             w_out=w_out, ln1_g=ln1_g, ln1_b=ln1_b, ln2_g=ln2_g, ln2_b=ln2_b)
    src, scale = _in_proj_columns()
    w_in_p = _gather_cols(w_in, src, scale).astype(BF16)
    tabs = _tables(s)
    n = b * s
    x2 = x.reshape(n, d)
    for l in range(DEPTH):
        wl = _layer_weights(p, l)
        za, fq, fk, fv, nq, nc, nsl, nwin = _in_proj(x2, w_in_p[l], tabs, s)
        q_all, k_all, v_all = _prep(za, fq, fk, fv, wl, tabs, b, s)
        o_attn = _flash(q_all, k_all, v_all)
        kcvc = _compress(nc.reshape(b, s // NSA_CMP_STRIDE, NSA_CMP_STRIDE * LANES), wl)
        nwin_pad = jnp.pad(nwin.reshape(b, s, 2 * LANES), ((0, 0), (NSA_WINDOW, 0), (0, 0)))
        o_nsa = _nsa(nq.reshape(b, s, NSA_HEADS * LANES), kcvc, nsl.reshape(b, s, 2 * LANES), nwin_pad, za)
        j = l // 2
        if l % 2 == 0:
            x2 = _out_proj(o_attn.reshape(n, -1), o_nsa.reshape(n, -1), x2, wl)
            d_ff = ffn_w1.shape[-1]
            nchunk = 2
            tf = d_ff // nchunk
            w1 = ffn_w1[j].reshape(d, nchunk, tf).transpose(1, 0, 2).astype(BF16)
            w3 = ffn_w3[j].reshape(d, nchunk, tf).transpose(1, 0, 2).astype(BF16)
            w2 = ffn_w2[j].reshape(nchunk, tf, d).astype(BF16)
            x2 = _ffn(x2, w1, w3, w2, wl["ln2_g"], wl["ln2_b"])
        else:
            rw = jnp.pad(router_w[j], ((0, 0), (0, LANES - N_EXPERTS)))
            rw_hi = rw.astype(BF16)
            rw_lo = (rw - rw_hi.astype(F32)).astype(BF16)
            rw = jnp.concatenate([jnp.concatenate([rw_hi, rw_lo], 1),
                                  jnp.concatenate([rw_hi, jnp.zeros_like(rw_lo)], 1)], 0)
            x2, gates, pos, post, cnt = _out_proj(o_attn.reshape(n, -1), o_nsa.reshape(n, -1), x2, wl, router=rw)
            cnt = cnt[::8, :N_EXPERTS].reshape(-1)
            x2 = _moe(x2, gates, pos, post, cnt, moe_w1[j].astype(BF16), moe_w3[j].astype(BF16),
                      moe_w2[j].astype(BF16), wl["ln2_g"], wl["ln2_b"])
    return x2.reshape(b, s, d)
```

```python
import functools

import numpy as np
import jax
import jax.numpy as jnp
from jax import lax
from jax.experimental import pallas as pl
from jax.experimental.pallas import tpu as pltpu

F32 = jnp.float32
BF16 = jnp.bfloat16

D_MODEL = 1024
HEAD_DIM = 64
LANES = 128
MLA_HEADS = 6
MLA_Q_RANK = 384
MLA_KV_RANK = 256
MLA_NOPE = 64
MLA_ROPE = 32
ROPE_THETA = 10000.0
FOX_HEADS = 6
NSA_HEADS = 4
NSA_CMP_BLOCK = 32
NSA_CMP_STRIDE = 16
NSA_CMP_HIDDEN = 128
NSA_SEL_BLOCK = 64
NSA_SEL_TOPN = 8
NSA_WINDOW = 256
NSA_FORCE_SCORE = 1.0e4
N_EXPERTS = 8
NORM_EPS = 1e-5
DEPTH = 2
DEEPNORM_ALPHA = (2 * DEPTH) ** 0.25
NEG = -1e30
VMEM_LIMIT = 56 * 1024 * 1024

KR_LANE = 64
FOXF_LANE = 96
NSAG_LANE = 102

ZA_W = 768
FQ_OFF = ZA_W
FK_OFF = FQ_OFF + 384
FV_OFF = FK_OFF + 384
NQ_OFF = FV_OFF + 384
NC_OFF = NQ_OFF + 256
NKV_OFF = NC_OFF + 128
IN_P = NKV_OFF + 256

NSA_AUX_LANE = HEAD_DIM
NSA_SEL_LANE = HEAD_DIM + 2
NSA_MASK_BIG = 2.0 ** 126

ALIBI = tuple(2.0 ** (-8.0 * (i + 1) / NSA_HEADS) for i in range(NSA_HEADS))


def _dot(a, b, **kw):
    return jnp.dot(a, b, preferred_element_type=F32, **kw)


def _dot_nt(a, b):
    return lax.dot_general(a, b, (((1,), (1,)), ((), ())), preferred_element_type=F32)


def _iota(shape, dim):
    return lax.broadcasted_iota(jnp.int32, shape, dim)


def _split3(x):
    hi = x.astype(BF16)
    r = x - hi.astype(F32)
    mid = r.astype(BF16)
    return hi, mid, (r - mid.astype(F32)).astype(BF16)


def _dot_exact_rhs(x, w01):
    hi, mid, lo = _split3(x)
    return _dot(jnp.concatenate([hi, mid, lo], 1), jnp.concatenate([w01, w01, w01], 0))


def _dot_split(x, w_stack):
    hi = x.astype(BF16)
    lo = (x - hi.astype(F32)).astype(BF16)
    r = _dot(jnp.concatenate([hi, lo], 1), w_stack)
    half = r.shape[1] // 2
    return r[:, :half] + r[:, half:]


def _cparams(sem):
    return pltpu.CompilerParams(dimension_semantics=sem, vmem_limit_bytes=VMEM_LIMIT)


def _in_proj_kernel(x_ref, w_ref, qaux_ref, kaux_ref, za_ref, fq_ref, fk_ref, fv_ref, nq_ref, nc_ref,
                    nsl_ref, nwin_ref):
    xb = x_ref[...].astype(BF16)

    def mm(a, b):
        return _dot(xb, w_ref[:, a:b])

    lane = _iota((1, LANES), 1)

    def halves(pair):
        return (jnp.where(lane < HEAD_DIM, pair, 0.0),
                jnp.where(lane < HEAD_DIM, pltpu.roll(pair, HEAD_DIM, 1), 0.0))

    def expand(out_ref, z, add=None):
        for pr in range(z.shape[1] // LANES):
            for i, blk in enumerate(halves(z[:, LANES * pr:LANES * (pr + 1)])):
                sl = slice(LANES * (2 * pr + i), LANES * (2 * pr + i + 1))
                out_ref[:, sl] = (blk if add is None else blk + add[:, sl]).astype(BF16)

    za_ref[...] = mm(0, ZA_W)
    expand(fq_ref, mm(FQ_OFF, FK_OFF))
    expand(fk_ref, mm(FK_OFF, FV_OFF))
    fv_ref[...] = mm(FV_OFF, NQ_OFF).astype(BF16)
    expand(nq_ref, mm(NQ_OFF, NC_OFF), qaux_ref[...])
    nc_ref[...] = mm(NC_OFF, NKV_OFF)
    ones_lane = jnp.where(lane == HEAD_DIM, 1.0, 0.0)
    kaux = kaux_ref[...]
    nkv = mm(NKV_OFF, IN_P)
    for i, out_ref in enumerate((nsl_ref, nwin_ref)):
        k_blk, v_blk = halves(nkv[:, LANES * i:LANES * (i + 1)])
        out_ref[:, 0:LANES] = (k_blk + kaux).astype(BF16)
        out_ref[:, LANES:2 * LANES] = (v_blk + ones_lane).astype(BF16)


def _in_proj(x2, w_p, tabs, s, tm=512):
    n = x2.shape[0]
    widths = [(ZA_W, F32), (768, BF16), (768, BF16), (384, BF16), (512, BF16), (128, F32), (256, BF16),
              (256, BF16)]
    return pl.pallas_call(
        _in_proj_kernel,
        grid=(n // tm,),
        in_specs=[pl.BlockSpec((tm, D_MODEL), lambda i: (i, 0)),
                  pl.BlockSpec((D_MODEL, IN_P), lambda i: (0, 0)),
                  pl.BlockSpec((1, NSA_HEADS * LANES), lambda i: (0, 0)),
                  pl.BlockSpec((tm, LANES), lambda i: (i % (s // tm), 0))],
        out_specs=[pl.BlockSpec((tm, w), lambda i: (i, 0)) for w, _ in widths],
        out_shape=[jax.ShapeDtypeStruct((n, w), dt) for w, dt in widths],
        compiler_params=_cparams(("parallel",)),
        name="in_proj",
    )(x2, w_p, tabs["nsa_qaux"], tabs["nsa_kaux"])


def _prep_kernel(za_ref, fq_ref, fk_ref, fv_ref, gcq_ref, wuq_ref, gckv_ref, wuk_ref, wuv_ref,
                 bf_ref, cos_ref, sina_ref, sinb_ref, pq_ref, pk_ref, oneq_ref, onek_ref,
                 q_out, k_out, v_out, carry_ref, *, ts):
    @pl.when(pl.program_id(1) == 0)
    def _():
        carry_ref[...] = jnp.zeros_like(carry_ref)

    za = za_ref[...]
    cq = za[:, 0:MLA_Q_RANK]
    ckv = za[:, MLA_Q_RANK:MLA_Q_RANK + MLA_KV_RANK]
    small = za[:, MLA_Q_RANK + MLA_KV_RANK:ZA_W]

    cos = cos_ref[...]
    sina = sina_ref[...]
    sinb = sinb_ref[...]

    def rope(blk):
        return blk * cos + pltpu.roll(blk, LANES - 16, 1) * sina + pltpu.roll(blk, 16, 1) * sinb

    xn = cq * lax.rsqrt(jnp.mean(cq * cq, -1, keepdims=True) + NORM_EPS) * gcq_ref[...]
    q = _dot(xn.astype(BF16), wuq_ref[...])
    cn = ckv * lax.rsqrt(jnp.mean(ckv * ckv, -1, keepdims=True) + NORM_EPS) * gckv_ref[...]
    cnb = cn.astype(BF16)
    kn = _dot(cnb, wuk_ref[...])
    v = _dot(cnb, wuv_ref[...])
    kr = rope(small)
    mla_scale = (MLA_NOPE + MLA_ROPE) ** -0.5
    lane = _iota((1, LANES), 1)

    def pad_v(blk):
        return jnp.where(lane < HEAD_DIM, blk, jnp.where(lane == HEAD_DIM, 1.0, 0.0)).astype(BF16)

    for h in range(MLA_HEADS):
        sl = slice(LANES * h, LANES * (h + 1))
        q_out[0, :, sl] = (rope(q[:, sl]) * mla_scale).astype(BF16)
        k_out[0, :, sl] = (kn[:, sl] + kr).astype(BF16)
        v_out[0, :, sl] = pad_v(v[:, sl])

    fmask = (lane >= FOXF_LANE) & (lane < FOXF_LANE + FOX_HEADS)
    lf = jnp.where(fmask, jax.nn.log_sigmoid(small + bf_ref[...]), 0.0)
    tril = jnp.where(_iota((ts, ts), 0) >= _iota((ts, ts), 1), 1.0, 0.0).astype(BF16)
    lf_hi, lf_mid, lf_lo = _split3(lf)
    part = _dot(tril, jnp.concatenate([lf_hi, lf_mid], 1))
    cs = part[:, :LANES] + part[:, LANES:] + _dot(tril, lf_lo) + carry_ref[...]
    carry_ref[...] = cs[ts - 1:ts, :]
    hi, mid, lo = (t.astype(F32) for t in _split3(cs))
    c3 = (hi + pltpu.roll(mid, FOX_HEADS, 1) + pltpu.roll(lo, 2 * FOX_HEADS, 1)).astype(BF16)
    augq = _dot(c3, pq_ref[...]) + oneq_ref[...]
    augk = _dot(c3, pk_ref[...]) + onek_ref[...]
    off = MLA_HEADS * LANES
    q_out[0, :, off:off + FOX_HEADS * LANES] = (fq_ref[...].astype(F32) + augq).astype(BF16)
    k_out[0, :, off:off + FOX_HEADS * LANES] = (fk_ref[...].astype(F32) + augk).astype(BF16)
    for pr in range(FOX_HEADS // 2):
        blk = fv_ref[:, LANES * pr:LANES * (pr + 1)].astype(F32)
        v_out[0, :, off + 2 * pr * LANES:off + (2 * pr + 1) * LANES] = pad_v(blk)
        v_out[0, :, off + (2 * pr + 1) * LANES:off + (2 * pr + 2) * LANES] = pad_v(pltpu.roll(blk, HEAD_DIM, 1))


def _prep(za, fq, fk, fv, wl, tabs, b, s, ts=512):
    ns = s // ts
    row = lambda w: pl.BlockSpec((ts, w), lambda bi, si: (bi * ns + si, 0))
    full = lambda a: pl.BlockSpec(a.shape, lambda bi, si: (0,) * a.ndim)
    tab = pl.BlockSpec((ts, LANES), lambda bi, si: (si, 0))
    consts = [wl["g_cq"], wl["w_uq"], wl["g_ckv"], wl["w_uk"], wl["w_uv"], wl["b_forget"]]
    tail = [tabs["pq"], tabs["pk"], tabs["oneq"], tabs["onek"]]
    nh = MLA_HEADS + FOX_HEADS
    return pl.pallas_call(
        functools.partial(_prep_kernel, ts=ts),
        grid=(b, ns),
        in_specs=[row(ZA_W), row(768), row(768), row(384)] + [full(a) for a in consts]
                 + [tab, tab, tab] + [full(a) for a in tail],
        out_specs=[pl.BlockSpec((1, ts, nh * LANES), lambda bi, si: (bi, si, 0)),
                   pl.BlockSpec((1, ts, nh * LANES), lambda bi, si: (bi, si, 0)),
                   pl.BlockSpec((1, ts, nh * LANES), lambda bi, si: (bi, si, 0))],
        out_shape=[jax.ShapeDtypeStruct((b, s, nh * LANES), BF16),
                   jax.ShapeDtypeStruct((b, s, nh * LANES), BF16),
                   jax.ShapeDtypeStruct((b, s, nh * LANES), BF16)],
        scratch_shapes=[pltpu.VMEM((1, LANES), F32)],
        compiler_params=_cparams(("arbitrary", "arbitrary")),
        name="head_prep",
    )(za, fq, fk, fv, *consts, tabs["cos"], tabs["sina"], tabs["sinb"], *tail)


def _flash_kernel(q_ref, k_ref, v_ref, o_ref, m_ref, acc_ref, s_ref, *, tq, tk):
    qi = pl.program_id(2)
    m_ref[...] = jnp.full_like(m_ref, NEG)
    acc_ref[...] = jnp.zeros_like(acc_ref)
    nsub = tq // tk
    nfull = qi * nsub
    heads = [slice(LANES * h, LANES * (h + 1)) for h in range(2)]

    def scores(j, r0, sl):
        k0 = pl.multiple_of(j * tk, tk)
        return _dot_nt(q_ref[0, r0:tq, sl], k_ref[0, pl.ds(k0, tk), sl])

    def consume(s, h, j, r0, masked):
        k0 = pl.multiple_of(j * tk, tk)
        if masked:
            s = jnp.where(_iota((tq - r0, tk), 0) >= _iota((tq - r0, tk), 1), s, NEG)
        chunks = [s[:, LANES * c:LANES * (c + 1)] for c in range(tk // LANES)]
        m_prev = m_ref[h, r0:tq, :]
        m_new = jnp.maximum(m_prev, jnp.max(functools.reduce(jnp.maximum, chunks), -1, keepdims=True))
        p = jnp.concatenate([jnp.exp(c - m_new) for c in chunks], 1).astype(BF16)
        acc_ref[h, r0:tq, :] = (jnp.exp(m_prev - m_new) * acc_ref[h, r0:tq, :]
                                + _dot(p, v_ref[0, pl.ds(k0, tk), heads[h]]))
        m_ref[h, r0:tq, :] = m_new

    assert nsub == 2
    for h in range(2):
        s_ref[0, h] = scores(0, 0, heads[h])

    def body(i, carry):
        j = 2 * i
        for b in range(2):
            for h in range(2):
                s_ref[1 - b, h] = scores(j + b + 1, 0, heads[h])
            for h in range(2):
                consume(s_ref[b, h], h, j + b, 0, False)
        return carry

    lax.fori_loop(0, qi, body, 0)
    for h in range(2):
        consume(s_ref[0, h], h, nfull, 0, True)
    for d in range(1, nsub):
        for h in range(2):
            consume(scores(nfull + d, d * tk, heads[h]), h, nfull + d, d * tk, True)

    lane = _iota((1, LANES), 1)
    o0 = acc_ref[0]
    o1 = acc_ref[1]
    o0 = o0 / o0[:, HEAD_DIM:HEAD_DIM + 1]
    o1 = o1 / o1[:, HEAD_DIM:HEAD_DIM + 1]
    o_ref[0] = jnp.where(lane < HEAD_DIM, o0, pltpu.roll(o1, HEAD_DIM, 1)).astype(o_ref.dtype)


def _flash(q_all, k_all, v_all, tq=1024, tk=512):
    b, s, _ = q_all.shape
    npair = (MLA_HEADS + FOX_HEADS) // 2
    return pl.pallas_call(
        functools.partial(_flash_kernel, tq=tq, tk=tk),
        grid=(b, npair, s // tq),
        in_specs=[pl.BlockSpec((1, tq, 2 * LANES), lambda bi, p, qi: (bi, qi, p)),
                  pl.BlockSpec((1, s, 2 * LANES), lambda bi, p, qi: (bi, 0, p)),
                  pl.BlockSpec((1, s, 2 * LANES), lambda bi, p, qi: (bi, 0, p))],
        out_specs=pl.BlockSpec((1, tq, LANES), lambda bi, p, qi: (bi, qi, p)),
        out_shape=jax.ShapeDtypeStruct((b, s, npair * LANES), BF16),
        scratch_shapes=[pltpu.VMEM((2, tq, LANES), F32), pltpu.VMEM((2, tq, LANES), F32),
                        pltpu.VMEM((2, 2, tq, tk), F32)],
        compiler_params=_cparams(("parallel", "parallel", "arbitrary")),
        name="flash_attn",
    )(q_all, k_all, v_all)


def _cmp_kernel(tc_ref, posa_ref, posb_ref, wa_ref, wb_ref, w2_ref, out_ref):
    tc = tc_ref[0]
    a = _dot((tc + posa_ref[...]).astype(BF16), wa_ref[...])
    b = _dot((tc + posb_ref[...]).astype(BF16), wb_ref[...])
    pre = a + pltpu.roll(b, b.shape[0] - 1, 0)
    hid = jax.nn.silu(pre)
    out_ref[0] = _dot(hid.astype(BF16), w2_ref[...]).astype(out_ref.dtype)


def _compress(tc, wl):
    b, nc, w = tc.shape
    full = lambda a: pl.BlockSpec(a.shape, lambda bi: (0,) * a.ndim)
    consts = [wl["cmp_posa"], wl["cmp_posb"], wl["cmp_wa"], wl["cmp_wb"], wl["cmp_w2"]]
    return pl.pallas_call(
        _cmp_kernel,
        grid=(b,),
        in_specs=[pl.BlockSpec((1, nc, w), lambda bi: (bi, 0, 0))] + [full(a) for a in consts],
        out_specs=pl.BlockSpec((1, nc, LANES), lambda bi: (bi, 0, 0)),
        out_shape=jax.ShapeDtypeStruct((b, nc, LANES), BF16),
        compiler_params=_cparams(("parallel",)),
        name="nsa_compress",
    )(tc, *consts)


def _masked_softmax(s, mask):
    s = jnp.where(mask, s, NEG)
    m = jnp.max(s, -1, keepdims=True)
    p = jnp.where(mask, jnp.exp(s - m), 0.0)
    return p / jnp.maximum(jnp.sum(p, -1, keepdims=True), 1e-30)


def _nsa_kernel(nq_ref, kc_ref, ksl_ref, kwin_ref, g_ref, o_ref, m_ref, acc_ref, qst_ref, s_ref, *, tq, n_cmp):
    qi = pl.program_id(1)
    t0 = pl.multiple_of(qi * tq, tq)
    rpos = t0 + _iota((tq, 1), 0)
    lane = _iota((1, LANES), 1)
    qs = [nq_ref[0, :, LANES * h:LANES * (h + 1)] for h in range(NSA_HEADS)]

    dist_i = rpos - (NSA_CMP_STRIDE * lane + NSA_CMP_BLOCK - 1)
    valid_c = (dist_i >= 0) & (lane < n_cmp)
    dist_c = dist_i.astype(F32)
    kc = kc_ref[0]
    psum = jnp.zeros((tq, LANES), F32)
    o_cmp = []
    for h in range(NSA_HEADS):
        q_head = jnp.where(lane < HEAD_DIM, qs[h], jnp.zeros_like(qs[h]))
        p = _masked_softmax(_dot_nt(q_head, kc) - ALIBI[h] * dist_c, valid_c)
        psum = psum + p
        o_cmp.append(_dot(p.astype(BF16), kc))

    n_i = _iota((LANES, LANES), 0)
    j_i = _iota((LANES, LANES), 1)
    ov = ((NSA_CMP_STRIDE * n_i < NSA_SEL_BLOCK * (j_i + 1))
          & (NSA_CMP_STRIDE * n_i + NSA_CMP_BLOCK > NSA_SEL_BLOCK * j_i)
          & (n_i < n_cmp))
    imp = _dot_exact_rhs(psum, jnp.where(ov, 1.0, 0.0).astype(BF16))
    cur = jnp.right_shift(rpos, 6)
    forced = (lane == 0) | (lane == cur) | (lane == cur - 1)
    future = lane * NSA_SEL_BLOCK > rpos
    n_blk = ksl_ref.shape[1] // NSA_SEL_BLOCK
    work = jnp.where(forced, NSA_FORCE_SCORE, jnp.where(future, -1.0, imp))
    work = jnp.where(lane < n_blk, work, -jnp.inf)
    work_t = work.T[0:n_blk, :]
    blk_id = _iota((n_blk, 1), 0)
    rank = jnp.zeros((n_blk, tq), F32)
    for jp in range(n_blk):
        other = work_t[jp:jp + 1, :]
        beats = (other > work_t) | ((other == work_t) & (jp < blk_id))
        rank = rank + jnp.where(beats, 1.0, 0.0)
    sel_t = jnp.where(rank < NSA_SEL_TOPN, 1.0, 0.0)
    sel = jnp.concatenate([sel_t, jnp.zeros((LANES - n_blk, tq), F32)], 0).T > 0.5
    sel_lanes = pltpu.roll(jnp.where(sel, 1.0, 0.0), NSA_SEL_LANE, 1)
    in_sel = (lane >= NSA_SEL_LANE) & (lane < NSA_SEL_LANE + n_blk)
    sel_bias = jnp.where(in_sel, (sel_lanes - 1.0) * NSA_MASK_BIG, 0.0)
    for h in range(NSA_HEADS):
        qst_ref[h * tq:(h + 1) * tq, :] = (qs[h].astype(F32) + sel_bias).astype(BF16)
    m_ref[...] = jnp.full_like(m_ref, NEG)
    acc_ref[...] = jnp.zeros_like(acc_ref)

    def online(s, v):
        chunks = [s[:, LANES * i:LANES * (i + 1)] for i in range(s.shape[1] // LANES)]
        m_prev = m_ref[...]
        m_new = jnp.maximum(m_prev, jnp.max(functools.reduce(jnp.maximum, chunks), -1, keepdims=True))
        p = jnp.concatenate([jnp.exp(ch - m_new) for ch in chunks], 1).astype(BF16)
        acc_ref[...] = jnp.exp(m_prev - m_new) * acc_ref[...] + _dot(p, v)
        m_ref[...] = m_new

    def scores(c):
        return _dot_nt(qst_ref[...], ksl_ref[0, pl.ds(pl.multiple_of(c * tq, tq), tq), 0:LANES])

    def values(c):
        return ksl_ref[0, pl.ds(pl.multiple_of(c * tq, tq), tq), LANES:2 * LANES]

    s_ref[0] = scores(0)

    def chunk_pair(i, carry):
        j = 2 * i
        s_ref[1] = scores(j + 1)
        online(s_ref[0], values(j))
        s_ref[0] = scores(j + 2)
        online(s_ref[1], values(j + 1))
        return carry

    lax.fori_loop(0, jnp.right_shift(qi, 1), chunk_pair, 0)

    @pl.when((qi & 1) == 1)
    def _():
        online(s_ref[0], values(qi - 1))
        s_ref[0] = scores(qi)

    nrow = NSA_HEADS * tq
    causal = (_iota((nrow, tq), 0) & (tq - 1)) >= _iota((nrow, tq), 1)
    online(jnp.where(causal, s_ref[0], NEG), values(qi))

    wlen = tq + NSA_WINDOW
    w0 = pl.multiple_of(jnp.maximum(t0 - NSA_WINDOW, 0), tq)
    k_w = kwin_ref[0, pl.ds(w0, wlen), 0:LANES]
    v_w = kwin_ref[0, pl.ds(w0, wlen), LANES:2 * LANES]
    d_w = (t0 - w0) + _iota((tq, wlen), 0) - _iota((tq, wlen), 1)
    keep_w = (d_w >= 0) & (d_w < NSA_WINDOW)

    gates = jax.nn.sigmoid(g_ref[...])
    for h in range(NSA_HEADS):
        s_w = jnp.where(keep_w, _dot_nt(qs[h], k_w), NEG)
        p_w = jnp.exp(s_w - jnp.max(s_w, -1, keepdims=True))
        o_win = _dot(p_w.astype(BF16), v_w)
        o_win = o_win / o_win[:, HEAD_DIM:HEAD_DIM + 1]
        o_sel = acc_ref[h * tq:(h + 1) * tq, :]
        o_sel = o_sel / o_sel[:, HEAD_DIM:HEAD_DIM + 1]
        g0 = NSAG_LANE + 3 * h
        o = (gates[:, g0:g0 + 1] * pltpu.roll(o_cmp[h], HEAD_DIM, 1) + gates[:, g0 + 1:g0 + 2] * o_sel
             + gates[:, g0 + 2:g0 + 3] * o_win)
        o_ref[0, :, LANES * h:LANES * (h + 1)] = jnp.where(lane < HEAD_DIM, o, 0.0).astype(o_ref.dtype)


def _nsa(nq, kcvc, nsl, nwin_pad, za, tq=256):
    b, s, _ = nq.shape
    nq_t = s // tq
    n_cmp = (s - NSA_CMP_BLOCK) // NSA_CMP_STRIDE + 1
    return pl.pallas_call(
        functools.partial(_nsa_kernel, tq=tq, n_cmp=n_cmp),
        grid=(b, nq_t),
        in_specs=[pl.BlockSpec((1, tq, NSA_HEADS * LANES), lambda bi, qi: (bi, qi, 0)),
                  pl.BlockSpec((1, kcvc.shape[1], LANES), lambda bi, qi: (bi, 0, 0)),
                  pl.BlockSpec((1, s, 2 * LANES), lambda bi, qi: (bi, 0, 0)),
                  pl.BlockSpec((1, s, 2 * LANES), lambda bi, qi: (bi, 0, 0)),
                  pl.BlockSpec((tq, LANES), lambda bi, qi: (bi * nq_t + qi, (ZA_W - LANES) // LANES))],
        out_specs=pl.BlockSpec((1, tq, NSA_HEADS * LANES), lambda bi, qi: (bi, qi, 0)),
        out_shape=jax.ShapeDtypeStruct((b, s, NSA_HEADS * LANES), BF16),
        scratch_shapes=[pltpu.VMEM((NSA_HEADS * tq, LANES), F32), pltpu.VMEM((NSA_HEADS * tq, LANES), F32),
                        pltpu.VMEM((NSA_HEADS * tq, LANES), BF16), pltpu.VMEM((2, NSA_HEADS * tq, tq), F32)],
        compiler_params=_cparams(("parallel", "arbitrary")),
        name="nsa_attn",
    )(nq, kcvc, nsl, nwin_pad, za)


def _layer_norm(y, g, b):
    mu = jnp.mean(y, -1, keepdims=True)
    yc = y - mu
    var = jnp.mean(yc * yc, -1, keepdims=True)
    return yc * lax.rsqrt(var + NORM_EPS) * g + b


ROUTE_ROWS = 512


def _out_proj_kernel(*refs, routed):
    if routed:
        (oa_ref, on_ref, x_ref, wa_ref, wb_ref, g_ref, b_ref, rw_ref,
         o_ref, gate_ref, pos_ref, post_ref, cnt_ref) = refs
    else:
        oa_ref, on_ref, x_ref, wa_ref, wb_ref, g_ref, b_ref, o_ref = refs
    mix = _dot(oa_ref[...], wa_ref[...]) + _dot(on_ref[...], wb_ref[...])
    x1 = _layer_norm(DEEPNORM_ALPHA * x_ref[...] + mix, g_ref[...], b_ref[...])
    o_ref[...] = x1
    if routed:
        tm = x1.shape[0]
        lane = _iota((1, LANES), 1)
        lane_f = lane.astype(F32)
        logits = jnp.where(lane < N_EXPERTS, _dot_split(x1, rw_ref[...]), NEG)
        ex = jnp.exp(logits - jnp.max(logits, -1, keepdims=True))
        probs = ex / jnp.sum(ex, -1, keepdims=True)
        p1 = jnp.max(probs, -1, keepdims=True)
        i1 = jnp.min(jnp.where(probs == p1, lane_f, float(LANES)), -1, keepdims=True)
        rest = jnp.where(lane_f == i1, -1.0, probs)
        p2 = jnp.max(rest, -1, keepdims=True)
        i2 = jnp.min(jnp.where(rest == p2, lane_f, float(LANES)), -1, keepdims=True)
        tot = p1 + p2
        gate_ref[...] = jnp.where(lane_f == i1, p1 / tot, jnp.where(lane_f == i2, p2 / tot, 0.0))
        chosen = (lane_f == i1) | (lane_f == i2)
        before = (_iota((tm, tm), 0) > _iota((tm, tm), 1)).astype(BF16)
        onehot = jnp.where(chosen, 1.0, 0.0)
        slot = _dot(before, onehot.astype(BF16))
        posm = jnp.where(chosen, slot, -1.0)
        pos_ref[...] = posm
        post_ref[...] = posm.T[0:N_EXPERTS, :]
        cnt_ref[...] = jnp.broadcast_to(jnp.sum(onehot, 0, keepdims=True), (8, LANES)).astype(jnp.int32)


def _out_proj(o_attn, o_nsa, x2, wl, router=None, tm=ROUTE_ROWS):
    n = x2.shape[0]
    routed = router is not None
    full = lambda a: pl.BlockSpec(a.shape, lambda i: (0,) * a.ndim)
    row = lambda w: pl.BlockSpec((tm, w), lambda i: (i, 0))
    consts = [wl["w_out_a"], wl["w_out_b"], wl["ln1_g"], wl["ln1_b"]] + ([router] if routed else [])
    out_specs = [row(D_MODEL)]
    out_shape = [jax.ShapeDtypeStruct((n, D_MODEL), F32)]
    if routed:
        out_specs += [row(LANES), row(LANES), pl.BlockSpec((N_EXPERTS, tm), lambda i: (i, 0)),
                      pl.BlockSpec((8, LANES), lambda i: (i, 0))]
        out_shape += [jax.ShapeDtypeStruct((n, LANES), F32), jax.ShapeDtypeStruct((n, LANES), F32),
                      jax.ShapeDtypeStruct((n // tm * N_EXPERTS, tm), F32),
                      jax.ShapeDtypeStruct((n // tm * 8, LANES), jnp.int32)]
    outs = pl.pallas_call(
        functools.partial(_out_proj_kernel, routed=routed),
        grid=(n // tm,),
        in_specs=[row(o_attn.shape[1]), row(o_nsa.shape[1]), row(D_MODEL)] + [full(a) for a in consts],
        out_specs=out_specs,
        out_shape=out_shape,
        compiler_params=_cparams(("parallel",)),
        name="out_proj_route_ln" if routed else "out_proj_ln",
    )(o_attn, o_nsa, x2, *consts)
    return outs if routed else outs[0]


def _ffn_kernel(x_ref, w1_ref, w3_ref, w2_ref, g_ref, b_ref, o_ref, acc_ref, xb_ref):
    c = pl.program_id(1)

    @pl.when(c == 0)
    def _():
        acc_ref[...] = jnp.zeros_like(acc_ref)
        xb_ref[...] = x_ref[...].astype(BF16)

    xb = xb_ref[...]
    a = jax.nn.silu(_dot(xb, w1_ref[...])) * _dot(xb, w3_ref[...])
    acc_ref[...] += _dot(a.astype(BF16), w2_ref[...])

    @pl.when(c == pl.num_programs(1) - 1)
    def _():
        o_ref[...] = _layer_norm(DEEPNORM_ALPHA * x_ref[...] + acc_ref[...], g_ref[...], b_ref[...])


def _ffn(x2, w1, w3, w2, ln_g, ln_b, tm=512, nchunk=2):
    n = x2.shape[0]
    tf = w1.shape[1] // nchunk
    full = lambda a: pl.BlockSpec(a.shape, lambda i, c: (0,) * a.ndim)
    return pl.pallas_call(
        _ffn_kernel,
        grid=(n // tm, nchunk),
        in_specs=[pl.BlockSpec((tm, D_MODEL), lambda i, c: (i, 0)),
                  pl.BlockSpec((D_MODEL, tf), lambda i, c: (0, c)),
                  pl.BlockSpec((D_MODEL, tf), lambda i, c: (0, c)),
                  pl.BlockSpec((tf, D_MODEL), lambda i, c: (c, 0)),
                  full(ln_g), full(ln_b)],
        out_specs=pl.BlockSpec((tm, D_MODEL), lambda i, c: (i, 0)),
        out_shape=jax.ShapeDtypeStruct((n, D_MODEL), F32),
        scratch_shapes=[pltpu.VMEM((tm, D_MODEL), F32), pltpu.VMEM((tm, D_MODEL), BF16)],
        compiler_params=_cparams(("parallel", "arbitrary")),
        name="ffn_ln",
    )(x2, w1, w3, w2, ln_g, ln_b)


MOE_CAP = 160
MOE_CAP_PAD = -(-MOE_CAP // LANES) * LANES
MOE_CHUNKS = -(-ROUTE_ROWS // MOE_CAP)


def _moe_kernel(cnt_ref, x_ref, gate_ref, pos_ref, *rest, groups):
    post_refs = rest[:groups]
    w1_ref, w3_ref, w2_ref, g_ref, b_ref, o_ref, acc_ref, xb_ref = rest[groups:]
    i = pl.program_id(0)
    e = pl.program_id(1)
    lane = _iota((1, LANES), 1)

    @pl.when(e == 0)
    def _():
        acc_ref[...] = jnp.zeros_like(acc_ref)
        xb_ref[...] = x_ref[...].astype(BF16)

    for gi in range(groups):
        rows = slice(ROUTE_ROWS * gi, ROUTE_ROWS * (gi + 1))
        cnt = cnt_ref[(i * groups + gi) * N_EXPERTS + e]
        gate_e = jnp.sum(jnp.where(lane == e, gate_ref[rows, :], 0.0), -1, keepdims=True)
        slot_col = jnp.sum(jnp.where(lane == e, pos_ref[rows, :], 0.0), -1, keepdims=True)
        slot_row = post_refs[gi][0]
        for k in range(MOE_CHUNKS):
            @pl.when(cnt > k * MOE_CAP)
            def _():
                want = (_iota((MOE_CAP, 1), 0) + k * MOE_CAP).astype(F32)
                pick = jnp.where(slot_row == want, 1.0, 0.0).astype(BF16)
                xg = _dot(pick, xb_ref[rows, :]).astype(BF16)
                a = jax.nn.silu(_dot(xg, w1_ref[0])) * _dot(xg, w3_ref[0])
                y = _dot(a.astype(BF16), w2_ref[0])
                col = _iota((1, MOE_CAP_PAD), 1)
                put = jnp.where((slot_col == (col + k * MOE_CAP).astype(F32)) & (col < MOE_CAP),
                                1.0, 0.0).astype(BF16)
                yb = y.astype(BF16)
                if MOE_CAP_PAD > MOE_CAP:
                    yb = jnp.concatenate([yb, jnp.zeros((MOE_CAP_PAD - MOE_CAP, D_MODEL), BF16)], 0)
                acc_ref[rows, :] += gate_e * _dot(put, yb)

    @pl.when(e == pl.num_programs(1) - 1)
    def _():
        o_ref[...] = _layer_norm(DEEPNORM_ALPHA * x_ref[...] + acc_ref[...], g_ref[...], b_ref[...])


def _moe(x2, gates, pos, post, cnt, w1, w3, w2, ln_g, ln_b, groups=2):
    n = x2.shape[0]
    ne, _, tf = w1.shape
    tm = groups * ROUTE_ROWS
    post3 = post.reshape(n // ROUTE_ROWS * N_EXPERTS, 1, ROUTE_ROWS)
    full = lambda a: pl.BlockSpec(a.shape, lambda i, e, c: (0,) * a.ndim)
    row = lambda w: pl.BlockSpec((tm, w), lambda i, e, c: (i, 0))
    post_spec = lambda gi: pl.BlockSpec((1, 1, ROUTE_ROWS),
                                        lambda i, e, c: ((i * groups + gi) * N_EXPERTS + e, 0, 0))
    grid_spec = pltpu.PrefetchScalarGridSpec(
        num_scalar_prefetch=1,
        grid=(n // tm, ne),
        in_specs=[row(D_MODEL), row(LANES), row(LANES)] + [post_spec(gi) for gi in range(groups)]
                 + [pl.BlockSpec((1, D_MODEL, tf), lambda i, e, c: (e, 0, 0)),
                    pl.BlockSpec((1, D_MODEL, tf), lambda i, e, c: (e, 0, 0)),
                    pl.BlockSpec((1, tf, D_MODEL), lambda i, e, c: (e, 0, 0)),
                    full(ln_g), full(ln_b)],
        out_specs=row(D_MODEL),
        scratch_shapes=[pltpu.VMEM((tm, D_MODEL), F32), pltpu.VMEM((tm, D_MODEL), BF16)],
    )
    return pl.pallas_call(
        functools.partial(_moe_kernel, groups=groups),
        grid_spec=grid_spec,
        out_shape=jax.ShapeDtypeStruct((n, D_MODEL), F32),
        compiler_params=_cparams(("parallel", "arbitrary")),
        name="moe_top2_ln",
    )(cnt, x2, gates, pos, *([post3] * groups), w1, w3, w2, ln_g, ln_b)


def _in_proj_columns():
    src = np.full((IN_P,), -1, np.int64)
    scale = np.ones((IN_P,), np.float32)
    o_cq, o_ckv, o_kr = 0, MLA_Q_RANK, MLA_Q_RANK + MLA_KV_RANK
    o_fox = o_kr + MLA_ROPE
    o_foxf = o_fox + 3 * FOX_HEADS * HEAD_DIM
    o_nq = o_foxf + FOX_HEADS
    o_nkv = o_nq + NSA_HEADS * HEAD_DIM
    o_ng = o_nkv + 6 * HEAD_DIM
    src[0:o_kr] = np.arange(o_kr)
    small = o_kr
    src[small + KR_LANE:small + KR_LANE + MLA_ROPE] = o_kr + np.arange(MLA_ROPE)
    src[small + FOXF_LANE:small + FOXF_LANE + FOX_HEADS] = o_foxf + np.arange(FOX_HEADS)
    src[small + NSAG_LANE:small + NSAG_LANE + 3 * NSA_HEADS] = o_ng + np.arange(3 * NSA_HEADS)
    n_fox = FOX_HEADS * HEAD_DIM
    src[FQ_OFF:FQ_OFF + 3 * n_fox] = o_fox + np.arange(3 * n_fox)
    scale[FQ_OFF:FQ_OFF + n_fox] = HEAD_DIM ** -0.5
    src[NQ_OFF:NQ_OFF + NSA_HEADS * HEAD_DIM] = o_nq + np.arange(NSA_HEADS * HEAD_DIM)
    scale[NQ_OFF:NQ_OFF + NSA_HEADS * HEAD_DIM] = HEAD_DIM ** -0.5
    src[NC_OFF:NC_OFF + 6 * HEAD_DIM] = o_nkv + np.arange(6 * HEAD_DIM)
    return src, scale


def _gather_cols(w, src, scale=None):
    scale = np.ones(src.shape, np.float32) if scale is None else scale
    parts, i, n = [], 0, len(src)
    while i < n:
        j = i + 1
        while j < n and scale[j] == scale[i] and (src[j] == src[j - 1] + 1 if src[i] >= 0 else src[j] < 0):
            j += 1
        if src[i] < 0:
            parts.append(jnp.zeros(w.shape[:-1] + (j - i,), w.dtype))
        else:
            seg = w[..., int(src[i]):int(src[i]) + (j - i)]
            parts.append(seg if scale[i] == 1.0 else seg * float(scale[i]))
        i = j
    return jnp.concatenate(parts, axis=-1)


def _tables(s):
    half = MLA_ROPE // 2
    freqs = ROPE_THETA ** (-jnp.arange(half, dtype=F32) / half)
    ang = jnp.arange(s).astype(F32)[:, None] * freqs[None, :]
    cos, sin = jnp.cos(ang), jnp.sin(ang)
    z = lambda w: jnp.zeros((s, w), F32)
    tabs = {
        "cos": jnp.concatenate([jnp.ones((s, MLA_NOPE), F32), cos, cos, z(LANES - MLA_NOPE - MLA_ROPE)], 1),
        "sina": jnp.concatenate([z(MLA_NOPE), -sin, z(half), z(LANES - MLA_NOPE - MLA_ROPE)], 1),
        "sinb": jnp.concatenate([z(MLA_NOPE), z(half), sin, z(LANES - MLA_NOPE - MLA_ROPE)], 1),
    }
    pq = np.zeros((LANES, FOX_HEADS * LANES), np.float32)
    pk = np.zeros((LANES, FOX_HEADS * LANES), np.float32)
    oneq = np.zeros((1, FOX_HEADS * LANES), np.float32)
    onek = np.zeros((1, FOX_HEADS * LANES), np.float32)
    for h in range(FOX_HEADS):
        for t in range(3):
            pq[FOXF_LANE + FOX_HEADS * t + h, LANES * h + HEAD_DIM + t] = 1.0
            pk[FOXF_LANE + FOX_HEADS * t + h, LANES * h + HEAD_DIM + 3 + t] = -1.0
            oneq[0, LANES * h + HEAD_DIM + 3 + t] = 1.0
            onek[0, LANES * h + HEAD_DIM + t] = 1.0
    tabs.update(pq=jnp.asarray(pq, BF16), pk=jnp.asarray(pk, BF16), oneq=jnp.asarray(oneq), onek=jnp.asarray(onek))
    qaux = np.zeros((1, NSA_HEADS * LANES), np.float32)
    for h in range(NSA_HEADS):
        qaux[0, LANES * h + NSA_AUX_LANE] = ALIBI[h] * NSA_SEL_BLOCK
        qaux[0, LANES * h + NSA_AUX_LANE + 1] = ALIBI[h]
    kaux = np.zeros((s, LANES), np.float32)
    pos = np.arange(s)
    kaux[:, NSA_AUX_LANE] = pos // NSA_SEL_BLOCK
    kaux[:, NSA_AUX_LANE + 1] = pos % NSA_SEL_BLOCK
    kaux[pos, NSA_SEL_LANE + pos // NSA_SEL_BLOCK] = 1.0
    tabs.update(nsa_qaux=jnp.asarray(qaux), nsa_kaux=jnp.asarray(kaux))
    return tabs


def _layer_weights(p, l):
    qd = MLA_NOPE + MLA_ROPE
    src_q = np.full((MLA_HEADS * LANES,), -1, np.int64)
    src_k = np.full((MLA_HEADS * LANES,), -1, np.int64)
    src_v = np.full((MLA_HEADS * LANES,), -1, np.int64)
    for h in range(MLA_HEADS):
        src_q[LANES * h + np.arange(qd)] = qd * h + np.arange(qd)
        src_k[LANES * h + np.arange(MLA_NOPE)] = 2 * HEAD_DIM * h + np.arange(MLA_NOPE)
        src_v[LANES * h + np.arange(HEAD_DIM)] = 2 * HEAD_DIM * h + MLA_NOPE + np.arange(HEAD_DIM)
    w_out = p["w_out"][l]
    n_attn = (MLA_HEADS + FOX_HEADS) * HEAD_DIM
    wb = w_out[n_attn:].reshape(NSA_HEADS, HEAD_DIM, D_MODEL)
    wb = jnp.concatenate([wb, jnp.zeros_like(wb)], axis=1).reshape(NSA_HEADS * LANES, D_MODEL)
    bf = jnp.zeros((1, LANES), F32).at[0, FOXF_LANE:FOXF_LANE + FOX_HEADS].set(p["b_forget"][l])

    half = NSA_CMP_BLOCK // 2
    kpos, vpos = p["cmp_k_pos"][l], p["cmp_v_pos"][l]
    posa = jnp.concatenate([kpos[:half], vpos[:half]], -1).reshape(1, half * LANES)
    posb = jnp.concatenate([kpos[half:], vpos[half:]], -1).reshape(1, half * LANES)

    def w1_half(lo):
        k = p["cmp_k_w1"][l].reshape(NSA_CMP_BLOCK, HEAD_DIM, NSA_CMP_HIDDEN)[lo:lo + half]
        v = p["cmp_v_w1"][l].reshape(NSA_CMP_BLOCK, HEAD_DIM, NSA_CMP_HIDDEN)[lo:lo + half]
        zk = jnp.zeros_like(k)
        top = jnp.concatenate([k, zk], -1)
        bot = jnp.concatenate([zk, v], -1)
        return jnp.concatenate([top, bot], 1).reshape(half * LANES, 2 * NSA_CMP_HIDDEN).astype(BF16)

    zw2 = jnp.zeros((NSA_CMP_HIDDEN, HEAD_DIM), F32)
    w2 = jnp.concatenate([jnp.concatenate([p["cmp_k_w2"][l], zw2], 1),
                          jnp.concatenate([zw2, p["cmp_v_w2"][l]], 1)], 0).astype(BF16)
    return {
        "g_cq": p["g_cq"][l][None, :], "g_ckv": p["g_ckv"][l][None, :],
        "w_uq": _gather_cols(p["w_uq"][l], src_q).astype(BF16),
        "w_uk": _gather_cols(p["w_ukv"][l], src_k).astype(BF16),
        "w_uv": _gather_cols(p["w_ukv"][l], src_v).astype(BF16),
        "b_forget": bf,
        "cmp_posa": posa, "cmp_posb": posb, "cmp_wa": w1_half(0), "cmp_wb": w1_half(half), "cmp_w2": w2,
        "w_out_a": w_out[:n_attn].astype(BF16), "w_out_b": wb.astype(BF16),
        "ln1_g": p["ln1_g"][l][None, :], "ln1_b": p["ln1_b"][l][None, :],
        "ln2_g": p["ln2_g"][l][None, :], "ln2_b": p["ln2_b"][l][None, :],
    }


def kernel(x, w_in, b_forget, g_cq, w_uq, g_ckv, w_ukv, cmp_k_pos, cmp_k_w1, cmp_k_w2, cmp_v_pos, cmp_v_w1,
           cmp_v_w2, w_out, ln1_g, ln1_b, ln2_g, ln2_b, ffn_w1, ffn_w3, ffn_w2, router_w, moe_w1, moe_w3,
           moe_w2):
    b, s, d = x.shape
    assert d == D_MODEL and s % 512 == 0 and s // NSA_CMP_STRIDE == LANES, (b, s, d)
    p = dict(b_forget=b_forget, g_cq=g_cq, w_uq=w_uq, g_ckv=g_ckv, w_ukv=w_ukv, cmp_k_pos=cmp_k_pos,
             cmp_k_w1=cmp_k_w1, cmp_k_w2=cmp_k_w2, cmp_v_pos=cmp_v_pos, cmp_v_w1=cmp_v_w1, cmp_v_w2=cmp_v_w2,
             w_out=w_out, ln1_g=ln1_g, ln1_b=ln1_b, ln2_g=ln2_g, ln2_b=ln2_b)
    src, scale = _in_proj_columns()
    w_in_p = _gather_cols(w_in, src, scale).astype(BF16)
    tabs = _tables(s)
    n = b * s
    x2 = x.reshape(n, d)
    for l in range(DEPTH):
        wl = _layer_weights(p, l)
        za, fq, fk, fv, nq, nc, nsl, nwin = _in_proj(x2, w_in_p[l], tabs, s)
        q_all, k_all, v_all = _prep(za, fq, fk, fv, wl, tabs, b, s)
        o_attn = _flash(q_all, k_all, v_all)
        kcvc = _compress(nc.reshape(b, s // NSA_CMP_STRIDE, NSA_CMP_STRIDE * LANES), wl)
        o_nsa = _nsa(nq.reshape(b, s, NSA_HEADS * LANES), kcvc, nsl.reshape(b, s, 2 * LANES),
                     nwin.reshape(b, s, 2 * LANES), za)
        j = l // 2
        if l % 2 == 0:
            x2 = _out_proj(o_attn.reshape(n, -1), o_nsa.reshape(n, -1), x2, wl)
            x2 = _ffn(x2, ffn_w1[j].astype(BF16), ffn_w3[j].astype(BF16), ffn_w2[j].astype(BF16),
                      wl["ln2_g"], wl["ln2_b"])
        else:
            rw = jnp.pad(router_w[j], ((0, 0), (0, LANES - N_EXPERTS)))
            rw_hi = rw.astype(BF16)
            rw_lo = (rw - rw_hi.astype(F32)).astype(BF16)
            rw = jnp.concatenate([jnp.concatenate([rw_hi, rw_lo], 1),
                                  jnp.concatenate([rw_hi, jnp.zeros_like(rw_lo)], 1)], 0)
            x2, gates, pos, post, cnt = _out_proj(o_attn.reshape(n, -1), o_nsa.reshape(n, -1), x2, wl, router=rw)
            cnt = cnt[::8, :N_EXPERTS].reshape(-1)
            x2 = _moe(x2, gates, pos, post, cnt, moe_w1[j].astype(BF16), moe_w3[j].astype(BF16),
                      moe_w2[j].astype(BF16), wl["ln2_g"], wl["ln2_b"])
    return x2.reshape(b, s, d)
```

```python
import functools

import numpy as np
import jax
import jax.numpy as jnp
from jax import lax
from jax.experimental import pallas as pl
from jax.experimental.pallas import tpu as pltpu

F32 = jnp.float32
BF16 = jnp.bfloat16

D_MODEL = 1024
HEAD_DIM = 64
LANES = 128
MLA_HEADS = 6
MLA_Q_RANK = 384
MLA_KV_RANK = 256
MLA_NOPE = 64
MLA_ROPE = 32
ROPE_THETA = 10000.0
FOX_HEADS = 6
NSA_HEADS = 4
NSA_CMP_BLOCK = 32
NSA_CMP_STRIDE = 16
NSA_CMP_HIDDEN = 128
NSA_SEL_BLOCK = 64
NSA_SEL_TOPN = 8
NSA_WINDOW = 256
NSA_FORCE_SCORE = 1.0e4
N_EXPERTS = 8
NORM_EPS = 1e-5
DEPTH = 2
DEEPNORM_ALPHA = (2 * DEPTH) ** 0.25
NEG = -1e30
VMEM_LIMIT = 56 * 1024 * 1024

KR_LANE = 64
FOXF_LANE = 96
NSAG_LANE = 102

ZA_W = 768
FQ_OFF = ZA_W
FK_OFF = FQ_OFF + 384
FV_OFF = FK_OFF + 384
NQ_OFF = FV_OFF + 384
NC_OFF = NQ_OFF + 256
NKV_OFF = NC_OFF + 128
IN_P = NKV_OFF + 256

NSA_AUX_LANE = HEAD_DIM
NSA_SEL_LANE = HEAD_DIM + 2
NSA_MASK_BIG = 2.0 ** 126

ALIBI = tuple(2.0 ** (-8.0 * (i + 1) / NSA_HEADS) for i in range(NSA_HEADS))


def _dot(a, b, **kw):
    return jnp.dot(a, b, preferred_element_type=F32, **kw)


def _dot_nt(a, b):
    return lax.dot_general(a, b, (((1,), (1,)), ((), ())), preferred_element_type=F32)


def _iota(shape, dim):
    return lax.broadcasted_iota(jnp.int32, shape, dim)


def _split3(x):
    hi = x.astype(BF16)
    r = x - hi.astype(F32)
    mid = r.astype(BF16)
    return hi, mid, (r - mid.astype(F32)).astype(BF16)


def _dot_exact_rhs(x, w01):
    hi, mid, lo = _split3(x)
    return _dot(jnp.concatenate([hi, mid, lo], 1), jnp.concatenate([w01, w01, w01], 0))


def _dot_split(x, w_stack):
    hi = x.astype(BF16)
    lo = (x - hi.astype(F32)).astype(BF16)
    r = _dot(jnp.concatenate([hi, lo], 1), w_stack)
    half = r.shape[1] // 2
    return r[:, :half] + r[:, half:]


def _cparams(sem):
    return pltpu.CompilerParams(dimension_semantics=sem, vmem_limit_bytes=VMEM_LIMIT)


def _front_kernel(x_ref, w_ref, qaux_ref, kaux_ref, gcq_ref, wuq_ref, gckv_ref, wuk_ref, wuv_ref, bf_ref,
                  cos_ref, sina_ref, sinb_ref, pq_ref, pk_ref, oneq_ref, onek_ref,
                  small_ref, nq_ref, nc_ref, nsl_ref, nwin_ref, q_out, k_out, v_out, carry_ref,
                  *, tm, tiles_per_seq):
    @pl.when(pl.program_id(0) % tiles_per_seq == 0)
    def _():
        carry_ref[...] = jnp.zeros_like(carry_ref)

    xb = x_ref[...].astype(BF16)

    def mm(a, b):
        return _dot(xb, w_ref[:, a:b])

    lane = _iota((1, LANES), 1)

    def halves(pair):
        return (jnp.where(lane < HEAD_DIM, pair, 0.0),
                jnp.where(lane < HEAD_DIM, pltpu.roll(pair, HEAD_DIM, 1), 0.0))

    def expand(z):
        return [blk for pr in range(z.shape[1] // LANES) for blk in halves(z[:, LANES * pr:LANES * (pr + 1)])]

    def pad_v(blk):
        return jnp.where(lane < HEAD_DIM, blk, jnp.where(lane == HEAD_DIM, 1.0, 0.0)).astype(BF16)

    cos = cos_ref[...]
    sina = sina_ref[...]
    sinb = sinb_ref[...]

    def rope(blk):
        return blk * cos + pltpu.roll(blk, LANES - 16, 1) * sina + pltpu.roll(blk, 16, 1) * sinb

    za = mm(0, ZA_W)
    cq = za[:, 0:MLA_Q_RANK]
    ckv = za[:, MLA_Q_RANK:MLA_Q_RANK + MLA_KV_RANK]
    small = za[:, MLA_Q_RANK + MLA_KV_RANK:ZA_W]
    small_ref[...] = small

    xn = cq * lax.rsqrt(jnp.mean(cq * cq, -1, keepdims=True) + NORM_EPS) * gcq_ref[...]
    q = _dot(xn.astype(BF16), wuq_ref[...])
    cn = ckv * lax.rsqrt(jnp.mean(ckv * ckv, -1, keepdims=True) + NORM_EPS) * gckv_ref[...]
    cnb = cn.astype(BF16)
    kn = _dot(cnb, wuk_ref[...])
    v = _dot(cnb, wuv_ref[...])
    kr = rope(small)
    mla_scale = (MLA_NOPE + MLA_ROPE) ** -0.5
    for h in range(MLA_HEADS):
        sl = slice(LANES * h, LANES * (h + 1))
        q_out[:, sl] = (rope(q[:, sl]) * mla_scale).astype(BF16)
        k_out[:, sl] = (kn[:, sl] + kr).astype(BF16)
        v_out[:, sl] = pad_v(v[:, sl])

    fmask = (lane >= FOXF_LANE) & (lane < FOXF_LANE + FOX_HEADS)
    lf = jnp.where(fmask, jax.nn.log_sigmoid(small + bf_ref[...]), 0.0)
    tril = jnp.where(_iota((tm, tm), 0) >= _iota((tm, tm), 1), 1.0, 0.0).astype(BF16)
    lf_hi, lf_mid, lf_lo = _split3(lf)
    part = _dot(tril, jnp.concatenate([lf_hi, lf_mid], 1))
    cs = part[:, :LANES] + part[:, LANES:] + _dot(tril, lf_lo) + carry_ref[...]
    carry_ref[...] = cs[tm - 1:tm, :]
    hi, mid, lo = (t.astype(F32) for t in _split3(cs))
    c3 = (hi + pltpu.roll(mid, FOX_HEADS, 1) + pltpu.roll(lo, 2 * FOX_HEADS, 1)).astype(BF16)
    augq = _dot(c3, pq_ref[...]) + oneq_ref[...]
    augk = _dot(c3, pk_ref[...]) + onek_ref[...]
    off = MLA_HEADS * LANES
    fq = expand(mm(FQ_OFF, FK_OFF))
    fk = expand(mm(FK_OFF, FV_OFF))
    fv = expand(mm(FV_OFF, NQ_OFF))
    for h in range(FOX_HEADS):
        sl = slice(LANES * h, LANES * (h + 1))
        dst = slice(off + LANES * h, off + LANES * (h + 1))
        q_out[:, dst] = (fq[h] + augq[:, sl]).astype(BF16)
        k_out[:, dst] = (fk[h] + augk[:, sl]).astype(BF16)
        v_out[:, dst] = pad_v(fv[h])

    qaux = qaux_ref[...]
    for h, blk in enumerate(expand(mm(NQ_OFF, NC_OFF))):
        sl = slice(LANES * h, LANES * (h + 1))
        nq_ref[:, sl] = (blk + qaux[:, sl]).astype(BF16)
    nc_ref[...] = mm(NC_OFF, NKV_OFF)
    kaux = kaux_ref[...]
    nkv = mm(NKV_OFF, IN_P)
    for i, out_ref in enumerate((nsl_ref, nwin_ref)):
        k_blk, v_blk = halves(nkv[:, LANES * i:LANES * (i + 1)])
        out_ref[:, 0:LANES] = (k_blk + kaux).astype(BF16)
        out_ref[:, LANES:2 * LANES] = pad_v(v_blk)


def _front(x2, w_p, wl, tabs, s, tm=512):
    n = x2.shape[0]
    ns = s // tm
    nh = MLA_HEADS + FOX_HEADS
    full = lambda a: pl.BlockSpec(a.shape, lambda i: (0,) * a.ndim)
    tab = pl.BlockSpec((tm, LANES), lambda i: (i % ns, 0))
    consts = [wl["g_cq"], wl["w_uq"], wl["g_ckv"], wl["w_uk"], wl["w_uv"], wl["b_forget"]]
    tail = [tabs["pq"], tabs["pk"], tabs["oneq"], tabs["onek"]]
    widths = [(LANES, F32), (NSA_HEADS * LANES, BF16), (LANES, F32), (2 * LANES, BF16), (2 * LANES, BF16),
              (nh * LANES, BF16), (nh * LANES, BF16), (nh * LANES, BF16)]
    return pl.pallas_call(
        functools.partial(_front_kernel, tm=tm, tiles_per_seq=ns),
        grid=(n // tm,),
        in_specs=[pl.BlockSpec((tm, D_MODEL), lambda i: (i, 0)), full(w_p), full(tabs["nsa_qaux"]), tab]
                 + [full(a) for a in consts] + [tab, tab, tab] + [full(a) for a in tail],
        out_specs=[pl.BlockSpec((tm, w), lambda i: (i, 0)) for w, _ in widths],
        out_shape=[jax.ShapeDtypeStruct((n, w), dt) for w, dt in widths],
        scratch_shapes=[pltpu.VMEM((1, LANES), F32)],
        compiler_params=_cparams(("arbitrary",)),
        name="front_proj",
    )(x2, w_p, tabs["nsa_qaux"], tabs["nsa_kaux"], *consts, tabs["cos"], tabs["sina"], tabs["sinb"], *tail)


def _flash_kernel(q_ref, k_ref, v_ref, o_ref, m_ref, acc_ref, s_ref, *, tq, tk):
    qi = pl.program_id(2)
    m_ref[...] = jnp.full_like(m_ref, NEG)
    acc_ref[...] = jnp.zeros_like(acc_ref)
    nsub = tq // tk
    nfull = qi * nsub
    heads = [slice(LANES * h, LANES * (h + 1)) for h in range(2)]

    def scores(j, r0, sl):
        k0 = pl.multiple_of(j * tk, tk)
        return _dot_nt(q_ref[0, r0:tq, sl], k_ref[0, pl.ds(k0, tk), sl])

    def consume(s, h, j, r0, masked):
        k0 = pl.multiple_of(j * tk, tk)
        if masked:
            s = jnp.where(_iota((tq - r0, tk), 0) >= _iota((tq - r0, tk), 1), s, NEG)
        chunks = [s[:, LANES * c:LANES * (c + 1)] for c in range(tk // LANES)]
        m_prev = m_ref[h, r0:tq, :]
        m_new = jnp.maximum(m_prev, jnp.max(functools.reduce(jnp.maximum, chunks), -1, keepdims=True))
        p = jnp.concatenate([jnp.exp(c - m_new) for c in chunks], 1).astype(BF16)
        acc_ref[h, r0:tq, :] = (jnp.exp(m_prev - m_new) * acc_ref[h, r0:tq, :]
                                + _dot(p, v_ref[0, pl.ds(k0, tk), heads[h]]))
        m_ref[h, r0:tq, :] = m_new

    assert nsub == 2
    for h in range(2):
        s_ref[0, h] = scores(0, 0, heads[h])

    def body(i, carry):
        j = 2 * i
        for b in range(2):
            for h in range(2):
                s_ref[1 - b, h] = scores(j + b + 1, 0, heads[h])
            for h in range(2):
                consume(s_ref[b, h], h, j + b, 0, False)
        return carry

    lax.fori_loop(0, qi, body, 0)
    for h in range(2):
        consume(s_ref[0, h], h, nfull, 0, True)
    for d in range(1, nsub):
        for h in range(2):
            consume(scores(nfull + d, d * tk, heads[h]), h, nfull + d, d * tk, True)

    lane = _iota((1, LANES), 1)
    o0 = acc_ref[0]
    o1 = acc_ref[1]
    o0 = o0 / o0[:, HEAD_DIM:HEAD_DIM + 1]
    o1 = o1 / o1[:, HEAD_DIM:HEAD_DIM + 1]
    o_ref[0] = jnp.where(lane < HEAD_DIM, o0, pltpu.roll(o1, HEAD_DIM, 1)).astype(o_ref.dtype)


def _flash(q_all, k_all, v_all, tq=1024, tk=512):
    b, s, _ = q_all.shape
    npair = (MLA_HEADS + FOX_HEADS) // 2
    return pl.pallas_call(
        functools.partial(_flash_kernel, tq=tq, tk=tk),
        grid=(b, npair, s // tq),
        in_specs=[pl.BlockSpec((1, tq, 2 * LANES), lambda bi, p, qi: (bi, qi, p)),
                  pl.BlockSpec((1, s, 2 * LANES), lambda bi, p, qi: (bi, 0, p)),
                  pl.BlockSpec((1, s, 2 * LANES), lambda bi, p, qi: (bi, 0, p))],
        out_specs=pl.BlockSpec((1, tq, LANES), lambda bi, p, qi: (bi, qi, p)),
        out_shape=jax.ShapeDtypeStruct((b, s, npair * LANES), BF16),
        scratch_shapes=[pltpu.VMEM((2, tq, LANES), F32), pltpu.VMEM((2, tq, LANES), F32),
                        pltpu.VMEM((2, 2, tq, tk), F32)],
        compiler_params=_cparams(("parallel", "parallel", "arbitrary")),
        name="flash_attn",
    )(q_all, k_all, v_all)


def _cmp_kernel(tc_ref, posa_ref, posb_ref, wa_ref, wb_ref, w2_ref, out_ref):
    tc = tc_ref[0]
    a = _dot((tc + posa_ref[...]).astype(BF16), wa_ref[...])
    b = _dot((tc + posb_ref[...]).astype(BF16), wb_ref[...])
    pre = a + pltpu.roll(b, b.shape[0] - 1, 0)
    hid = jax.nn.silu(pre)
    out_ref[0] = _dot(hid.astype(BF16), w2_ref[...]).astype(out_ref.dtype)


def _compress(tc, wl):
    b, nc, w = tc.shape
    full = lambda a: pl.BlockSpec(a.shape, lambda bi: (0,) * a.ndim)
    consts = [wl["cmp_posa"], wl["cmp_posb"], wl["cmp_wa"], wl["cmp_wb"], wl["cmp_w2"]]
    return pl.pallas_call(
        _cmp_kernel,
        grid=(b,),
        in_specs=[pl.BlockSpec((1, nc, w), lambda bi: (bi, 0, 0))] + [full(a) for a in consts],
        out_specs=pl.BlockSpec((1, nc, LANES), lambda bi: (bi, 0, 0)),
        out_shape=jax.ShapeDtypeStruct((b, nc, LANES), BF16),
        compiler_params=_cparams(("parallel",)),
        name="nsa_compress",
    )(tc, *consts)


def _masked_softmax(s, mask):
    s = jnp.where(mask, s, NEG)
    m = jnp.max(s, -1, keepdims=True)
    p = jnp.where(mask, jnp.exp(s - m), 0.0)
    return p / jnp.maximum(jnp.sum(p, -1, keepdims=True), 1e-30)


def _nsa_kernel(nq_ref, kc_ref, ksl_ref, kwin_ref, g_ref, o_ref, m_ref, acc_ref, qst_ref, s_ref, *, tq, n_cmp):
    qi = pl.program_id(1)
    t0 = pl.multiple_of(qi * tq, tq)
    rpos = t0 + _iota((tq, 1), 0)
    lane = _iota((1, LANES), 1)
    qs = [nq_ref[0, :, LANES * h:LANES * (h + 1)] for h in range(NSA_HEADS)]

    dist_i = rpos - (NSA_CMP_STRIDE * lane + NSA_CMP_BLOCK - 1)
    valid_c = (dist_i >= 0) & (lane < n_cmp)
    dist_c = dist_i.astype(F32)
    kc = kc_ref[0]
    psum = jnp.zeros((tq, LANES), F32)
    o_cmp = []
    for h in range(NSA_HEADS):
        q_head = jnp.where(lane < HEAD_DIM, qs[h], jnp.zeros_like(qs[h]))
        p = _masked_softmax(_dot_nt(q_head, kc) - ALIBI[h] * dist_c, valid_c)
        psum = psum + p
        o_cmp.append(_dot(p.astype(BF16), kc))

    n_i = _iota((LANES, LANES), 0)
    j_i = _iota((LANES, LANES), 1)
    ov = ((NSA_CMP_STRIDE * n_i < NSA_SEL_BLOCK * (j_i + 1))
          & (NSA_CMP_STRIDE * n_i + NSA_CMP_BLOCK > NSA_SEL_BLOCK * j_i)
          & (n_i < n_cmp))
    imp = _dot_exact_rhs(psum, jnp.where(ov, 1.0, 0.0).astype(BF16))
    cur = jnp.right_shift(rpos, 6)
    forced = (lane == 0) | (lane == cur) | (lane == cur - 1)
    future = lane * NSA_SEL_BLOCK > rpos
    n_blk = ksl_ref.shape[1] // NSA_SEL_BLOCK
    work = jnp.where(forced, NSA_FORCE_SCORE, jnp.where(future, -1.0, imp))
    work = jnp.where(lane < n_blk, work, -jnp.inf)
    work_t = work.T[0:n_blk, :]
    blk_id = _iota((n_blk, 1), 0)
    rank = jnp.zeros((n_blk, tq), F32)
    for jp in range(n_blk):
        other = work_t[jp:jp + 1, :]
        beats = (other > work_t) | ((other == work_t) & (jp < blk_id))
        rank = rank + jnp.where(beats, 1.0, 0.0)
    sel_t = jnp.where(rank < NSA_SEL_TOPN, 1.0, 0.0)
    sel = jnp.concatenate([sel_t, jnp.zeros((LANES - n_blk, tq), F32)], 0).T > 0.5
    sel_lanes = pltpu.roll(jnp.where(sel, 1.0, 0.0), NSA_SEL_LANE, 1)
    in_sel = (lane >= NSA_SEL_LANE) & (lane < NSA_SEL_LANE + n_blk)
    sel_bias = jnp.where(in_sel, (sel_lanes - 1.0) * NSA_MASK_BIG, 0.0)
    for h in range(NSA_HEADS):
        qst_ref[h * tq:(h + 1) * tq, :] = (qs[h].astype(F32) + sel_bias).astype(BF16)
    m_ref[...] = jnp.full_like(m_ref, NEG)
    acc_ref[...] = jnp.zeros_like(acc_ref)

    def online(s, v):
        chunks = [s[:, LANES * i:LANES * (i + 1)] for i in range(s.shape[1] // LANES)]
        m_prev = m_ref[...]
        m_new = jnp.maximum(m_prev, jnp.max(functools.reduce(jnp.maximum, chunks), -1, keepdims=True))
        p = jnp.concatenate([jnp.exp(ch - m_new) for ch in chunks], 1).astype(BF16)
        acc_ref[...] = jnp.exp(m_prev - m_new) * acc_ref[...] + _dot(p, v)
        m_ref[...] = m_new

    def scores(c):
        return _dot_nt(qst_ref[...], ksl_ref[0, pl.ds(pl.multiple_of(c * tq, tq), tq), 0:LANES])

    def values(c):
        return ksl_ref[0, pl.ds(pl.multiple_of(c * tq, tq), tq), LANES:2 * LANES]

    s_ref[0] = scores(0)

    def chunk_pair(i, carry):
        j = 2 * i
        s_ref[1] = scores(j + 1)
        online(s_ref[0], values(j))
        s_ref[0] = scores(j + 2)
        online(s_ref[1], values(j + 1))
        return carry

    lax.fori_loop(0, jnp.right_shift(qi, 1), chunk_pair, 0)

    @pl.when((qi & 1) == 1)
    def _():
        online(s_ref[0], values(qi - 1))
        s_ref[0] = scores(qi)

    nrow = NSA_HEADS * tq
    causal = (_iota((nrow, tq), 0) & (tq - 1)) >= _iota((nrow, tq), 1)
    online(jnp.where(causal, s_ref[0], NEG), values(qi))

    wlen = tq + NSA_WINDOW
    w0 = pl.multiple_of(jnp.maximum(t0 - NSA_WINDOW, 0), tq)
    k_w = kwin_ref[0, pl.ds(w0, wlen), 0:LANES]
    v_w = kwin_ref[0, pl.ds(w0, wlen), LANES:2 * LANES]
    d_w = (t0 - w0) + _iota((tq, wlen), 0) - _iota((tq, wlen), 1)
    keep_w = (d_w >= 0) & (d_w < NSA_WINDOW)

    gates = jax.nn.sigmoid(g_ref[...])
    for h in range(NSA_HEADS):
        s_w = jnp.where(keep_w, _dot_nt(qs[h], k_w), NEG)
        p_w = jnp.exp(s_w - jnp.max(s_w, -1, keepdims=True))
        o_win = _dot(p_w.astype(BF16), v_w)
        o_win = o_win / o_win[:, HEAD_DIM:HEAD_DIM + 1]
        o_sel = acc_ref[h * tq:(h + 1) * tq, :]
        o_sel = o_sel / o_sel[:, HEAD_DIM:HEAD_DIM + 1]
        g0 = NSAG_LANE + 3 * h
        o = (gates[:, g0:g0 + 1] * pltpu.roll(o_cmp[h], HEAD_DIM, 1) + gates[:, g0 + 1:g0 + 2] * o_sel
             + gates[:, g0 + 2:g0 + 3] * o_win)
        o_ref[0, :, LANES * h:LANES * (h + 1)] = jnp.where(lane < HEAD_DIM, o, 0.0).astype(o_ref.dtype)


def _nsa(nq, kcvc, nsl, nwin, small, tq=256):
    b, s, _ = nq.shape
    nq_t = s // tq
    n_cmp = (s - NSA_CMP_BLOCK) // NSA_CMP_STRIDE + 1
    return pl.pallas_call(
        functools.partial(_nsa_kernel, tq=tq, n_cmp=n_cmp),
        grid=(b, nq_t),
        in_specs=[pl.BlockSpec((1, tq, NSA_HEADS * LANES), lambda bi, qi: (bi, qi, 0)),
                  pl.BlockSpec((1, kcvc.shape[1], LANES), lambda bi, qi: (bi, 0, 0)),
                  pl.BlockSpec((1, s, 2 * LANES), lambda bi, qi: (bi, 0, 0)),
                  pl.BlockSpec((1, s, 2 * LANES), lambda bi, qi: (bi, 0, 0)),
                  pl.BlockSpec((tq, LANES), lambda bi, qi: (bi * nq_t + qi, 0))],
        out_specs=pl.BlockSpec((1, tq, NSA_HEADS * LANES), lambda bi, qi: (bi, qi, 0)),
        out_shape=jax.ShapeDtypeStruct((b, s, NSA_HEADS * LANES), BF16),
        scratch_shapes=[pltpu.VMEM((NSA_HEADS * tq, LANES), F32), pltpu.VMEM((NSA_HEADS * tq, LANES), F32),
                        pltpu.VMEM((NSA_HEADS * tq, LANES), BF16), pltpu.VMEM((2, NSA_HEADS * tq, tq), F32)],
        compiler_params=_cparams(("parallel", "arbitrary")),
        name="nsa_attn",
    )(nq, kcvc, nsl, nwin, small)


def _layer_norm(y, g, b):
    mu = jnp.mean(y, -1, keepdims=True)
    yc = y - mu
    var = jnp.mean(yc * yc, -1, keepdims=True)
    return yc * lax.rsqrt(var + NORM_EPS) * g + b


ROUTE_ROWS = 512


def _out_proj_kernel(*refs, routed):
    if routed:
        (oa_ref, on_ref, x_ref, wa_ref, wb_ref, g_ref, b_ref, rw_ref,
         o_ref, gate_ref, pos_ref, post_ref, cnt_ref) = refs
    else:
        oa_ref, on_ref, x_ref, wa_ref, wb_ref, g_ref, b_ref, o_ref = refs
    mix = _dot(oa_ref[...], wa_ref[...]) + _dot(on_ref[...], wb_ref[...])
    x1 = _layer_norm(DEEPNORM_ALPHA * x_ref[...] + mix, g_ref[...], b_ref[...])
    o_ref[...] = x1
    if routed:
        tm = x1.shape[0]
        lane = _iota((1, LANES), 1)
        lane_f = lane.astype(F32)
        logits = jnp.where(lane < N_EXPERTS, _dot_split(x1, rw_ref[...]), NEG)
        ex = jnp.exp(logits - jnp.max(logits, -1, keepdims=True))
        probs = ex / jnp.sum(ex, -1, keepdims=True)
        p1 = jnp.max(probs, -1, keepdims=True)
        i1 = jnp.min(jnp.where(probs == p1, lane_f, float(LANES)), -1, keepdims=True)
        rest = jnp.where(lane_f == i1, -1.0, probs)
        p2 = jnp.max(rest, -1, keepdims=True)
        i2 = jnp.min(jnp.where(rest == p2, lane_f, float(LANES)), -1, keepdims=True)
        tot = p1 + p2
        gate_ref[...] = jnp.where(lane_f == i1, p1 / tot, jnp.where(lane_f == i2, p2 / tot, 0.0))
        chosen = (lane_f == i1) | (lane_f == i2)
        before = (_iota((tm, tm), 0) > _iota((tm, tm), 1)).astype(BF16)
        onehot = jnp.where(chosen, 1.0, 0.0)
        slot = _dot(before, onehot.astype(BF16))
        posm = jnp.where(chosen, slot, -1.0)
        pos_ref[...] = posm
        post_ref[...] = posm.T[0:N_EXPERTS, :]
        cnt_ref[...] = jnp.broadcast_to(jnp.sum(onehot, 0, keepdims=True), (8, LANES)).astype(jnp.int32)


def _out_proj(o_attn, o_nsa, x2, wl, router=None, tm=ROUTE_ROWS):
    n = x2.shape[0]
    routed = router is not None
    full = lambda a: pl.BlockSpec(a.shape, lambda i: (0,) * a.ndim)
    row = lambda w: pl.BlockSpec((tm, w), lambda i: (i, 0))
    consts = [wl["w_out_a"], wl["w_out_b"], wl["ln1_g"], wl["ln1_b"]] + ([router] if routed else [])
    out_specs = [row(D_MODEL)]
    out_shape = [jax.ShapeDtypeStruct((n, D_MODEL), F32)]
    if routed:
        out_specs += [row(LANES), row(LANES), pl.BlockSpec((N_EXPERTS, tm), lambda i: (i, 0)),
                      pl.BlockSpec((8, LANES), lambda i: (i, 0))]
        out_shape += [jax.ShapeDtypeStruct((n, LANES), F32), jax.ShapeDtypeStruct((n, LANES), F32),
                      jax.ShapeDtypeStruct((n // tm * N_EXPERTS, tm), F32),
                      jax.ShapeDtypeStruct((n // tm * 8, LANES), jnp.int32)]
    outs = pl.pallas_call(
        functools.partial(_out_proj_kernel, routed=routed),
        grid=(n // tm,),
        in_specs=[row(o_attn.shape[1]), row(o_nsa.shape[1]), row(D_MODEL)] + [full(a) for a in consts],
        out_specs=out_specs,
        out_shape=out_shape,
        compiler_params=_cparams(("parallel",)),
        name="out_proj_route_ln" if routed else "out_proj_ln",
    )(o_attn, o_nsa, x2, *consts)
    return outs if routed else outs[0]


def _ffn_kernel(x_ref, w1_ref, w3_ref, w2_ref, g_ref, b_ref, o_ref, acc_ref, xb_ref):
    c = pl.program_id(1)

    @pl.when(c == 0)
    def _():
        acc_ref[...] = jnp.zeros_like(acc_ref)
        xb_ref[...] = x_ref[...].astype(BF16)

    xb = xb_ref[...]
    a = jax.nn.silu(_dot(xb, w1_ref[...])) * _dot(xb, w3_ref[...])
    acc_ref[...] += _dot(a.astype(BF16), w2_ref[...])

    @pl.when(c == pl.num_programs(1) - 1)
    def _():
        o_ref[...] = _layer_norm(DEEPNORM_ALPHA * x_ref[...] + acc_ref[...], g_ref[...], b_ref[...])


def _ffn(x2, w1, w3, w2, ln_g, ln_b, tm=512, nchunk=2):
    n = x2.shape[0]
    tf = w1.shape[1] // nchunk
    full = lambda a: pl.BlockSpec(a.shape, lambda i, c: (0,) * a.ndim)
    return pl.pallas_call(
        _ffn_kernel,
        grid=(n // tm, nchunk),
        in_specs=[pl.BlockSpec((tm, D_MODEL), lambda i, c: (i, 0)),
                  pl.BlockSpec((D_MODEL, tf), lambda i, c: (0, c)),
                  pl.BlockSpec((D_MODEL, tf), lambda i, c: (0, c)),
                  pl.BlockSpec((tf, D_MODEL), lambda i, c: (c, 0)),
                  full(ln_g), full(ln_b)],
        out_specs=pl.BlockSpec((tm, D_MODEL), lambda i, c: (i, 0)),
        out_shape=jax.ShapeDtypeStruct((n, D_MODEL), F32),
        scratch_shapes=[pltpu.VMEM((tm, D_MODEL), F32), pltpu.VMEM((tm, D_MODEL), BF16)],
        compiler_params=_cparams(("parallel", "arbitrary")),
        name="ffn_ln",
    )(x2, w1, w3, w2, ln_g, ln_b)


MOE_CAP = 160
MOE_CAP_PAD = -(-MOE_CAP // LANES) * LANES
MOE_CHUNKS = -(-ROUTE_ROWS // MOE_CAP)


def _moe_kernel(cnt_ref, x_ref, gate_ref, pos_ref, *rest, groups):
    post_refs = rest[:groups]
    w1_ref, w3_ref, w2_ref, g_ref, b_ref, o_ref, acc_ref, xb_ref = rest[groups:]
    i = pl.program_id(0)
    e = pl.program_id(1)
    lane = _iota((1, LANES), 1)

    @pl.when(e == 0)
    def _():
        acc_ref[...] = jnp.zeros_like(acc_ref)
        xb_ref[...] = x_ref[...].astype(BF16)

    for gi in range(groups):
        rows = slice(ROUTE_ROWS * gi, ROUTE_ROWS * (gi + 1))
        cnt = cnt_ref[(i * groups + gi) * N_EXPERTS + e]
        gate_e = jnp.sum(jnp.where(lane == e, gate_ref[rows, :], 0.0), -1, keepdims=True)
        slot_col = jnp.sum(jnp.where(lane == e, pos_ref[rows, :], 0.0), -1, keepdims=True)
        slot_row = post_refs[gi][0]
        for k in range(MOE_CHUNKS):
            @pl.when(cnt > k * MOE_CAP)
            def _():
                want = (_iota((MOE_CAP, 1), 0) + k * MOE_CAP).astype(F32)
                pick = jnp.where(slot_row == want, 1.0, 0.0).astype(BF16)
                xg = _dot(pick, xb_ref[rows, :]).astype(BF16)
                a = jax.nn.silu(_dot(xg, w1_ref[0])) * _dot(xg, w3_ref[0])
                y = _dot(a.astype(BF16), w2_ref[0])
                col = _iota((1, MOE_CAP_PAD), 1)
                put = jnp.where((slot_col == (col + k * MOE_CAP).astype(F32)) & (col < MOE_CAP),
                                1.0, 0.0).astype(BF16)
                yb = y.astype(BF16)
                if MOE_CAP_PAD > MOE_CAP:
                    yb = jnp.concatenate([yb, jnp.zeros((MOE_CAP_PAD - MOE_CAP, D_MODEL), BF16)], 0)
                acc_ref[rows, :] += gate_e * _dot(put, yb)

    @pl.when(e == pl.num_programs(1) - 1)
    def _():
        o_ref[...] = _layer_norm(DEEPNORM_ALPHA * x_ref[...] + acc_ref[...], g_ref[...], b_ref[...])


def _moe(x2, gates, pos, post, cnt, w1, w3, w2, ln_g, ln_b, groups=2):
    n = x2.shape[0]
    ne, _, tf = w1.shape
    tm = groups * ROUTE_ROWS
    post3 = post.reshape(n // ROUTE_ROWS * N_EXPERTS, 1, ROUTE_ROWS)
    full = lambda a: pl.BlockSpec(a.shape, lambda i, e, c: (0,) * a.ndim)
    row = lambda w: pl.BlockSpec((tm, w), lambda i, e, c: (i, 0))
    post_spec = lambda gi: pl.BlockSpec((1, 1, ROUTE_ROWS),
                                        lambda i, e, c: ((i * groups + gi) * N_EXPERTS + e, 0, 0))
    grid_spec = pltpu.PrefetchScalarGridSpec(
        num_scalar_prefetch=1,
        grid=(n // tm, ne),
        in_specs=[row(D_MODEL), row(LANES), row(LANES)] + [post_spec(gi) for gi in range(groups)]
                 + [pl.BlockSpec((1, D_MODEL, tf), lambda i, e, c: (e, 0, 0)),
                    pl.BlockSpec((1, D_MODEL, tf), lambda i, e, c: (e, 0, 0)),
                    pl.BlockSpec((1, tf, D_MODEL), lambda i, e, c: (e, 0, 0)),
                    full(ln_g), full(ln_b)],
        out_specs=row(D_MODEL),
        scratch_shapes=[pltpu.VMEM((tm, D_MODEL), F32), pltpu.VMEM((tm, D_MODEL), BF16)],
    )
    return pl.pallas_call(
        functools.partial(_moe_kernel, groups=groups),
        grid_spec=grid_spec,
        out_shape=jax.ShapeDtypeStruct((n, D_MODEL), F32),
        compiler_params=_cparams(("parallel", "arbitrary")),
        name="moe_top2_ln",
    )(cnt, x2, gates, pos, *([post3] * groups), w1, w3, w2, ln_g, ln_b)


def _in_proj_columns():
    src = np.full((IN_P,), -1, np.int64)
    scale = np.ones((IN_P,), np.float32)
    o_cq, o_ckv, o_kr = 0, MLA_Q_RANK, MLA_Q_RANK + MLA_KV_RANK
    o_fox = o_kr + MLA_ROPE
    o_foxf = o_fox + 3 * FOX_HEADS * HEAD_DIM
    o_nq = o_foxf + FOX_HEADS
    o_nkv = o_nq + NSA_HEADS * HEAD_DIM
    o_ng = o_nkv + 6 * HEAD_DIM
    src[0:o_kr] = np.arange(o_kr)
    small = o_kr
    src[small + KR_LANE:small + KR_LANE + MLA_ROPE] = o_kr + np.arange(MLA_ROPE)
    src[small + FOXF_LANE:small + FOXF_LANE + FOX_HEADS] = o_foxf + np.arange(FOX_HEADS)
    src[small + NSAG_LANE:small + NSAG_LANE + 3 * NSA_HEADS] = o_ng + np.arange(3 * NSA_HEADS)
    n_fox = FOX_HEADS * HEAD_DIM
    src[FQ_OFF:FQ_OFF + 3 * n_fox] = o_fox + np.arange(3 * n_fox)
    scale[FQ_OFF:FQ_OFF + n_fox] = HEAD_DIM ** -0.5
    src[NQ_OFF:NQ_OFF + NSA_HEADS * HEAD_DIM] = o_nq + np.arange(NSA_HEADS * HEAD_DIM)
    scale[NQ_OFF:NQ_OFF + NSA_HEADS * HEAD_DIM] = HEAD_DIM ** -0.5
    src[NC_OFF:NC_OFF + 6 * HEAD_DIM] = o_nkv + np.arange(6 * HEAD_DIM)
    return src, scale


def _gather_cols(w, src, scale=None):
    scale = np.ones(src.shape, np.float32) if scale is None else scale
    parts, i, n = [], 0, len(src)
    while i < n:
        j = i + 1
        while j < n and scale[j] == scale[i] and (src[j] == src[j - 1] + 1 if src[i] >= 0 else src[j] < 0):
            j += 1
        if src[i] < 0:
            parts.append(jnp.zeros(w.shape[:-1] + (j - i,), w.dtype))
        else:
            seg = w[..., int(src[i]):int(src[i]) + (j - i)]
            parts.append(seg if scale[i] == 1.0 else seg * float(scale[i]))
        i = j
    return jnp.concatenate(parts, axis=-1)


def _tables(s):
    half = MLA_ROPE // 2
    freqs = ROPE_THETA ** (-jnp.arange(half, dtype=F32) / half)
    ang = jnp.arange(s).astype(F32)[:, None] * freqs[None, :]
    cos, sin = jnp.cos(ang), jnp.sin(ang)
    z = lambda w: jnp.zeros((s, w), F32)
    tabs = {
        "cos": jnp.concatenate([jnp.ones((s, MLA_NOPE), F32), cos, cos, z(LANES - MLA_NOPE - MLA_ROPE)], 1),
        "sina": jnp.concatenate([z(MLA_NOPE), -sin, z(half), z(LANES - MLA_NOPE - MLA_ROPE)], 1),
        "sinb": jnp.concatenate([z(MLA_NOPE), z(half), sin, z(LANES - MLA_NOPE - MLA_ROPE)], 1),
    }
    pq = np.zeros((LANES, FOX_HEADS * LANES), np.float32)
    pk = np.zeros((LANES, FOX_HEADS * LANES), np.float32)
    oneq = np.zeros((1, FOX_HEADS * LANES), np.float32)
    onek = np.zeros((1, FOX_HEADS * LANES), np.float32)
    for h in range(FOX_HEADS):
        for t in range(3):
            pq[FOXF_LANE + FOX_HEADS * t + h, LANES * h + HEAD_DIM + t] = 1.0
            pk[FOXF_LANE + FOX_HEADS * t + h, LANES * h + HEAD_DIM + 3 + t] = -1.0
            oneq[0, LANES * h + HEAD_DIM + 3 + t] = 1.0
            onek[0, LANES * h + HEAD_DIM + t] = 1.0
    tabs.update(pq=jnp.asarray(pq, BF16), pk=jnp.asarray(pk, BF16), oneq=jnp.asarray(oneq), onek=jnp.asarray(onek))
    qaux = np.zeros((1, NSA_HEADS * LANES), np.float32)
    for h in range(NSA_HEADS):
        qaux[0, LANES * h + NSA_AUX_LANE] = ALIBI[h] * NSA_SEL_BLOCK
        qaux[0, LANES * h + NSA_AUX_LANE + 1] = ALIBI[h]
    kaux = np.zeros((s, LANES), np.float32)
    pos = np.arange(s)
    kaux[:, NSA_AUX_LANE] = pos // NSA_SEL_BLOCK
    kaux[:, NSA_AUX_LANE + 1] = pos % NSA_SEL_BLOCK
    kaux[pos, NSA_SEL_LANE + pos // NSA_SEL_BLOCK] = 1.0
    tabs.update(nsa_qaux=jnp.asarray(qaux), nsa_kaux=jnp.asarray(kaux))
    return tabs


def _layer_weights(p, l):
    qd = MLA_NOPE + MLA_ROPE
    src_q = np.full((MLA_HEADS * LANES,), -1, np.int64)
    src_k = np.full((MLA_HEADS * LANES,), -1, np.int64)
    src_v = np.full((MLA_HEADS * LANES,), -1, np.int64)
    for h in range(MLA_HEADS):
        src_q[LANES * h + np.arange(qd)] = qd * h + np.arange(qd)
        src_k[LANES * h + np.arange(MLA_NOPE)] = 2 * HEAD_DIM * h + np.arange(MLA_NOPE)
        src_v[LANES * h + np.arange(HEAD_DIM)] = 2 * HEAD_DIM * h + MLA_NOPE + np.arange(HEAD_DIM)
    w_out = p["w_out"][l]
    n_attn = (MLA_HEADS + FOX_HEADS) * HEAD_DIM
    wb = w_out[n_attn:].reshape(NSA_HEADS, HEAD_DIM, D_MODEL)
    wb = jnp.concatenate([wb, jnp.zeros_like(wb)], axis=1).reshape(NSA_HEADS * LANES, D_MODEL)
    bf = jnp.zeros((1, LANES), F32).at[0, FOXF_LANE:FOXF_LANE + FOX_HEADS].set(p["b_forget"][l])

    half = NSA_CMP_BLOCK // 2
    kpos, vpos = p["cmp_k_pos"][l], p["cmp_v_pos"][l]
    posa = jnp.concatenate([kpos[:half], vpos[:half]], -1).reshape(1, half * LANES)
    posb = jnp.concatenate([kpos[half:], vpos[half:]], -1).reshape(1, half * LANES)

    def w1_half(lo):
        k = p["cmp_k_w1"][l].reshape(NSA_CMP_BLOCK, HEAD_DIM, NSA_CMP_HIDDEN)[lo:lo + half]
        v = p["cmp_v_w1"][l].reshape(NSA_CMP_BLOCK, HEAD_DIM, NSA_CMP_HIDDEN)[lo:lo + half]
        zk = jnp.zeros_like(k)
        top = jnp.concatenate([k, zk], -1)
        bot = jnp.concatenate([zk, v], -1)
        return jnp.concatenate([top, bot], 1).reshape(half * LANES, 2 * NSA_CMP_HIDDEN).astype(BF16)

    zw2 = jnp.zeros((NSA_CMP_HIDDEN, HEAD_DIM), F32)
    w2 = jnp.concatenate([jnp.concatenate([p["cmp_k_w2"][l], zw2], 1),
                          jnp.concatenate([zw2, p["cmp_v_w2"][l]], 1)], 0).astype(BF16)
    return {
        "g_cq": p["g_cq"][l][None, :], "g_ckv": p["g_ckv"][l][None, :],
        "w_uq": _gather_cols(p["w_uq"][l], src_q).astype(BF16),
        "w_uk": _gather_cols(p["w_ukv"][l], src_k).astype(BF16),
        "w_uv": _gather_cols(p["w_ukv"][l], src_v).astype(BF16),
        "b_forget": bf,
        "cmp_posa": posa, "cmp_posb": posb, "cmp_wa": w1_half(0), "cmp_wb": w1_half(half), "cmp_w2": w2,
        "w_out_a": w_out[:n_attn].astype(BF16), "w_out_b": wb.astype(BF16),
        "ln1_g": p["ln1_g"][l][None, :], "ln1_b": p["ln1_b"][l][None, :],
        "ln2_g": p["ln2_g"][l][None, :], "ln2_b": p["ln2_b"][l][None, :],
    }


def kernel(x, w_in, b_forget, g_cq, w_uq, g_ckv, w_ukv, cmp_k_pos, cmp_k_w1, cmp_k_w2, cmp_v_pos, cmp_v_w1,
           cmp_v_w2, w_out, ln1_g, ln1_b, ln2_g, ln2_b, ffn_w1, ffn_w3, ffn_w2, router_w, moe_w1, moe_w3,
           moe_w2):
    b, s, d = x.shape
    assert d == D_MODEL and s % 512 == 0 and s // NSA_CMP_STRIDE == LANES, (b, s, d)
    p = dict(b_forget=b_forget, g_cq=g_cq, w_uq=w_uq, g_ckv=g_ckv, w_ukv=w_ukv, cmp_k_pos=cmp_k_pos,
             cmp_k_w1=cmp_k_w1, cmp_k_w2=cmp_k_w2, cmp_v_pos=cmp_v_pos, cmp_v_w1=cmp_v_w1, cmp_v_w2=cmp_v_w2,
             w_out=w_out, ln1_g=ln1_g, ln1_b=ln1_b, ln2_g=ln2_g, ln2_b=ln2_b)
    src, scale = _in_proj_columns()
    w_in_p = _gather_cols(w_in, src, scale).astype(BF16)
    tabs = _tables(s)
    n = b * s
    x2 = x.reshape(n, d)
    for l in range(DEPTH):
        wl = _layer_weights(p, l)
        small, nq, nc, nsl, nwin, q_all, k_all, v_all = _front(x2, w_in_p[l], wl, tabs, s)
        o_attn = _flash(*(t.reshape(b, s, -1) for t in (q_all, k_all, v_all)))
        kcvc = _compress(nc.reshape(b, s // NSA_CMP_STRIDE, NSA_CMP_STRIDE * LANES), wl)
        o_nsa = _nsa(nq.reshape(b, s, NSA_HEADS * LANES), kcvc, nsl.reshape(b, s, 2 * LANES),
                     nwin.reshape(b, s, 2 * LANES), small)
        j = l // 2
        if l % 2 == 0:
            x2 = _out_proj(o_attn.reshape(n, -1), o_nsa.reshape(n, -1), x2, wl)
            x2 = _ffn(x2, ffn_w1[j].astype(BF16), ffn_w3[j].astype(BF16), ffn_w2[j].astype(BF16),
                      wl["ln2_g"], wl["ln2_b"])
        else:
            rw = jnp.pad(router_w[j], ((0, 0), (0, LANES - N_EXPERTS)))
            rw_hi = rw.astype(BF16)
            rw_lo = (rw - rw_hi.astype(F32)).astype(BF16)
            rw = jnp.concatenate([jnp.concatenate([rw_hi, rw_lo], 1),
                                  jnp.concatenate([rw_hi, jnp.zeros_like(rw_lo)], 1)], 0)
            x2, gates, pos, post, cnt = _out_proj(o_attn.reshape(n, -1), o_nsa.reshape(n, -1), x2, wl, router=rw)
            cnt = cnt[::8, :N_EXPERTS].reshape(-1)
            x2 = _moe(x2, gates, pos, post, cnt, moe_w1[j].astype(BF16), moe_w3[j].astype(BF16),
                      moe_w2[j].astype(BF16), wl["ln2_g"], wl["ln2_b"])
    return x2.reshape(b, s, d)
```

```python
import functools

import numpy as np
import jax
import jax.numpy as jnp
from jax import lax
from jax.experimental import pallas as pl
from jax.experimental.pallas import tpu as pltpu

F32 = jnp.float32
BF16 = jnp.bfloat16

D_MODEL = 1024
HEAD_DIM = 64
LANES = 128
MLA_HEADS = 6
MLA_Q_RANK = 384
MLA_KV_RANK = 256
MLA_NOPE = 64
MLA_ROPE = 32
ROPE_THETA = 10000.0
FOX_HEADS = 6
NSA_HEADS = 4
NSA_CMP_BLOCK = 32
NSA_CMP_STRIDE = 16
NSA_CMP_HIDDEN = 128
NSA_SEL_BLOCK = 64
NSA_SEL_TOPN = 8
NSA_WINDOW = 256
NSA_FORCE_SCORE = 1.0e4
N_EXPERTS = 8
NORM_EPS = 1e-5
DEPTH = 2
DEEPNORM_ALPHA = (2 * DEPTH) ** 0.25
NEG = -1e30
VMEM_LIMIT = 56 * 1024 * 1024

ROW_TILE = 512
FLASH_TQ, FLASH_TK = 1024, 512
NSA_TQ = 256

KR_LANE = 64
FOXF_LANE = 96
NSAG_LANE = 102

ZA_W = 768
FQ_OFF = ZA_W
FK_OFF = FQ_OFF + 384
FV_OFF = FK_OFF + 384
NQ_OFF = FV_OFF + 384
NC_OFF = NQ_OFF + 256
NKV_OFF = NC_OFF + 128
IN_P = NKV_OFF + 256

NSA_AUX_LANE = HEAD_DIM
NSA_SEL_LANE = HEAD_DIM + 2
NSA_MASK_BIG = 2.0 ** 126

ALIBI = tuple(2.0 ** (-8.0 * (i + 1) / NSA_HEADS) for i in range(NSA_HEADS))


def _dot(a, b, **kw):
    return jnp.dot(a, b, preferred_element_type=F32, **kw)


def _dot_nt(a, b):
    return lax.dot_general(a, b, (((1,), (1,)), ((), ())), preferred_element_type=F32)


def _iota(shape, dim):
    return lax.broadcasted_iota(jnp.int32, shape, dim)


def _split3(x):
    hi = x.astype(BF16)
    r = x - hi.astype(F32)
    mid = r.astype(BF16)
    return hi, mid, (r - mid.astype(F32)).astype(BF16)


def _dot_exact_rhs(x, w01):
    hi, mid, lo = _split3(x)
    return _dot(jnp.concatenate([hi, mid, lo], 1), jnp.concatenate([w01, w01, w01], 0))


def _dot_split(x, w_stack):
    hi = x.astype(BF16)
    lo = (x - hi.astype(F32)).astype(BF16)
    r = _dot(jnp.concatenate([hi, lo], 1), w_stack)
    half = r.shape[1] // 2
    return r[:, :half] + r[:, half:]


def _cparams(sem):
    return pltpu.CompilerParams(dimension_semantics=sem, vmem_limit_bytes=VMEM_LIMIT)


def _front_kernel(x_ref, w_ref, qaux_ref, kaux_ref, gcq_ref, wuq_ref, gckv_ref, wuk_ref, wuv_ref, bf_ref,
                  cos_ref, sina_ref, sinb_ref, pq_ref, pk_ref, oneq_ref, onek_ref,
                  small_ref, nq_ref, nc_ref, nsl_ref, nwin_ref, q_out, k_out, v_out, carry_ref,
                  *, tm, tiles_per_seq):
    @pl.when(pl.program_id(0) % tiles_per_seq == 0)
    def _():
        carry_ref[...] = jnp.zeros_like(carry_ref)

    xb = x_ref[...].astype(BF16)

    def mm(a, b):
        return _dot(xb, w_ref[:, a:b])

    lane = _iota((1, LANES), 1)

    def halves(pair):
        return (jnp.where(lane < HEAD_DIM, pair, 0.0),
                jnp.where(lane < HEAD_DIM, pltpu.roll(pair, HEAD_DIM, 1), 0.0))

    def expand(z):
        return [blk for pr in range(z.shape[1] // LANES) for blk in halves(z[:, LANES * pr:LANES * (pr + 1)])]

    def pad_v(blk):
        return jnp.where(lane < HEAD_DIM, blk, jnp.where(lane == HEAD_DIM, 1.0, 0.0)).astype(BF16)

    cos = cos_ref[...]
    sina = sina_ref[...]
    sinb = sinb_ref[...]

    def rope(blk):
        return blk * cos + pltpu.roll(blk, LANES - 16, 1) * sina + pltpu.roll(blk, 16, 1) * sinb

    za = mm(0, ZA_W)
    cq = za[:, 0:MLA_Q_RANK]
    ckv = za[:, MLA_Q_RANK:MLA_Q_RANK + MLA_KV_RANK]
    small = za[:, MLA_Q_RANK + MLA_KV_RANK:ZA_W]
    small_ref[...] = small

    xn = cq * lax.rsqrt(jnp.mean(cq * cq, -1, keepdims=True) + NORM_EPS) * gcq_ref[...]
    q = _dot(xn.astype(BF16), wuq_ref[...])
    cn = ckv * lax.rsqrt(jnp.mean(ckv * ckv, -1, keepdims=True) + NORM_EPS) * gckv_ref[...]
    cnb = cn.astype(BF16)
    kn = _dot(cnb, wuk_ref[...])
    v = _dot(cnb, wuv_ref[...])
    kr = rope(small)
    mla_scale = (MLA_NOPE + MLA_ROPE) ** -0.5
    for h in range(MLA_HEADS):
        sl = slice(LANES * h, LANES * (h + 1))
        q_out[:, sl] = (rope(q[:, sl]) * mla_scale).astype(BF16)
        k_out[:, sl] = (kn[:, sl] + kr).astype(BF16)
        v_out[:, sl] = pad_v(v[:, sl])

    fmask = (lane >= FOXF_LANE) & (lane < FOXF_LANE + FOX_HEADS)
    lf = jnp.where(fmask, jax.nn.log_sigmoid(small + bf_ref[...]), 0.0)
    tril = jnp.where(_iota((tm, tm), 0) >= _iota((tm, tm), 1), 1.0, 0.0).astype(BF16)
    lf_hi, lf_mid, lf_lo = _split3(lf)
    part = _dot(tril, jnp.concatenate([lf_hi, lf_mid], 1))
    cs = part[:, :LANES] + part[:, LANES:] + _dot(tril, lf_lo) + carry_ref[...]
    carry_ref[...] = cs[tm - 1:tm, :]
    hi, mid, lo = (t.astype(F32) for t in _split3(cs))
    c3 = (hi + pltpu.roll(mid, FOX_HEADS, 1) + pltpu.roll(lo, 2 * FOX_HEADS, 1)).astype(BF16)
    augq = _dot(c3, pq_ref[...]) + oneq_ref[...]
    augk = _dot(c3, pk_ref[...]) + onek_ref[...]
    off = MLA_HEADS * LANES
    fq = expand(mm(FQ_OFF, FK_OFF))
    fk = expand(mm(FK_OFF, FV_OFF))
    fv = expand(mm(FV_OFF, NQ_OFF))
    for h in range(FOX_HEADS):
        sl = slice(LANES * h, LANES * (h + 1))
        dst = slice(off + LANES * h, off + LANES * (h + 1))
        q_out[:, dst] = (fq[h] + augq[:, sl]).astype(BF16)
        k_out[:, dst] = (fk[h] + augk[:, sl]).astype(BF16)
        v_out[:, dst] = pad_v(fv[h])

    qaux = qaux_ref[...]
    for h, blk in enumerate(expand(mm(NQ_OFF, NC_OFF))):
        sl = slice(LANES * h, LANES * (h + 1))
        nq_ref[:, sl] = (blk + qaux[:, sl]).astype(BF16)
    nc_ref[...] = mm(NC_OFF, NKV_OFF)
    kaux = kaux_ref[...]
    nkv = mm(NKV_OFF, IN_P)
    for i, out_ref in enumerate((nsl_ref, nwin_ref)):
        k_blk, v_blk = halves(nkv[:, LANES * i:LANES * (i + 1)])
        out_ref[:, 0:LANES] = (k_blk + kaux).astype(BF16)
        out_ref[:, LANES:2 * LANES] = pad_v(v_blk)


def _front(x2, w_p, wl, tabs, s, tm=ROW_TILE):
    n = x2.shape[0]
    ns = s // tm
    nh = MLA_HEADS + FOX_HEADS
    full = lambda a: pl.BlockSpec(a.shape, lambda i: (0,) * a.ndim)
    tab = pl.BlockSpec((tm, LANES), lambda i: (i % ns, 0))
    consts = [wl["g_cq"], wl["w_uq"], wl["g_ckv"], wl["w_uk"], wl["w_uv"], wl["b_forget"]]
    tail = [tabs["pq"], tabs["pk"], tabs["oneq"], tabs["onek"]]
    widths = [(LANES, F32), (NSA_HEADS * LANES, BF16), (LANES, F32), (2 * LANES, BF16), (2 * LANES, BF16),
              (nh * LANES, BF16), (nh * LANES, BF16), (nh * LANES, BF16)]
    return pl.pallas_call(
        functools.partial(_front_kernel, tm=tm, tiles_per_seq=ns),
        grid=(n // tm,),
        in_specs=[pl.BlockSpec((tm, D_MODEL), lambda i: (i, 0)), full(w_p), full(tabs["nsa_qaux"]), tab]
                 + [full(a) for a in consts] + [tab, tab, tab] + [full(a) for a in tail],
        out_specs=[pl.BlockSpec((tm, w), lambda i: (i, 0)) for w, _ in widths],
        out_shape=[jax.ShapeDtypeStruct((n, w), dt) for w, dt in widths],
        scratch_shapes=[pltpu.VMEM((1, LANES), F32)],
        compiler_params=_cparams(("arbitrary",)),
        name="front_proj",
    )(x2, w_p, tabs["nsa_qaux"], tabs["nsa_kaux"], *consts, tabs["cos"], tabs["sina"], tabs["sinb"], *tail)


def _flash_kernel(q_ref, k_ref, v_ref, o_ref, m_ref, acc_ref, s_ref, *, tq, tk):
    qi = pl.program_id(2)
    m_ref[...] = jnp.full_like(m_ref, NEG)
    acc_ref[...] = jnp.zeros_like(acc_ref)
    nsub = tq // tk
    nfull = qi * nsub
    heads = [slice(LANES * h, LANES * (h + 1)) for h in range(2)]

    def scores(j, r0, sl):
        k0 = pl.multiple_of(j * tk, tk)
        return _dot_nt(q_ref[0, r0:tq, sl], k_ref[0, pl.ds(k0, tk), sl])

    def consume(s, h, j, r0, masked):
        k0 = pl.multiple_of(j * tk, tk)
        if masked:
            s = jnp.where(_iota((tq - r0, tk), 0) >= _iota((tq - r0, tk), 1), s, NEG)
        chunks = [s[:, LANES * c:LANES * (c + 1)] for c in range(tk // LANES)]
        m_prev = m_ref[h, r0:tq, :]
        m_new = jnp.maximum(m_prev, jnp.max(functools.reduce(jnp.maximum, chunks), -1, keepdims=True))
        p = jnp.concatenate([jnp.exp(c - m_new) for c in chunks], 1).astype(BF16)
        acc_ref[h, r0:tq, :] = (jnp.exp(m_prev - m_new) * acc_ref[h, r0:tq, :]
                                + _dot(p, v_ref[0, pl.ds(k0, tk), heads[h]]))
        m_ref[h, r0:tq, :] = m_new

    assert nsub == 2
    for h in range(2):
        s_ref[0, h] = scores(0, 0, heads[h])

    def body(i, carry):
        j = 2 * i
        for b in range(2):
            for h in range(2):
                s_ref[1 - b, h] = scores(j + b + 1, 0, heads[h])
            for h in range(2):
                consume(s_ref[b, h], h, j + b, 0, False)
        return carry

    lax.fori_loop(0, qi, body, 0)
    for h in range(2):
        consume(s_ref[0, h], h, nfull, 0, True)
    for d in range(1, nsub):
        for h in range(2):
            consume(scores(nfull + d, d * tk, heads[h]), h, nfull + d, d * tk, True)

    lane = _iota((1, LANES), 1)
    o0 = acc_ref[0]
    o1 = acc_ref[1]
    o0 = o0 / o0[:, HEAD_DIM:HEAD_DIM + 1]
    o1 = o1 / o1[:, HEAD_DIM:HEAD_DIM + 1]
    o_ref[0] = jnp.where(lane < HEAD_DIM, o0, pltpu.roll(o1, HEAD_DIM, 1)).astype(o_ref.dtype)


def _flash(q_all, k_all, v_all, tq=FLASH_TQ, tk=FLASH_TK):
    b, s, _ = q_all.shape
    npair = (MLA_HEADS + FOX_HEADS) // 2
    return pl.pallas_call(
        functools.partial(_flash_kernel, tq=tq, tk=tk),
        grid=(b, npair, s // tq),
        in_specs=[pl.BlockSpec((1, tq, 2 * LANES), lambda bi, p, qi: (bi, qi, p)),
                  pl.BlockSpec((1, s, 2 * LANES), lambda bi, p, qi: (bi, 0, p)),
                  pl.BlockSpec((1, s, 2 * LANES), lambda bi, p, qi: (bi, 0, p))],
        out_specs=pl.BlockSpec((1, tq, LANES), lambda bi, p, qi: (bi, qi, p)),
        out_shape=jax.ShapeDtypeStruct((b, s, npair * LANES), BF16),
        scratch_shapes=[pltpu.VMEM((2, tq, LANES), F32), pltpu.VMEM((2, tq, LANES), F32),
                        pltpu.VMEM((2, 2, tq, tk), F32)],
        compiler_params=_cparams(("parallel", "parallel", "arbitrary")),
        name="flash_attn",
    )(q_all, k_all, v_all)


def _cmp_kernel(nc_ref, posa_ref, posb_ref, wa_ref, wb_ref, w2_ref, out_ref, *, n_chunk):
    a = jnp.zeros((n_chunk, 2 * NSA_CMP_HIDDEN), F32)
    b = jnp.zeros((n_chunk, 2 * NSA_CMP_HIDDEN), F32)
    for l in range(NSA_CMP_STRIDE):
        t = nc_ref[0, pl.ds(l, n_chunk, stride=NSA_CMP_STRIDE), :]
        sl = slice(LANES * l, LANES * (l + 1))
        a = a + _dot((t + posa_ref[:, sl]).astype(BF16), wa_ref[sl, :])
        b = b + _dot((t + posb_ref[:, sl]).astype(BF16), wb_ref[sl, :])
    pre = a + pltpu.roll(b, n_chunk - 1, 0)
    hid = jax.nn.silu(pre)
    out_ref[0] = _dot(hid.astype(BF16), w2_ref[...]).astype(out_ref.dtype)


def _compress(nc3, wl):
    b, s, _ = nc3.shape
    n_chunk = s // NSA_CMP_STRIDE
    full = lambda a: pl.BlockSpec(a.shape, lambda bi: (0,) * a.ndim)
    consts = [wl["cmp_posa"], wl["cmp_posb"], wl["cmp_wa"], wl["cmp_wb"], wl["cmp_w2"]]
    return pl.pallas_call(
        functools.partial(_cmp_kernel, n_chunk=n_chunk),
        grid=(b,),
        in_specs=[pl.BlockSpec((1, s, LANES), lambda bi: (bi, 0, 0))] + [full(a) for a in consts],
        out_specs=pl.BlockSpec((1, n_chunk, LANES), lambda bi: (bi, 0, 0)),
        out_shape=jax.ShapeDtypeStruct((b, n_chunk, LANES), BF16),
        compiler_params=_cparams(("parallel",)),
        name="nsa_compress",
    )(nc3, *consts)


def _masked_softmax(s, mask):
    s = jnp.where(mask, s, NEG)
    m = jnp.max(s, -1, keepdims=True)
    p = jnp.where(mask, jnp.exp(s - m), 0.0)
    return p / jnp.maximum(jnp.sum(p, -1, keepdims=True), 1e-30)


def _nsa_kernel(nq_ref, kc_ref, ksl_ref, kwin_ref, g_ref, o_ref, m_ref, acc_ref, qst_ref, s_ref, *, tq, n_cmp):
    qi = pl.program_id(1)
    t0 = pl.multiple_of(qi * tq, tq)
    rpos = t0 + _iota((tq, 1), 0)
    lane = _iota((1, LANES), 1)
    qs = [nq_ref[0, :, LANES * h:LANES * (h + 1)] for h in range(NSA_HEADS)]

    dist_i = rpos - (NSA_CMP_STRIDE * lane + NSA_CMP_BLOCK - 1)
    valid_c = (dist_i >= 0) & (lane < n_cmp)
    dist_c = dist_i.astype(F32)
    kc = kc_ref[0]
    psum = jnp.zeros((tq, LANES), F32)
    o_cmp = []
    for h in range(NSA_HEADS):
        q_head = jnp.where(lane < HEAD_DIM, qs[h], jnp.zeros_like(qs[h]))
        p = _masked_softmax(_dot_nt(q_head, kc) - ALIBI[h] * dist_c, valid_c)
        psum = psum + p
        o_cmp.append(_dot(p.astype(BF16), kc))

    n_i = _iota((LANES, LANES), 0)
    j_i = _iota((LANES, LANES), 1)
    ov = ((NSA_CMP_STRIDE * n_i < NSA_SEL_BLOCK * (j_i + 1))
          & (NSA_CMP_STRIDE * n_i + NSA_CMP_BLOCK > NSA_SEL_BLOCK * j_i)
          & (n_i < n_cmp))
    imp = _dot_exact_rhs(psum, jnp.where(ov, 1.0, 0.0).astype(BF16))
    cur = jnp.right_shift(rpos, 6)
    forced = (lane == 0) | (lane == cur) | (lane == cur - 1)
    future = lane * NSA_SEL_BLOCK > rpos
    n_blk = ksl_ref.shape[1] // NSA_SEL_BLOCK
    work = jnp.where(forced, NSA_FORCE_SCORE, jnp.where(future, -1.0, imp))
    work = jnp.where(lane < n_blk, work, -jnp.inf)
    work_t = work.T[0:n_blk, :]
    blk_id = _iota((n_blk, 1), 0)
    rank = jnp.zeros((n_blk, tq), F32)
    for jp in range(n_blk):
        other = work_t[jp:jp + 1, :]
        beats = (other > work_t) | ((other == work_t) & (jp < blk_id))
        rank = rank + jnp.where(beats, 1.0, 0.0)
    sel_t = jnp.where(rank < NSA_SEL_TOPN, 1.0, 0.0)
    sel = jnp.concatenate([sel_t, jnp.zeros((LANES - n_blk, tq), F32)], 0).T > 0.5
    sel_lanes = pltpu.roll(jnp.where(sel, 1.0, 0.0), NSA_SEL_LANE, 1)
    in_sel = (lane >= NSA_SEL_LANE) & (lane < NSA_SEL_LANE + n_blk)
    sel_bias = jnp.where(in_sel, (sel_lanes - 1.0) * NSA_MASK_BIG, 0.0)
    for h in range(NSA_HEADS):
        qst_ref[h * tq:(h + 1) * tq, :] = (qs[h].astype(F32) + sel_bias).astype(BF16)
    m_ref[...] = jnp.full_like(m_ref, NEG)
    acc_ref[...] = jnp.zeros_like(acc_ref)

    def online(s, v):
        chunks = [s[:, LANES * i:LANES * (i + 1)] for i in range(s.shape[1] // LANES)]
        m_prev = m_ref[...]
        m_new = jnp.maximum(m_prev, jnp.max(functools.reduce(jnp.maximum, chunks), -1, keepdims=True))
        p = jnp.concatenate([jnp.exp(ch - m_new) for ch in chunks], 1).astype(BF16)
        acc_ref[...] = jnp.exp(m_prev - m_new) * acc_ref[...] + _dot(p, v)
        m_ref[...] = m_new

    def scores(c):
        return _dot_nt(qst_ref[...], ksl_ref[0, pl.ds(pl.multiple_of(c * tq, tq), tq), 0:LANES])

    def values(c):
        return ksl_ref[0, pl.ds(pl.multiple_of(c * tq, tq), tq), LANES:2 * LANES]

    s_ref[0] = scores(0)

    def chunk_pair(i, carry):
        j = 2 * i
        s_ref[1] = scores(j + 1)
        online(s_ref[0], values(j))
        s_ref[0] = scores(j + 2)
        online(s_ref[1], values(j + 1))
        return carry

    lax.fori_loop(0, jnp.right_shift(qi, 1), chunk_pair, 0)

    @pl.when((qi & 1) == 1)
    def _():
        online(s_ref[0], values(qi - 1))
        s_ref[0] = scores(qi)

    nrow = NSA_HEADS * tq
    causal = (_iota((nrow, tq), 0) & (tq - 1)) >= _iota((nrow, tq), 1)
    online(jnp.where(causal, s_ref[0], NEG), values(qi))

    wlen = tq + NSA_WINDOW
    w0 = pl.multiple_of(jnp.maximum(t0 - NSA_WINDOW, 0), tq)
    k_w = kwin_ref[0, pl.ds(w0, wlen), 0:LANES]
    v_w = kwin_ref[0, pl.ds(w0, wlen), LANES:2 * LANES]
    d_w = (t0 - w0) + _iota((tq, wlen), 0) - _iota((tq, wlen), 1)
    keep_w = (d_w >= 0) & (d_w < NSA_WINDOW)

    gates = jax.nn.sigmoid(g_ref[...])
    for h in range(NSA_HEADS):
        s_w = jnp.where(keep_w, _dot_nt(qs[h], k_w), NEG)
        p_w = jnp.exp(s_w - jnp.max(s_w, -1, keepdims=True))
        o_win = _dot(p_w.astype(BF16), v_w)
        o_win = o_win / o_win[:, HEAD_DIM:HEAD_DIM + 1]
        o_sel = acc_ref[h * tq:(h + 1) * tq, :]
        o_sel = o_sel / o_sel[:, HEAD_DIM:HEAD_DIM + 1]
        g0 = NSAG_LANE + 3 * h
        o = (gates[:, g0:g0 + 1] * pltpu.roll(o_cmp[h], HEAD_DIM, 1) + gates[:, g0 + 1:g0 + 2] * o_sel
             + gates[:, g0 + 2:g0 + 3] * o_win)
        o_ref[0, :, LANES * h:LANES * (h + 1)] = jnp.where(lane < HEAD_DIM, o, 0.0).astype(o_ref.dtype)


def _nsa(nq, kcvc, nsl, nwin, small, tq=NSA_TQ):
    b, s, _ = nq.shape
    nq_t = s // tq
    n_cmp = (s - NSA_CMP_BLOCK) // NSA_CMP_STRIDE + 1
    return pl.pallas_call(
        functools.partial(_nsa_kernel, tq=tq, n_cmp=n_cmp),
        grid=(b, nq_t),
        in_specs=[pl.BlockSpec((1, tq, NSA_HEADS * LANES), lambda bi, qi: (bi, qi, 0)),
                  pl.BlockSpec((1, kcvc.shape[1], LANES), lambda bi, qi: (bi, 0, 0)),
                  pl.BlockSpec((1, s, 2 * LANES), lambda bi, qi: (bi, 0, 0)),
                  pl.BlockSpec((1, s, 2 * LANES), lambda bi, qi: (bi, 0, 0)),
                  pl.BlockSpec((tq, LANES), lambda bi, qi: (bi * nq_t + qi, 0))],
        out_specs=pl.BlockSpec((1, tq, NSA_HEADS * LANES), lambda bi, qi: (bi, qi, 0)),
        out_shape=jax.ShapeDtypeStruct((b, s, NSA_HEADS * LANES), BF16),
        scratch_shapes=[pltpu.VMEM((NSA_HEADS * tq, LANES), F32), pltpu.VMEM((NSA_HEADS * tq, LANES), F32),
                        pltpu.VMEM((NSA_HEADS * tq, LANES), BF16), pltpu.VMEM((2, NSA_HEADS * tq, tq), F32)],
        compiler_params=_cparams(("parallel", "arbitrary")),
        name="nsa_attn",
    )(nq, kcvc, nsl, nwin, small)


def _layer_norm(y, g, b):
    mu = jnp.mean(y, -1, keepdims=True)
    yc = y - mu
    var = jnp.mean(yc * yc, -1, keepdims=True)
    return yc * lax.rsqrt(var + NORM_EPS) * g + b


ROUTE_ROWS = ROW_TILE


def _out_proj_kernel(*refs, routed):
    if routed:
        (oa_ref, on_ref, x_ref, wa_ref, wb_ref, g_ref, b_ref, rw_ref,
         o_ref, gate_ref, pos_ref, post_ref, cnt_ref) = refs
    else:
        oa_ref, on_ref, x_ref, wa_ref, wb_ref, g_ref, b_ref, o_ref = refs
    mix = _dot(oa_ref[...], wa_ref[...]) + _dot(on_ref[...], wb_ref[...])
    x1 = _layer_norm(DEEPNORM_ALPHA * x_ref[...] + mix, g_ref[...], b_ref[...])
    o_ref[...] = x1
    if routed:
        tm = x1.shape[0]
        lane = _iota((1, LANES), 1)
        lane_f = lane.astype(F32)
        logits = jnp.where(lane < N_EXPERTS, _dot_split(x1, rw_ref[...]), NEG)
        ex = jnp.exp(logits - jnp.max(logits, -1, keepdims=True))
        probs = ex / jnp.sum(ex, -1, keepdims=True)
        p1 = jnp.max(probs, -1, keepdims=True)
        i1 = jnp.min(jnp.where(probs == p1, lane_f, float(LANES)), -1, keepdims=True)
        rest = jnp.where(lane_f == i1, -1.0, probs)
        p2 = jnp.max(rest, -1, keepdims=True)
        i2 = jnp.min(jnp.where(rest == p2, lane_f, float(LANES)), -1, keepdims=True)
        tot = p1 + p2
        gate_ref[...] = jnp.where(lane_f == i1, p1 / tot, jnp.where(lane_f == i2, p2 / tot, 0.0))
        chosen = (lane_f == i1) | (lane_f == i2)
        before = (_iota((tm, tm), 0) > _iota((tm, tm), 1)).astype(BF16)
        onehot = jnp.where(chosen, 1.0, 0.0)
        slot = _dot(before, onehot.astype(BF16))
        posm = jnp.where(chosen, slot, -1.0)
        pos_ref[...] = posm
        post_ref[...] = posm.T[0:N_EXPERTS, :]
        cnt_ref[...] = jnp.broadcast_to(jnp.sum(onehot, 0, keepdims=True), (8, LANES)).astype(jnp.int32)


def _out_proj(o_attn, o_nsa, x2, wl, router=None, tm=ROUTE_ROWS):
    n = x2.shape[0]
    routed = router is not None
    full = lambda a: pl.BlockSpec(a.shape, lambda i: (0,) * a.ndim)
    row = lambda w: pl.BlockSpec((tm, w), lambda i: (i, 0))
    consts = [wl["w_out_a"], wl["w_out_b"], wl["ln1_g"], wl["ln1_b"]] + ([router] if routed else [])
    out_specs = [row(D_MODEL)]
    out_shape = [jax.ShapeDtypeStruct((n, D_MODEL), F32)]
    if routed:
        out_specs += [row(LANES), row(LANES), pl.BlockSpec((N_EXPERTS, tm), lambda i: (i, 0)),
                      pl.BlockSpec((8, LANES), lambda i: (i, 0))]
        out_shape += [jax.ShapeDtypeStruct((n, LANES), F32), jax.ShapeDtypeStruct((n, LANES), F32),
                      jax.ShapeDtypeStruct((n // tm * N_EXPERTS, tm), F32),
                      jax.ShapeDtypeStruct((n // tm * 8, LANES), jnp.int32)]
    outs = pl.pallas_call(
        functools.partial(_out_proj_kernel, routed=routed),
        grid=(n // tm,),
        in_specs=[row(o_attn.shape[1]), row(o_nsa.shape[1]), row(D_MODEL)] + [full(a) for a in consts],
        out_specs=out_specs,
        out_shape=out_shape,
        compiler_params=_cparams(("parallel",)),
        name="out_proj_route_ln" if routed else "out_proj_ln",
    )(o_attn, o_nsa, x2, *consts)
    return outs if routed else outs[0]


def _ffn_kernel(x_ref, w1_ref, w3_ref, w2_ref, g_ref, b_ref, o_ref, acc_ref, xb_ref):
    c = pl.program_id(1)

    @pl.when(c == 0)
    def _():
        acc_ref[...] = jnp.zeros_like(acc_ref)
        xb_ref[...] = x_ref[...].astype(BF16)

    xb = xb_ref[...]
    a = jax.nn.silu(_dot(xb, w1_ref[...])) * _dot(xb, w3_ref[...])
    acc_ref[...] += _dot(a.astype(BF16), w2_ref[...])

    @pl.when(c == pl.num_programs(1) - 1)
    def _():
        o_ref[...] = _layer_norm(DEEPNORM_ALPHA * x_ref[...] + acc_ref[...], g_ref[...], b_ref[...])


def _ffn(x2, w1, w3, w2, ln_g, ln_b, tm=ROW_TILE, nchunk=2):
    n = x2.shape[0]
    tf = w1.shape[1] // nchunk
    full = lambda a: pl.BlockSpec(a.shape, lambda i, c: (0,) * a.ndim)
    return pl.pallas_call(
        _ffn_kernel,
        grid=(n // tm, nchunk),
        in_specs=[pl.BlockSpec((tm, D_MODEL), lambda i, c: (i, 0)),
                  pl.BlockSpec((D_MODEL, tf), lambda i, c: (0, c)),
                  pl.BlockSpec((D_MODEL, tf), lambda i, c: (0, c)),
                  pl.BlockSpec((tf, D_MODEL), lambda i, c: (c, 0)),
                  full(ln_g), full(ln_b)],
        out_specs=pl.BlockSpec((tm, D_MODEL), lambda i, c: (i, 0)),
        out_shape=jax.ShapeDtypeStruct((n, D_MODEL), F32),
        scratch_shapes=[pltpu.VMEM((tm, D_MODEL), F32), pltpu.VMEM((tm, D_MODEL), BF16)],
        compiler_params=_cparams(("parallel", "arbitrary")),
        name="ffn_ln",
    )(x2, w1, w3, w2, ln_g, ln_b)


MOE_CAP = 160
MOE_CAP_PAD = -(-MOE_CAP // LANES) * LANES
MOE_CHUNKS = -(-ROUTE_ROWS // MOE_CAP)


def _moe_kernel(cnt_ref, x_ref, gate_ref, pos_ref, *rest, groups):
    post_refs = rest[:groups]
    w1_ref, w3_ref, w2_ref, g_ref, b_ref, o_ref, acc_ref, xb_ref = rest[groups:]
    i = pl.program_id(0)
    e = pl.program_id(1)
    lane = _iota((1, LANES), 1)

    @pl.when(e == 0)
    def _():
        acc_ref[...] = jnp.zeros_like(acc_ref)
        xb_ref[...] = x_ref[...].astype(BF16)

    for gi in range(groups):
        rows = slice(ROUTE_ROWS * gi, ROUTE_ROWS * (gi + 1))
        cnt = cnt_ref[(i * groups + gi) * N_EXPERTS + e]
        gate_e = jnp.sum(jnp.where(lane == e, gate_ref[rows, :], 0.0), -1, keepdims=True)
        slot_col = jnp.sum(jnp.where(lane == e, pos_ref[rows, :], 0.0), -1, keepdims=True)
        slot_row = post_refs[gi][0]
        for k in range(MOE_CHUNKS):
            @pl.when(cnt > k * MOE_CAP)
            def _():
                want = (_iota((MOE_CAP, 1), 0) + k * MOE_CAP).astype(F32)
                pick = jnp.where(slot_row == want, 1.0, 0.0).astype(BF16)
                xg = _dot(pick, xb_ref[rows, :]).astype(BF16)
                a = jax.nn.silu(_dot(xg, w1_ref[0])) * _dot(xg, w3_ref[0])
                y = _dot(a.astype(BF16), w2_ref[0])
                col = _iota((1, MOE_CAP_PAD), 1)
                put = jnp.where((slot_col == (col + k * MOE_CAP).astype(F32)) & (col < MOE_CAP),
                                1.0, 0.0).astype(BF16)
                yb = y.astype(BF16)
                if MOE_CAP_PAD > MOE_CAP:
                    yb = jnp.concatenate([yb, jnp.zeros((MOE_CAP_PAD - MOE_CAP, D_MODEL), BF16)], 0)
                acc_ref[rows, :] += gate_e * _dot(put, yb)

    @pl.when(e == pl.num_programs(1) - 1)
    def _():
        o_ref[...] = _layer_norm(DEEPNORM_ALPHA * x_ref[...] + acc_ref[...], g_ref[...], b_ref[...])


def _moe(x2, gates, pos, post, cnt, w1, w3, w2, ln_g, ln_b, groups=2):
    n = x2.shape[0]
    ne, _, tf = w1.shape
    tm = groups * ROUTE_ROWS
    post3 = post.reshape(n // ROUTE_ROWS * N_EXPERTS, 1, ROUTE_ROWS)
    full = lambda a: pl.BlockSpec(a.shape, lambda i, e, c: (0,) * a.ndim)
    row = lambda w: pl.BlockSpec((tm, w), lambda i, e, c: (i, 0))
    post_spec = lambda gi: pl.BlockSpec((1, 1, ROUTE_ROWS),
                                        lambda i, e, c: ((i * groups + gi) * N_EXPERTS + e, 0, 0))
    grid_spec = pltpu.PrefetchScalarGridSpec(
        num_scalar_prefetch=1,
        grid=(n // tm, ne),
        in_specs=[row(D_MODEL), row(LANES), row(LANES)] + [post_spec(gi) for gi in range(groups)]
                 + [pl.BlockSpec((1, D_MODEL, tf), lambda i, e, c: (e, 0, 0)),
                    pl.BlockSpec((1, D_MODEL, tf), lambda i, e, c: (e, 0, 0)),
                    pl.BlockSpec((1, tf, D_MODEL), lambda i, e, c: (e, 0, 0)),
                    full(ln_g), full(ln_b)],
        out_specs=row(D_MODEL),
        scratch_shapes=[pltpu.VMEM((tm, D_MODEL), F32), pltpu.VMEM((tm, D_MODEL), BF16)],
    )
    return pl.pallas_call(
        functools.partial(_moe_kernel, groups=groups),
        grid_spec=grid_spec,
        out_shape=jax.ShapeDtypeStruct((n, D_MODEL), F32),
        compiler_params=_cparams(("parallel", "arbitrary")),
        name="moe_top2_ln",
    )(cnt, x2, gates, pos, *([post3] * groups), w1, w3, w2, ln_g, ln_b)


def _in_proj_columns():
    src = np.full((IN_P,), -1, np.int64)
    scale = np.ones((IN_P,), np.float32)
    o_cq, o_ckv, o_kr = 0, MLA_Q_RANK, MLA_Q_RANK + MLA_KV_RANK
    o_fox = o_kr + MLA_ROPE
    o_foxf = o_fox + 3 * FOX_HEADS * HEAD_DIM
    o_nq = o_foxf + FOX_HEADS
    o_nkv = o_nq + NSA_HEADS * HEAD_DIM
    o_ng = o_nkv + 6 * HEAD_DIM
    src[0:o_kr] = np.arange(o_kr)
    small = o_kr
    src[small + KR_LANE:small + KR_LANE + MLA_ROPE] = o_kr + np.arange(MLA_ROPE)
    src[small + FOXF_LANE:small + FOXF_LANE + FOX_HEADS] = o_foxf + np.arange(FOX_HEADS)
    src[small + NSAG_LANE:small + NSAG_LANE + 3 * NSA_HEADS] = o_ng + np.arange(3 * NSA_HEADS)
    n_fox = FOX_HEADS * HEAD_DIM
    src[FQ_OFF:FQ_OFF + 3 * n_fox] = o_fox + np.arange(3 * n_fox)
    scale[FQ_OFF:FQ_OFF + n_fox] = HEAD_DIM ** -0.5
    src[NQ_OFF:NQ_OFF + NSA_HEADS * HEAD_DIM] = o_nq + np.arange(NSA_HEADS * HEAD_DIM)
    scale[NQ_OFF:NQ_OFF + NSA_HEADS * HEAD_DIM] = HEAD_DIM ** -0.5
    src[NC_OFF:NC_OFF + 6 * HEAD_DIM] = o_nkv + np.arange(6 * HEAD_DIM)
    return src, scale


def _gather_cols(w, src, scale=None):
    scale = np.ones(src.shape, np.float32) if scale is None else scale
    parts, i, n = [], 0, len(src)
    while i < n:
        j = i + 1
        while j < n and scale[j] == scale[i] and (src[j] == src[j - 1] + 1 if src[i] >= 0 else src[j] < 0):
            j += 1
        if src[i] < 0:
            parts.append(jnp.zeros(w.shape[:-1] + (j - i,), w.dtype))
        else:
            seg = w[..., int(src[i]):int(src[i]) + (j - i)]
            parts.append(seg if scale[i] == 1.0 else seg * float(scale[i]))
        i = j
    return jnp.concatenate(parts, axis=-1)


def _tables(s):
    half = MLA_ROPE // 2
    freqs = ROPE_THETA ** (-jnp.arange(half, dtype=F32) / half)
    ang = jnp.arange(s).astype(F32)[:, None] * freqs[None, :]
    cos, sin = jnp.cos(ang), jnp.sin(ang)
    z = lambda w: jnp.zeros((s, w), F32)
    tabs = {
        "cos": jnp.concatenate([jnp.ones((s, MLA_NOPE), F32), cos, cos, z(LANES - MLA_NOPE - MLA_ROPE)], 1),
        "sina": jnp.concatenate([z(MLA_NOPE), -sin, z(half), z(LANES - MLA_NOPE - MLA_ROPE)], 1),
        "sinb": jnp.concatenate([z(MLA_NOPE), z(half), sin, z(LANES - MLA_NOPE - MLA_ROPE)], 1),
    }
    pq = np.zeros((LANES, FOX_HEADS * LANES), np.float32)
    pk = np.zeros((LANES, FOX_HEADS * LANES), np.float32)
    oneq = np.zeros((1, FOX_HEADS * LANES), np.float32)
    onek = np.zeros((1, FOX_HEADS * LANES), np.float32)
    for h in range(FOX_HEADS):
        for t in range(3):
            pq[FOXF_LANE + FOX_HEADS * t + h, LANES * h + HEAD_DIM + t] = 1.0
            pk[FOXF_LANE + FOX_HEADS * t + h, LANES * h + HEAD_DIM + 3 + t] = -1.0
            oneq[0, LANES * h + HEAD_DIM + 3 + t] = 1.0
            onek[0, LANES * h + HEAD_DIM + t] = 1.0
    tabs.update(pq=jnp.asarray(pq, BF16), pk=jnp.asarray(pk, BF16), oneq=jnp.asarray(oneq), onek=jnp.asarray(onek))
    qaux = np.zeros((1, NSA_HEADS * LANES), np.float32)
    for h in range(NSA_HEADS):
        qaux[0, LANES * h + NSA_AUX_LANE] = ALIBI[h] * NSA_SEL_BLOCK
        qaux[0, LANES * h + NSA_AUX_LANE + 1] = ALIBI[h]
    kaux = np.zeros((s, LANES), np.float32)
    pos = np.arange(s)
    kaux[:, NSA_AUX_LANE] = pos // NSA_SEL_BLOCK
    kaux[:, NSA_AUX_LANE + 1] = pos % NSA_SEL_BLOCK
    kaux[pos, NSA_SEL_LANE + pos // NSA_SEL_BLOCK] = 1.0
    tabs.update(nsa_qaux=jnp.asarray(qaux), nsa_kaux=jnp.asarray(kaux))
    return tabs


def _layer_weights(p, l):
    qd = MLA_NOPE + MLA_ROPE
    src_q = np.full((MLA_HEADS * LANES,), -1, np.int64)
    src_k = np.full((MLA_HEADS * LANES,), -1, np.int64)
    src_v = np.full((MLA_HEADS * LANES,), -1, np.int64)
    for h in range(MLA_HEADS):
        src_q[LANES * h + np.arange(qd)] = qd * h + np.arange(qd)
        src_k[LANES * h + np.arange(MLA_NOPE)] = 2 * HEAD_DIM * h + np.arange(MLA_NOPE)
        src_v[LANES * h + np.arange(HEAD_DIM)] = 2 * HEAD_DIM * h + MLA_NOPE + np.arange(HEAD_DIM)
    w_out = p["w_out"][l]
    n_attn = (MLA_HEADS + FOX_HEADS) * HEAD_DIM
    wb = w_out[n_attn:].reshape(NSA_HEADS, HEAD_DIM, D_MODEL)
    wb = jnp.concatenate([wb, jnp.zeros_like(wb)], axis=1).reshape(NSA_HEADS * LANES, D_MODEL)
    bf = jnp.zeros((1, LANES), F32).at[0, FOXF_LANE:FOXF_LANE + FOX_HEADS].set(p["b_forget"][l])

    half = NSA_CMP_BLOCK // 2
    kpos, vpos = p["cmp_k_pos"][l], p["cmp_v_pos"][l]
    posa = jnp.concatenate([kpos[:half], vpos[:half]], -1).reshape(1, half * LANES)
    posb = jnp.concatenate([kpos[half:], vpos[half:]], -1).reshape(1, half * LANES)

    def w1_half(lo):
        k = p["cmp_k_w1"][l].reshape(NSA_CMP_BLOCK, HEAD_DIM, NSA_CMP_HIDDEN)[lo:lo + half]
        v = p["cmp_v_w1"][l].reshape(NSA_CMP_BLOCK, HEAD_DIM, NSA_CMP_HIDDEN)[lo:lo + half]
        zk = jnp.zeros_like(k)
        top = jnp.concatenate([k, zk], -1)
        bot = jnp.concatenate([zk, v], -1)
        return jnp.concatenate([top, bot], 1).reshape(half * LANES, 2 * NSA_CMP_HIDDEN).astype(BF16)

    zw2 = jnp.zeros((NSA_CMP_HIDDEN, HEAD_DIM), F32)
    w2 = jnp.concatenate([jnp.concatenate([p["cmp_k_w2"][l], zw2], 1),
                          jnp.concatenate([zw2, p["cmp_v_w2"][l]], 1)], 0).astype(BF16)
    return {
        "g_cq": p["g_cq"][l][None, :], "g_ckv": p["g_ckv"][l][None, :],
        "w_uq": _gather_cols(p["w_uq"][l], src_q).astype(BF16),
        "w_uk": _gather_cols(p["w_ukv"][l], src_k).astype(BF16),
        "w_uv": _gather_cols(p["w_ukv"][l], src_v).astype(BF16),
        "b_forget": bf,
        "cmp_posa": posa, "cmp_posb": posb, "cmp_wa": w1_half(0), "cmp_wb": w1_half(half), "cmp_w2": w2,
        "w_out_a": w_out[:n_attn].astype(BF16), "w_out_b": wb.astype(BF16),
        "ln1_g": p["ln1_g"][l][None, :], "ln1_b": p["ln1_b"][l][None, :],
        "ln2_g": p["ln2_g"][l][None, :], "ln2_b": p["ln2_b"][l][None, :],
    }


def kernel(x, w_in, b_forget, g_cq, w_uq, g_ckv, w_ukv, cmp_k_pos, cmp_k_w1, cmp_k_w2, cmp_v_pos, cmp_v_w1,
           cmp_v_w2, w_out, ln1_g, ln1_b, ln2_g, ln2_b, ffn_w1, ffn_w3, ffn_w2, router_w, moe_w1, moe_w3,
           moe_w2):
    b, s, d = x.shape
    assert d == D_MODEL and s % 512 == 0 and s // NSA_CMP_STRIDE == LANES, (b, s, d)
    p = dict(b_forget=b_forget, g_cq=g_cq, w_uq=w_uq, g_ckv=g_ckv, w_ukv=w_ukv, cmp_k_pos=cmp_k_pos,
             cmp_k_w1=cmp_k_w1, cmp_k_w2=cmp_k_w2, cmp_v_pos=cmp_v_pos, cmp_v_w1=cmp_v_w1, cmp_v_w2=cmp_v_w2,
             w_out=w_out, ln1_g=ln1_g, ln1_b=ln1_b, ln2_g=ln2_g, ln2_b=ln2_b)
    src, scale = _in_proj_columns()
    w_in_p = _gather_cols(w_in, src, scale).astype(BF16)
    tabs = _tables(s)
    n = b * s
    x2 = x.reshape(n, d)
    for l in range(DEPTH):
        wl = _layer_weights(p, l)
        small, nq, nc, nsl, nwin, q_all, k_all, v_all = _front(x2, w_in_p[l], wl, tabs, s)
        o_attn = _flash(*(t.reshape(b, s, -1) for t in (q_all, k_all, v_all)))
        kcvc = _compress(nc.reshape(b, s, LANES), wl)
        o_nsa = _nsa(nq.reshape(b, s, NSA_HEADS * LANES), kcvc, nsl.reshape(b, s, 2 * LANES),
                     nwin.reshape(b, s, 2 * LANES), small)
        j = l // 2
        if l % 2 == 0:
            x2 = _out_proj(o_attn.reshape(n, -1), o_nsa.reshape(n, -1), x2, wl)
            x2 = _ffn(x2, ffn_w1[j].astype(BF16), ffn_w3[j].astype(BF16), ffn_w2[j].astype(BF16),
                      wl["ln2_g"], wl["ln2_b"])
        else:
            rw = jnp.pad(router_w[j], ((0, 0), (0, LANES - N_EXPERTS)))
            rw_hi = rw.astype(BF16)
            rw_lo = (rw - rw_hi.astype(F32)).astype(BF16)
            rw = jnp.concatenate([jnp.concatenate([rw_hi, rw_lo], 1),
                                  jnp.concatenate([rw_hi, jnp.zeros_like(rw_lo)], 1)], 0)
            x2, gates, pos, post, cnt = _out_proj(o_attn.reshape(n, -1), o_nsa.reshape(n, -1), x2, wl, router=rw)
            cnt = cnt[::8, :N_EXPERTS].reshape(-1)
            x2 = _moe(x2, gates, pos, post, cnt, moe_w1[j].astype(BF16), moe_w3[j].astype(BF16),
                      moe_w2[j].astype(BF16), wl["ln2_g"], wl["ln2_b"])
    return x2.reshape(b, s, d)
```

```python
import functools

import numpy as np
import jax
import jax.numpy as jnp
from jax import lax
from jax.experimental import pallas as pl
from jax.experimental.pallas import tpu as pltpu

F32 = jnp.float32
BF16 = jnp.bfloat16

D_MODEL = 1024
HEAD_DIM = 64
LANES = 128
MLA_HEADS = 6
MLA_Q_RANK = 384
MLA_KV_RANK = 256
MLA_NOPE = 64
MLA_ROPE = 32
ROPE_THETA = 10000.0
FOX_HEADS = 6
NSA_HEADS = 4
NSA_CMP_BLOCK = 32
NSA_CMP_STRIDE = 16
NSA_CMP_HIDDEN = 128
NSA_SEL_BLOCK = 64
NSA_SEL_TOPN = 8
NSA_WINDOW = 256
NSA_FORCE_SCORE = 1.0e4
N_EXPERTS = 8
NORM_EPS = 1e-5
DEPTH = 2
DEEPNORM_ALPHA = (2 * DEPTH) ** 0.25
NEG = -1e30
VMEM_LIMIT = 56 * 1024 * 1024

ROW_TILE = 512
FLASH_TQ, FLASH_TK = 1024, 512
NSA_TQ = 512

KR_LANE = 64
FOXF_LANE = 96
NSAG_LANE = 102

ZA_W = 768
FQ_OFF = ZA_W
FK_OFF = FQ_OFF + 384
FV_OFF = FK_OFF + 384
NQ_OFF = FV_OFF + 384
NC_OFF = NQ_OFF + 256
NKV_OFF = NC_OFF + 128
IN_P = NKV_OFF + 256

NSA_AUX_LANE = HEAD_DIM
NSA_SEL_LANE = HEAD_DIM + 2
NSA_MASK_BIG = 2.0 ** 126

ALIBI = tuple(2.0 ** (-8.0 * (i + 1) / NSA_HEADS) for i in range(NSA_HEADS))


def _dot(a, b, **kw):
    return jnp.dot(a, b, preferred_element_type=F32, **kw)


def _dot_nt(a, b):
    return lax.dot_general(a, b, (((1,), (1,)), ((), ())), preferred_element_type=F32)


def _iota(shape, dim):
    return lax.broadcasted_iota(jnp.int32, shape, dim)


def _split3(x):
    hi = x.astype(BF16)
    r = x - hi.astype(F32)
    mid = r.astype(BF16)
    return hi, mid, (r - mid.astype(F32)).astype(BF16)


def _dot_exact_rhs(x, w01):
    hi, mid, lo = _split3(x)
    return _dot(jnp.concatenate([hi, mid, lo], 1), jnp.concatenate([w01, w01, w01], 0))


def _dot_split(x, w_stack):
    hi = x.astype(BF16)
    lo = (x - hi.astype(F32)).astype(BF16)
    r = _dot(jnp.concatenate([hi, lo], 1), w_stack)
    half = r.shape[1] // 2
    return r[:, :half] + r[:, half:]


def _cparams(sem):
    return pltpu.CompilerParams(dimension_semantics=sem, vmem_limit_bytes=VMEM_LIMIT)


def _front_kernel(x_ref, w_ref, qaux_ref, kaux_ref, gcq_ref, wuq_ref, gckv_ref, wuk_ref, wuv_ref, bf_ref,
                  cos_ref, sina_ref, sinb_ref, pq_ref, pk_ref, oneq_ref, onek_ref,
                  small_ref, nq_ref, nc_ref, nsl_ref, nwin_ref, q_out, k_out, v_out, carry_ref,
                  *, tm, tiles_per_seq):
    @pl.when(pl.program_id(0) % tiles_per_seq == 0)
    def _():
        carry_ref[...] = jnp.zeros_like(carry_ref)

    xb = x_ref[...].astype(BF16)

    def mm(a, b):
        return _dot(xb, w_ref[:, a:b])

    lane = _iota((1, LANES), 1)

    def halves(pair):
        return (jnp.where(lane < HEAD_DIM, pair, 0.0),
                jnp.where(lane < HEAD_DIM, pltpu.roll(pair, HEAD_DIM, 1), 0.0))

    def expand(z):
        return [blk for pr in range(z.shape[1] // LANES) for blk in halves(z[:, LANES * pr:LANES * (pr + 1)])]

    def pad_v(blk):
        return jnp.where(lane < HEAD_DIM, blk, jnp.where(lane == HEAD_DIM, 1.0, 0.0)).astype(BF16)

    cos = cos_ref[...]
    sina = sina_ref[...]
    sinb = sinb_ref[...]

    def rope(blk):
        return blk * cos + pltpu.roll(blk, LANES - 16, 1) * sina + pltpu.roll(blk, 16, 1) * sinb

    za = mm(0, ZA_W)
    cq = za[:, 0:MLA_Q_RANK]
    ckv = za[:, MLA_Q_RANK:MLA_Q_RANK + MLA_KV_RANK]
    small = za[:, MLA_Q_RANK + MLA_KV_RANK:ZA_W]
    small_ref[...] = small

    xn = cq * lax.rsqrt(jnp.mean(cq * cq, -1, keepdims=True) + NORM_EPS) * gcq_ref[...]
    q = _dot(xn.astype(BF16), wuq_ref[...])
    cn = ckv * lax.rsqrt(jnp.mean(ckv * ckv, -1, keepdims=True) + NORM_EPS) * gckv_ref[...]
    cnb = cn.astype(BF16)
    kn = _dot(cnb, wuk_ref[...])
    v = _dot(cnb, wuv_ref[...])
    kr = rope(small)
    mla_scale = (MLA_NOPE + MLA_ROPE) ** -0.5
    for h in range(MLA_HEADS):
        sl = slice(LANES * h, LANES * (h + 1))
        q_out[:, sl] = (rope(q[:, sl]) * mla_scale).astype(BF16)
        k_out[:, sl] = (kn[:, sl] + kr).astype(BF16)
        v_out[:, sl] = pad_v(v[:, sl])

    fmask = (lane >= FOXF_LANE) & (lane < FOXF_LANE + FOX_HEADS)
    lf = jnp.where(fmask, jax.nn.log_sigmoid(small + bf_ref[...]), 0.0)
    tril = jnp.where(_iota((tm, tm), 0) >= _iota((tm, tm), 1), 1.0, 0.0).astype(BF16)
    lf_hi, lf_mid, lf_lo = _split3(lf)
    part = _dot(tril, jnp.concatenate([lf_hi, lf_mid], 1))
    cs = part[:, :LANES] + part[:, LANES:] + _dot(tril, lf_lo) + carry_ref[...]
    carry_ref[...] = cs[tm - 1:tm, :]
    hi, mid, lo = (t.astype(F32) for t in _split3(cs))
    c3 = (hi + pltpu.roll(mid, FOX_HEADS, 1) + pltpu.roll(lo, 2 * FOX_HEADS, 1)).astype(BF16)
    augq = _dot(c3, pq_ref[...]) + oneq_ref[...]
    augk = _dot(c3, pk_ref[...]) + onek_ref[...]
    off = MLA_HEADS * LANES
    fq = expand(mm(FQ_OFF, FK_OFF))
    fk = expand(mm(FK_OFF, FV_OFF))
    fv = expand(mm(FV_OFF, NQ_OFF))
    for h in range(FOX_HEADS):
        sl = slice(LANES * h, LANES * (h + 1))
        dst = slice(off + LANES * h, off + LANES * (h + 1))
        q_out[:, dst] = (fq[h] + augq[:, sl]).astype(BF16)
        k_out[:, dst] = (fk[h] + augk[:, sl]).astype(BF16)
        v_out[:, dst] = pad_v(fv[h])

    qaux = qaux_ref[...]
    for h, blk in enumerate(expand(mm(NQ_OFF, NC_OFF))):
        sl = slice(LANES * h, LANES * (h + 1))
        nq_ref[:, sl] = (blk + qaux[:, sl]).astype(BF16)
    nc_ref[...] = mm(NC_OFF, NKV_OFF)
    kaux = kaux_ref[...]
    nkv = mm(NKV_OFF, IN_P)
    for i, out_ref in enumerate((nsl_ref, nwin_ref)):
        k_blk, v_blk = halves(nkv[:, LANES * i:LANES * (i + 1)])
        out_ref[:, 0:LANES] = (k_blk + kaux).astype(BF16)
        out_ref[:, LANES:2 * LANES] = pad_v(v_blk)


def _front(x2, w_p, wl, tabs, s, tm=ROW_TILE):
    n = x2.shape[0]
    ns = s // tm
    nh = MLA_HEADS + FOX_HEADS
    full = lambda a: pl.BlockSpec(a.shape, lambda i: (0,) * a.ndim)
    tab = pl.BlockSpec((tm, LANES), lambda i: (i % ns, 0))
    consts = [wl["g_cq"], wl["w_uq"], wl["g_ckv"], wl["w_uk"], wl["w_uv"], wl["b_forget"]]
    tail = [tabs["pq"], tabs["pk"], tabs["oneq"], tabs["onek"]]
    widths = [(LANES, F32), (NSA_HEADS * LANES, BF16), (LANES, F32), (2 * LANES, BF16), (2 * LANES, BF16),
              (nh * LANES, BF16), (nh * LANES, BF16), (nh * LANES, BF16)]
    return pl.pallas_call(
        functools.partial(_front_kernel, tm=tm, tiles_per_seq=ns),
        grid=(n // tm,),
        in_specs=[pl.BlockSpec((tm, D_MODEL), lambda i: (i, 0)), full(w_p), full(tabs["nsa_qaux"]), tab]
                 + [full(a) for a in consts] + [tab, tab, tab] + [full(a) for a in tail],
        out_specs=[pl.BlockSpec((tm, w), lambda i: (i, 0)) for w, _ in widths],
        out_shape=[jax.ShapeDtypeStruct((n, w), dt) for w, dt in widths],
        scratch_shapes=[pltpu.VMEM((1, LANES), F32)],
        compiler_params=_cparams(("arbitrary",)),
        name="front_proj",
    )(x2, w_p, tabs["nsa_qaux"], tabs["nsa_kaux"], *consts, tabs["cos"], tabs["sina"], tabs["sinb"], *tail)


def _flash_kernel(q_ref, k_ref, v_ref, o_ref, m_ref, acc_ref, s_ref, *, tq, tk):
    qi = pl.program_id(2)
    m_ref[...] = jnp.full_like(m_ref, NEG)
    acc_ref[...] = jnp.zeros_like(acc_ref)
    nsub = tq // tk
    nfull = qi * nsub
    heads = [slice(LANES * h, LANES * (h + 1)) for h in range(2)]

    def scores(j, r0, sl):
        k0 = pl.multiple_of(j * tk, tk)
        return _dot_nt(q_ref[0, r0:tq, sl], k_ref[0, pl.ds(k0, tk), sl])

    def consume(s, h, j, r0, masked):
        k0 = pl.multiple_of(j * tk, tk)
        if masked:
            s = jnp.where(_iota((tq - r0, tk), 0) >= _iota((tq - r0, tk), 1), s, NEG)
        chunks = [s[:, LANES * c:LANES * (c + 1)] for c in range(tk // LANES)]
        m_prev = m_ref[h, r0:tq, :]
        m_new = jnp.maximum(m_prev, jnp.max(functools.reduce(jnp.maximum, chunks), -1, keepdims=True))
        p = jnp.concatenate([jnp.exp(c - m_new) for c in chunks], 1).astype(BF16)
        acc_ref[h, r0:tq, :] = (jnp.exp(m_prev - m_new) * acc_ref[h, r0:tq, :]
                                + _dot(p, v_ref[0, pl.ds(k0, tk), heads[h]]))
        m_ref[h, r0:tq, :] = m_new

    assert nsub == 2
    for h in range(2):
        s_ref[0, h] = scores(0, 0, heads[h])

    def body(i, carry):
        j = 2 * i
        for b in range(2):
            for h in range(2):
                s_ref[1 - b, h] = scores(j + b + 1, 0, heads[h])
            for h in range(2):
                consume(s_ref[b, h], h, j + b, 0, False)
        return carry

    lax.fori_loop(0, qi, body, 0)
    for h in range(2):
        consume(s_ref[0, h], h, nfull, 0, True)
    for d in range(1, nsub):
        for h in range(2):
            consume(scores(nfull + d, d * tk, heads[h]), h, nfull + d, d * tk, True)

    lane = _iota((1, LANES), 1)
    o0 = acc_ref[0]
    o1 = acc_ref[1]
    o0 = o0 / o0[:, HEAD_DIM:HEAD_DIM + 1]
    o1 = o1 / o1[:, HEAD_DIM:HEAD_DIM + 1]
    o_ref[0] = jnp.where(lane < HEAD_DIM, o0, pltpu.roll(o1, HEAD_DIM, 1)).astype(o_ref.dtype)


def _flash(q_all, k_all, v_all, tq=FLASH_TQ, tk=FLASH_TK):
    b, s, _ = q_all.shape
    npair = (MLA_HEADS + FOX_HEADS) // 2
    return pl.pallas_call(
        functools.partial(_flash_kernel, tq=tq, tk=tk),
        grid=(b, npair, s // tq),
        in_specs=[pl.BlockSpec((1, tq, 2 * LANES), lambda bi, p, qi: (bi, qi, p)),
                  pl.BlockSpec((1, s, 2 * LANES), lambda bi, p, qi: (bi, 0, p)),
                  pl.BlockSpec((1, s, 2 * LANES), lambda bi, p, qi: (bi, 0, p))],
        out_specs=pl.BlockSpec((1, tq, LANES), lambda bi, p, qi: (bi, qi, p)),
        out_shape=jax.ShapeDtypeStruct((b, s, npair * LANES), BF16),
        scratch_shapes=[pltpu.VMEM((2, tq, LANES), F32), pltpu.VMEM((2, tq, LANES), F32),
                        pltpu.VMEM((2, 2, tq, tk), F32)],
        compiler_params=_cparams(("parallel", "parallel", "arbitrary")),
        name="flash_attn",
    )(q_all, k_all, v_all)


def _cmp_kernel(nc_ref, posa_ref, posb_ref, wa_ref, wb_ref, w2_ref, out_ref, *, n_chunk):
    a = jnp.zeros((n_chunk, 2 * NSA_CMP_HIDDEN), F32)
    b = jnp.zeros((n_chunk, 2 * NSA_CMP_HIDDEN), F32)
    for l in range(NSA_CMP_STRIDE):
        t = nc_ref[0, pl.ds(l, n_chunk, stride=NSA_CMP_STRIDE), :]
        sl = slice(LANES * l, LANES * (l + 1))
        a = a + _dot((t + posa_ref[:, sl]).astype(BF16), wa_ref[sl, :])
        b = b + _dot((t + posb_ref[:, sl]).astype(BF16), wb_ref[sl, :])
    pre = a + pltpu.roll(b, n_chunk - 1, 0)
    hid = jax.nn.silu(pre)
    out_ref[0] = _dot(hid.astype(BF16), w2_ref[...]).astype(out_ref.dtype)


def _compress(nc3, wl):
    b, s, _ = nc3.shape
    n_chunk = s // NSA_CMP_STRIDE
    full = lambda a: pl.BlockSpec(a.shape, lambda bi: (0,) * a.ndim)
    consts = [wl["cmp_posa"], wl["cmp_posb"], wl["cmp_wa"], wl["cmp_wb"], wl["cmp_w2"]]
    return pl.pallas_call(
        functools.partial(_cmp_kernel, n_chunk=n_chunk),
        grid=(b,),
        in_specs=[pl.BlockSpec((1, s, LANES), lambda bi: (bi, 0, 0))] + [full(a) for a in consts],
        out_specs=pl.BlockSpec((1, n_chunk, LANES), lambda bi: (bi, 0, 0)),
        out_shape=jax.ShapeDtypeStruct((b, n_chunk, LANES), BF16),
        compiler_params=_cparams(("parallel",)),
        name="nsa_compress",
    )(nc3, *consts)


def _masked_softmax(s, mask):
    s = jnp.where(mask, s, NEG)
    m = jnp.max(s, -1, keepdims=True)
    p = jnp.where(mask, jnp.exp(s - m), 0.0)
    return p / jnp.maximum(jnp.sum(p, -1, keepdims=True), 1e-30)


def _nsa_kernel(nq_ref, kc_ref, ksl_ref, kwin_ref, g_ref, o_ref, m_ref, acc_ref, qst_ref, s_ref, *, tq, n_cmp):
    qi = pl.program_id(1)
    t0 = pl.multiple_of(qi * tq, tq)
    rpos = t0 + _iota((tq, 1), 0)
    lane = _iota((1, LANES), 1)
    qs = [nq_ref[0, :, LANES * h:LANES * (h + 1)] for h in range(NSA_HEADS)]

    dist_i = rpos - (NSA_CMP_STRIDE * lane + NSA_CMP_BLOCK - 1)
    valid_c = (dist_i >= 0) & (lane < n_cmp)
    dist_c = dist_i.astype(F32)
    kc = kc_ref[0]
    psum = jnp.zeros((tq, LANES), F32)
    o_cmp = []
    for h in range(NSA_HEADS):
        q_head = jnp.where(lane < HEAD_DIM, qs[h], jnp.zeros_like(qs[h]))
        p = _masked_softmax(_dot_nt(q_head, kc) - ALIBI[h] * dist_c, valid_c)
        psum = psum + p
        o_cmp.append(_dot(p.astype(BF16), kc))

    n_i = _iota((LANES, LANES), 0)
    j_i = _iota((LANES, LANES), 1)
    ov = ((NSA_CMP_STRIDE * n_i < NSA_SEL_BLOCK * (j_i + 1))
          & (NSA_CMP_STRIDE * n_i + NSA_CMP_BLOCK > NSA_SEL_BLOCK * j_i)
          & (n_i < n_cmp))
    imp = _dot_exact_rhs(psum, jnp.where(ov, 1.0, 0.0).astype(BF16))
    cur = jnp.right_shift(rpos, 6)
    forced = (lane == 0) | (lane == cur) | (lane == cur - 1)
    future = lane * NSA_SEL_BLOCK > rpos
    n_blk = ksl_ref.shape[1] // NSA_SEL_BLOCK
    work = jnp.where(forced, NSA_FORCE_SCORE, jnp.where(future, -1.0, imp))
    work = jnp.where(lane < n_blk, work, -jnp.inf)
    work_t = work.T[0:n_blk, :]
    blk_id = _iota((n_blk, 1), 0)
    rank = jnp.zeros((n_blk, tq), F32)
    for jp in range(n_blk):
        other = work_t[jp:jp + 1, :]
        beats = (other > work_t) | ((other == work_t) & (jp < blk_id))
        rank = rank + jnp.where(beats, 1.0, 0.0)
    sel_t = jnp.where(rank < NSA_SEL_TOPN, 1.0, 0.0)
    sel = jnp.concatenate([sel_t, jnp.zeros((LANES - n_blk, tq), F32)], 0).T > 0.5
    sel_lanes = pltpu.roll(jnp.where(sel, 1.0, 0.0), NSA_SEL_LANE, 1)
    in_sel = (lane >= NSA_SEL_LANE) & (lane < NSA_SEL_LANE + n_blk)
    sel_bias = jnp.where(in_sel, (sel_lanes - 1.0) * NSA_MASK_BIG, 0.0)
    for h in range(NSA_HEADS):
        qst_ref[h * tq:(h + 1) * tq, :] = (qs[h].astype(F32) + sel_bias).astype(BF16)
    m_ref[...] = jnp.full_like(m_ref, NEG)
    acc_ref[...] = jnp.zeros_like(acc_ref)

    def online(s, v):
        chunks = [s[:, LANES * i:LANES * (i + 1)] for i in range(s.shape[1] // LANES)]
        m_prev = m_ref[...]
        m_new = jnp.maximum(m_prev, jnp.max(functools.reduce(jnp.maximum, chunks), -1, keepdims=True))
        p = jnp.concatenate([jnp.exp(ch - m_new) for ch in chunks], 1).astype(BF16)
        acc_ref[...] = jnp.exp(m_prev - m_new) * acc_ref[...] + _dot(p, v)
        m_ref[...] = m_new

    def scores(c):
        return _dot_nt(qst_ref[...], ksl_ref[0, pl.ds(pl.multiple_of(c * tq, tq), tq), 0:LANES])

    def values(c):
        return ksl_ref[0, pl.ds(pl.multiple_of(c * tq, tq), tq), LANES:2 * LANES]

    s_ref[0] = scores(0)

    def chunk_pair(i, carry):
        j = 2 * i
        s_ref[1] = scores(j + 1)
        online(s_ref[0], values(j))
        s_ref[0] = scores(j + 2)
        online(s_ref[1], values(j + 1))
        return carry

    lax.fori_loop(0, jnp.right_shift(qi, 1), chunk_pair, 0)

    @pl.when((qi & 1) == 1)
    def _():
        online(s_ref[0], values(qi - 1))
        s_ref[0] = scores(qi)

    nrow = NSA_HEADS * tq
    causal = (_iota((nrow, tq), 0) & (tq - 1)) >= _iota((nrow, tq), 1)
    online(jnp.where(causal, s_ref[0], NEG), values(qi))

    wlen = tq + NSA_WINDOW
    w0 = pl.multiple_of(jnp.maximum(t0 - NSA_WINDOW, 0), min(tq, NSA_WINDOW))
    k_w = kwin_ref[0, pl.ds(w0, wlen), 0:LANES]
    v_w = kwin_ref[0, pl.ds(w0, wlen), LANES:2 * LANES]
    d_w = (t0 - w0) + _iota((tq, wlen), 0) - _iota((tq, wlen), 1)
    keep_w = (d_w >= 0) & (d_w < NSA_WINDOW)

    gates = jax.nn.sigmoid(g_ref[...])
    for h in range(NSA_HEADS):
        s_w = jnp.where(keep_w, _dot_nt(qs[h], k_w), NEG)
        p_w = jnp.exp(s_w - jnp.max(s_w, -1, keepdims=True))
        o_win = _dot(p_w.astype(BF16), v_w)
        o_win = o_win / o_win[:, HEAD_DIM:HEAD_DIM + 1]
        o_sel = acc_ref[h * tq:(h + 1) * tq, :]
        o_sel = o_sel / o_sel[:, HEAD_DIM:HEAD_DIM + 1]
        g0 = NSAG_LANE + 3 * h
        o = (gates[:, g0:g0 + 1] * pltpu.roll(o_cmp[h], HEAD_DIM, 1) + gates[:, g0 + 1:g0 + 2] * o_sel
             + gates[:, g0 + 2:g0 + 3] * o_win)
        o_ref[0, :, LANES * h:LANES * (h + 1)] = jnp.where(lane < HEAD_DIM, o, 0.0).astype(o_ref.dtype)


def _nsa(nq, kcvc, nsl, nwin, small, tq=NSA_TQ):
    b, s, _ = nq.shape
    nq_t = s // tq
    n_cmp = (s - NSA_CMP_BLOCK) // NSA_CMP_STRIDE + 1
    return pl.pallas_call(
        functools.partial(_nsa_kernel, tq=tq, n_cmp=n_cmp),
        grid=(b, nq_t),
        in_specs=[pl.BlockSpec((1, tq, NSA_HEADS * LANES), lambda bi, qi: (bi, qi, 0)),
                  pl.BlockSpec((1, kcvc.shape[1], LANES), lambda bi, qi: (bi, 0, 0)),
                  pl.BlockSpec((1, s, 2 * LANES), lambda bi, qi: (bi, 0, 0)),
                  pl.BlockSpec((1, s, 2 * LANES), lambda bi, qi: (bi, 0, 0)),
                  pl.BlockSpec((tq, LANES), lambda bi, qi: (bi * nq_t + qi, 0))],
        out_specs=pl.BlockSpec((1, tq, NSA_HEADS * LANES), lambda bi, qi: (bi, qi, 0)),
        out_shape=jax.ShapeDtypeStruct((b, s, NSA_HEADS * LANES), BF16),
        scratch_shapes=[pltpu.VMEM((NSA_HEADS * tq, LANES), F32), pltpu.VMEM((NSA_HEADS * tq, LANES), F32),
                        pltpu.VMEM((NSA_HEADS * tq, LANES), BF16), pltpu.VMEM((2, NSA_HEADS * tq, tq), F32)],
        compiler_params=_cparams(("parallel", "arbitrary")),
        name="nsa_attn",
    )(nq, kcvc, nsl, nwin, small)


def _layer_norm(y, g, b):
    mu = jnp.mean(y, -1, keepdims=True)
    yc = y - mu
    var = jnp.mean(yc * yc, -1, keepdims=True)
    return yc * lax.rsqrt(var + NORM_EPS) * g + b


ROUTE_ROWS = ROW_TILE


def _out_proj_kernel(*refs, routed):
    if routed:
        (oa_ref, on_ref, x_ref, wa_ref, wb_ref, g_ref, b_ref, rw_ref,
         o_ref, gate_ref, pos_ref, post_ref, cnt_ref) = refs
    else:
        oa_ref, on_ref, x_ref, wa_ref, wb_ref, g_ref, b_ref, o_ref = refs
    mix = _dot(oa_ref[...], wa_ref[...]) + _dot(on_ref[...], wb_ref[...])
    x1 = _layer_norm(DEEPNORM_ALPHA * x_ref[...] + mix, g_ref[...], b_ref[...])
    o_ref[...] = x1
    if routed:
        tm = x1.shape[0]
        lane = _iota((1, LANES), 1)
        lane_f = lane.astype(F32)
        logits = jnp.where(lane < N_EXPERTS, _dot_split(x1, rw_ref[...]), NEG)
        ex = jnp.exp(logits - jnp.max(logits, -1, keepdims=True))
        probs = ex / jnp.sum(ex, -1, keepdims=True)
        p1 = jnp.max(probs, -1, keepdims=True)
        i1 = jnp.min(jnp.where(probs == p1, lane_f, float(LANES)), -1, keepdims=True)
        rest = jnp.where(lane_f == i1, -1.0, probs)
        p2 = jnp.max(rest, -1, keepdims=True)
        i2 = jnp.min(jnp.where(rest == p2, lane_f, float(LANES)), -1, keepdims=True)
        tot = p1 + p2
        gate_ref[...] = jnp.where(lane_f == i1, p1 / tot, jnp.where(lane_f == i2, p2 / tot, 0.0))
        chosen = (lane_f == i1) | (lane_f == i2)
        before = (_iota((tm, tm), 0) > _iota((tm, tm), 1)).astype(BF16)
        onehot = jnp.where(chosen, 1.0, 0.0)
        slot = _dot(before, onehot.astype(BF16))
        posm = jnp.where(chosen, slot, -1.0)
        pos_ref[...] = posm
        post_ref[...] = posm.T[0:N_EXPERTS, :]
        cnt_ref[...] = jnp.broadcast_to(jnp.sum(onehot, 0, keepdims=True), (8, LANES)).astype(jnp.int32)


def _out_proj(o_attn, o_nsa, x2, wl, router=None, tm=ROUTE_ROWS):
    n = x2.shape[0]
    routed = router is not None
    full = lambda a: pl.BlockSpec(a.shape, lambda i: (0,) * a.ndim)
    row = lambda w: pl.BlockSpec((tm, w), lambda i: (i, 0))
    consts = [wl["w_out_a"], wl["w_out_b"], wl["ln1_g"], wl["ln1_b"]] + ([router] if routed else [])
    out_specs = [row(D_MODEL)]
    out_shape = [jax.ShapeDtypeStruct((n, D_MODEL), F32)]
    if routed:
        out_specs += [row(LANES), row(LANES), pl.BlockSpec((N_EXPERTS, tm), lambda i: (i, 0)),
                      pl.BlockSpec((8, LANES), lambda i: (i, 0))]
        out_shape += [jax.ShapeDtypeStruct((n, LANES), F32), jax.ShapeDtypeStruct((n, LANES), F32),
                      jax.ShapeDtypeStruct((n // tm * N_EXPERTS, tm), F32),
                      jax.ShapeDtypeStruct((n // tm * 8, LANES), jnp.int32)]
    outs = pl.pallas_call(
        functools.partial(_out_proj_kernel, routed=routed),
        grid=(n // tm,),
        in_specs=[row(o_attn.shape[1]), row(o_nsa.shape[1]), row(D_MODEL)] + [full(a) for a in consts],
        out_specs=out_specs,
        out_shape=out_shape,
        compiler_params=_cparams(("parallel",)),
        name="out_proj_route_ln" if routed else "out_proj_ln",
    )(o_attn, o_nsa, x2, *consts)
    return outs if routed else outs[0]


def _ffn_kernel(x_ref, w1_ref, w3_ref, w2_ref, g_ref, b_ref, o_ref, acc_ref, xb_ref):
    c = pl.program_id(1)

    @pl.when(c == 0)
    def _():
        acc_ref[...] = jnp.zeros_like(acc_ref)
        xb_ref[...] = x_ref[...].astype(BF16)

    xb = xb_ref[...]
    a = jax.nn.silu(_dot(xb, w1_ref[...])) * _dot(xb, w3_ref[...])
    acc_ref[...] += _dot(a.astype(BF16), w2_ref[...])

    @pl.when(c == pl.num_programs(1) - 1)
    def _():
        o_ref[...] = _layer_norm(DEEPNORM_ALPHA * x_ref[...] + acc_ref[...], g_ref[...], b_ref[...])


def _ffn(x2, w1, w3, w2, ln_g, ln_b, tm=ROW_TILE, nchunk=2):
    n = x2.shape[0]
    tf = w1.shape[1] // nchunk
    full = lambda a: pl.BlockSpec(a.shape, lambda i, c: (0,) * a.ndim)
    return pl.pallas_call(
        _ffn_kernel,
        grid=(n // tm, nchunk),
        in_specs=[pl.BlockSpec((tm, D_MODEL), lambda i, c: (i, 0)),
                  pl.BlockSpec((D_MODEL, tf), lambda i, c: (0, c)),
                  pl.BlockSpec((D_MODEL, tf), lambda i, c: (0, c)),
                  pl.BlockSpec((tf, D_MODEL), lambda i, c: (c, 0)),
                  full(ln_g), full(ln_b)],
        out_specs=pl.BlockSpec((tm, D_MODEL), lambda i, c: (i, 0)),
        out_shape=jax.ShapeDtypeStruct((n, D_MODEL), F32),
        scratch_shapes=[pltpu.VMEM((tm, D_MODEL), F32), pltpu.VMEM((tm, D_MODEL), BF16)],
        compiler_params=_cparams(("parallel", "arbitrary")),
        name="ffn_ln",
    )(x2, w1, w3, w2, ln_g, ln_b)


MOE_CAP = 160
MOE_CAP_PAD = -(-MOE_CAP // LANES) * LANES
MOE_CHUNKS = -(-ROUTE_ROWS // MOE_CAP)


def _moe_kernel(cnt_ref, x_ref, gate_ref, pos_ref, *rest, groups):
    post_refs = rest[:groups]
    w1_ref, w3_ref, w2_ref, g_ref, b_ref, o_ref, acc_ref, xb_ref = rest[groups:]
    i = pl.program_id(0)
    e = pl.program_id(1)
    lane = _iota((1, LANES), 1)

    @pl.when(e == 0)
    def _():
        acc_ref[...] = jnp.zeros_like(acc_ref)
        xb_ref[...] = x_ref[...].astype(BF16)

    for gi in range(groups):
        rows = slice(ROUTE_ROWS * gi, ROUTE_ROWS * (gi + 1))
        cnt = cnt_ref[(i * groups + gi) * N_EXPERTS + e]
        gate_e = jnp.sum(jnp.where(lane == e, gate_ref[rows, :], 0.0), -1, keepdims=True)
        slot_col = jnp.sum(jnp.where(lane == e, pos_ref[rows, :], 0.0), -1, keepdims=True)
        slot_row = post_refs[gi][0]
        for k in range(MOE_CHUNKS):
            @pl.when(cnt > k * MOE_CAP)
            def _():
                want = (_iota((MOE_CAP, 1), 0) + k * MOE_CAP).astype(F32)
                pick = jnp.where(slot_row == want, 1.0, 0.0).astype(BF16)
                xg = _dot(pick, xb_ref[rows, :]).astype(BF16)
                a = jax.nn.silu(_dot(xg, w1_ref[0])) * _dot(xg, w3_ref[0])
                y = _dot(a.astype(BF16), w2_ref[0])
                col = _iota((1, MOE_CAP_PAD), 1)
                put = jnp.where((slot_col == (col + k * MOE_CAP).astype(F32)) & (col < MOE_CAP),
                                1.0, 0.0).astype(BF16)
                yb = y.astype(BF16)
                if MOE_CAP_PAD > MOE_CAP:
                    yb = jnp.concatenate([yb, jnp.zeros((MOE_CAP_PAD - MOE_CAP, D_MODEL), BF16)], 0)
                acc_ref[rows, :] += gate_e * _dot(put, yb)

    @pl.when(e == pl.num_programs(1) - 1)
    def _():
        o_ref[...] = _layer_norm(DEEPNORM_ALPHA * x_ref[...] + acc_ref[...], g_ref[...], b_ref[...])


def _moe(x2, gates, pos, post, cnt, w1, w3, w2, ln_g, ln_b, groups=2):
    n = x2.shape[0]
    ne, _, tf = w1.shape
    tm = groups * ROUTE_ROWS
    post3 = post.reshape(n // ROUTE_ROWS * N_EXPERTS, 1, ROUTE_ROWS)
    full = lambda a: pl.BlockSpec(a.shape, lambda i, e, c: (0,) * a.ndim)
    row = lambda w: pl.BlockSpec((tm, w), lambda i, e, c: (i, 0))
    post_spec = lambda gi: pl.BlockSpec((1, 1, ROUTE_ROWS),
                                        lambda i, e, c: ((i * groups + gi) * N_EXPERTS + e, 0, 0))
    grid_spec = pltpu.PrefetchScalarGridSpec(
        num_scalar_prefetch=1,
        grid=(n // tm, ne),
        in_specs=[row(D_MODEL), row(LANES), row(LANES)] + [post_spec(gi) for gi in range(groups)]
                 + [pl.BlockSpec((1, D_MODEL, tf), lambda i, e, c: (e, 0, 0)),
                    pl.BlockSpec((1, D_MODEL, tf), lambda i, e, c: (e, 0, 0)),
                    pl.BlockSpec((1, tf, D_MODEL), lambda i, e, c: (e, 0, 0)),
                    full(ln_g), full(ln_b)],
        out_specs=row(D_MODEL),
        scratch_shapes=[pltpu.VMEM((tm, D_MODEL), F32), pltpu.VMEM((tm, D_MODEL), BF16)],
    )
    return pl.pallas_call(
        functools.partial(_moe_kernel, groups=groups),
        grid_spec=grid_spec,
        out_shape=jax.ShapeDtypeStruct((n, D_MODEL), F32),
        compiler_params=_cparams(("parallel", "arbitrary")),
        name="moe_top2_ln",
    )(cnt, x2, gates, pos, *([post3] * groups), w1, w3, w2, ln_g, ln_b)


def _in_proj_columns():
    src = np.full((IN_P,), -1, np.int64)
    scale = np.ones((IN_P,), np.float32)
    o_cq, o_ckv, o_kr = 0, MLA_Q_RANK, MLA_Q_RANK + MLA_KV_RANK
    o_fox = o_kr + MLA_ROPE
    o_foxf = o_fox + 3 * FOX_HEADS * HEAD_DIM
    o_nq = o_foxf + FOX_HEADS
    o_nkv = o_nq + NSA_HEADS * HEAD_DIM
    o_ng = o_nkv + 6 * HEAD_DIM
    src[0:o_kr] = np.arange(o_kr)
    small = o_kr
    src[small + KR_LANE:small + KR_LANE + MLA_ROPE] = o_kr + np.arange(MLA_ROPE)
    src[small + FOXF_LANE:small + FOXF_LANE + FOX_HEADS] = o_foxf + np.arange(FOX_HEADS)
    src[small + NSAG_LANE:small + NSAG_LANE + 3 * NSA_HEADS] = o_ng + np.arange(3 * NSA_HEADS)
    n_fox = FOX_HEADS * HEAD_DIM
    src[FQ_OFF:FQ_OFF + 3 * n_fox] = o_fox + np.arange(3 * n_fox)
    scale[FQ_OFF:FQ_OFF + n_fox] = HEAD_DIM ** -0.5
    src[NQ_OFF:NQ_OFF + NSA_HEADS * HEAD_DIM] = o_nq + np.arange(NSA_HEADS * HEAD_DIM)
    scale[NQ_OFF:NQ_OFF + NSA_HEADS * HEAD_DIM] = HEAD_DIM ** -0.5
    src[NC_OFF:NC_OFF + 6 * HEAD_DIM] = o_nkv + np.arange(6 * HEAD_DIM)
    return src, scale


def _gather_cols(w, src, scale=None):
    scale = np.ones(src.shape, np.float32) if scale is None else scale
    parts, i, n = [], 0, len(src)
    while i < n:
        j = i + 1
        while j < n and scale[j] == scale[i] and (src[j] == src[j - 1] + 1 if src[i] >= 0 else src[j] < 0):
            j += 1
        if src[i] < 0:
            parts.append(jnp.zeros(w.shape[:-1] + (j - i,), w.dtype))
        else:
            seg = w[..., int(src[i]):int(src[i]) + (j - i)]
            parts.append(seg if scale[i] == 1.0 else seg * float(scale[i]))
        i = j
    return jnp.concatenate(parts, axis=-1)


def _tables(s):
    half = MLA_ROPE // 2
    freqs = ROPE_THETA ** (-jnp.arange(half, dtype=F32) / half)
    ang = jnp.arange(s).astype(F32)[:, None] * freqs[None, :]
    cos, sin = jnp.cos(ang), jnp.sin(ang)
    z = lambda w: jnp.zeros((s, w), F32)
    tabs = {
        "cos": jnp.concatenate([jnp.ones((s, MLA_NOPE), F32), cos, cos, z(LANES - MLA_NOPE - MLA_ROPE)], 1),
        "sina": jnp.concatenate([z(MLA_NOPE), -sin, z(half), z(LANES - MLA_NOPE - MLA_ROPE)], 1),
        "sinb": jnp.concatenate([z(MLA_NOPE), z(half), sin, z(LANES - MLA_NOPE - MLA_ROPE)], 1),
    }
    pq = np.zeros((LANES, FOX_HEADS * LANES), np.float32)
    pk = np.zeros((LANES, FOX_HEADS * LANES), np.float32)
    oneq = np.zeros((1, FOX_HEADS * LANES), np.float32)
    onek = np.zeros((1, FOX_HEADS * LANES), np.float32)
    for h in range(FOX_HEADS):
        for t in range(3):
            pq[FOXF_LANE + FOX_HEADS * t + h, LANES * h + HEAD_DIM + t] = 1.0
            pk[FOXF_LANE + FOX_HEADS * t + h, LANES * h + HEAD_DIM + 3 + t] = -1.0
            oneq[0, LANES * h + HEAD_DIM + 3 + t] = 1.0
            onek[0, LANES * h + HEAD_DIM + t] = 1.0
    tabs.update(pq=jnp.asarray(pq, BF16), pk=jnp.asarray(pk, BF16), oneq=jnp.asarray(oneq), onek=jnp.asarray(onek))
    qaux = np.zeros((1, NSA_HEADS * LANES), np.float32)
    for h in range(NSA_HEADS):
        qaux[0, LANES * h + NSA_AUX_LANE] = ALIBI[h] * NSA_SEL_BLOCK
        qaux[0, LANES * h + NSA_AUX_LANE + 1] = ALIBI[h]
    kaux = np.zeros((s, LANES), np.float32)
    pos = np.arange(s)
    kaux[:, NSA_AUX_LANE] = pos // NSA_SEL_BLOCK
    kaux[:, NSA_AUX_LANE + 1] = pos % NSA_SEL_BLOCK
    kaux[pos, NSA_SEL_LANE + pos // NSA_SEL_BLOCK] = 1.0
    tabs.update(nsa_qaux=jnp.asarray(qaux), nsa_kaux=jnp.asarray(kaux))
    return tabs


def _layer_weights(p, l):
    qd = MLA_NOPE + MLA_ROPE
    src_q = np.full((MLA_HEADS * LANES,), -1, np.int64)
    src_k = np.full((MLA_HEADS * LANES,), -1, np.int64)
    src_v = np.full((MLA_HEADS * LANES,), -1, np.int64)
    for h in range(MLA_HEADS):
        src_q[LANES * h + np.arange(qd)] = qd * h + np.arange(qd)
        src_k[LANES * h + np.arange(MLA_NOPE)] = 2 * HEAD_DIM * h + np.arange(MLA_NOPE)
        src_v[LANES * h + np.arange(HEAD_DIM)] = 2 * HEAD_DIM * h + MLA_NOPE + np.arange(HEAD_DIM)
    w_out = p["w_out"][l]
    n_attn = (MLA_HEADS + FOX_HEADS) * HEAD_DIM
    wb = w_out[n_attn:].reshape(NSA_HEADS, HEAD_DIM, D_MODEL)
    wb = jnp.concatenate([wb, jnp.zeros_like(wb)], axis=1).reshape(NSA_HEADS * LANES, D_MODEL)
    bf = jnp.zeros((1, LANES), F32).at[0, FOXF_LANE:FOXF_LANE + FOX_HEADS].set(p["b_forget"][l])

    half = NSA_CMP_BLOCK // 2
    kpos, vpos = p["cmp_k_pos"][l], p["cmp_v_pos"][l]
    posa = jnp.concatenate([kpos[:half], vpos[:half]], -1).reshape(1, half * LANES)
    posb = jnp.concatenate([kpos[half:], vpos[half:]], -1).reshape(1, half * LANES)

    def w1_half(lo):
        k = p["cmp_k_w1"][l].reshape(NSA_CMP_BLOCK, HEAD_DIM, NSA_CMP_HIDDEN)[lo:lo + half]
        v = p["cmp_v_w1"][l].reshape(NSA_CMP_BLOCK, HEAD_DIM, NSA_CMP_HIDDEN)[lo:lo + half]
        zk = jnp.zeros_like(k)
        top = jnp.concatenate([k, zk], -1)
        bot = jnp.concatenate([zk, v], -1)
        return jnp.concatenate([top, bot], 1).reshape(half * LANES, 2 * NSA_CMP_HIDDEN).astype(BF16)

    zw2 = jnp.zeros((NSA_CMP_HIDDEN, HEAD_DIM), F32)
    w2 = jnp.concatenate([jnp.concatenate([p["cmp_k_w2"][l], zw2], 1),
                          jnp.concatenate([zw2, p["cmp_v_w2"][l]], 1)], 0).astype(BF16)
    return {
        "g_cq": p["g_cq"][l][None, :], "g_ckv": p["g_ckv"][l][None, :],
        "w_uq": _gather_cols(p["w_uq"][l], src_q).astype(BF16),
        "w_uk": _gather_cols(p["w_ukv"][l], src_k).astype(BF16),
        "w_uv": _gather_cols(p["w_ukv"][l], src_v).astype(BF16),
        "b_forget": bf,
        "cmp_posa": posa, "cmp_posb": posb, "cmp_wa": w1_half(0), "cmp_wb": w1_half(half), "cmp_w2": w2,
        "w_out_a": w_out[:n_attn].astype(BF16), "w_out_b": wb.astype(BF16),
        "ln1_g": p["ln1_g"][l][None, :], "ln1_b": p["ln1_b"][l][None, :],
        "ln2_g": p["ln2_g"][l][None, :], "ln2_b": p["ln2_b"][l][None, :],
    }


def kernel(x, w_in, b_forget, g_cq, w_uq, g_ckv, w_ukv, cmp_k_pos, cmp_k_w1, cmp_k_w2, cmp_v_pos, cmp_v_w1,
           cmp_v_w2, w_out, ln1_g, ln1_b, ln2_g, ln2_b, ffn_w1, ffn_w3, ffn_w2, router_w, moe_w1, moe_w3,
           moe_w2):
    b, s, d = x.shape
    assert d == D_MODEL and s % 512 == 0 and s // NSA_CMP_STRIDE == LANES, (b, s, d)
    p = dict(b_forget=b_forget, g_cq=g_cq, w_uq=w_uq, g_ckv=g_ckv, w_ukv=w_ukv, cmp_k_pos=cmp_k_pos,
             cmp_k_w1=cmp_k_w1, cmp_k_w2=cmp_k_w2, cmp_v_pos=cmp_v_pos, cmp_v_w1=cmp_v_w1, cmp_v_w2=cmp_v_w2,
             w_out=w_out, ln1_g=ln1_g, ln1_b=ln1_b, ln2_g=ln2_g, ln2_b=ln2_b)
    src, scale = _in_proj_columns()
    w_in_p = _gather_cols(w_in, src, scale).astype(BF16)
    tabs = _tables(s)
    n = b * s
    x2 = x.reshape(n, d)
    for l in range(DEPTH):
        wl = _layer_weights(p, l)
        small, nq, nc, nsl, nwin, q_all, k_all, v_all = _front(x2, w_in_p[l], wl, tabs, s)
        o_attn = _flash(*(t.reshape(b, s, -1) for t in (q_all, k_all, v_all)))
        kcvc = _compress(nc.reshape(b, s, LANES), wl)
        o_nsa = _nsa(nq.reshape(b, s, NSA_HEADS * LANES), kcvc, nsl.reshape(b, s, 2 * LANES),
                     nwin.reshape(b, s, 2 * LANES), small)
        j = l // 2
        if l % 2 == 0:
            x2 = _out_proj(o_attn.reshape(n, -1), o_nsa.reshape(n, -1), x2, wl)
            x2 = _ffn(x2, ffn_w1[j].astype(BF16), ffn_w3[j].astype(BF16), ffn_w2[j].astype(BF16),
                      wl["ln2_g"], wl["ln2_b"])
        else:
            rw = jnp.pad(router_w[j], ((0, 0), (0, LANES - N_EXPERTS)))
            rw_hi = rw.astype(BF16)
            rw_lo = (rw - rw_hi.astype(F32)).astype(BF16)
            rw = jnp.concatenate([jnp.concatenate([rw_hi, rw_lo], 1),
                                  jnp.concatenate([rw_hi, jnp.zeros_like(rw_lo)], 1)], 0)
            x2, gates, pos, post, cnt = _out_proj(o_attn.reshape(n, -1), o_nsa.reshape(n, -1), x2, wl, router=rw)
            cnt = cnt[::8, :N_EXPERTS].reshape(-1)
            x2 = _moe(x2, gates, pos, post, cnt, moe_w1[j].astype(BF16), moe_w3[j].astype(BF16),
                      moe_w2[j].astype(BF16), wl["ln2_g"], wl["ln2_b"])
    return x2.reshape(b, s, d)
```

```python
import functools

import numpy as np
import jax
import jax.numpy as jnp
from jax import lax
from jax.experimental import pallas as pl
from jax.experimental.pallas import tpu as pltpu

F32 = jnp.float32
BF16 = jnp.bfloat16

D_MODEL = 1024
HEAD_DIM = 64
LANES = 128
MLA_HEADS = 6
MLA_Q_RANK = 384
MLA_KV_RANK = 256
MLA_NOPE = 64
MLA_ROPE = 32
ROPE_THETA = 10000.0
FOX_HEADS = 6
NSA_HEADS = 4
NSA_CMP_BLOCK = 32
NSA_CMP_STRIDE = 16
NSA_CMP_HIDDEN = 128
NSA_SEL_BLOCK = 64
NSA_SEL_TOPN = 8
NSA_WINDOW = 256
NSA_FORCE_SCORE = 1.0e4
N_EXPERTS = 8
NORM_EPS = 1e-5
DEPTH = 2
DEEPNORM_ALPHA = (2 * DEPTH) ** 0.25
NEG = -1e30
VMEM_LIMIT = 56 * 1024 * 1024

ROW_TILE = 512
FLASH_TQ, FLASH_TK = 2048, 512
NSA_TQ = 512

KR_LANE = 64
FOXF_LANE = 96
NSAG_LANE = 102

ZA_W = 768
FQ_OFF = ZA_W
FK_OFF = FQ_OFF + 384
FV_OFF = FK_OFF + 384
NQ_OFF = FV_OFF + 384
NC_OFF = NQ_OFF + 256
NKV_OFF = NC_OFF + 128
IN_P = NKV_OFF + 256

NSA_AUX_LANE = HEAD_DIM
NSA_SEL_LANE = HEAD_DIM + 2
NSA_MASK_BIG = 2.0 ** 126

ALIBI = tuple(2.0 ** (-8.0 * (i + 1) / NSA_HEADS) for i in range(NSA_HEADS))


def _dot(a, b, **kw):
    return jnp.dot(a, b, preferred_element_type=F32, **kw)


def _dot_nt(a, b):
    return lax.dot_general(a, b, (((1,), (1,)), ((), ())), preferred_element_type=F32)


def _iota(shape, dim):
    return lax.broadcasted_iota(jnp.int32, shape, dim)


def _split3(x):
    hi = x.astype(BF16)
    r = x - hi.astype(F32)
    mid = r.astype(BF16)
    return hi, mid, (r - mid.astype(F32)).astype(BF16)


def _dot_exact_rhs(x, w01):
    hi, mid, lo = _split3(x)
    return _dot(jnp.concatenate([hi, mid, lo], 1), jnp.concatenate([w01, w01, w01], 0))


def _dot_split(x, w_stack):
    hi = x.astype(BF16)
    lo = (x - hi.astype(F32)).astype(BF16)
    r = _dot(jnp.concatenate([hi, lo], 1), w_stack)
    half = r.shape[1] // 2
    return r[:, :half] + r[:, half:]


def _cparams(sem):
    return pltpu.CompilerParams(dimension_semantics=sem, vmem_limit_bytes=VMEM_LIMIT)


def _front_kernel(x_ref, w_ref, qaux_ref, kaux_ref, gcq_ref, wuq_ref, gckv_ref, wuk_ref, wuv_ref, bf_ref,
                  cos_ref, sina_ref, sinb_ref, pq_ref, pk_ref, oneq_ref, onek_ref,
                  small_ref, nq_ref, nc_ref, nsl_ref, nwin_ref, q_out, k_out, v_out, carry_ref,
                  *, tm, tiles_per_seq):
    @pl.when(pl.program_id(0) % tiles_per_seq == 0)
    def _():
        carry_ref[...] = jnp.zeros_like(carry_ref)

    xb = x_ref[...].astype(BF16)

    def mm(a, b):
        return _dot(xb, w_ref[:, a:b])

    lane = _iota((1, LANES), 1)

    def halves(pair):
        return (jnp.where(lane < HEAD_DIM, pair, 0.0),
                jnp.where(lane < HEAD_DIM, pltpu.roll(pair, HEAD_DIM, 1), 0.0))

    def expand(z):
        return [blk for pr in range(z.shape[1] // LANES) for blk in halves(z[:, LANES * pr:LANES * (pr + 1)])]

    def pad_v(blk):
        return jnp.where(lane < HEAD_DIM, blk, jnp.where(lane == HEAD_DIM, 1.0, 0.0)).astype(BF16)

    cos = cos_ref[...]
    sina = sina_ref[...]
    sinb = sinb_ref[...]

    def rope(blk):
        return blk * cos + pltpu.roll(blk, LANES - 16, 1) * sina + pltpu.roll(blk, 16, 1) * sinb

    za = mm(0, ZA_W)
    cq = za[:, 0:MLA_Q_RANK]
    ckv = za[:, MLA_Q_RANK:MLA_Q_RANK + MLA_KV_RANK]
    small = za[:, MLA_Q_RANK + MLA_KV_RANK:ZA_W]
    small_ref[...] = small

    xn = cq * lax.rsqrt(jnp.mean(cq * cq, -1, keepdims=True) + NORM_EPS) * gcq_ref[...]
    q = _dot(xn.astype(BF16), wuq_ref[...])
    cn = ckv * lax.rsqrt(jnp.mean(ckv * ckv, -1, keepdims=True) + NORM_EPS) * gckv_ref[...]
    cnb = cn.astype(BF16)
    kn = _dot(cnb, wuk_ref[...])
    v = _dot(cnb, wuv_ref[...])
    kr = rope(small)
    mla_scale = (MLA_NOPE + MLA_ROPE) ** -0.5
    for h in range(MLA_HEADS):
        sl = slice(LANES * h, LANES * (h + 1))
        q_out[:, sl] = (rope(q[:, sl]) * mla_scale).astype(BF16)
        k_out[:, sl] = (kn[:, sl] + kr).astype(BF16)
        v_out[:, sl] = pad_v(v[:, sl])

    fmask = (lane >= FOXF_LANE) & (lane < FOXF_LANE + FOX_HEADS)
    lf = jnp.where(fmask, jax.nn.log_sigmoid(small + bf_ref[...]), 0.0)
    tril = jnp.where(_iota((tm, tm), 0) >= _iota((tm, tm), 1), 1.0, 0.0).astype(BF16)
    lf_hi, lf_mid, lf_lo = _split3(lf)
    part = _dot(tril, jnp.concatenate([lf_hi, lf_mid], 1))
    cs = part[:, :LANES] + part[:, LANES:] + _dot(tril, lf_lo) + carry_ref[...]
    carry_ref[...] = cs[tm - 1:tm, :]
    hi, mid, lo = (t.astype(F32) for t in _split3(cs))
    c3 = (hi + pltpu.roll(mid, FOX_HEADS, 1) + pltpu.roll(lo, 2 * FOX_HEADS, 1)).astype(BF16)
    augq = _dot(c3, pq_ref[...]) + oneq_ref[...]
    augk = _dot(c3, pk_ref[...]) + onek_ref[...]
    off = MLA_HEADS * LANES
    fq = expand(mm(FQ_OFF, FK_OFF))
    fk = expand(mm(FK_OFF, FV_OFF))
    fv = expand(mm(FV_OFF, NQ_OFF))
    for h in range(FOX_HEADS):
        sl = slice(LANES * h, LANES * (h + 1))
        dst = slice(off + LANES * h, off + LANES * (h + 1))
        q_out[:, dst] = (fq[h] + augq[:, sl]).astype(BF16)
        k_out[:, dst] = (fk[h] + augk[:, sl]).astype(BF16)
        v_out[:, dst] = pad_v(fv[h])

    qaux = qaux_ref[...]
    for h, blk in enumerate(expand(mm(NQ_OFF, NC_OFF))):
        sl = slice(LANES * h, LANES * (h + 1))
        nq_ref[:, sl] = (blk + qaux[:, sl]).astype(BF16)
    nc_ref[...] = mm(NC_OFF, NKV_OFF)
    kaux = kaux_ref[...]
    nkv = mm(NKV_OFF, IN_P)
    for i, out_ref in enumerate((nsl_ref, nwin_ref)):
        k_blk, v_blk = halves(nkv[:, LANES * i:LANES * (i + 1)])
        out_ref[:, 0:LANES] = (k_blk + kaux).astype(BF16)
        out_ref[:, LANES:2 * LANES] = pad_v(v_blk)


def _front(x2, w_p, wl, tabs, s, tm=ROW_TILE):
    n = x2.shape[0]
    ns = s // tm
    nh = MLA_HEADS + FOX_HEADS
    full = lambda a: pl.BlockSpec(a.shape, lambda i: (0,) * a.ndim)
    tab = pl.BlockSpec((tm, LANES), lambda i: (i % ns, 0))
    consts = [wl["g_cq"], wl["w_uq"], wl["g_ckv"], wl["w_uk"], wl["w_uv"], wl["b_forget"]]
    tail = [tabs["pq"], tabs["pk"], tabs["oneq"], tabs["onek"]]
    widths = [(LANES, F32), (NSA_HEADS * LANES, BF16), (LANES, F32), (2 * LANES, BF16), (2 * LANES, BF16),
              (nh * LANES, BF16), (nh * LANES, BF16), (nh * LANES, BF16)]
    return pl.pallas_call(
        functools.partial(_front_kernel, tm=tm, tiles_per_seq=ns),
        grid=(n // tm,),
        in_specs=[pl.BlockSpec((tm, D_MODEL), lambda i: (i, 0)), full(w_p), full(tabs["nsa_qaux"]), tab]
                 + [full(a) for a in consts] + [tab, tab, tab] + [full(a) for a in tail],
        out_specs=[pl.BlockSpec((tm, w), lambda i: (i, 0)) for w, _ in widths],
        out_shape=[jax.ShapeDtypeStruct((n, w), dt) for w, dt in widths],
        scratch_shapes=[pltpu.VMEM((1, LANES), F32)],
        compiler_params=_cparams(("arbitrary",)),
        name="front_proj",
    )(x2, w_p, tabs["nsa_qaux"], tabs["nsa_kaux"], *consts, tabs["cos"], tabs["sina"], tabs["sinb"], *tail)


def _flash_kernel(q_ref, k_ref, v_ref, o_ref, m_ref, acc_ref, s_ref, *, tq, tk):
    qi = pl.program_id(2)
    m_ref[...] = jnp.full_like(m_ref, NEG)
    acc_ref[...] = jnp.zeros_like(acc_ref)
    nsub = tq // tk
    nfull = qi * nsub
    heads = [slice(LANES * h, LANES * (h + 1)) for h in range(2)]

    def scores(j, r0, sl):
        k0 = pl.multiple_of(j * tk, tk)
        return _dot_nt(q_ref[0, r0:tq, sl], k_ref[0, pl.ds(k0, tk), sl])

    def consume(s, h, j, r0, masked):
        k0 = pl.multiple_of(j * tk, tk)
        if masked:
            s = jnp.where(_iota((tq - r0, tk), 0) >= _iota((tq - r0, tk), 1), s, NEG)
        chunks = [s[:, LANES * c:LANES * (c + 1)] for c in range(tk // LANES)]
        m_prev = m_ref[h, r0:tq, :]
        m_new = jnp.maximum(m_prev, jnp.max(functools.reduce(jnp.maximum, chunks), -1, keepdims=True))
        p = jnp.concatenate([jnp.exp(c - m_new) for c in chunks], 1).astype(BF16)
        acc_ref[h, r0:tq, :] = (jnp.exp(m_prev - m_new) * acc_ref[h, r0:tq, :]
                                + _dot(p, v_ref[0, pl.ds(k0, tk), heads[h]]))
        m_ref[h, r0:tq, :] = m_new

    assert nsub % 2 == 0
    for h in range(2):
        s_ref[0, h] = scores(0, 0, heads[h])

    def body(i, carry):
        j = 2 * i
        for b in range(2):
            for h in range(2):
                s_ref[1 - b, h] = scores(j + b + 1, 0, heads[h])
            for h in range(2):
                consume(s_ref[b, h], h, j + b, 0, False)
        return carry

    lax.fori_loop(0, qi * (nsub // 2), body, 0)
    for h in range(2):
        consume(s_ref[0, h], h, nfull, 0, True)
    for d in range(1, nsub):
        for h in range(2):
            consume(scores(nfull + d, d * tk, heads[h]), h, nfull + d, d * tk, True)

    lane = _iota((1, LANES), 1)
    o0 = acc_ref[0]
    o1 = acc_ref[1]
    o0 = o0 / o0[:, HEAD_DIM:HEAD_DIM + 1]
    o1 = o1 / o1[:, HEAD_DIM:HEAD_DIM + 1]
    o_ref[0] = jnp.where(lane < HEAD_DIM, o0, pltpu.roll(o1, HEAD_DIM, 1)).astype(o_ref.dtype)


def _flash(q_all, k_all, v_all, tq=FLASH_TQ, tk=FLASH_TK):
    b, s, _ = q_all.shape
    npair = (MLA_HEADS + FOX_HEADS) // 2
    return pl.pallas_call(
        functools.partial(_flash_kernel, tq=tq, tk=tk),
        grid=(b, npair, s // tq),
        in_specs=[pl.BlockSpec((1, tq, 2 * LANES), lambda bi, p, qi: (bi, qi, p)),
                  pl.BlockSpec((1, s, 2 * LANES), lambda bi, p, qi: (bi, 0, p)),
                  pl.BlockSpec((1, s, 2 * LANES), lambda bi, p, qi: (bi, 0, p))],
        out_specs=pl.BlockSpec((1, tq, LANES), lambda bi, p, qi: (bi, qi, p)),
        out_shape=jax.ShapeDtypeStruct((b, s, npair * LANES), BF16),
        scratch_shapes=[pltpu.VMEM((2, tq, LANES), F32), pltpu.VMEM((2, tq, LANES), F32),
                        pltpu.VMEM((2, 2, tq, tk), F32)],
        compiler_params=_cparams(("parallel", "parallel", "arbitrary")),
        name="flash_attn",
    )(q_all, k_all, v_all)


def _cmp_kernel(nc_ref, posa_ref, posb_ref, wa_ref, wb_ref, w2_ref, out_ref, *, n_chunk):
    a = jnp.zeros((n_chunk, 2 * NSA_CMP_HIDDEN), F32)
    b = jnp.zeros((n_chunk, 2 * NSA_CMP_HIDDEN), F32)
    for l in range(NSA_CMP_STRIDE):
        t = nc_ref[0, pl.ds(l, n_chunk, stride=NSA_CMP_STRIDE), :]
        sl = slice(LANES * l, LANES * (l + 1))
        a = a + _dot((t + posa_ref[:, sl]).astype(BF16), wa_ref[sl, :])
        b = b + _dot((t + posb_ref[:, sl]).astype(BF16), wb_ref[sl, :])
    pre = a + pltpu.roll(b, n_chunk - 1, 0)
    hid = jax.nn.silu(pre)
    out_ref[0] = _dot(hid.astype(BF16), w2_ref[...]).astype(out_ref.dtype)


def _compress(nc3, wl):
    b, s, _ = nc3.shape
    n_chunk = s // NSA_CMP_STRIDE
    full = lambda a: pl.BlockSpec(a.shape, lambda bi: (0,) * a.ndim)
    consts = [wl["cmp_posa"], wl["cmp_posb"], wl["cmp_wa"], wl["cmp_wb"], wl["cmp_w2"]]
    return pl.pallas_call(
        functools.partial(_cmp_kernel, n_chunk=n_chunk),
        grid=(b,),
        in_specs=[pl.BlockSpec((1, s, LANES), lambda bi: (bi, 0, 0))] + [full(a) for a in consts],
        out_specs=pl.BlockSpec((1, n_chunk, LANES), lambda bi: (bi, 0, 0)),
        out_shape=jax.ShapeDtypeStruct((b, n_chunk, LANES), BF16),
        compiler_params=_cparams(("parallel",)),
        name="nsa_compress",
    )(nc3, *consts)


def _masked_softmax(s, mask):
    s = jnp.where(mask, s, NEG)
    m = jnp.max(s, -1, keepdims=True)
    p = jnp.where(mask, jnp.exp(s - m), 0.0)
    return p / jnp.maximum(jnp.sum(p, -1, keepdims=True), 1e-30)


def _nsa_kernel(nq_ref, kc_ref, ksl_ref, kwin_ref, g_ref, o_ref, m_ref, acc_ref, qst_ref, s_ref, *, tq, n_cmp):
    qi = pl.program_id(1)
    t0 = pl.multiple_of(qi * tq, tq)
    rpos = t0 + _iota((tq, 1), 0)
    lane = _iota((1, LANES), 1)
    qs = [nq_ref[0, :, LANES * h:LANES * (h + 1)] for h in range(NSA_HEADS)]

    dist_i = rpos - (NSA_CMP_STRIDE * lane + NSA_CMP_BLOCK - 1)
    valid_c = (dist_i >= 0) & (lane < n_cmp)
    dist_c = dist_i.astype(F32)
    kc = kc_ref[0]
    psum = jnp.zeros((tq, LANES), F32)
    o_cmp = []
    for h in range(NSA_HEADS):
        q_head = jnp.where(lane < HEAD_DIM, qs[h], jnp.zeros_like(qs[h]))
        p = _masked_softmax(_dot_nt(q_head, kc) - ALIBI[h] * dist_c, valid_c)
        psum = psum + p
        o_cmp.append(_dot(p.astype(BF16), kc))

    n_i = _iota((LANES, LANES), 0)
    j_i = _iota((LANES, LANES), 1)
    ov = ((NSA_CMP_STRIDE * n_i < NSA_SEL_BLOCK * (j_i + 1))
          & (NSA_CMP_STRIDE * n_i + NSA_CMP_BLOCK > NSA_SEL_BLOCK * j_i)
          & (n_i < n_cmp))
    imp = _dot_exact_rhs(psum, jnp.where(ov, 1.0, 0.0).astype(BF16))
    cur = jnp.right_shift(rpos, 6)
    forced = (lane == 0) | (lane == cur) | (lane == cur - 1)
    future = lane * NSA_SEL_BLOCK > rpos
    n_blk = ksl_ref.shape[1] // NSA_SEL_BLOCK
    work = jnp.where(forced, NSA_FORCE_SCORE, jnp.where(future, -1.0, imp))
    work = jnp.where(lane < n_blk, work, -jnp.inf)
    work_t = work.T[0:n_blk, :]
    blk_id = _iota((n_blk, 1), 0)
    rank = jnp.zeros((n_blk, tq), F32)
    for jp in range(n_blk):
        other = work_t[jp:jp + 1, :]
        beats = (other > work_t) | ((other == work_t) & (jp < blk_id))
        rank = rank + jnp.where(beats, 1.0, 0.0)
    sel_t = jnp.where(rank < NSA_SEL_TOPN, 1.0, 0.0)
    sel = jnp.concatenate([sel_t, jnp.zeros((LANES - n_blk, tq), F32)], 0).T > 0.5
    sel_lanes = pltpu.roll(jnp.where(sel, 1.0, 0.0), NSA_SEL_LANE, 1)
    in_sel = (lane >= NSA_SEL_LANE) & (lane < NSA_SEL_LANE + n_blk)
    sel_bias = jnp.where(in_sel, (sel_lanes - 1.0) * NSA_MASK_BIG, 0.0)
    for h in range(NSA_HEADS):
        qst_ref[h * tq:(h + 1) * tq, :] = (qs[h].astype(F32) + sel_bias).astype(BF16)
    m_ref[...] = jnp.full_like(m_ref, NEG)
    acc_ref[...] = jnp.zeros_like(acc_ref)

    def online(s, v):
        chunks = [s[:, LANES * i:LANES * (i + 1)] for i in range(s.shape[1] // LANES)]
        m_prev = m_ref[...]
        m_new = jnp.maximum(m_prev, jnp.max(functools.reduce(jnp.maximum, chunks), -1, keepdims=True))
        p = jnp.concatenate([jnp.exp(ch - m_new) for ch in chunks], 1).astype(BF16)
        acc_ref[...] = jnp.exp(m_prev - m_new) * acc_ref[...] + _dot(p, v)
        m_ref[...] = m_new

    def scores(c):
        return _dot_nt(qst_ref[...], ksl_ref[0, pl.ds(pl.multiple_of(c * tq, tq), tq), 0:LANES])

    def values(c):
        return ksl_ref[0, pl.ds(pl.multiple_of(c * tq, tq), tq), LANES:2 * LANES]

    s_ref[0] = scores(0)

    def chunk_pair(i, carry):
        j = 2 * i
        s_ref[1] = scores(j + 1)
        online(s_ref[0], values(j))
        s_ref[0] = scores(j + 2)
        online(s_ref[1], values(j + 1))
        return carry

    lax.fori_loop(0, jnp.right_shift(qi, 1), chunk_pair, 0)

    @pl.when((qi & 1) == 1)
    def _():
        online(s_ref[0], values(qi - 1))
        s_ref[0] = scores(qi)

    nrow = NSA_HEADS * tq
    causal = (_iota((nrow, tq), 0) & (tq - 1)) >= _iota((nrow, tq), 1)
    online(jnp.where(causal, s_ref[0], NEG), values(qi))

    wlen = tq + NSA_WINDOW
    w0 = pl.multiple_of(jnp.maximum(t0 - NSA_WINDOW, 0), min(tq, NSA_WINDOW))
    k_w = kwin_ref[0, pl.ds(w0, wlen), 0:LANES]
    v_w = kwin_ref[0, pl.ds(w0, wlen), LANES:2 * LANES]
    d_w = (t0 - w0) + _iota((tq, wlen), 0) - _iota((tq, wlen), 1)
    keep_w = (d_w >= 0) & (d_w < NSA_WINDOW)

    gates = jax.nn.sigmoid(g_ref[...])
    for h in range(NSA_HEADS):
        s_w = jnp.where(keep_w, _dot_nt(qs[h], k_w), NEG)
        p_w = jnp.exp(s_w - jnp.max(s_w, -1, keepdims=True))
        o_win = _dot(p_w.astype(BF16), v_w)
        o_win = o_win / o_win[:, HEAD_DIM:HEAD_DIM + 1]
        o_sel = acc_ref[h * tq:(h + 1) * tq, :]
        o_sel = o_sel / o_sel[:, HEAD_DIM:HEAD_DIM + 1]
        g0 = NSAG_LANE + 3 * h
        o = (gates[:, g0:g0 + 1] * pltpu.roll(o_cmp[h], HEAD_DIM, 1) + gates[:, g0 + 1:g0 + 2] * o_sel
             + gates[:, g0 + 2:g0 + 3] * o_win)
        o_ref[0, :, LANES * h:LANES * (h + 1)] = jnp.where(lane < HEAD_DIM, o, 0.0).astype(o_ref.dtype)


def _nsa(nq, kcvc, nsl, nwin, small, tq=NSA_TQ):
    b, s, _ = nq.shape
    nq_t = s // tq
    n_cmp = (s - NSA_CMP_BLOCK) // NSA_CMP_STRIDE + 1
    return pl.pallas_call(
        functools.partial(_nsa_kernel, tq=tq, n_cmp=n_cmp),
        grid=(b, nq_t),
        in_specs=[pl.BlockSpec((1, tq, NSA_HEADS * LANES), lambda bi, qi: (bi, qi, 0)),
                  pl.BlockSpec((1, kcvc.shape[1], LANES), lambda bi, qi: (bi, 0, 0)),
                  pl.BlockSpec((1, s, 2 * LANES), lambda bi, qi: (bi, 0, 0)),
                  pl.BlockSpec((1, s, 2 * LANES), lambda bi, qi: (bi, 0, 0)),
                  pl.BlockSpec((tq, LANES), lambda bi, qi: (bi * nq_t + qi, 0))],
        out_specs=pl.BlockSpec((1, tq, NSA_HEADS * LANES), lambda bi, qi: (bi, qi, 0)),
        out_shape=jax.ShapeDtypeStruct((b, s, NSA_HEADS * LANES), BF16),
        scratch_shapes=[pltpu.VMEM((NSA_HEADS * tq, LANES), F32), pltpu.VMEM((NSA_HEADS * tq, LANES), F32),
                        pltpu.VMEM((NSA_HEADS * tq, LANES), BF16), pltpu.VMEM((2, NSA_HEADS * tq, tq), F32)],
        compiler_params=_cparams(("parallel", "arbitrary")),
        name="nsa_attn",
    )(nq, kcvc, nsl, nwin, small)


def _layer_norm(y, g, b):
    mu = jnp.mean(y, -1, keepdims=True)
    yc = y - mu
    var = jnp.mean(yc * yc, -1, keepdims=True)
    return yc * lax.rsqrt(var + NORM_EPS) * g + b


ROUTE_ROWS = ROW_TILE


def _out_proj_kernel(*refs, routed):
    if routed:
        (oa_ref, on_ref, x_ref, wa_ref, wb_ref, g_ref, b_ref, rw_ref,
         o_ref, gate_ref, pos_ref, post_ref, cnt_ref) = refs
    else:
        oa_ref, on_ref, x_ref, wa_ref, wb_ref, g_ref, b_ref, o_ref = refs
    mix = _dot(oa_ref[...], wa_ref[...]) + _dot(on_ref[...], wb_ref[...])
    x1 = _layer_norm(DEEPNORM_ALPHA * x_ref[...] + mix, g_ref[...], b_ref[...])
    o_ref[...] = x1
    if routed:
        tm = x1.shape[0]
        lane = _iota((1, LANES), 1)
        lane_f = lane.astype(F32)
        logits = jnp.where(lane < N_EXPERTS, _dot_split(x1, rw_ref[...]), NEG)
        ex = jnp.exp(logits - jnp.max(logits, -1, keepdims=True))
        probs = ex / jnp.sum(ex, -1, keepdims=True)
        p1 = jnp.max(probs, -1, keepdims=True)
        i1 = jnp.min(jnp.where(probs == p1, lane_f, float(LANES)), -1, keepdims=True)
        rest = jnp.where(lane_f == i1, -1.0, probs)
        p2 = jnp.max(rest, -1, keepdims=True)
        i2 = jnp.min(jnp.where(rest == p2, lane_f, float(LANES)), -1, keepdims=True)
        tot = p1 + p2
        gate_ref[...] = jnp.where(lane_f == i1, p1 / tot, jnp.where(lane_f == i2, p2 / tot, 0.0))
        chosen = (lane_f == i1) | (lane_f == i2)
        before = (_iota((tm, tm), 0) > _iota((tm, tm), 1)).astype(BF16)
        onehot = jnp.where(chosen, 1.0, 0.0)
        slot = _dot(before, onehot.astype(BF16))
        posm = jnp.where(chosen, slot, -1.0)
        pos_ref[...] = posm
        post_ref[...] = posm.T[0:N_EXPERTS, :]
        cnt_ref[...] = jnp.broadcast_to(jnp.sum(onehot, 0, keepdims=True), (8, LANES)).astype(jnp.int32)


def _out_proj(o_attn, o_nsa, x2, wl, router=None, tm=ROUTE_ROWS):
    n = x2.shape[0]
    routed = router is not None
    full = lambda a: pl.BlockSpec(a.shape, lambda i: (0,) * a.ndim)
    row = lambda w: pl.BlockSpec((tm, w), lambda i: (i, 0))
    consts = [wl["w_out_a"], wl["w_out_b"], wl["ln1_g"], wl["ln1_b"]] + ([router] if routed else [])
    out_specs = [row(D_MODEL)]
    out_shape = [jax.ShapeDtypeStruct((n, D_MODEL), F32)]
    if routed:
        out_specs += [row(LANES), row(LANES), pl.BlockSpec((N_EXPERTS, tm), lambda i: (i, 0)),
                      pl.BlockSpec((8, LANES), lambda i: (i, 0))]
        out_shape += [jax.ShapeDtypeStruct((n, LANES), F32), jax.ShapeDtypeStruct((n, LANES), F32),
                      jax.ShapeDtypeStruct((n // tm * N_EXPERTS, tm), F32),
                      jax.ShapeDtypeStruct((n // tm * 8, LANES), jnp.int32)]
    outs = pl.pallas_call(
        functools.partial(_out_proj_kernel, routed=routed),
        grid=(n // tm,),
        in_specs=[row(o_attn.shape[1]), row(o_nsa.shape[1]), row(D_MODEL)] + [full(a) for a in consts],
        out_specs=out_specs,
        out_shape=out_shape,
        compiler_params=_cparams(("parallel",)),
        name="out_proj_route_ln" if routed else "out_proj_ln",
    )(o_attn, o_nsa, x2, *consts)
    return outs if routed else outs[0]


def _ffn_kernel(x_ref, w1_ref, w3_ref, w2_ref, g_ref, b_ref, o_ref, acc_ref, xb_ref):
    c = pl.program_id(1)

    @pl.when(c == 0)
    def _():
        acc_ref[...] = jnp.zeros_like(acc_ref)
        xb_ref[...] = x_ref[...].astype(BF16)

    xb = xb_ref[...]
    a = jax.nn.silu(_dot(xb, w1_ref[...])) * _dot(xb, w3_ref[...])
    acc_ref[...] += _dot(a.astype(BF16), w2_ref[...])

    @pl.when(c == pl.num_programs(1) - 1)
    def _():
        o_ref[...] = _layer_norm(DEEPNORM_ALPHA * x_ref[...] + acc_ref[...], g_ref[...], b_ref[...])


def _ffn(x2, w1, w3, w2, ln_g, ln_b, tm=ROW_TILE, nchunk=2):
    n = x2.shape[0]
    tf = w1.shape[1] // nchunk
    full = lambda a: pl.BlockSpec(a.shape, lambda i, c: (0,) * a.ndim)
    return pl.pallas_call(
        _ffn_kernel,
        grid=(n // tm, nchunk),
        in_specs=[pl.BlockSpec((tm, D_MODEL), lambda i, c: (i, 0)),
                  pl.BlockSpec((D_MODEL, tf), lambda i, c: (0, c)),
                  pl.BlockSpec((D_MODEL, tf), lambda i, c: (0, c)),
                  pl.BlockSpec((tf, D_MODEL), lambda i, c: (c, 0)),
                  full(ln_g), full(ln_b)],
        out_specs=pl.BlockSpec((tm, D_MODEL), lambda i, c: (i, 0)),
        out_shape=jax.ShapeDtypeStruct((n, D_MODEL), F32),
        scratch_shapes=[pltpu.VMEM((tm, D_MODEL), F32), pltpu.VMEM((tm, D_MODEL), BF16)],
        compiler_params=_cparams(("parallel", "arbitrary")),
        name="ffn_ln",
    )(x2, w1, w3, w2, ln_g, ln_b)


MOE_CAP = 160
MOE_CAP_PAD = -(-MOE_CAP // LANES) * LANES
MOE_CHUNKS = -(-ROUTE_ROWS // MOE_CAP)


def _moe_kernel(cnt_ref, x_ref, gate_ref, pos_ref, *rest, groups):
    post_refs = rest[:groups]
    w1_ref, w3_ref, w2_ref, g_ref, b_ref, o_ref, acc_ref, xb_ref = rest[groups:]
    i = pl.program_id(0)
    e = pl.program_id(1)
    lane = _iota((1, LANES), 1)

    @pl.when(e == 0)
    def _():
        acc_ref[...] = jnp.zeros_like(acc_ref)
        xb_ref[...] = x_ref[...].astype(BF16)

    for gi in range(groups):
        rows = slice(ROUTE_ROWS * gi, ROUTE_ROWS * (gi + 1))
        cnt = cnt_ref[(i * groups + gi) * N_EXPERTS + e]
        gate_e = jnp.sum(jnp.where(lane == e, gate_ref[rows, :], 0.0), -1, keepdims=True)
        slot_col = jnp.sum(jnp.where(lane == e, pos_ref[rows, :], 0.0), -1, keepdims=True)
        slot_row = post_refs[gi][0]
        for k in range(MOE_CHUNKS):
            @pl.when(cnt > k * MOE_CAP)
            def _():
                want = (_iota((MOE_CAP, 1), 0) + k * MOE_CAP).astype(F32)
                pick = jnp.where(slot_row == want, 1.0, 0.0).astype(BF16)
                xg = _dot(pick, xb_ref[rows, :]).astype(BF16)
                a = jax.nn.silu(_dot(xg, w1_ref[0])) * _dot(xg, w3_ref[0])
                y = _dot(a.astype(BF16), w2_ref[0])
                col = _iota((1, MOE_CAP_PAD), 1)
                put = jnp.where((slot_col == (col + k * MOE_CAP).astype(F32)) & (col < MOE_CAP),
                                1.0, 0.0).astype(BF16)
                yb = y.astype(BF16)
                if MOE_CAP_PAD > MOE_CAP:
                    yb = jnp.concatenate([yb, jnp.zeros((MOE_CAP_PAD - MOE_CAP, D_MODEL), BF16)], 0)
                acc_ref[rows, :] += gate_e * _dot(put, yb)

    @pl.when(e == pl.num_programs(1) - 1)
    def _():
        o_ref[...] = _layer_norm(DEEPNORM_ALPHA * x_ref[...] + acc_ref[...], g_ref[...], b_ref[...])


def _moe(x2, gates, pos, post, cnt, w1, w3, w2, ln_g, ln_b, groups=2):
    n = x2.shape[0]
    ne, _, tf = w1.shape
    tm = groups * ROUTE_ROWS
    post3 = post.reshape(n // ROUTE_ROWS * N_EXPERTS, 1, ROUTE_ROWS)
    full = lambda a: pl.BlockSpec(a.shape, lambda i, e, c: (0,) * a.ndim)
    row = lambda w: pl.BlockSpec((tm, w), lambda i, e, c: (i, 0))
    post_spec = lambda gi: pl.BlockSpec((1, 1, ROUTE_ROWS),
                                        lambda i, e, c: ((i * groups + gi) * N_EXPERTS + e, 0, 0))
    grid_spec = pltpu.PrefetchScalarGridSpec(
        num_scalar_prefetch=1,
        grid=(n // tm, ne),
        in_specs=[row(D_MODEL), row(LANES), row(LANES)] + [post_spec(gi) for gi in range(groups)]
                 + [pl.BlockSpec((1, D_MODEL, tf), lambda i, e, c: (e, 0, 0)),
                    pl.BlockSpec((1, D_MODEL, tf), lambda i, e, c: (e, 0, 0)),
                    pl.BlockSpec((1, tf, D_MODEL), lambda i, e, c: (e, 0, 0)),
                    full(ln_g), full(ln_b)],
        out_specs=row(D_MODEL),
        scratch_shapes=[pltpu.VMEM((tm, D_MODEL), F32), pltpu.VMEM((tm, D_MODEL), BF16)],
    )
    return pl.pallas_call(
        functools.partial(_moe_kernel, groups=groups),
        grid_spec=grid_spec,
        out_shape=jax.ShapeDtypeStruct((n, D_MODEL), F32),
        compiler_params=_cparams(("parallel", "arbitrary")),
        name="moe_top2_ln",
    )(cnt, x2, gates, pos, *([post3] * groups), w1, w3, w2, ln_g, ln_b)


def _in_proj_columns():
    src = np.full((IN_P,), -1, np.int64)
    scale = np.ones((IN_P,), np.float32)
    o_cq, o_ckv, o_kr = 0, MLA_Q_RANK, MLA_Q_RANK + MLA_KV_RANK
    o_fox = o_kr + MLA_ROPE
    o_foxf = o_fox + 3 * FOX_HEADS * HEAD_DIM
    o_nq = o_foxf + FOX_HEADS
    o_nkv = o_nq + NSA_HEADS * HEAD_DIM
    o_ng = o_nkv + 6 * HEAD_DIM
    src[0:o_kr] = np.arange(o_kr)
    small = o_kr
    src[small + KR_LANE:small + KR_LANE + MLA_ROPE] = o_kr + np.arange(MLA_ROPE)
    src[small + FOXF_LANE:small + FOXF_LANE + FOX_HEADS] = o_foxf + np.arange(FOX_HEADS)
    src[small + NSAG_LANE:small + NSAG_LANE + 3 * NSA_HEADS] = o_ng + np.arange(3 * NSA_HEADS)
    n_fox = FOX_HEADS * HEAD_DIM
    src[FQ_OFF:FQ_OFF + 3 * n_fox] = o_fox + np.arange(3 * n_fox)
    scale[FQ_OFF:FQ_OFF + n_fox] = HEAD_DIM ** -0.5
    src[NQ_OFF:NQ_OFF + NSA_HEADS * HEAD_DIM] = o_nq + np.arange(NSA_HEADS * HEAD_DIM)
    scale[NQ_OFF:NQ_OFF + NSA_HEADS * HEAD_DIM] = HEAD_DIM ** -0.5
    src[NC_OFF:NC_OFF + 6 * HEAD_DIM] = o_nkv + np.arange(6 * HEAD_DIM)
    return src, scale


def _gather_cols(w, src, scale=None):
    scale = np.ones(src.shape, np.float32) if scale is None else scale
    parts, i, n = [], 0, len(src)
    while i < n:
        j = i + 1
        while j < n and scale[j] == scale[i] and (src[j] == src[j - 1] + 1 if src[i] >= 0 else src[j] < 0):
            j += 1
        if src[i] < 0:
            parts.append(jnp.zeros(w.shape[:-1] + (j - i,), w.dtype))
        else:
            seg = w[..., int(src[i]):int(src[i]) + (j - i)]
            parts.append(seg if scale[i] == 1.0 else seg * float(scale[i]))
        i = j
    return jnp.concatenate(parts, axis=-1)


def _tables(s):
    half = MLA_ROPE // 2
    freqs = ROPE_THETA ** (-jnp.arange(half, dtype=F32) / half)
    ang = jnp.arange(s).astype(F32)[:, None] * freqs[None, :]
    cos, sin = jnp.cos(ang), jnp.sin(ang)
    z = lambda w: jnp.zeros((s, w), F32)
    tabs = {
        "cos": jnp.concatenate([jnp.ones((s, MLA_NOPE), F32), cos, cos, z(LANES - MLA_NOPE - MLA_ROPE)], 1),
        "sina": jnp.concatenate([z(MLA_NOPE), -sin, z(half), z(LANES - MLA_NOPE - MLA_ROPE)], 1),
        "sinb": jnp.concatenate([z(MLA_NOPE), z(half), sin, z(LANES - MLA_NOPE - MLA_ROPE)], 1),
    }
    pq = np.zeros((LANES, FOX_HEADS * LANES), np.float32)
    pk = np.zeros((LANES, FOX_HEADS * LANES), np.float32)
    oneq = np.zeros((1, FOX_HEADS * LANES), np.float32)
    onek = np.zeros((1, FOX_HEADS * LANES), np.float32)
    for h in range(FOX_HEADS):
        for t in range(3):
            pq[FOXF_LANE + FOX_HEADS * t + h, LANES * h + HEAD_DIM + t] = 1.0
            pk[FOXF_LANE + FOX_HEADS * t + h, LANES * h + HEAD_DIM + 3 + t] = -1.0
            oneq[0, LANES * h + HEAD_DIM + 3 + t] = 1.0
            onek[0, LANES * h + HEAD_DIM + t] = 1.0
    tabs.update(pq=jnp.asarray(pq, BF16), pk=jnp.asarray(pk, BF16), oneq=jnp.asarray(oneq), onek=jnp.asarray(onek))
    qaux = np.zeros((1, NSA_HEADS * LANES), np.float32)
    for h in range(NSA_HEADS):
        qaux[0, LANES * h + NSA_AUX_LANE] = ALIBI[h] * NSA_SEL_BLOCK
        qaux[0, LANES * h + NSA_AUX_LANE + 1] = ALIBI[h]
    kaux = np.zeros((s, LANES), np.float32)
    pos = np.arange(s)
    kaux[:, NSA_AUX_LANE] = pos // NSA_SEL_BLOCK
    kaux[:, NSA_AUX_LANE + 1] = pos % NSA_SEL_BLOCK
    kaux[pos, NSA_SEL_LANE + pos // NSA_SEL_BLOCK] = 1.0
    tabs.update(nsa_qaux=jnp.asarray(qaux), nsa_kaux=jnp.asarray(kaux))
    return tabs


def _layer_weights(p, l):
    qd = MLA_NOPE + MLA_ROPE
    src_q = np.full((MLA_HEADS * LANES,), -1, np.int64)
    src_k = np.full((MLA_HEADS * LANES,), -1, np.int64)
    src_v = np.full((MLA_HEADS * LANES,), -1, np.int64)
    for h in range(MLA_HEADS):
        src_q[LANES * h + np.arange(qd)] = qd * h + np.arange(qd)
        src_k[LANES * h + np.arange(MLA_NOPE)] = 2 * HEAD_DIM * h + np.arange(MLA_NOPE)
        src_v[LANES * h + np.arange(HEAD_DIM)] = 2 * HEAD_DIM * h + MLA_NOPE + np.arange(HEAD_DIM)
    w_out = p["w_out"][l]
    n_attn = (MLA_HEADS + FOX_HEADS) * HEAD_DIM
    wb = w_out[n_attn:].reshape(NSA_HEADS, HEAD_DIM, D_MODEL)
    wb = jnp.concatenate([wb, jnp.zeros_like(wb)], axis=1).reshape(NSA_HEADS * LANES, D_MODEL)
    bf = jnp.zeros((1, LANES), F32).at[0, FOXF_LANE:FOXF_LANE + FOX_HEADS].set(p["b_forget"][l])

    half = NSA_CMP_BLOCK // 2
    kpos, vpos = p["cmp_k_pos"][l], p["cmp_v_pos"][l]
    posa = jnp.concatenate([kpos[:half], vpos[:half]], -1).reshape(1, half * LANES)
    posb = jnp.concatenate([kpos[half:], vpos[half:]], -1).reshape(1, half * LANES)

    def w1_half(lo):
        k = p["cmp_k_w1"][l].reshape(NSA_CMP_BLOCK, HEAD_DIM, NSA_CMP_HIDDEN)[lo:lo + half]
        v = p["cmp_v_w1"][l].reshape(NSA_CMP_BLOCK, HEAD_DIM, NSA_CMP_HIDDEN)[lo:lo + half]
        zk = jnp.zeros_like(k)
        top = jnp.concatenate([k, zk], -1)
        bot = jnp.concatenate([zk, v], -1)
        return jnp.concatenate([top, bot], 1).reshape(half * LANES, 2 * NSA_CMP_HIDDEN).astype(BF16)

    zw2 = jnp.zeros((NSA_CMP_HIDDEN, HEAD_DIM), F32)
    w2 = jnp.concatenate([jnp.concatenate([p["cmp_k_w2"][l], zw2], 1),
                          jnp.concatenate([zw2, p["cmp_v_w2"][l]], 1)], 0).astype(BF16)
    return {
        "g_cq": p["g_cq"][l][None, :], "g_ckv": p["g_ckv"][l][None, :],
        "w_uq": _gather_cols(p["w_uq"][l], src_q).astype(BF16),
        "w_uk": _gather_cols(p["w_ukv"][l], src_k).astype(BF16),
        "w_uv": _gather_cols(p["w_ukv"][l], src_v).astype(BF16),
        "b_forget": bf,
        "cmp_posa": posa, "cmp_posb": posb, "cmp_wa": w1_half(0), "cmp_wb": w1_half(half), "cmp_w2": w2,
        "w_out_a": w_out[:n_attn].astype(BF16), "w_out_b": wb.astype(BF16),
        "ln1_g": p["ln1_g"][l][None, :], "ln1_b": p["ln1_b"][l][None, :],
        "ln2_g": p["ln2_g"][l][None, :], "ln2_b": p["ln2_b"][l][None, :],
    }


def kernel(x, w_in, b_forget, g_cq, w_uq, g_ckv, w_ukv, cmp_k_pos, cmp_k_w1, cmp_k_w2, cmp_v_pos, cmp_v_w1,
           cmp_v_w2, w_out, ln1_g, ln1_b, ln2_g, ln2_b, ffn_w1, ffn_w3, ffn_w2, router_w, moe_w1, moe_w3,
           moe_w2):
    b, s, d = x.shape
    assert d == D_MODEL and s % 512 == 0 and s // NSA_CMP_STRIDE == LANES, (b, s, d)
    p = dict(b_forget=b_forget, g_cq=g_cq, w_uq=w_uq, g_ckv=g_ckv, w_ukv=w_ukv, cmp_k_pos=cmp_k_pos,
             cmp_k_w1=cmp_k_w1, cmp_k_w2=cmp_k_w2, cmp_v_pos=cmp_v_pos, cmp_v_w1=cmp_v_w1, cmp_v_w2=cmp_v_w2,
             w_out=w_out, ln1_g=ln1_g, ln1_b=ln1_b, ln2_g=ln2_g, ln2_b=ln2_b)
    src, scale = _in_proj_columns()
    w_in_p = _gather_cols(w_in, src, scale).astype(BF16)
    tabs = _tables(s)
    n = b * s
    x2 = x.reshape(n, d)
    for l in range(DEPTH):
        wl = _layer_weights(p, l)
        small, nq, nc, nsl, nwin, q_all, k_all, v_all = _front(x2, w_in_p[l], wl, tabs, s)
        o_attn = _flash(*(t.reshape(b, s, -1) for t in (q_all, k_all, v_all)))
        kcvc = _compress(nc.reshape(b, s, LANES), wl)
        o_nsa = _nsa(nq.reshape(b, s, NSA_HEADS * LANES), kcvc, nsl.reshape(b, s, 2 * LANES),
                     nwin.reshape(b, s, 2 * LANES), small)
        j = l // 2
        if l % 2 == 0:
            x2 = _out_proj(o_attn.reshape(n, -1), o_nsa.reshape(n, -1), x2, wl)
            x2 = _ffn(x2, ffn_w1[j].astype(BF16), ffn_w3[j].astype(BF16), ffn_w2[j].astype(BF16),
                      wl["ln2_g"], wl["ln2_b"])
        else:
            rw = jnp.pad(router_w[j], ((0, 0), (0, LANES - N_EXPERTS)))
            rw_hi = rw.astype(BF16)
            rw_lo = (rw - rw_hi.astype(F32)).astype(BF16)
            rw = jnp.concatenate([jnp.concatenate([rw_hi, rw_lo], 1),
                                  jnp.concatenate([rw_hi, jnp.zeros_like(rw_lo)], 1)], 0)
            x2, gates, pos, post, cnt = _out_proj(o_attn.reshape(n, -1), o_nsa.reshape(n, -1), x2, wl, router=rw)
            cnt = cnt[::8, :N_EXPERTS].reshape(-1)
            x2 = _moe(x2, gates, pos, post, cnt, moe_w1[j].astype(BF16), moe_w3[j].astype(BF16),
                      moe_w2[j].astype(BF16), wl["ln2_g"], wl["ln2_b"])
    return x2.reshape(b, s, d)
```

```python
import functools

import numpy as np
import jax
import jax.numpy as jnp
from jax import lax
from jax.experimental import pallas as pl
from jax.experimental.pallas import tpu as pltpu

F32 = jnp.float32
BF16 = jnp.bfloat16

D_MODEL = 1024
HEAD_DIM = 64
LANES = 128
MLA_HEADS = 6
MLA_Q_RANK = 384
MLA_KV_RANK = 256
MLA_NOPE = 64
MLA_ROPE = 32
ROPE_THETA = 10000.0
FOX_HEADS = 6
NSA_HEADS = 4
NSA_CMP_BLOCK = 32
NSA_CMP_STRIDE = 16
NSA_CMP_HIDDEN = 128
NSA_SEL_BLOCK = 64
NSA_SEL_TOPN = 8
NSA_WINDOW = 256
NSA_FORCE_SCORE = 1.0e4
N_EXPERTS = 8
NORM_EPS = 1e-5
DEPTH = 2
DEEPNORM_ALPHA = (2 * DEPTH) ** 0.25
NEG = -1e30
VMEM_LIMIT = 56 * 1024 * 1024

ROW_TILE = 512
FLASH_TQ, FLASH_TK = 2048, 512
NSA_TQ = 512

KR_LANE = 64
FOXF_LANE = 96
NSAG_LANE = 102

ZA_W = 768
FQ_OFF = ZA_W
FK_OFF = FQ_OFF + 384
FV_OFF = FK_OFF + 384
NQ_OFF = FV_OFF + 384
NC_OFF = NQ_OFF + 256
NKV_OFF = NC_OFF + 128
IN_P = NKV_OFF + 256

NSA_AUX_LANE = HEAD_DIM
NSA_SEL_LANE = HEAD_DIM + 2
NSA_MASK_BIG = 2.0 ** 126

ALIBI = tuple(2.0 ** (-8.0 * (i + 1) / NSA_HEADS) for i in range(NSA_HEADS))


def _dot(a, b, **kw):
    return jnp.dot(a, b, preferred_element_type=F32, **kw)


def _dot_nt(a, b):
    return lax.dot_general(a, b, (((1,), (1,)), ((), ())), preferred_element_type=F32)


def _iota(shape, dim):
    return lax.broadcasted_iota(jnp.int32, shape, dim)


def _split3(x):
    hi = x.astype(BF16)
    r = x - hi.astype(F32)
    mid = r.astype(BF16)
    return hi, mid, (r - mid.astype(F32)).astype(BF16)


def _dot_exact_rhs(x, w01):
    hi, mid, lo = _split3(x)
    return _dot(jnp.concatenate([hi, mid, lo], 1), jnp.concatenate([w01, w01, w01], 0))


def _dot_split(x, w_stack):
    hi = x.astype(BF16)
    lo = (x - hi.astype(F32)).astype(BF16)
    r = _dot(jnp.concatenate([hi, lo], 1), w_stack)
    half = r.shape[1] // 2
    return r[:, :half] + r[:, half:]


def _cparams(sem):
    return pltpu.CompilerParams(dimension_semantics=sem, vmem_limit_bytes=VMEM_LIMIT)


def _front_kernel(x_ref, w_ref, qaux_ref, kaux_ref, gcq_ref, wuq_ref, gckv_ref, wuk_ref, wuv_ref, bf_ref,
                  cos_ref, sina_ref, sinb_ref, pq_ref, pk_ref, oneq_ref, onek_ref,
                  small_ref, nq_ref, nc_ref, nsl_ref, nwin_ref, q_out, k_out, v_out, carry_ref,
                  *, tm, tiles_per_seq):
    @pl.when(pl.program_id(0) % tiles_per_seq == 0)
    def _():
        carry_ref[...] = jnp.zeros_like(carry_ref)

    xb = x_ref[...].astype(BF16)

    def mm(a, b):
        return _dot(xb, w_ref[:, a:b])

    lane = _iota((1, LANES), 1)

    def halves(pair):
        return (jnp.where(lane < HEAD_DIM, pair, 0.0),
                jnp.where(lane < HEAD_DIM, pltpu.roll(pair, HEAD_DIM, 1), 0.0))

    def expand(z):
        return [blk for pr in range(z.shape[1] // LANES) for blk in halves(z[:, LANES * pr:LANES * (pr + 1)])]

    def pad_v(blk):
        return jnp.where(lane < HEAD_DIM, blk, jnp.where(lane == HEAD_DIM, 1.0, 0.0)).astype(BF16)

    cos = cos_ref[...]
    sina = sina_ref[...]
    sinb = sinb_ref[...]

    def rope(blk):
        return blk * cos + pltpu.roll(blk, LANES - 16, 1) * sina + pltpu.roll(blk, 16, 1) * sinb

    za = mm(0, ZA_W)
    cq = za[:, 0:MLA_Q_RANK]
    ckv = za[:, MLA_Q_RANK:MLA_Q_RANK + MLA_KV_RANK]
    small = za[:, MLA_Q_RANK + MLA_KV_RANK:ZA_W]
    small_ref[...] = small

    xn = cq * lax.rsqrt(jnp.mean(cq * cq, -1, keepdims=True) + NORM_EPS) * gcq_ref[...]
    q = _dot(xn.astype(BF16), wuq_ref[...])
    cn = ckv * lax.rsqrt(jnp.mean(ckv * ckv, -1, keepdims=True) + NORM_EPS) * gckv_ref[...]
    cnb = cn.astype(BF16)
    kn = _dot(cnb, wuk_ref[...])
    v = _dot(cnb, wuv_ref[...])
    kr = rope(small)
    mla_scale = (MLA_NOPE + MLA_ROPE) ** -0.5
    for h in range(MLA_HEADS):
        sl = slice(LANES * h, LANES * (h + 1))
        q_out[:, sl] = (rope(q[:, sl]) * mla_scale).astype(BF16)
        k_out[:, sl] = (kn[:, sl] + kr).astype(BF16)
        v_out[:, sl] = pad_v(v[:, sl])

    fmask = (lane >= FOXF_LANE) & (lane < FOXF_LANE + FOX_HEADS)
    lf = jnp.where(fmask, jax.nn.log_sigmoid(small + bf_ref[...]), 0.0)
    tril = jnp.where(_iota((tm, tm), 0) >= _iota((tm, tm), 1), 1.0, 0.0).astype(BF16)
    lf_hi, lf_mid, lf_lo = _split3(lf)
    part = _dot(tril, jnp.concatenate([lf_hi, lf_mid], 1))
    cs = part[:, :LANES] + part[:, LANES:] + _dot(tril, lf_lo) + carry_ref[...]
    carry_ref[...] = cs[tm - 1:tm, :]
    hi, mid, lo = (t.astype(F32) for t in _split3(cs))
    c3 = (hi + pltpu.roll(mid, FOX_HEADS, 1) + pltpu.roll(lo, 2 * FOX_HEADS, 1)).astype(BF16)
    augq = _dot(c3, pq_ref[...]) + oneq_ref[...]
    augk = _dot(c3, pk_ref[...]) + onek_ref[...]
    off = MLA_HEADS * LANES
    fq = expand(mm(FQ_OFF, FK_OFF))
    fk = expand(mm(FK_OFF, FV_OFF))
    fv = expand(mm(FV_OFF, NQ_OFF))
    for h in range(FOX_HEADS):
        sl = slice(LANES * h, LANES * (h + 1))
        dst = slice(off + LANES * h, off + LANES * (h + 1))
        q_out[:, dst] = (fq[h] + augq[:, sl]).astype(BF16)
        k_out[:, dst] = (fk[h] + augk[:, sl]).astype(BF16)
        v_out[:, dst] = pad_v(fv[h])

    qaux = qaux_ref[...]
    for h, blk in enumerate(expand(mm(NQ_OFF, NC_OFF))):
        sl = slice(LANES * h, LANES * (h + 1))
        nq_ref[:, sl] = (blk + qaux[:, sl]).astype(BF16)
    nc_ref[...] = mm(NC_OFF, NKV_OFF)
    kaux = kaux_ref[...]
    nkv = mm(NKV_OFF, IN_P)
    for i, out_ref in enumerate((nsl_ref, nwin_ref)):
        k_blk, v_blk = halves(nkv[:, LANES * i:LANES * (i + 1)])
        out_ref[:, 0:LANES] = (k_blk + kaux).astype(BF16)
        out_ref[:, LANES:2 * LANES] = pad_v(v_blk)


def _front(x2, w_p, wl, tabs, s, tm=ROW_TILE):
    n = x2.shape[0]
    ns = s // tm
    nh = MLA_HEADS + FOX_HEADS
    full = lambda a: pl.BlockSpec(a.shape, lambda i: (0,) * a.ndim)
    tab = pl.BlockSpec((tm, LANES), lambda i: (i % ns, 0))
    consts = [wl["g_cq"], wl["w_uq"], wl["g_ckv"], wl["w_uk"], wl["w_uv"], wl["b_forget"]]
    tail = [tabs["pq"], tabs["pk"], tabs["oneq"], tabs["onek"]]
    widths = [(LANES, F32), (NSA_HEADS * LANES, BF16), (LANES, F32), (2 * LANES, BF16), (2 * LANES, BF16),
              (nh * LANES, BF16), (nh * LANES, BF16), (nh * LANES, BF16)]
    return pl.pallas_call(
        functools.partial(_front_kernel, tm=tm, tiles_per_seq=ns),
        grid=(n // tm,),
        in_specs=[pl.BlockSpec((tm, D_MODEL), lambda i: (i, 0)), full(w_p), full(tabs["nsa_qaux"]), tab]
                 + [full(a) for a in consts] + [tab, tab, tab] + [full(a) for a in tail],
        out_specs=[pl.BlockSpec((tm, w), lambda i: (i, 0)) for w, _ in widths],
        out_shape=[jax.ShapeDtypeStruct((n, w), dt) for w, dt in widths],
        scratch_shapes=[pltpu.VMEM((1, LANES), F32)],
        compiler_params=_cparams(("arbitrary",)),
        name="front_proj",
    )(x2, w_p, tabs["nsa_qaux"], tabs["nsa_kaux"], *consts, tabs["cos"], tabs["sina"], tabs["sinb"], *tail)


def _flash_kernel(q_ref, k_ref, v_ref, o_ref, m_ref, acc_ref, s_ref, *, tq, tk):
    qi = pl.program_id(2)
    m_ref[...] = jnp.full_like(m_ref, NEG)
    acc_ref[...] = jnp.zeros_like(acc_ref)
    nsub = tq // tk
    nfull = qi * nsub
    heads = [slice(LANES * h, LANES * (h + 1)) for h in range(2)]

    def scores(j, r0, sl):
        k0 = pl.multiple_of(j * tk, tk)
        return _dot_nt(q_ref[0, r0:tq, sl], k_ref[0, pl.ds(k0, tk), sl])

    def consume(s, h, j, r0, masked):
        k0 = pl.multiple_of(j * tk, tk)
        if masked:
            s = jnp.where(_iota((tq - r0, tk), 0) >= _iota((tq - r0, tk), 1), s, NEG)
        chunks = [s[:, LANES * c:LANES * (c + 1)] for c in range(tk // LANES)]
        m_prev = m_ref[h, r0:tq, :]
        m_new = jnp.maximum(m_prev, jnp.max(functools.reduce(jnp.maximum, chunks), -1, keepdims=True))
        p = jnp.concatenate([jnp.exp(c - m_new) for c in chunks], 1).astype(BF16)
        acc_ref[h, r0:tq, :] = (jnp.exp(m_prev - m_new) * acc_ref[h, r0:tq, :]
                                + _dot(p, v_ref[0, pl.ds(k0, tk), heads[h]]))
        m_ref[h, r0:tq, :] = m_new

    assert nsub % 2 == 0
    for h in range(2):
        s_ref[0, h] = scores(0, 0, heads[h])

    def body(i, carry):
        j = 2 * i
        for b in range(2):
            for h in range(2):
                s_ref[1 - b, h] = scores(j + b + 1, 0, heads[h])
            for h in range(2):
                consume(s_ref[b, h], h, j + b, 0, False)
        return carry

    lax.fori_loop(0, qi * (nsub // 2), body, 0)
    for h in range(2):
        consume(s_ref[0, h], h, nfull, 0, True)
    for d in range(1, nsub):
        for h in range(2):
            consume(scores(nfull + d, d * tk, heads[h]), h, nfull + d, d * tk, True)

    lane = _iota((1, LANES), 1)
    o0 = acc_ref[0]
    o1 = acc_ref[1]
    o0 = o0 / o0[:, HEAD_DIM:HEAD_DIM + 1]
    o1 = o1 / o1[:, HEAD_DIM:HEAD_DIM + 1]
    o_ref[0] = jnp.where(lane < HEAD_DIM, o0, pltpu.roll(o1, HEAD_DIM, 1)).astype(o_ref.dtype)


def _flash(q_all, k_all, v_all, tq=FLASH_TQ, tk=FLASH_TK):
    b, s, _ = q_all.shape
    npair = (MLA_HEADS + FOX_HEADS) // 2
    return pl.pallas_call(
        functools.partial(_flash_kernel, tq=tq, tk=tk),
        grid=(b, npair, s // tq),
        in_specs=[pl.BlockSpec((1, tq, 2 * LANES), lambda bi, p, qi: (bi, qi, p)),
                  pl.BlockSpec((1, s, 2 * LANES), lambda bi, p, qi: (bi, 0, p)),
                  pl.BlockSpec((1, s, 2 * LANES), lambda bi, p, qi: (bi, 0, p))],
        out_specs=pl.BlockSpec((1, tq, LANES), lambda bi, p, qi: (bi, qi, p)),
        out_shape=jax.ShapeDtypeStruct((b, s, npair * LANES), BF16),
        scratch_shapes=[pltpu.VMEM((2, tq, LANES), F32), pltpu.VMEM((2, tq, LANES), F32),
                        pltpu.VMEM((2, 2, tq, tk), F32)],
        compiler_params=_cparams(("parallel", "parallel", "arbitrary")),
        name="flash_attn",
    )(q_all, k_all, v_all)


def _cmp_kernel(nc_ref, posa_ref, posb_ref, wa_ref, wb_ref, w2_ref, out_ref, *, n_chunk):
    a = jnp.zeros((n_chunk, 2 * NSA_CMP_HIDDEN), F32)
    b = jnp.zeros((n_chunk, 2 * NSA_CMP_HIDDEN), F32)
    for l in range(NSA_CMP_STRIDE):
        t = nc_ref[0, pl.ds(l, n_chunk, stride=NSA_CMP_STRIDE), :]
        sl = slice(LANES * l, LANES * (l + 1))
        a = a + _dot((t + posa_ref[:, sl]).astype(BF16), wa_ref[sl, :])
        b = b + _dot((t + posb_ref[:, sl]).astype(BF16), wb_ref[sl, :])
    pre = a + pltpu.roll(b, n_chunk - 1, 0)
    hid = jax.nn.silu(pre)
    out_ref[0] = _dot(hid.astype(BF16), w2_ref[...]).astype(out_ref.dtype)


def _compress(nc3, wl):
    b, s, _ = nc3.shape
    n_chunk = s // NSA_CMP_STRIDE
    full = lambda a: pl.BlockSpec(a.shape, lambda bi: (0,) * a.ndim)
    consts = [wl["cmp_posa"], wl["cmp_posb"], wl["cmp_wa"], wl["cmp_wb"], wl["cmp_w2"]]
    return pl.pallas_call(
        functools.partial(_cmp_kernel, n_chunk=n_chunk),
        grid=(b,),
        in_specs=[pl.BlockSpec((1, s, LANES), lambda bi: (bi, 0, 0))] + [full(a) for a in consts],
        out_specs=pl.BlockSpec((1, n_chunk, LANES), lambda bi: (bi, 0, 0)),
        out_shape=jax.ShapeDtypeStruct((b, n_chunk, LANES), BF16),
        compiler_params=_cparams(("parallel",)),
        name="nsa_compress",
    )(nc3, *consts)


def _masked_softmax(s, mask):
    s = jnp.where(mask, s, NEG)
    m = jnp.max(s, -1, keepdims=True)
    p = jnp.where(mask, jnp.exp(s - m), 0.0)
    return p / jnp.maximum(jnp.sum(p, -1, keepdims=True), 1e-30)


def _nsa_kernel(nq_ref, kc_ref, ksl_ref, kwin_ref, g_ref, o_ref, m_ref, acc_ref, qst_ref, s_ref, *, tq, n_cmp):
    qi = pl.program_id(1)
    t0 = pl.multiple_of(qi * tq, tq)
    rpos = t0 + _iota((tq, 1), 0)
    lane = _iota((1, LANES), 1)
    qs = [nq_ref[0, :, LANES * h:LANES * (h + 1)] for h in range(NSA_HEADS)]

    dist_i = rpos - (NSA_CMP_STRIDE * lane + NSA_CMP_BLOCK - 1)
    valid_c = (dist_i >= 0) & (lane < n_cmp)
    dist_c = dist_i.astype(F32)
    kc = kc_ref[0]
    psum = jnp.zeros((tq, LANES), F32)
    o_cmp = []
    for h in range(NSA_HEADS):
        q_head = jnp.where(lane < HEAD_DIM, qs[h], jnp.zeros_like(qs[h]))
        p = _masked_softmax(_dot_nt(q_head, kc) - ALIBI[h] * dist_c, valid_c)
        psum = psum + p
        o_cmp.append(_dot(p.astype(BF16), kc))

    n_i = _iota((LANES, LANES), 0)
    j_i = _iota((LANES, LANES), 1)
    ov = ((NSA_CMP_STRIDE * n_i < NSA_SEL_BLOCK * (j_i + 1))
          & (NSA_CMP_STRIDE * n_i + NSA_CMP_BLOCK > NSA_SEL_BLOCK * j_i)
          & (n_i < n_cmp))
    imp = _dot_exact_rhs(psum, jnp.where(ov, 1.0, 0.0).astype(BF16))
    cur = jnp.right_shift(rpos, 6)
    forced = (lane == 0) | (lane == cur) | (lane == cur - 1)
    future = lane * NSA_SEL_BLOCK > rpos
    n_blk = ksl_ref.shape[1] // NSA_SEL_BLOCK
    work = jnp.where(forced, NSA_FORCE_SCORE, jnp.where(future, -1.0, imp))
    work = jnp.where(lane < n_blk, work, -jnp.inf)
    work_t = work.T[0:n_blk, :]
    blk_id = _iota((n_blk, 1), 0)
    rank = jnp.zeros((n_blk, tq), F32)
    for jp in range(n_blk):
        other = work_t[jp:jp + 1, :]
        beats = (other > work_t) | ((other == work_t) & (jp < blk_id))
        rank = rank + jnp.where(beats, 1.0, 0.0)
    sel_t = jnp.where(rank < NSA_SEL_TOPN, 1.0, 0.0)
    sel = jnp.concatenate([sel_t, jnp.zeros((LANES - n_blk, tq), F32)], 0).T > 0.5
    sel_lanes = pltpu.roll(jnp.where(sel, 1.0, 0.0), NSA_SEL_LANE, 1)
    in_sel = (lane >= NSA_SEL_LANE) & (lane < NSA_SEL_LANE + n_blk)
    sel_bias = jnp.where(in_sel, (sel_lanes - 1.0) * NSA_MASK_BIG, 0.0)
    for h in range(NSA_HEADS):
        qst_ref[h * tq:(h + 1) * tq, :] = (qs[h].astype(F32) + sel_bias).astype(BF16)
    m_ref[...] = jnp.full_like(m_ref, NEG)
    acc_ref[...] = jnp.zeros_like(acc_ref)

    def online(s, v):
        chunks = [s[:, LANES * i:LANES * (i + 1)] for i in range(s.shape[1] // LANES)]
        m_prev = m_ref[...]
        m_new = jnp.maximum(m_prev, jnp.max(functools.reduce(jnp.maximum, chunks), -1, keepdims=True))
        p = jnp.concatenate([jnp.exp(ch - m_new) for ch in chunks], 1).astype(BF16)
        acc_ref[...] = jnp.exp(m_prev - m_new) * acc_ref[...] + _dot(p, v)
        m_ref[...] = m_new

    def scores(c):
        return _dot_nt(qst_ref[...], ksl_ref[0, pl.ds(pl.multiple_of(c * tq, tq), tq), 0:LANES])

    def values(c):
        return ksl_ref[0, pl.ds(pl.multiple_of(c * tq, tq), tq), LANES:2 * LANES]

    s_ref[0] = scores(0)

    def chunk_pair(i, carry):
        j = 2 * i
        s_ref[1] = scores(j + 1)
        online(s_ref[0], values(j))
        s_ref[0] = scores(j + 2)
        online(s_ref[1], values(j + 1))
        return carry

    lax.fori_loop(0, jnp.right_shift(qi, 1), chunk_pair, 0)

    @pl.when((qi & 1) == 1)
    def _():
        online(s_ref[0], values(qi - 1))
        s_ref[0] = scores(qi)

    nrow = NSA_HEADS * tq
    causal = (_iota((nrow, tq), 0) & (tq - 1)) >= _iota((nrow, tq), 1)
    online(jnp.where(causal, s_ref[0], NEG), values(qi))

    wlen = tq + NSA_WINDOW
    w0 = pl.multiple_of(jnp.maximum(t0 - NSA_WINDOW, 0), min(tq, NSA_WINDOW))
    k_w = kwin_ref[0, pl.ds(w0, wlen), 0:LANES]
    v_w = kwin_ref[0, pl.ds(w0, wlen), LANES:2 * LANES]
    d_w = (t0 - w0) + _iota((tq, wlen), 0) - _iota((tq, wlen), 1)
    keep_w = (d_w >= 0) & (d_w < NSA_WINDOW)

    gates = jax.nn.sigmoid(g_ref[...])
    for h in range(NSA_HEADS):
        s_w = jnp.where(keep_w, _dot_nt(qs[h], k_w), NEG)
        p_w = jnp.exp(s_w - jnp.max(s_w, -1, keepdims=True))
        o_win = _dot(p_w.astype(BF16), v_w)
        o_win = o_win / o_win[:, HEAD_DIM:HEAD_DIM + 1]
        o_sel = acc_ref[h * tq:(h + 1) * tq, :]
        o_sel = o_sel / o_sel[:, HEAD_DIM:HEAD_DIM + 1]
        g0 = NSAG_LANE + 3 * h
        o = (gates[:, g0:g0 + 1] * pltpu.roll(o_cmp[h], HEAD_DIM, 1) + gates[:, g0 + 1:g0 + 2] * o_sel
             + gates[:, g0 + 2:g0 + 3] * o_win)
        o_ref[0, :, LANES * h:LANES * (h + 1)] = jnp.where(lane < HEAD_DIM, o, 0.0).astype(o_ref.dtype)


def _nsa(nq, kcvc, nsl, nwin, small, tq=NSA_TQ):
    b, s, _ = nq.shape
    nq_t = s // tq
    n_cmp = (s - NSA_CMP_BLOCK) // NSA_CMP_STRIDE + 1
    return pl.pallas_call(
        functools.partial(_nsa_kernel, tq=tq, n_cmp=n_cmp),
        grid=(b, nq_t),
        in_specs=[pl.BlockSpec((1, tq, NSA_HEADS * LANES), lambda bi, qi: (bi, qi, 0)),
                  pl.BlockSpec((1, kcvc.shape[1], LANES), lambda bi, qi: (bi, 0, 0)),
                  pl.BlockSpec((1, s, 2 * LANES), lambda bi, qi: (bi, 0, 0)),
                  pl.BlockSpec((1, s, 2 * LANES), lambda bi, qi: (bi, 0, 0)),
                  pl.BlockSpec((tq, LANES), lambda bi, qi: (bi * nq_t + qi, 0))],
        out_specs=pl.BlockSpec((1, tq, NSA_HEADS * LANES), lambda bi, qi: (bi, qi, 0)),
        out_shape=jax.ShapeDtypeStruct((b, s, NSA_HEADS * LANES), BF16),
        scratch_shapes=[pltpu.VMEM((NSA_HEADS * tq, LANES), F32), pltpu.VMEM((NSA_HEADS * tq, LANES), F32),
                        pltpu.VMEM((NSA_HEADS * tq, LANES), BF16), pltpu.VMEM((2, NSA_HEADS * tq, tq), F32)],
        compiler_params=_cparams(("parallel", "arbitrary")),
        name="nsa_attn",
    )(nq, kcvc, nsl, nwin, small)


def _layer_norm(y, g, b):
    mu = jnp.mean(y, -1, keepdims=True)
    yc = y - mu
    var = jnp.mean(yc * yc, -1, keepdims=True)
    return yc * lax.rsqrt(var + NORM_EPS) * g + b


ROUTE_ROWS = ROW_TILE


def _out_proj_kernel(*refs, routed):
    if routed:
        (oa_ref, on_ref, x_ref, wa_ref, wb_ref, g_ref, b_ref, rw_ref,
         o_ref, gate_ref, pos_ref, post_ref, cnt_ref) = refs
    else:
        oa_ref, on_ref, x_ref, wa_ref, wb_ref, g_ref, b_ref, o_ref = refs
    mix = _dot(oa_ref[...], wa_ref[...]) + _dot(on_ref[...], wb_ref[...])
    x1 = _layer_norm(DEEPNORM_ALPHA * x_ref[...] + mix, g_ref[...], b_ref[...])
    o_ref[...] = x1
    if routed:
        tm = x1.shape[0]
        lane = _iota((1, LANES), 1)
        lane_f = lane.astype(F32)
        logits = jnp.where(lane < N_EXPERTS, _dot_split(x1, rw_ref[...]), NEG)
        ex = jnp.exp(logits - jnp.max(logits, -1, keepdims=True))
        probs = ex / jnp.sum(ex, -1, keepdims=True)
        p1 = jnp.max(probs, -1, keepdims=True)
        i1 = jnp.min(jnp.where(probs == p1, lane_f, float(LANES)), -1, keepdims=True)
        rest = jnp.where(lane_f == i1, -1.0, probs)
        p2 = jnp.max(rest, -1, keepdims=True)
        i2 = jnp.min(jnp.where(rest == p2, lane_f, float(LANES)), -1, keepdims=True)
        tot = p1 + p2
        gate_ref[...] = jnp.where(lane_f == i1, p1 / tot, jnp.where(lane_f == i2, p2 / tot, 0.0))
        chosen = (lane_f == i1) | (lane_f == i2)
        before = (_iota((tm, tm), 0) > _iota((tm, tm), 1)).astype(BF16)
        onehot = jnp.where(chosen, 1.0, 0.0)
        slot = _dot(before, onehot.astype(BF16))
        posm = jnp.where(chosen, slot, -1.0)
        pos_ref[...] = posm
        post_ref[...] = posm.T[0:N_EXPERTS, :]
        cnt_ref[...] = jnp.broadcast_to(jnp.sum(onehot, 0, keepdims=True), (8, LANES)).astype(jnp.int32)


def _out_proj(o_attn, o_nsa, x2, wl, router=None, tm=ROUTE_ROWS):
    n = x2.shape[0]
    routed = router is not None
    full = lambda a: pl.BlockSpec(a.shape, lambda i: (0,) * a.ndim)
    row = lambda w: pl.BlockSpec((tm, w), lambda i: (i, 0))
    consts = [wl["w_out_a"], wl["w_out_b"], wl["ln1_g"], wl["ln1_b"]] + ([router] if routed else [])
    out_specs = [row(D_MODEL)]
    out_shape = [jax.ShapeDtypeStruct((n, D_MODEL), F32)]
    if routed:
        out_specs += [row(LANES), row(LANES), pl.BlockSpec((N_EXPERTS, tm), lambda i: (i, 0)),
                      pl.BlockSpec((8, LANES), lambda i: (i, 0))]
        out_shape += [jax.ShapeDtypeStruct((n, LANES), F32), jax.ShapeDtypeStruct((n, LANES), F32),
                      jax.ShapeDtypeStruct((n // tm * N_EXPERTS, tm), F32),
                      jax.ShapeDtypeStruct((n // tm * 8, LANES), jnp.int32)]
    outs = pl.pallas_call(
        functools.partial(_out_proj_kernel, routed=routed),
        grid=(n // tm,),
        in_specs=[row(o_attn.shape[1]), row(o_nsa.shape[1]), row(D_MODEL)] + [full(a) for a in consts],
        out_specs=out_specs,
        out_shape=out_shape,
        compiler_params=_cparams(("parallel",)),
        name="out_proj_route_ln" if routed else "out_proj_ln",
    )(o_attn, o_nsa, x2, *consts)
    return outs if routed else outs[0]


def _ffn_kernel(x_ref, w1_ref, w3_ref, w2_ref, g_ref, b_ref, o_ref, acc_ref, xb_ref):
    c = pl.program_id(1)

    @pl.when(c == 0)
    def _():
        acc_ref[...] = jnp.zeros_like(acc_ref)
        xb_ref[...] = x_ref[...].astype(BF16)

    xb = xb_ref[...]
    a = jax.nn.silu(_dot(xb, w1_ref[...])) * _dot(xb, w3_ref[...])
    acc_ref[...] += _dot(a.astype(BF16), w2_ref[...])

    @pl.when(c == pl.num_programs(1) - 1)
    def _():
        o_ref[...] = _layer_norm(DEEPNORM_ALPHA * x_ref[...] + acc_ref[...], g_ref[...], b_ref[...])


def _ffn(x2, w1, w3, w2, ln_g, ln_b, tm=2 * ROW_TILE, nchunk=2):
    n = x2.shape[0]
    tf = w1.shape[1] // nchunk
    full = lambda a: pl.BlockSpec(a.shape, lambda i, c: (0,) * a.ndim)
    return pl.pallas_call(
        _ffn_kernel,
        grid=(n // tm, nchunk),
        in_specs=[pl.BlockSpec((tm, D_MODEL), lambda i, c: (i, 0)),
                  pl.BlockSpec((D_MODEL, tf), lambda i, c: (0, c)),
                  pl.BlockSpec((D_MODEL, tf), lambda i, c: (0, c)),
                  pl.BlockSpec((tf, D_MODEL), lambda i, c: (c, 0)),
                  full(ln_g), full(ln_b)],
        out_specs=pl.BlockSpec((tm, D_MODEL), lambda i, c: (i, 0)),
        out_shape=jax.ShapeDtypeStruct((n, D_MODEL), F32),
        scratch_shapes=[pltpu.VMEM((tm, D_MODEL), F32), pltpu.VMEM((tm, D_MODEL), BF16)],
        compiler_params=_cparams(("parallel", "arbitrary")),
        name="ffn_ln",
    )(x2, w1, w3, w2, ln_g, ln_b)


MOE_CAP = 160
MOE_CAP_PAD = -(-MOE_CAP // LANES) * LANES
MOE_CHUNKS = -(-ROUTE_ROWS // MOE_CAP)


def _moe_kernel(cnt_ref, x_ref, gate_ref, pos_ref, *rest, groups):
    post_refs = rest[:groups]
    w1_ref, w3_ref, w2_ref, g_ref, b_ref, o_ref, acc_ref, xb_ref = rest[groups:]
    i = pl.program_id(0)
    e = pl.program_id(1)
    lane = _iota((1, LANES), 1)

    @pl.when(e == 0)
    def _():
        acc_ref[...] = jnp.zeros_like(acc_ref)
        xb_ref[...] = x_ref[...].astype(BF16)

    for gi in range(groups):
        rows = slice(ROUTE_ROWS * gi, ROUTE_ROWS * (gi + 1))
        cnt = cnt_ref[(i * groups + gi) * N_EXPERTS + e]
        gate_e = jnp.sum(jnp.where(lane == e, gate_ref[rows, :], 0.0), -1, keepdims=True)
        slot_col = jnp.sum(jnp.where(lane == e, pos_ref[rows, :], 0.0), -1, keepdims=True)
        slot_row = post_refs[gi][0]
        for k in range(MOE_CHUNKS):
            @pl.when(cnt > k * MOE_CAP)
            def _():
                want = (_iota((MOE_CAP, 1), 0) + k * MOE_CAP).astype(F32)
                pick = jnp.where(slot_row == want, 1.0, 0.0).astype(BF16)
                xg = _dot(pick, xb_ref[rows, :]).astype(BF16)
                a = jax.nn.silu(_dot(xg, w1_ref[0])) * _dot(xg, w3_ref[0])
                y = _dot(a.astype(BF16), w2_ref[0])
                col = _iota((1, MOE_CAP_PAD), 1)
                put = jnp.where((slot_col == (col + k * MOE_CAP).astype(F32)) & (col < MOE_CAP),
                                1.0, 0.0).astype(BF16)
                yb = y.astype(BF16)
                if MOE_CAP_PAD > MOE_CAP:
                    yb = jnp.concatenate([yb, jnp.zeros((MOE_CAP_PAD - MOE_CAP, D_MODEL), BF16)], 0)
                acc_ref[rows, :] += gate_e * _dot(put, yb)

    @pl.when(e == pl.num_programs(1) - 1)
    def _():
        o_ref[...] = _layer_norm(DEEPNORM_ALPHA * x_ref[...] + acc_ref[...], g_ref[...], b_ref[...])


def _moe(x2, gates, pos, post, cnt, w1, w3, w2, ln_g, ln_b, groups=2):
    n = x2.shape[0]
    ne, _, tf = w1.shape
    tm = groups * ROUTE_ROWS
    post3 = post.reshape(n // ROUTE_ROWS * N_EXPERTS, 1, ROUTE_ROWS)
    full = lambda a: pl.BlockSpec(a.shape, lambda i, e, c: (0,) * a.ndim)
    row = lambda w: pl.BlockSpec((tm, w), lambda i, e, c: (i, 0))
    post_spec = lambda gi: pl.BlockSpec((1, 1, ROUTE_ROWS),
                                        lambda i, e, c: ((i * groups + gi) * N_EXPERTS + e, 0, 0))
    grid_spec = pltpu.PrefetchScalarGridSpec(
        num_scalar_prefetch=1,
        grid=(n // tm, ne),
        in_specs=[row(D_MODEL), row(LANES), row(LANES)] + [post_spec(gi) for gi in range(groups)]
                 + [pl.BlockSpec((1, D_MODEL, tf), lambda i, e, c: (e, 0, 0)),
                    pl.BlockSpec((1, D_MODEL, tf), lambda i, e, c: (e, 0, 0)),
                    pl.BlockSpec((1, tf, D_MODEL), lambda i, e, c: (e, 0, 0)),
                    full(ln_g), full(ln_b)],
        out_specs=row(D_MODEL),
        scratch_shapes=[pltpu.VMEM((tm, D_MODEL), F32), pltpu.VMEM((tm, D_MODEL), BF16)],
    )
    return pl.pallas_call(
        functools.partial(_moe_kernel, groups=groups),
        grid_spec=grid_spec,
        out_shape=jax.ShapeDtypeStruct((n, D_MODEL), F32),
        compiler_params=_cparams(("parallel", "arbitrary")),
        name="moe_top2_ln",
    )(cnt, x2, gates, pos, *([post3] * groups), w1, w3, w2, ln_g, ln_b)


def _in_proj_columns():
    src = np.full((IN_P,), -1, np.int64)
    scale = np.ones((IN_P,), np.float32)
    o_cq, o_ckv, o_kr = 0, MLA_Q_RANK, MLA_Q_RANK + MLA_KV_RANK
    o_fox = o_kr + MLA_ROPE
    o_foxf = o_fox + 3 * FOX_HEADS * HEAD_DIM
    o_nq = o_foxf + FOX_HEADS
    o_nkv = o_nq + NSA_HEADS * HEAD_DIM
    o_ng = o_nkv + 6 * HEAD_DIM
    src[0:o_kr] = np.arange(o_kr)
    small = o_kr
    src[small + KR_LANE:small + KR_LANE + MLA_ROPE] = o_kr + np.arange(MLA_ROPE)
    src[small + FOXF_LANE:small + FOXF_LANE + FOX_HEADS] = o_foxf + np.arange(FOX_HEADS)
    src[small + NSAG_LANE:small + NSAG_LANE + 3 * NSA_HEADS] = o_ng + np.arange(3 * NSA_HEADS)
    n_fox = FOX_HEADS * HEAD_DIM
    src[FQ_OFF:FQ_OFF + 3 * n_fox] = o_fox + np.arange(3 * n_fox)
    scale[FQ_OFF:FQ_OFF + n_fox] = HEAD_DIM ** -0.5
    src[NQ_OFF:NQ_OFF + NSA_HEADS * HEAD_DIM] = o_nq + np.arange(NSA_HEADS * HEAD_DIM)
    scale[NQ_OFF:NQ_OFF + NSA_HEADS * HEAD_DIM] = HEAD_DIM ** -0.5
    src[NC_OFF:NC_OFF + 6 * HEAD_DIM] = o_nkv + np.arange(6 * HEAD_DIM)
    return src, scale


def _gather_cols(w, src, scale=None):
    scale = np.ones(src.shape, np.float32) if scale is None else scale
    parts, i, n = [], 0, len(src)
    while i < n:
        j = i + 1
        while j < n and scale[j] == scale[i] and (src[j] == src[j - 1] + 1 if src[i] >= 0 else src[j] < 0):
            j += 1
        if src[i] < 0:
            parts.append(jnp.zeros(w.shape[:-1] + (j - i,), w.dtype))
        else:
            seg = w[..., int(src[i]):int(src[i]) + (j - i)]
            parts.append(seg if scale[i] == 1.0 else seg * float(scale[i]))
        i = j
    return jnp.concatenate(parts, axis=-1)


def _tables(s):
    half = MLA_ROPE // 2
    freqs = ROPE_THETA ** (-jnp.arange(half, dtype=F32) / half)
    ang = jnp.arange(s).astype(F32)[:, None] * freqs[None, :]
    cos, sin = jnp.cos(ang), jnp.sin(ang)
    z = lambda w: jnp.zeros((s, w), F32)
    tabs = {
        "cos": jnp.concatenate([jnp.ones((s, MLA_NOPE), F32), cos, cos, z(LANES - MLA_NOPE - MLA_ROPE)], 1),
        "sina": jnp.concatenate([z(MLA_NOPE), -sin, z(half), z(LANES - MLA_NOPE - MLA_ROPE)], 1),
        "sinb": jnp.concatenate([z(MLA_NOPE), z(half), sin, z(LANES - MLA_NOPE - MLA_ROPE)], 1),
    }
    pq = np.zeros((LANES, FOX_HEADS * LANES), np.float32)
    pk = np.zeros((LANES, FOX_HEADS * LANES), np.float32)
    oneq = np.zeros((1, FOX_HEADS * LANES), np.float32)
    onek = np.zeros((1, FOX_HEADS * LANES), np.float32)
    for h in range(FOX_HEADS):
        for t in range(3):
            pq[FOXF_LANE + FOX_HEADS * t + h, LANES * h + HEAD_DIM + t] = 1.0
            pk[FOXF_LANE + FOX_HEADS * t + h, LANES * h + HEAD_DIM + 3 + t] = -1.0
            oneq[0, LANES * h + HEAD_DIM + 3 + t] = 1.0
            onek[0, LANES * h + HEAD_DIM + t] = 1.0
    tabs.update(pq=jnp.asarray(pq, BF16), pk=jnp.asarray(pk, BF16), oneq=jnp.asarray(oneq), onek=jnp.asarray(onek))
    qaux = np.zeros((1, NSA_HEADS * LANES), np.float32)
    for h in range(NSA_HEADS):
        qaux[0, LANES * h + NSA_AUX_LANE] = ALIBI[h] * NSA_SEL_BLOCK
        qaux[0, LANES * h + NSA_AUX_LANE + 1] = ALIBI[h]
    kaux = np.zeros((s, LANES), np.float32)
    pos = np.arange(s)
    kaux[:, NSA_AUX_LANE] = pos // NSA_SEL_BLOCK
    kaux[:, NSA_AUX_LANE + 1] = pos % NSA_SEL_BLOCK
    kaux[pos, NSA_SEL_LANE + pos // NSA_SEL_BLOCK] = 1.0
    tabs.update(nsa_qaux=jnp.asarray(qaux), nsa_kaux=jnp.asarray(kaux))
    return tabs


def _layer_weights(p, l):
    qd = MLA_NOPE + MLA_ROPE
    src_q = np.full((MLA_HEADS * LANES,), -1, np.int64)
    src_k = np.full((MLA_HEADS * LANES,), -1, np.int64)
    src_v = np.full((MLA_HEADS * LANES,), -1, np.int64)
    for h in range(MLA_HEADS):
        src_q[LANES * h + np.arange(qd)] = qd * h + np.arange(qd)
        src_k[LANES * h + np.arange(MLA_NOPE)] = 2 * HEAD_DIM * h + np.arange(MLA_NOPE)
        src_v[LANES * h + np.arange(HEAD_DIM)] = 2 * HEAD_DIM * h + MLA_NOPE + np.arange(HEAD_DIM)
    w_out = p["w_out"][l]
    n_attn = (MLA_HEADS + FOX_HEADS) * HEAD_DIM
    wb = w_out[n_attn:].reshape(NSA_HEADS, HEAD_DIM, D_MODEL)
    wb = jnp.concatenate([wb, jnp.zeros_like(wb)], axis=1).reshape(NSA_HEADS * LANES, D_MODEL)
    bf = jnp.zeros((1, LANES), F32).at[0, FOXF_LANE:FOXF_LANE + FOX_HEADS].set(p["b_forget"][l])

    half = NSA_CMP_BLOCK // 2
    kpos, vpos = p["cmp_k_pos"][l], p["cmp_v_pos"][l]
    posa = jnp.concatenate([kpos[:half], vpos[:half]], -1).reshape(1, half * LANES)
    posb = jnp.concatenate([kpos[half:], vpos[half:]], -1).reshape(1, half * LANES)

    def w1_half(lo):
        k = p["cmp_k_w1"][l].reshape(NSA_CMP_BLOCK, HEAD_DIM, NSA_CMP_HIDDEN)[lo:lo + half]
        v = p["cmp_v_w1"][l].reshape(NSA_CMP_BLOCK, HEAD_DIM, NSA_CMP_HIDDEN)[lo:lo + half]
        zk = jnp.zeros_like(k)
        top = jnp.concatenate([k, zk], -1)
        bot = jnp.concatenate([zk, v], -1)
        return jnp.concatenate([top, bot], 1).reshape(half * LANES, 2 * NSA_CMP_HIDDEN).astype(BF16)

    zw2 = jnp.zeros((NSA_CMP_HIDDEN, HEAD_DIM), F32)
    w2 = jnp.concatenate([jnp.concatenate([p["cmp_k_w2"][l], zw2], 1),
                          jnp.concatenate([zw2, p["cmp_v_w2"][l]], 1)], 0).astype(BF16)
    return {
        "g_cq": p["g_cq"][l][None, :], "g_ckv": p["g_ckv"][l][None, :],
        "w_uq": _gather_cols(p["w_uq"][l], src_q).astype(BF16),
        "w_uk": _gather_cols(p["w_ukv"][l], src_k).astype(BF16),
        "w_uv": _gather_cols(p["w_ukv"][l], src_v).astype(BF16),
        "b_forget": bf,
        "cmp_posa": posa, "cmp_posb": posb, "cmp_wa": w1_half(0), "cmp_wb": w1_half(half), "cmp_w2": w2,
        "w_out_a": w_out[:n_attn].astype(BF16), "w_out_b": wb.astype(BF16),
        "ln1_g": p["ln1_g"][l][None, :], "ln1_b": p["ln1_b"][l][None, :],
        "ln2_g": p["ln2_g"][l][None, :], "ln2_b": p["ln2_b"][l][None, :],
    }


def kernel(x, w_in, b_forget, g_cq, w_uq, g_ckv, w_ukv, cmp_k_pos, cmp_k_w1, cmp_k_w2, cmp_v_pos, cmp_v_w1,
           cmp_v_w2, w_out, ln1_g, ln1_b, ln2_g, ln2_b, ffn_w1, ffn_w3, ffn_w2, router_w, moe_w1, moe_w3,
           moe_w2):
    b, s, d = x.shape
    assert d == D_MODEL and s % 512 == 0 and s // NSA_CMP_STRIDE == LANES, (b, s, d)
    p = dict(b_forget=b_forget, g_cq=g_cq, w_uq=w_uq, g_ckv=g_ckv, w_ukv=w_ukv, cmp_k_pos=cmp_k_pos,
             cmp_k_w1=cmp_k_w1, cmp_k_w2=cmp_k_w2, cmp_v_pos=cmp_v_pos, cmp_v_w1=cmp_v_w1, cmp_v_w2=cmp_v_w2,
             w_out=w_out, ln1_g=ln1_g, ln1_b=ln1_b, ln2_g=ln2_g, ln2_b=ln2_b)
    src, scale = _in_proj_columns()
    w_in_p = _gather_cols(w_in, src, scale).astype(BF16)
    tabs = _tables(s)
    n = b * s
    x2 = x.reshape(n, d)
    for l in range(DEPTH):
        wl = _layer_weights(p, l)
        small, nq, nc, nsl, nwin, q_all, k_all, v_all = _front(x2, w_in_p[l], wl, tabs, s)
        o_attn = _flash(*(t.reshape(b, s, -1) for t in (q_all, k_all, v_all)))
        kcvc = _compress(nc.reshape(b, s, LANES), wl)
        o_nsa = _nsa(nq.reshape(b, s, NSA_HEADS * LANES), kcvc, nsl.reshape(b, s, 2 * LANES),
                     nwin.reshape(b, s, 2 * LANES), small)
        j = l // 2
        if l % 2 == 0:
            x2 = _out_proj(o_attn.reshape(n, -1), o_nsa.reshape(n, -1), x2, wl)
            x2 = _ffn(x2, ffn_w1[j].astype(BF16), ffn_w3[j].astype(BF16), ffn_w2[j].astype(BF16),
                      wl["ln2_g"], wl["ln2_b"])
        else:
            rw = jnp.pad(router_w[j], ((0, 0), (0, LANES - N_EXPERTS)))
            rw_hi = rw.astype(BF16)
            rw_lo = (rw - rw_hi.astype(F32)).astype(BF16)
            rw = jnp.concatenate([jnp.concatenate([rw_hi, rw_lo], 1),
                                  jnp.concatenate([rw_hi, jnp.zeros_like(rw_lo)], 1)], 0)
            x2, gates, pos, post, cnt = _out_proj(o_attn.reshape(n, -1), o_nsa.reshape(n, -1), x2, wl, router=rw)
            cnt = cnt[::8, :N_EXPERTS].reshape(-1)
            x2 = _moe(x2, gates, pos, post, cnt, moe_w1[j].astype(BF16), moe_w3[j].astype(BF16),
                      moe_w2[j].astype(BF16), wl["ln2_g"], wl["ln2_b"])
    return x2.reshape(b, s, d)
```

```python
import functools

import numpy as np
import jax
import jax.numpy as jnp
from jax import lax
from jax.experimental import pallas as pl
from jax.experimental.pallas import tpu as pltpu

F32 = jnp.float32
BF16 = jnp.bfloat16

D_MODEL = 1024
HEAD_DIM = 64
LANES = 128
MLA_HEADS = 6
MLA_Q_RANK = 384
MLA_KV_RANK = 256
MLA_NOPE = 64
MLA_ROPE = 32
ROPE_THETA = 10000.0
FOX_HEADS = 6
NSA_HEADS = 4
NSA_CMP_BLOCK = 32
NSA_CMP_STRIDE = 16
NSA_CMP_HIDDEN = 128
NSA_SEL_BLOCK = 64
NSA_SEL_TOPN = 8
NSA_WINDOW = 256
NSA_FORCE_SCORE = 1.0e4
N_EXPERTS = 8
NORM_EPS = 1e-5
DEPTH = 2
DEEPNORM_ALPHA = (2 * DEPTH) ** 0.25
NEG = -1e30
VMEM_LIMIT = 56 * 1024 * 1024

ROW_TILE = 512
FLASH_TQ, FLASH_TK = 2048, 512
NSA_TQ = 512

KR_LANE = 64
FOXF_LANE = 96
NSAG_LANE = 102

ZA_W = 768
FQ_OFF = ZA_W
FK_OFF = FQ_OFF + 384
FV_OFF = FK_OFF + 384
NQ_OFF = FV_OFF + 384
NC_OFF = NQ_OFF + 256
NKV_OFF = NC_OFF + 128
IN_P = NKV_OFF + 256

NSA_AUX_LANE = HEAD_DIM
NSA_SEL_LANE = HEAD_DIM + 2
NSA_MASK_BIG = 2.0 ** 126

ALIBI = tuple(2.0 ** (-8.0 * (i + 1) / NSA_HEADS) for i in range(NSA_HEADS))


def _dot(a, b, **kw):
    return jnp.dot(a, b, preferred_element_type=F32, **kw)


def _dot_nt(a, b):
    return lax.dot_general(a, b, (((1,), (1,)), ((), ())), preferred_element_type=F32)


def _iota(shape, dim):
    return lax.broadcasted_iota(jnp.int32, shape, dim)


def _split3(x):
    hi = x.astype(BF16)
    r = x - hi.astype(F32)
    mid = r.astype(BF16)
    return hi, mid, (r - mid.astype(F32)).astype(BF16)


def _dot_exact_rhs(x, w01):
    hi, mid, lo = _split3(x)
    return _dot(jnp.concatenate([hi, mid, lo], 1), jnp.concatenate([w01, w01, w01], 0))


def _dot_split(x, w_stack):
    hi = x.astype(BF16)
    lo = (x - hi.astype(F32)).astype(BF16)
    r = _dot(jnp.concatenate([hi, lo], 1), w_stack)
    half = r.shape[1] // 2
    return r[:, :half] + r[:, half:]


def _cparams(sem):
    return pltpu.CompilerParams(dimension_semantics=sem, vmem_limit_bytes=VMEM_LIMIT)


def _front_kernel(x_ref, w_ref, qaux_ref, kaux_ref, gcq_ref, wuq_ref, gckv_ref, wuk_ref, wuv_ref, bf_ref,
                  cos_ref, sina_ref, sinb_ref, pq_ref, pk_ref, oneq_ref, onek_ref,
                  small_ref, nq_ref, nc_ref, nsl_ref, nwin_ref, q_out, k_out, v_out, carry_ref,
                  *, tm, tiles_per_seq):
    @pl.when(pl.program_id(0) % tiles_per_seq == 0)
    def _():
        carry_ref[...] = jnp.zeros_like(carry_ref)

    xb = x_ref[...].astype(BF16)

    def mm(a, b):
        return _dot(xb, w_ref[:, a:b])

    lane = _iota((1, LANES), 1)

    def halves(pair):
        return (jnp.where(lane < HEAD_DIM, pair, 0.0),
                jnp.where(lane < HEAD_DIM, pltpu.roll(pair, HEAD_DIM, 1), 0.0))

    def expand(z):
        return [blk for pr in range(z.shape[1] // LANES) for blk in halves(z[:, LANES * pr:LANES * (pr + 1)])]

    def pad_v(blk):
        return jnp.where(lane < HEAD_DIM, blk, jnp.where(lane == HEAD_DIM, 1.0, 0.0)).astype(BF16)

    cos = cos_ref[...]
    sina = sina_ref[...]
    sinb = sinb_ref[...]

    def rope(blk):
        return blk * cos + pltpu.roll(blk, LANES - 16, 1) * sina + pltpu.roll(blk, 16, 1) * sinb

    za = mm(0, ZA_W)
    cq = za[:, 0:MLA_Q_RANK]
    ckv = za[:, MLA_Q_RANK:MLA_Q_RANK + MLA_KV_RANK]
    small = za[:, MLA_Q_RANK + MLA_KV_RANK:ZA_W]
    small_ref[...] = small

    xn = cq * lax.rsqrt(jnp.mean(cq * cq, -1, keepdims=True) + NORM_EPS) * gcq_ref[...]
    q = _dot(xn.astype(BF16), wuq_ref[...])
    cn = ckv * lax.rsqrt(jnp.mean(ckv * ckv, -1, keepdims=True) + NORM_EPS) * gckv_ref[...]
    cnb = cn.astype(BF16)
    kn = _dot(cnb, wuk_ref[...])
    v = _dot(cnb, wuv_ref[...])
    kr = rope(small)
    mla_scale = (MLA_NOPE + MLA_ROPE) ** -0.5
    for h in range(MLA_HEADS):
        sl = slice(LANES * h, LANES * (h + 1))
        q_out[:, sl] = (rope(q[:, sl]) * mla_scale).astype(BF16)
        k_out[:, sl] = (kn[:, sl] + kr).astype(BF16)
        v_out[:, sl] = pad_v(v[:, sl])

    fmask = (lane >= FOXF_LANE) & (lane < FOXF_LANE + FOX_HEADS)
    lf = jnp.where(fmask, jax.nn.log_sigmoid(small + bf_ref[...]), 0.0)
    tril = jnp.where(_iota((tm, tm), 0) >= _iota((tm, tm), 1), 1.0, 0.0).astype(BF16)
    lf_hi, lf_mid, lf_lo = _split3(lf)
    part = _dot(tril, jnp.concatenate([lf_hi, lf_mid], 1))
    cs = part[:, :LANES] + part[:, LANES:] + _dot(tril, lf_lo) + carry_ref[...]
    carry_ref[...] = cs[tm - 1:tm, :]
    hi, mid, lo = (t.astype(F32) for t in _split3(cs))
    c3 = (hi + pltpu.roll(mid, FOX_HEADS, 1) + pltpu.roll(lo, 2 * FOX_HEADS, 1)).astype(BF16)
    augq = _dot(c3, pq_ref[...]) + oneq_ref[...]
    augk = _dot(c3, pk_ref[...]) + onek_ref[...]
    off = MLA_HEADS * LANES
    fq = expand(mm(FQ_OFF, FK_OFF))
    fk = expand(mm(FK_OFF, FV_OFF))
    fv = expand(mm(FV_OFF, NQ_OFF))
    for h in range(FOX_HEADS):
        sl = slice(LANES * h, LANES * (h + 1))
        dst = slice(off + LANES * h, off + LANES * (h + 1))
        q_out[:, dst] = (fq[h] + augq[:, sl]).astype(BF16)
        k_out[:, dst] = (fk[h] + augk[:, sl]).astype(BF16)
        v_out[:, dst] = pad_v(fv[h])

    qaux = qaux_ref[...]
    for h, blk in enumerate(expand(mm(NQ_OFF, NC_OFF))):
        sl = slice(LANES * h, LANES * (h + 1))
        nq_ref[:, sl] = (blk + qaux[:, sl]).astype(BF16)
    nc_ref[...] = mm(NC_OFF, NKV_OFF)
    kaux = kaux_ref[...]
    nkv = mm(NKV_OFF, IN_P)
    for i, out_ref in enumerate((nsl_ref, nwin_ref)):
        k_blk, v_blk = halves(nkv[:, LANES * i:LANES * (i + 1)])
        out_ref[:, 0:LANES] = (k_blk + kaux).astype(BF16)
        out_ref[:, LANES:2 * LANES] = pad_v(v_blk)


def _front(x2, w_p, wl, tabs, s, tm=ROW_TILE):
    n = x2.shape[0]
    ns = s // tm
    nh = MLA_HEADS + FOX_HEADS
    full = lambda a: pl.BlockSpec(a.shape, lambda i: (0,) * a.ndim)
    tab = pl.BlockSpec((tm, LANES), lambda i: (i % ns, 0))
    consts = [wl["g_cq"], wl["w_uq"], wl["g_ckv"], wl["w_uk"], wl["w_uv"], wl["b_forget"]]
    tail = [tabs["pq"], tabs["pk"], tabs["oneq"], tabs["onek"]]
    widths = [(LANES, F32), (NSA_HEADS * LANES, BF16), (LANES, F32), (2 * LANES, BF16), (2 * LANES, BF16),
              (nh * LANES, BF16), (nh * LANES, BF16), (nh * LANES, BF16)]
    return pl.pallas_call(
        functools.partial(_front_kernel, tm=tm, tiles_per_seq=ns),
        grid=(n // tm,),
        in_specs=[pl.BlockSpec((tm, D_MODEL), lambda i: (i, 0)), full(w_p), full(tabs["nsa_qaux"]), tab]
                 + [full(a) for a in consts] + [tab, tab, tab] + [full(a) for a in tail],
        out_specs=[pl.BlockSpec((tm, w), lambda i: (i, 0)) for w, _ in widths],
        out_shape=[jax.ShapeDtypeStruct((n, w), dt) for w, dt in widths],
        scratch_shapes=[pltpu.VMEM((1, LANES), F32)],
        compiler_params=_cparams(("arbitrary",)),
        name="front_proj",
    )(x2, w_p, tabs["nsa_qaux"], tabs["nsa_kaux"], *consts, tabs["cos"], tabs["sina"], tabs["sinb"], *tail)


def _flash_kernel(q_ref, k_ref, v_ref, o_ref, m_ref, acc_ref, s_ref, *, tq, tk):
    qi = pl.program_id(2)
    m_ref[...] = jnp.full_like(m_ref, NEG)
    acc_ref[...] = jnp.zeros_like(acc_ref)
    nsub = tq // tk
    nfull = qi * nsub
    heads = [slice(LANES * h, LANES * (h + 1)) for h in range(2)]

    def scores(j, r0, sl):
        k0 = pl.multiple_of(j * tk, tk)
        return _dot_nt(q_ref[0, r0:tq, sl], k_ref[0, pl.ds(k0, tk), sl])

    def consume(s, h, j, r0, masked):
        k0 = pl.multiple_of(j * tk, tk)
        if masked:
            s = jnp.where(_iota((tq - r0, tk), 0) >= _iota((tq - r0, tk), 1), s, NEG)
        chunks = [s[:, LANES * c:LANES * (c + 1)] for c in range(tk // LANES)]
        m_prev = m_ref[h, r0:tq, :]
        m_new = jnp.maximum(m_prev, jnp.max(functools.reduce(jnp.maximum, chunks), -1, keepdims=True))
        p = jnp.concatenate([jnp.exp(c - m_new) for c in chunks], 1).astype(BF16)
        acc_ref[h, r0:tq, :] = (jnp.exp(m_prev - m_new) * acc_ref[h, r0:tq, :]
                                + _dot(p, v_ref[0, pl.ds(k0, tk), heads[h]]))
        m_ref[h, r0:tq, :] = m_new

    assert nsub % 2 == 0
    for h in range(2):
        s_ref[0, h] = scores(0, 0, heads[h])

    def body(i, carry):
        j = 2 * i
        for b in range(2):
            for h in range(2):
                s_ref[1 - b, h] = scores(j + b + 1, 0, heads[h])
            for h in range(2):
                consume(s_ref[b, h], h, j + b, 0, False)
        return carry

    lax.fori_loop(0, qi * (nsub // 2), body, 0)
    for h in range(2):
        consume(s_ref[0, h], h, nfull, 0, True)
    for d in range(1, nsub):
        for h in range(2):
            consume(scores(nfull + d, d * tk, heads[h]), h, nfull + d, d * tk, True)

    lane = _iota((1, LANES), 1)
    o0 = acc_ref[0]
    o1 = acc_ref[1]
    o0 = o0 / o0[:, HEAD_DIM:HEAD_DIM + 1]
    o1 = o1 / o1[:, HEAD_DIM:HEAD_DIM + 1]
    o_ref[0] = jnp.where(lane < HEAD_DIM, o0, pltpu.roll(o1, HEAD_DIM, 1)).astype(o_ref.dtype)


def _flash(q_all, k_all, v_all, tq=FLASH_TQ, tk=FLASH_TK):
    b, s, _ = q_all.shape
    npair = (MLA_HEADS + FOX_HEADS) // 2
    return pl.pallas_call(
        functools.partial(_flash_kernel, tq=tq, tk=tk),
        grid=(b, npair, s // tq),
        in_specs=[pl.BlockSpec((1, tq, 2 * LANES), lambda bi, p, qi: (bi, qi, p)),
                  pl.BlockSpec((1, s, 2 * LANES), lambda bi, p, qi: (bi, 0, p)),
                  pl.BlockSpec((1, s, 2 * LANES), lambda bi, p, qi: (bi, 0, p))],
        out_specs=pl.BlockSpec((1, tq, LANES), lambda bi, p, qi: (bi, qi, p)),
        out_shape=jax.ShapeDtypeStruct((b, s, npair * LANES), BF16),
        scratch_shapes=[pltpu.VMEM((2, tq, LANES), F32), pltpu.VMEM((2, tq, LANES), F32),
                        pltpu.VMEM((2, 2, tq, tk), F32)],
        compiler_params=_cparams(("parallel", "parallel", "arbitrary")),
        name="flash_attn",
    )(q_all, k_all, v_all)


def _cmp_kernel(nc_ref, posa_ref, posb_ref, wa_ref, wb_ref, w2_ref, out_ref, *, n_chunk):
    a = jnp.zeros((n_chunk, 2 * NSA_CMP_HIDDEN), F32)
    b = jnp.zeros((n_chunk, 2 * NSA_CMP_HIDDEN), F32)
    for l in range(NSA_CMP_STRIDE):
        t = nc_ref[0, pl.ds(l, n_chunk, stride=NSA_CMP_STRIDE), :]
        sl = slice(LANES * l, LANES * (l + 1))
        a = a + _dot((t + posa_ref[:, sl]).astype(BF16), wa_ref[sl, :])
        b = b + _dot((t + posb_ref[:, sl]).astype(BF16), wb_ref[sl, :])
    pre = a + pltpu.roll(b, n_chunk - 1, 0)
    hid = jax.nn.silu(pre)
    out_ref[0] = _dot(hid.astype(BF16), w2_ref[...]).astype(out_ref.dtype)


def _compress(nc3, wl):
    b, s, _ = nc3.shape
    n_chunk = s // NSA_CMP_STRIDE
    full = lambda a: pl.BlockSpec(a.shape, lambda bi: (0,) * a.ndim)
    consts = [wl["cmp_posa"], wl["cmp_posb"], wl["cmp_wa"], wl["cmp_wb"], wl["cmp_w2"]]
    return pl.pallas_call(
        functools.partial(_cmp_kernel, n_chunk=n_chunk),
        grid=(b,),
        in_specs=[pl.BlockSpec((1, s, LANES), lambda bi: (bi, 0, 0))] + [full(a) for a in consts],
        out_specs=pl.BlockSpec((1, n_chunk, LANES), lambda bi: (bi, 0, 0)),
        out_shape=jax.ShapeDtypeStruct((b, n_chunk, LANES), BF16),
        compiler_params=_cparams(("parallel",)),
        name="nsa_compress",
    )(nc3, *consts)


def _masked_softmax(s, mask):
    s = jnp.where(mask, s, NEG)
    m = jnp.max(s, -1, keepdims=True)
    p = jnp.where(mask, jnp.exp(s - m), 0.0)
    return p / jnp.maximum(jnp.sum(p, -1, keepdims=True), 1e-30)


def _nsa_kernel(nq_ref, kc_ref, ksl_ref, kwin_ref, g_ref, o_ref, m_ref, acc_ref, qst_ref, s_ref, *, tq, n_cmp):
    qi = pl.program_id(1)
    t0 = pl.multiple_of(qi * tq, tq)
    rpos = t0 + _iota((tq, 1), 0)
    lane = _iota((1, LANES), 1)
    qs = [nq_ref[0, :, LANES * h:LANES * (h + 1)] for h in range(NSA_HEADS)]

    dist_i = rpos - (NSA_CMP_STRIDE * lane + NSA_CMP_BLOCK - 1)
    valid_c = (dist_i >= 0) & (lane < n_cmp)
    dist_c = dist_i.astype(F32)
    kc = kc_ref[0]
    psum = jnp.zeros((tq, LANES), F32)
    o_cmp = []
    for h in range(NSA_HEADS):
        q_head = jnp.where(lane < HEAD_DIM, qs[h], jnp.zeros_like(qs[h]))
        p = _masked_softmax(_dot_nt(q_head, kc) - ALIBI[h] * dist_c, valid_c)
        psum = psum + p
        o_cmp.append(_dot(p.astype(BF16), kc))

    n_i = _iota((LANES, LANES), 0)
    j_i = _iota((LANES, LANES), 1)
    ov = ((NSA_CMP_STRIDE * n_i < NSA_SEL_BLOCK * (j_i + 1))
          & (NSA_CMP_STRIDE * n_i + NSA_CMP_BLOCK > NSA_SEL_BLOCK * j_i)
          & (n_i < n_cmp))
    imp = _dot_exact_rhs(psum, jnp.where(ov, 1.0, 0.0).astype(BF16))
    cur = jnp.right_shift(rpos, 6)
    forced = (lane == 0) | (lane == cur) | (lane == cur - 1)
    future = lane * NSA_SEL_BLOCK > rpos
    n_blk = ksl_ref.shape[1] // NSA_SEL_BLOCK
    work = jnp.where(forced, NSA_FORCE_SCORE, jnp.where(future, -1.0, imp))
    work = jnp.where(lane < n_blk, work, -jnp.inf)
    work_t = work.T[0:n_blk, :]
    blk_id = _iota((n_blk, 1), 0)
    rank = jnp.zeros((n_blk, tq), F32)
    for jp in range(n_blk):
        other = work_t[jp:jp + 1, :]
        beats = (other > work_t) | ((other == work_t) & (jp < blk_id))
        rank = rank + jnp.where(beats, 1.0, 0.0)
    sel_t = jnp.where(rank < NSA_SEL_TOPN, 1.0, 0.0)
    sel = jnp.concatenate([sel_t, jnp.zeros((LANES - n_blk, tq), F32)], 0).T > 0.5
    sel_lanes = pltpu.roll(jnp.where(sel, 1.0, 0.0), NSA_SEL_LANE, 1)
    in_sel = (lane >= NSA_SEL_LANE) & (lane < NSA_SEL_LANE + n_blk)
    sel_bias = jnp.where(in_sel, (sel_lanes - 1.0) * NSA_MASK_BIG, 0.0)
    for h in range(NSA_HEADS):
        qst_ref[h * tq:(h + 1) * tq, :] = (qs[h].astype(F32) + sel_bias).astype(BF16)
    m_ref[...] = jnp.full_like(m_ref, NEG)
    acc_ref[...] = jnp.zeros_like(acc_ref)

    def online(s, v):
        chunks = [s[:, LANES * i:LANES * (i + 1)] for i in range(s.shape[1] // LANES)]
        m_prev = m_ref[...]
        m_new = jnp.maximum(m_prev, jnp.max(functools.reduce(jnp.maximum, chunks), -1, keepdims=True))
        p = jnp.concatenate([jnp.exp(ch - m_new) for ch in chunks], 1).astype(BF16)
        acc_ref[...] = jnp.exp(m_prev - m_new) * acc_ref[...] + _dot(p, v)
        m_ref[...] = m_new

    def scores(c):
        return _dot_nt(qst_ref[...], ksl_ref[0, pl.ds(pl.multiple_of(c * tq, tq), tq), 0:LANES])

    def values(c):
        return ksl_ref[0, pl.ds(pl.multiple_of(c * tq, tq), tq), LANES:2 * LANES]

    s_ref[0] = scores(0)

    def chunk_pair(i, carry):
        j = 2 * i
        s_ref[1] = scores(j + 1)
        online(s_ref[0], values(j))
        s_ref[0] = scores(j + 2)
        online(s_ref[1], values(j + 1))
        return carry

    lax.fori_loop(0, jnp.right_shift(qi, 1), chunk_pair, 0)

    @pl.when((qi & 1) == 1)
    def _():
        online(s_ref[0], values(qi - 1))
        s_ref[0] = scores(qi)

    nrow = NSA_HEADS * tq
    causal = (_iota((nrow, tq), 0) & (tq - 1)) >= _iota((nrow, tq), 1)
    online(jnp.where(causal, s_ref[0], NEG), values(qi))

    wlen = 2 * NSA_WINDOW
    o_wins = [[] for _ in range(NSA_HEADS)]
    for part in range(tq // NSA_WINDOW):
        r0 = t0 + part * NSA_WINDOW
        w0 = pl.multiple_of(jnp.maximum(r0 - NSA_WINDOW, 0), NSA_WINDOW)
        k_w = kwin_ref[0, pl.ds(w0, wlen), 0:LANES]
        v_w = kwin_ref[0, pl.ds(w0, wlen), LANES:2 * LANES]
        d_w = (r0 - w0) + _iota((NSA_WINDOW, wlen), 0) - _iota((NSA_WINDOW, wlen), 1)
        keep_w = (d_w >= 0) & (d_w < NSA_WINDOW)
        for h in range(NSA_HEADS):
            q_part = qs[h][part * NSA_WINDOW:(part + 1) * NSA_WINDOW, :]
            s_w = jnp.where(keep_w, _dot_nt(q_part, k_w), NEG)
            p_w = jnp.exp(s_w - jnp.max(s_w, -1, keepdims=True))
            o_wins[h].append(_dot(p_w.astype(BF16), v_w))

    gates = jax.nn.sigmoid(g_ref[...])
    for h in range(NSA_HEADS):
        o_win = jnp.concatenate(o_wins[h], 0)
        o_win = o_win / o_win[:, HEAD_DIM:HEAD_DIM + 1]
        o_sel = acc_ref[h * tq:(h + 1) * tq, :]
        o_sel = o_sel / o_sel[:, HEAD_DIM:HEAD_DIM + 1]
        g0 = NSAG_LANE + 3 * h
        o = (gates[:, g0:g0 + 1] * pltpu.roll(o_cmp[h], HEAD_DIM, 1) + gates[:, g0 + 1:g0 + 2] * o_sel
             + gates[:, g0 + 2:g0 + 3] * o_win)
        o_ref[0, :, LANES * h:LANES * (h + 1)] = jnp.where(lane < HEAD_DIM, o, 0.0).astype(o_ref.dtype)


def _nsa(nq, kcvc, nsl, nwin, small, tq=NSA_TQ):
    b, s, _ = nq.shape
    nq_t = s // tq
    n_cmp = (s - NSA_CMP_BLOCK) // NSA_CMP_STRIDE + 1
    return pl.pallas_call(
        functools.partial(_nsa_kernel, tq=tq, n_cmp=n_cmp),
        grid=(b, nq_t),
        in_specs=[pl.BlockSpec((1, tq, NSA_HEADS * LANES), lambda bi, qi: (bi, qi, 0)),
                  pl.BlockSpec((1, kcvc.shape[1], LANES), lambda bi, qi: (bi, 0, 0)),
                  pl.BlockSpec((1, s, 2 * LANES), lambda bi, qi: (bi, 0, 0)),
                  pl.BlockSpec((1, s, 2 * LANES), lambda bi, qi: (bi, 0, 0)),
                  pl.BlockSpec((tq, LANES), lambda bi, qi: (bi * nq_t + qi, 0))],
        out_specs=pl.BlockSpec((1, tq, NSA_HEADS * LANES), lambda bi, qi: (bi, qi, 0)),
        out_shape=jax.ShapeDtypeStruct((b, s, NSA_HEADS * LANES), BF16),
        scratch_shapes=[pltpu.VMEM((NSA_HEADS * tq, LANES), F32), pltpu.VMEM((NSA_HEADS * tq, LANES), F32),
                        pltpu.VMEM((NSA_HEADS * tq, LANES), BF16), pltpu.VMEM((2, NSA_HEADS * tq, tq), F32)],
        compiler_params=_cparams(("parallel", "arbitrary")),
        name="nsa_attn",
    )(nq, kcvc, nsl, nwin, small)


def _layer_norm(y, g, b):
    mu = jnp.mean(y, -1, keepdims=True)
    yc = y - mu
    var = jnp.mean(yc * yc, -1, keepdims=True)
    return yc * lax.rsqrt(var + NORM_EPS) * g + b


ROUTE_ROWS = ROW_TILE


def _out_proj_kernel(*refs, routed):
    if routed:
        (oa_ref, on_ref, x_ref, wa_ref, wb_ref, g_ref, b_ref, rw_ref,
         o_ref, gate_ref, pos_ref, post_ref, cnt_ref) = refs
    else:
        oa_ref, on_ref, x_ref, wa_ref, wb_ref, g_ref, b_ref, o_ref = refs
    mix = _dot(oa_ref[...], wa_ref[...]) + _dot(on_ref[...], wb_ref[...])
    x1 = _layer_norm(DEEPNORM_ALPHA * x_ref[...] + mix, g_ref[...], b_ref[...])
    o_ref[...] = x1
    if routed:
        tm = x1.shape[0]
        lane = _iota((1, LANES), 1)
        lane_f = lane.astype(F32)
        logits = jnp.where(lane < N_EXPERTS, _dot_split(x1, rw_ref[...]), NEG)
        ex = jnp.exp(logits - jnp.max(logits, -1, keepdims=True))
        probs = ex / jnp.sum(ex, -1, keepdims=True)
        p1 = jnp.max(probs, -1, keepdims=True)
        i1 = jnp.min(jnp.where(probs == p1, lane_f, float(LANES)), -1, keepdims=True)
        rest = jnp.where(lane_f == i1, -1.0, probs)
        p2 = jnp.max(rest, -1, keepdims=True)
        i2 = jnp.min(jnp.where(rest == p2, lane_f, float(LANES)), -1, keepdims=True)
        tot = p1 + p2
        gate_ref[...] = jnp.where(lane_f == i1, p1 / tot, jnp.where(lane_f == i2, p2 / tot, 0.0))
        chosen = (lane_f == i1) | (lane_f == i2)
        before = (_iota((tm, tm), 0) > _iota((tm, tm), 1)).astype(BF16)
        onehot = jnp.where(chosen, 1.0, 0.0)
        slot = _dot(before, onehot.astype(BF16))
        posm = jnp.where(chosen, slot, -1.0)
        pos_ref[...] = posm
        post_ref[...] = posm.T[0:N_EXPERTS, :]
        cnt_ref[...] = jnp.broadcast_to(jnp.sum(onehot, 0, keepdims=True), (8, LANES)).astype(jnp.int32)


def _out_proj(o_attn, o_nsa, x2, wl, router=None, tm=ROUTE_ROWS):
    n = x2.shape[0]
    routed = router is not None
    full = lambda a: pl.BlockSpec(a.shape, lambda i: (0,) * a.ndim)
    row = lambda w: pl.BlockSpec((tm, w), lambda i: (i, 0))
    consts = [wl["w_out_a"], wl["w_out_b"], wl["ln1_g"], wl["ln1_b"]] + ([router] if routed else [])
    out_specs = [row(D_MODEL)]
    out_shape = [jax.ShapeDtypeStruct((n, D_MODEL), F32)]
    if routed:
        out_specs += [row(LANES), row(LANES), pl.BlockSpec((N_EXPERTS, tm), lambda i: (i, 0)),
                      pl.BlockSpec((8, LANES), lambda i: (i, 0))]
        out_shape += [jax.ShapeDtypeStruct((n, LANES), F32), jax.ShapeDtypeStruct((n, LANES), F32),
                      jax.ShapeDtypeStruct((n // tm * N_EXPERTS, tm), F32),
                      jax.ShapeDtypeStruct((n // tm * 8, LANES), jnp.int32)]
    outs = pl.pallas_call(
        functools.partial(_out_proj_kernel, routed=routed),
        grid=(n // tm,),
        in_specs=[row(o_attn.shape[1]), row(o_nsa.shape[1]), row(D_MODEL)] + [full(a) for a in consts],
        out_specs=out_specs,
        out_shape=out_shape,
        compiler_params=_cparams(("parallel",)),
        name="out_proj_route_ln" if routed else "out_proj_ln",
    )(o_attn, o_nsa, x2, *consts)
    return outs if routed else outs[0]


def _ffn_kernel(x_ref, w1_ref, w3_ref, w2_ref, g_ref, b_ref, o_ref, acc_ref, xb_ref):
    c = pl.program_id(1)

    @pl.when(c == 0)
    def _():
        acc_ref[...] = jnp.zeros_like(acc_ref)
        xb_ref[...] = x_ref[...].astype(BF16)

    xb = xb_ref[...]
    a = jax.nn.silu(_dot(xb, w1_ref[...])) * _dot(xb, w3_ref[...])
    acc_ref[...] += _dot(a.astype(BF16), w2_ref[...])

    @pl.when(c == pl.num_programs(1) - 1)
    def _():
        o_ref[...] = _layer_norm(DEEPNORM_ALPHA * x_ref[...] + acc_ref[...], g_ref[...], b_ref[...])


def _ffn(x2, w1, w3, w2, ln_g, ln_b, tm=2 * ROW_TILE, nchunk=2):
    n = x2.shape[0]
    tf = w1.shape[1] // nchunk
    full = lambda a: pl.BlockSpec(a.shape, lambda i, c: (0,) * a.ndim)
    return pl.pallas_call(
        _ffn_kernel,
        grid=(n // tm, nchunk),
        in_specs=[pl.BlockSpec((tm, D_MODEL), lambda i, c: (i, 0)),
                  pl.BlockSpec((D_MODEL, tf), lambda i, c: (0, c)),
                  pl.BlockSpec((D_MODEL, tf), lambda i, c: (0, c)),
                  pl.BlockSpec((tf, D_MODEL), lambda i, c: (c, 0)),
                  full(ln_g), full(ln_b)],
        out_specs=pl.BlockSpec((tm, D_MODEL), lambda i, c: (i, 0)),
        out_shape=jax.ShapeDtypeStruct((n, D_MODEL), F32),
        scratch_shapes=[pltpu.VMEM((tm, D_MODEL), F32), pltpu.VMEM((tm, D_MODEL), BF16)],
        compiler_params=_cparams(("parallel", "arbitrary")),
        name="ffn_ln",
    )(x2, w1, w3, w2, ln_g, ln_b)


MOE_CAP = 160
MOE_CAP_PAD = -(-MOE_CAP // LANES) * LANES
MOE_CHUNKS = -(-ROUTE_ROWS // MOE_CAP)


def _moe_kernel(cnt_ref, x_ref, gate_ref, pos_ref, *rest, groups):
    post_refs = rest[:groups]
    w1_ref, w3_ref, w2_ref, g_ref, b_ref, o_ref, acc_ref, xb_ref = rest[groups:]
    i = pl.program_id(0)
    e = pl.program_id(1)
    lane = _iota((1, LANES), 1)

    @pl.when(e == 0)
    def _():
        acc_ref[...] = jnp.zeros_like(acc_ref)
        xb_ref[...] = x_ref[...].astype(BF16)

    for gi in range(groups):
        rows = slice(ROUTE_ROWS * gi, ROUTE_ROWS * (gi + 1))
        cnt = cnt_ref[(i * groups + gi) * N_EXPERTS + e]
        gate_e = jnp.sum(jnp.where(lane == e, gate_ref[rows, :], 0.0), -1, keepdims=True)
        slot_col = jnp.sum(jnp.where(lane == e, pos_ref[rows, :], 0.0), -1, keepdims=True)
        slot_row = post_refs[gi][0]
        for k in range(MOE_CHUNKS):
            @pl.when(cnt > k * MOE_CAP)
            def _():
                want = (_iota((MOE_CAP, 1), 0) + k * MOE_CAP).astype(F32)
                pick = jnp.where(slot_row == want, 1.0, 0.0).astype(BF16)
                xg = _dot(pick, xb_ref[rows, :]).astype(BF16)
                a = jax.nn.silu(_dot(xg, w1_ref[0])) * _dot(xg, w3_ref[0])
                y = _dot(a.astype(BF16), w2_ref[0])
                col = _iota((1, MOE_CAP_PAD), 1)
                put = jnp.where((slot_col == (col + k * MOE_CAP).astype(F32)) & (col < MOE_CAP),
                                1.0, 0.0).astype(BF16)
                yb = y.astype(BF16)
                if MOE_CAP_PAD > MOE_CAP:
                    yb = jnp.concatenate([yb, jnp.zeros((MOE_CAP_PAD - MOE_CAP, D_MODEL), BF16)], 0)
                acc_ref[rows, :] += gate_e * _dot(put, yb)

    @pl.when(e == pl.num_programs(1) - 1)
    def _():
        o_ref[...] = _layer_norm(DEEPNORM_ALPHA * x_ref[...] + acc_ref[...], g_ref[...], b_ref[...])


def _moe(x2, gates, pos, post, cnt, w1, w3, w2, ln_g, ln_b, groups=2):
    n = x2.shape[0]
    ne, _, tf = w1.shape
    tm = groups * ROUTE_ROWS
    post3 = post.reshape(n // ROUTE_ROWS * N_EXPERTS, 1, ROUTE_ROWS)
    full = lambda a: pl.BlockSpec(a.shape, lambda i, e, c: (0,) * a.ndim)
    row = lambda w: pl.BlockSpec((tm, w), lambda i, e, c: (i, 0))
    post_spec = lambda gi: pl.BlockSpec((1, 1, ROUTE_ROWS),
                                        lambda i, e, c: ((i * groups + gi) * N_EXPERTS + e, 0, 0))
    grid_spec = pltpu.PrefetchScalarGridSpec(
        num_scalar_prefetch=1,
        grid=(n // tm, ne),
        in_specs=[row(D_MODEL), row(LANES), row(LANES)] + [post_spec(gi) for gi in range(groups)]
                 + [pl.BlockSpec((1, D_MODEL, tf), lambda i, e, c: (e, 0, 0)),
                    pl.BlockSpec((1, D_MODEL, tf), lambda i, e, c: (e, 0, 0)),
                    pl.BlockSpec((1, tf, D_MODEL), lambda i, e, c: (e, 0, 0)),
                    full(ln_g), full(ln_b)],
        out_specs=row(D_MODEL),
        scratch_shapes=[pltpu.VMEM((tm, D_MODEL), F32), pltpu.VMEM((tm, D_MODEL), BF16)],
    )
    return pl.pallas_call(
        functools.partial(_moe_kernel, groups=groups),
        grid_spec=grid_spec,
        out_shape=jax.ShapeDtypeStruct((n, D_MODEL), F32),
        compiler_params=_cparams(("parallel", "arbitrary")),
        name="moe_top2_ln",
    )(cnt, x2, gates, pos, *([post3] * groups), w1, w3, w2, ln_g, ln_b)


def _in_proj_columns():
    src = np.full((IN_P,), -1, np.int64)
    scale = np.ones((IN_P,), np.float32)
    o_cq, o_ckv, o_kr = 0, MLA_Q_RANK, MLA_Q_RANK + MLA_KV_RANK
    o_fox = o_kr + MLA_ROPE
    o_foxf = o_fox + 3 * FOX_HEADS * HEAD_DIM
    o_nq = o_foxf + FOX_HEADS
    o_nkv = o_nq + NSA_HEADS * HEAD_DIM
    o_ng = o_nkv + 6 * HEAD_DIM
    src[0:o_kr] = np.arange(o_kr)
    small = o_kr
    src[small + KR_LANE:small + KR_LANE + MLA_ROPE] = o_kr + np.arange(MLA_ROPE)
    src[small + FOXF_LANE:small + FOXF_LANE + FOX_HEADS] = o_foxf + np.arange(FOX_HEADS)
    src[small + NSAG_LANE:small + NSAG_LANE + 3 * NSA_HEADS] = o_ng + np.arange(3 * NSA_HEADS)
    n_fox = FOX_HEADS * HEAD_DIM
    src[FQ_OFF:FQ_OFF + 3 * n_fox] = o_fox + np.arange(3 * n_fox)
    scale[FQ_OFF:FQ_OFF + n_fox] = HEAD_DIM ** -0.5
    src[NQ_OFF:NQ_OFF + NSA_HEADS * HEAD_DIM] = o_nq + np.arange(NSA_HEADS * HEAD_DIM)
    scale[NQ_OFF:NQ_OFF + NSA_HEADS * HEAD_DIM] = HEAD_DIM ** -0.5
    src[NC_OFF:NC_OFF + 6 * HEAD_DIM] = o_nkv + np.arange(6 * HEAD_DIM)
    return src, scale


def _gather_cols(w, src, scale=None):
    scale = np.ones(src.shape, np.float32) if scale is None else scale
    parts, i, n = [], 0, len(src)
    while i < n:
        j = i + 1
        while j < n and scale[j] == scale[i] and (src[j] == src[j - 1] + 1 if src[i] >= 0 else src[j] < 0):
            j += 1
        if src[i] < 0:
            parts.append(jnp.zeros(w.shape[:-1] + (j - i,), w.dtype))
        else:
            seg = w[..., int(src[i]):int(src[i]) + (j - i)]
            parts.append(seg if scale[i] == 1.0 else seg * float(scale[i]))
        i = j
    return jnp.concatenate(parts, axis=-1)


def _tables(s):
    half = MLA_ROPE // 2
    freqs = ROPE_THETA ** (-jnp.arange(half, dtype=F32) / half)
    ang = jnp.arange(s).astype(F32)[:, None] * freqs[None, :]
    cos, sin = jnp.cos(ang), jnp.sin(ang)
    z = lambda w: jnp.zeros((s, w), F32)
    tabs = {
        "cos": jnp.concatenate([jnp.ones((s, MLA_NOPE), F32), cos, cos, z(LANES - MLA_NOPE - MLA_ROPE)], 1),
        "sina": jnp.concatenate([z(MLA_NOPE), -sin, z(half), z(LANES - MLA_NOPE - MLA_ROPE)], 1),
        "sinb": jnp.concatenate([z(MLA_NOPE), z(half), sin, z(LANES - MLA_NOPE - MLA_ROPE)], 1),
    }
    pq = np.zeros((LANES, FOX_HEADS * LANES), np.float32)
    pk = np.zeros((LANES, FOX_HEADS * LANES), np.float32)
    oneq = np.zeros((1, FOX_HEADS * LANES), np.float32)
    onek = np.zeros((1, FOX_HEADS * LANES), np.float32)
    for h in range(FOX_HEADS):
        for t in range(3):
            pq[FOXF_LANE + FOX_HEADS * t + h, LANES * h + HEAD_DIM + t] = 1.0
            pk[FOXF_LANE + FOX_HEADS * t + h, LANES * h + HEAD_DIM + 3 + t] = -1.0
            oneq[0, LANES * h + HEAD_DIM + 3 + t] = 1.0
            onek[0, LANES * h + HEAD_DIM + t] = 1.0
    tabs.update(pq=jnp.asarray(pq, BF16), pk=jnp.asarray(pk, BF16), oneq=jnp.asarray(oneq), onek=jnp.asarray(onek))
    qaux = np.zeros((1, NSA_HEADS * LANES), np.float32)
    for h in range(NSA_HEADS):
        qaux[0, LANES * h + NSA_AUX_LANE] = ALIBI[h] * NSA_SEL_BLOCK
        qaux[0, LANES * h + NSA_AUX_LANE + 1] = ALIBI[h]
    kaux = np.zeros((s, LANES), np.float32)
    pos = np.arange(s)
    kaux[:, NSA_AUX_LANE] = pos // NSA_SEL_BLOCK
    kaux[:, NSA_AUX_LANE + 1] = pos % NSA_SEL_BLOCK
    kaux[pos, NSA_SEL_LANE + pos // NSA_SEL_BLOCK] = 1.0
    tabs.update(nsa_qaux=jnp.asarray(qaux), nsa_kaux=jnp.asarray(kaux))
    return tabs


def _layer_weights(p, l):
    qd = MLA_NOPE + MLA_ROPE
    src_q = np.full((MLA_HEADS * LANES,), -1, np.int64)
    src_k = np.full((MLA_HEADS * LANES,), -1, np.int64)
    src_v = np.full((MLA_HEADS * LANES,), -1, np.int64)
    for h in range(MLA_HEADS):
        src_q[LANES * h + np.arange(qd)] = qd * h + np.arange(qd)
        src_k[LANES * h + np.arange(MLA_NOPE)] = 2 * HEAD_DIM * h + np.arange(MLA_NOPE)
        src_v[LANES * h + np.arange(HEAD_DIM)] = 2 * HEAD_DIM * h + MLA_NOPE + np.arange(HEAD_DIM)
    w_out = p["w_out"][l]
    n_attn = (MLA_HEADS + FOX_HEADS) * HEAD_DIM
    wb = w_out[n_attn:].reshape(NSA_HEADS, HEAD_DIM, D_MODEL)
    wb = jnp.concatenate([wb, jnp.zeros_like(wb)], axis=1).reshape(NSA_HEADS * LANES, D_MODEL)
    bf = jnp.zeros((1, LANES), F32).at[0, FOXF_LANE:FOXF_LANE + FOX_HEADS].set(p["b_forget"][l])

    half = NSA_CMP_BLOCK // 2
    kpos, vpos = p["cmp_k_pos"][l], p["cmp_v_pos"][l]
    posa = jnp.concatenate([kpos[:half], vpos[:half]], -1).reshape(1, half * LANES)
    posb = jnp.concatenate([kpos[half:], vpos[half:]], -1).reshape(1, half * LANES)

    def w1_half(lo):
        k = p["cmp_k_w1"][l].reshape(NSA_CMP_BLOCK, HEAD_DIM, NSA_CMP_HIDDEN)[lo:lo + half]
        v = p["cmp_v_w1"][l].reshape(NSA_CMP_BLOCK, HEAD_DIM, NSA_CMP_HIDDEN)[lo:lo + half]
        zk = jnp.zeros_like(k)
        top = jnp.concatenate([k, zk], -1)
        bot = jnp.concatenate([zk, v], -1)
        return jnp.concatenate([top, bot], 1).reshape(half * LANES, 2 * NSA_CMP_HIDDEN).astype(BF16)

    zw2 = jnp.zeros((NSA_CMP_HIDDEN, HEAD_DIM), F32)
    w2 = jnp.concatenate([jnp.concatenate([p["cmp_k_w2"][l], zw2], 1),
                          jnp.concatenate([zw2, p["cmp_v_w2"][l]], 1)], 0).astype(BF16)
    return {
        "g_cq": p["g_cq"][l][None, :], "g_ckv": p["g_ckv"][l][None, :],
        "w_uq": _gather_cols(p["w_uq"][l], src_q).astype(BF16),
        "w_uk": _gather_cols(p["w_ukv"][l], src_k).astype(BF16),
        "w_uv": _gather_cols(p["w_ukv"][l], src_v).astype(BF16),
        "b_forget": bf,
        "cmp_posa": posa, "cmp_posb": posb, "cmp_wa": w1_half(0), "cmp_wb": w1_half(half), "cmp_w2": w2,
        "w_out_a": w_out[:n_attn].astype(BF16), "w_out_b": wb.astype(BF16),
        "ln1_g": p["ln1_g"][l][None, :], "ln1_b": p["ln1_b"][l][None, :],
        "ln2_g": p["ln2_g"][l][None, :], "ln2_b": p["ln2_b"][l][None, :],
    }


def kernel(x, w_in, b_forget, g_cq, w_uq, g_ckv, w_ukv, cmp_k_pos, cmp_k_w1, cmp_k_w2, cmp_v_pos, cmp_v_w1,
           cmp_v_w2, w_out, ln1_g, ln1_b, ln2_g, ln2_b, ffn_w1, ffn_w3, ffn_w2, router_w, moe_w1, moe_w3,
           moe_w2):
    b, s, d = x.shape
    assert d == D_MODEL and s % 512 == 0 and s // NSA_CMP_STRIDE == LANES, (b, s, d)
    p = dict(b_forget=b_forget, g_cq=g_cq, w_uq=w_uq, g_ckv=g_ckv, w_ukv=w_ukv, cmp_k_pos=cmp_k_pos,
             cmp_k_w1=cmp_k_w1, cmp_k_w2=cmp_k_w2, cmp_v_pos=cmp_v_pos, cmp_v_w1=cmp_v_w1, cmp_v_w2=cmp_v_w2,
             w_out=w_out, ln1_g=ln1_g, ln1_b=ln1_b, ln2_g=ln2_g, ln2_b=ln2_b)
    src, scale = _in_proj_columns()
    w_in_p = _gather_cols(w_in, src, scale).astype(BF16)
    tabs = _tables(s)
    n = b * s
    x2 = x.reshape(n, d)
    for l in range(DEPTH):
        wl = _layer_weights(p, l)
        small, nq, nc, nsl, nwin, q_all, k_all, v_all = _front(x2, w_in_p[l], wl, tabs, s)
        o_attn = _flash(*(t.reshape(b, s, -1) for t in (q_all, k_all, v_all)))
        kcvc = _compress(nc.reshape(b, s, LANES), wl)
        o_nsa = _nsa(nq.reshape(b, s, NSA_HEADS * LANES), kcvc, nsl.reshape(b, s, 2 * LANES),
                     nwin.reshape(b, s, 2 * LANES), small)
        j = l // 2
        if l % 2 == 0:
            x2 = _out_proj(o_attn.reshape(n, -1), o_nsa.reshape(n, -1), x2, wl)
            x2 = _ffn(x2, ffn_w1[j].astype(BF16), ffn_w3[j].astype(BF16), ffn_w2[j].astype(BF16),
                      wl["ln2_g"], wl["ln2_b"])
        else:
            rw = jnp.pad(router_w[j], ((0, 0), (0, LANES - N_EXPERTS)))
            rw_hi = rw.astype(BF16)
            rw_lo = (rw - rw_hi.astype(F32)).astype(BF16)
            rw = jnp.concatenate([jnp.concatenate([rw_hi, rw_lo], 1),
                                  jnp.concatenate([rw_hi, jnp.zeros_like(rw_lo)], 1)], 0)
            x2, gates, pos, post, cnt = _out_proj(o_attn.reshape(n, -1), o_nsa.reshape(n, -1), x2, wl, router=rw)
            cnt = cnt[::8, :N_EXPERTS].reshape(-1)
            x2 = _moe(x2, gates, pos, post, cnt, moe_w1[j].astype(BF16), moe_w3[j].astype(BF16),
                      moe_w2[j].astype(BF16), wl["ln2_g"], wl["ln2_b"])
    return x2.reshape(b, s, d)
```

```python
import functools

import numpy as np
import jax
import jax.numpy as jnp
from jax import lax
from jax.experimental import pallas as pl
from jax.experimental.pallas import tpu as pltpu

F32 = jnp.float32
BF16 = jnp.bfloat16

D_MODEL = 1024
HEAD_DIM = 64
LANES = 128
MLA_HEADS = 6
MLA_Q_RANK = 384
MLA_KV_RANK = 256
MLA_NOPE = 64
MLA_ROPE = 32
ROPE_THETA = 10000.0
FOX_HEADS = 6
NSA_HEADS = 4
NSA_CMP_BLOCK = 32
NSA_CMP_STRIDE = 16
NSA_CMP_HIDDEN = 128
NSA_SEL_BLOCK = 64
NSA_SEL_TOPN = 8
NSA_WINDOW = 256
NSA_FORCE_SCORE = 1.0e4
N_EXPERTS = 8
NORM_EPS = 1e-5
DEPTH = 2
DEEPNORM_ALPHA = (2 * DEPTH) ** 0.25
NEG = -1e30
VMEM_LIMIT = 56 * 1024 * 1024

ROW_TILE = 512
FLASH_TQ, FLASH_TK = 2048, 512
NSA_TQ = 512

KR_LANE = 64
FOXF_LANE = 96
NSAG_LANE = 102

ZA_W = 768
FQ_OFF = ZA_W
FK_OFF = FQ_OFF + 384
FV_OFF = FK_OFF + 384
NQ_OFF = FV_OFF + 384
NC_OFF = NQ_OFF + 256
NKV_OFF = NC_OFF + 128
IN_P = NKV_OFF + 256

NSA_AUX_LANE = HEAD_DIM
NSA_SEL_LANE = HEAD_DIM + 2
NSA_MASK_BIG = 2.0 ** 126

ALIBI = tuple(2.0 ** (-8.0 * (i + 1) / NSA_HEADS) for i in range(NSA_HEADS))


def _dot(a, b, **kw):
    return jnp.dot(a, b, preferred_element_type=F32, **kw)


def _dot_nt(a, b):
    return lax.dot_general(a, b, (((1,), (1,)), ((), ())), preferred_element_type=F32)


def _iota(shape, dim):
    return lax.broadcasted_iota(jnp.int32, shape, dim)


def _split3(x):
    hi = x.astype(BF16)
    r = x - hi.astype(F32)
    mid = r.astype(BF16)
    return hi, mid, (r - mid.astype(F32)).astype(BF16)


def _dot_exact_rhs(x, w01):
    hi, mid, lo = _split3(x)
    return _dot(jnp.concatenate([hi, mid, lo], 1), jnp.concatenate([w01, w01, w01], 0))


def _dot_split(x, w_stack):
    hi = x.astype(BF16)
    lo = (x - hi.astype(F32)).astype(BF16)
    r = _dot(jnp.concatenate([hi, lo], 1), w_stack)
    half = r.shape[1] // 2
    return r[:, :half] + r[:, half:]


def _cparams(sem):
    return pltpu.CompilerParams(dimension_semantics=sem, vmem_limit_bytes=VMEM_LIMIT)


def _front_kernel(x_ref, w_ref, qaux_ref, kaux_ref, gcq_ref, wuq_ref, gckv_ref, wuk_ref, wuv_ref, bf_ref,
                  cos_ref, sina_ref, sinb_ref, pq_ref, pk_ref, oneq_ref, onek_ref,
                  small_ref, nq_ref, nc_ref, nsl_ref, nwin_ref, q_out, k_out, v_out, carry_ref,
                  *, tm, tiles_per_seq):
    @pl.when(pl.program_id(0) % tiles_per_seq == 0)
    def _():
        carry_ref[...] = jnp.zeros_like(carry_ref)

    xb = x_ref[...].astype(BF16)

    def mm(a, b):
        return _dot(xb, w_ref[:, a:b])

    lane = _iota((1, LANES), 1)

    def halves(pair):
        return (jnp.where(lane < HEAD_DIM, pair, 0.0),
                jnp.where(lane < HEAD_DIM, pltpu.roll(pair, HEAD_DIM, 1), 0.0))

    def expand(z):
        return [blk for pr in range(z.shape[1] // LANES) for blk in halves(z[:, LANES * pr:LANES * (pr + 1)])]

    def pad_v(blk):
        return jnp.where(lane < HEAD_DIM, blk, jnp.where(lane == HEAD_DIM, 1.0, 0.0)).astype(BF16)

    cos = cos_ref[...]
    sina = sina_ref[...]
    sinb = sinb_ref[...]

    def rope(blk):
        return blk * cos + pltpu.roll(blk, LANES - 16, 1) * sina + pltpu.roll(blk, 16, 1) * sinb

    za = mm(0, ZA_W)
    cq = za[:, 0:MLA_Q_RANK]
    ckv = za[:, MLA_Q_RANK:MLA_Q_RANK + MLA_KV_RANK]
    small = za[:, MLA_Q_RANK + MLA_KV_RANK:ZA_W]
    small_ref[...] = small

    xn = cq * lax.rsqrt(jnp.mean(cq * cq, -1, keepdims=True) + NORM_EPS) * gcq_ref[...]
    q = _dot(xn.astype(BF16), wuq_ref[...])
    cn = ckv * lax.rsqrt(jnp.mean(ckv * ckv, -1, keepdims=True) + NORM_EPS) * gckv_ref[...]
    cnb = cn.astype(BF16)
    kn = _dot(cnb, wuk_ref[...])
    v = _dot(cnb, wuv_ref[...])
    kr = rope(small)
    mla_scale = (MLA_NOPE + MLA_ROPE) ** -0.5
    for h in range(MLA_HEADS):
        sl = slice(LANES * h, LANES * (h + 1))
        q_out[:, sl] = (rope(q[:, sl]) * mla_scale).astype(BF16)
        k_out[:, sl] = (kn[:, sl] + kr).astype(BF16)
        v_out[:, sl] = pad_v(v[:, sl])

    fmask = (lane >= FOXF_LANE) & (lane < FOXF_LANE + FOX_HEADS)
    lf = jnp.where(fmask, jax.nn.log_sigmoid(small + bf_ref[...]), 0.0)
    tril = jnp.where(_iota((tm, tm), 0) >= _iota((tm, tm), 1), 1.0, 0.0).astype(BF16)
    lf_hi, lf_mid, lf_lo = _split3(lf)
    part = _dot(tril, jnp.concatenate([lf_hi, lf_mid], 1))
    cs = part[:, :LANES] + part[:, LANES:] + _dot(tril, lf_lo) + carry_ref[...]
    carry_ref[...] = cs[tm - 1:tm, :]
    hi, mid, lo = (t.astype(F32) for t in _split3(cs))
    c3 = (hi + pltpu.roll(mid, FOX_HEADS, 1) + pltpu.roll(lo, 2 * FOX_HEADS, 1)).astype(BF16)
    augq = _dot(c3, pq_ref[...]) + oneq_ref[...]
    augk = _dot(c3, pk_ref[...]) + onek_ref[...]
    off = MLA_HEADS * LANES
    fq = expand(mm(FQ_OFF, FK_OFF))
    fk = expand(mm(FK_OFF, FV_OFF))
    fv = expand(mm(FV_OFF, NQ_OFF))
    for h in range(FOX_HEADS):
        sl = slice(LANES * h, LANES * (h + 1))
        dst = slice(off + LANES * h, off + LANES * (h + 1))
        q_out[:, dst] = (fq[h] + augq[:, sl]).astype(BF16)
        k_out[:, dst] = (fk[h] + augk[:, sl]).astype(BF16)
        v_out[:, dst] = pad_v(fv[h])

    qaux = qaux_ref[...]
    for h, blk in enumerate(expand(mm(NQ_OFF, NC_OFF))):
        sl = slice(LANES * h, LANES * (h + 1))
        nq_ref[:, sl] = (blk + qaux[:, sl]).astype(BF16)
    nc_ref[...] = mm(NC_OFF, NKV_OFF)
    kaux = kaux_ref[...]
    nkv = mm(NKV_OFF, IN_P)
    for i, out_ref in enumerate((nsl_ref, nwin_ref)):
        k_blk, v_blk = halves(nkv[:, LANES * i:LANES * (i + 1)])
        out_ref[:, 0:LANES] = (k_blk + kaux).astype(BF16)
        out_ref[:, LANES:2 * LANES] = pad_v(v_blk)


def _front(x2, w_p, wl, tabs, s, tm=2 * ROW_TILE):
    n = x2.shape[0]
    ns = s // tm
    nh = MLA_HEADS + FOX_HEADS
    full = lambda a: pl.BlockSpec(a.shape, lambda i: (0,) * a.ndim)
    tab = pl.BlockSpec((tm, LANES), lambda i: (i % ns, 0))
    consts = [wl["g_cq"], wl["w_uq"], wl["g_ckv"], wl["w_uk"], wl["w_uv"], wl["b_forget"]]
    tail = [tabs["pq"], tabs["pk"], tabs["oneq"], tabs["onek"]]
    widths = [(LANES, F32), (NSA_HEADS * LANES, BF16), (LANES, F32), (2 * LANES, BF16), (2 * LANES, BF16),
              (nh * LANES, BF16), (nh * LANES, BF16), (nh * LANES, BF16)]
    return pl.pallas_call(
        functools.partial(_front_kernel, tm=tm, tiles_per_seq=ns),
        grid=(n // tm,),
        in_specs=[pl.BlockSpec((tm, D_MODEL), lambda i: (i, 0)), full(w_p), full(tabs["nsa_qaux"]), tab]
                 + [full(a) for a in consts] + [tab, tab, tab] + [full(a) for a in tail],
        out_specs=[pl.BlockSpec((tm, w), lambda i: (i, 0)) for w, _ in widths],
        out_shape=[jax.ShapeDtypeStruct((n, w), dt) for w, dt in widths],
        scratch_shapes=[pltpu.VMEM((1, LANES), F32)],
        compiler_params=_cparams(("arbitrary",)),
        name="front_proj",
    )(x2, w_p, tabs["nsa_qaux"], tabs["nsa_kaux"], *consts, tabs["cos"], tabs["sina"], tabs["sinb"], *tail)


def _flash_kernel(q_ref, k_ref, v_ref, o_ref, m_ref, acc_ref, s_ref, *, tq, tk):
    qi = pl.program_id(2)
    m_ref[...] = jnp.full_like(m_ref, NEG)
    acc_ref[...] = jnp.zeros_like(acc_ref)
    nsub = tq // tk
    nfull = qi * nsub
    heads = [slice(LANES * h, LANES * (h + 1)) for h in range(2)]

    def scores(j, r0, sl):
        k0 = pl.multiple_of(j * tk, tk)
        return _dot_nt(q_ref[0, r0:tq, sl], k_ref[0, pl.ds(k0, tk), sl])

    def consume(s, h, j, r0, masked):
        k0 = pl.multiple_of(j * tk, tk)
        if masked:
            s = jnp.where(_iota((tq - r0, tk), 0) >= _iota((tq - r0, tk), 1), s, NEG)
        chunks = [s[:, LANES * c:LANES * (c + 1)] for c in range(tk // LANES)]
        m_prev = m_ref[h, r0:tq, :]
        m_new = jnp.maximum(m_prev, jnp.max(functools.reduce(jnp.maximum, chunks), -1, keepdims=True))
        p = jnp.concatenate([jnp.exp(c - m_new) for c in chunks], 1).astype(BF16)
        acc_ref[h, r0:tq, :] = (jnp.exp(m_prev - m_new) * acc_ref[h, r0:tq, :]
                                + _dot(p, v_ref[0, pl.ds(k0, tk), heads[h]]))
        m_ref[h, r0:tq, :] = m_new

    assert nsub % 2 == 0
    for h in range(2):
        s_ref[0, h] = scores(0, 0, heads[h])

    def body(i, carry):
        j = 2 * i
        for b in range(2):
            for h in range(2):
                s_ref[1 - b, h] = scores(j + b + 1, 0, heads[h])
            for h in range(2):
                consume(s_ref[b, h], h, j + b, 0, False)
        return carry

    lax.fori_loop(0, qi * (nsub // 2), body, 0)
    for h in range(2):
        consume(s_ref[0, h], h, nfull, 0, True)
    for d in range(1, nsub):
        for h in range(2):
            consume(scores(nfull + d, d * tk, heads[h]), h, nfull + d, d * tk, True)

    lane = _iota((1, LANES), 1)
    o0 = acc_ref[0]
    o1 = acc_ref[1]
    o0 = o0 / o0[:, HEAD_DIM:HEAD_DIM + 1]
    o1 = o1 / o1[:, HEAD_DIM:HEAD_DIM + 1]
    o_ref[0] = jnp.where(lane < HEAD_DIM, o0, pltpu.roll(o1, HEAD_DIM, 1)).astype(o_ref.dtype)


def _flash(q_all, k_all, v_all, tq=FLASH_TQ, tk=FLASH_TK):
    b, s, _ = q_all.shape
    npair = (MLA_HEADS + FOX_HEADS) // 2
    return pl.pallas_call(
        functools.partial(_flash_kernel, tq=tq, tk=tk),
        grid=(b, npair, s // tq),
        in_specs=[pl.BlockSpec((1, tq, 2 * LANES), lambda bi, p, qi: (bi, qi, p)),
                  pl.BlockSpec((1, s, 2 * LANES), lambda bi, p, qi: (bi, 0, p)),
                  pl.BlockSpec((1, s, 2 * LANES), lambda bi, p, qi: (bi, 0, p))],
        out_specs=pl.BlockSpec((1, tq, LANES), lambda bi, p, qi: (bi, qi, p)),
        out_shape=jax.ShapeDtypeStruct((b, s, npair * LANES), BF16),
        scratch_shapes=[pltpu.VMEM((2, tq, LANES), F32), pltpu.VMEM((2, tq, LANES), F32),
                        pltpu.VMEM((2, 2, tq, tk), F32)],
        compiler_params=_cparams(("parallel", "parallel", "arbitrary")),
        name="flash_attn",
    )(q_all, k_all, v_all)


def _cmp_kernel(nc_ref, posa_ref, posb_ref, wa_ref, wb_ref, w2_ref, out_ref, *, n_chunk):
    a = jnp.zeros((n_chunk, 2 * NSA_CMP_HIDDEN), F32)
    b = jnp.zeros((n_chunk, 2 * NSA_CMP_HIDDEN), F32)
    for l in range(NSA_CMP_STRIDE):
        t = nc_ref[0, pl.ds(l, n_chunk, stride=NSA_CMP_STRIDE), :]
        sl = slice(LANES * l, LANES * (l + 1))
        a = a + _dot((t + posa_ref[:, sl]).astype(BF16), wa_ref[sl, :])
        b = b + _dot((t + posb_ref[:, sl]).astype(BF16), wb_ref[sl, :])
    pre = a + pltpu.roll(b, n_chunk - 1, 0)
    hid = jax.nn.silu(pre)
    out_ref[0] = _dot(hid.astype(BF16), w2_ref[...]).astype(out_ref.dtype)


def _compress(nc3, wl):
    b, s, _ = nc3.shape
    n_chunk = s // NSA_CMP_STRIDE
    full = lambda a: pl.BlockSpec(a.shape, lambda bi: (0,) * a.ndim)
    consts = [wl["cmp_posa"], wl["cmp_posb"], wl["cmp_wa"], wl["cmp_wb"], wl["cmp_w2"]]
    return pl.pallas_call(
        functools.partial(_cmp_kernel, n_chunk=n_chunk),
        grid=(b,),
        in_specs=[pl.BlockSpec((1, s, LANES), lambda bi: (bi, 0, 0))] + [full(a) for a in consts],
        out_specs=pl.BlockSpec((1, n_chunk, LANES), lambda bi: (bi, 0, 0)),
        out_shape=jax.ShapeDtypeStruct((b, n_chunk, LANES), BF16),
        compiler_params=_cparams(("parallel",)),
        name="nsa_compress",
    )(nc3, *consts)


def _masked_softmax(s, mask):
    s = jnp.where(mask, s, NEG)
    m = jnp.max(s, -1, keepdims=True)
    p = jnp.where(mask, jnp.exp(s - m), 0.0)
    return p / jnp.maximum(jnp.sum(p, -1, keepdims=True), 1e-30)


def _nsa_kernel(nq_ref, kc_ref, ksl_ref, kwin_ref, g_ref, o_ref, m_ref, acc_ref, qst_ref, s_ref, *, tq, n_cmp):
    qi = pl.program_id(1)
    t0 = pl.multiple_of(qi * tq, tq)
    rpos = t0 + _iota((tq, 1), 0)
    lane = _iota((1, LANES), 1)
    qs = [nq_ref[0, :, LANES * h:LANES * (h + 1)] for h in range(NSA_HEADS)]

    dist_i = rpos - (NSA_CMP_STRIDE * lane + NSA_CMP_BLOCK - 1)
    valid_c = (dist_i >= 0) & (lane < n_cmp)
    dist_c = dist_i.astype(F32)
    kc = kc_ref[0]
    psum = jnp.zeros((tq, LANES), F32)
    o_cmp = []
    for h in range(NSA_HEADS):
        q_head = jnp.where(lane < HEAD_DIM, qs[h], jnp.zeros_like(qs[h]))
        p = _masked_softmax(_dot_nt(q_head, kc) - ALIBI[h] * dist_c, valid_c)
        psum = psum + p
        o_cmp.append(_dot(p.astype(BF16), kc))

    n_i = _iota((LANES, LANES), 0)
    j_i = _iota((LANES, LANES), 1)
    ov = ((NSA_CMP_STRIDE * n_i < NSA_SEL_BLOCK * (j_i + 1))
          & (NSA_CMP_STRIDE * n_i + NSA_CMP_BLOCK > NSA_SEL_BLOCK * j_i)
          & (n_i < n_cmp))
    imp = _dot_exact_rhs(psum, jnp.where(ov, 1.0, 0.0).astype(BF16))
    cur = jnp.right_shift(rpos, 6)
    forced = (lane == 0) | (lane == cur) | (lane == cur - 1)
    future = lane * NSA_SEL_BLOCK > rpos
    n_blk = ksl_ref.shape[1] // NSA_SEL_BLOCK
    work = jnp.where(forced, NSA_FORCE_SCORE, jnp.where(future, -1.0, imp))
    work = jnp.where(lane < n_blk, work, -jnp.inf)
    work_t = work.T[0:n_blk, :]
    blk_id = _iota((n_blk, 1), 0)
    rank = jnp.zeros((n_blk, tq), F32)
    for jp in range(n_blk):
        other = work_t[jp:jp + 1, :]
        beats = (other > work_t) | ((other == work_t) & (jp < blk_id))
        rank = rank + jnp.where(beats, 1.0, 0.0)
    sel_t = jnp.where(rank < NSA_SEL_TOPN, 1.0, 0.0)
    sel = jnp.concatenate([sel_t, jnp.zeros((LANES - n_blk, tq), F32)], 0).T > 0.5
    sel_lanes = pltpu.roll(jnp.where(sel, 1.0, 0.0), NSA_SEL_LANE, 1)
    in_sel = (lane >= NSA_SEL_LANE) & (lane < NSA_SEL_LANE + n_blk)
    sel_bias = jnp.where(in_sel, (sel_lanes - 1.0) * NSA_MASK_BIG, 0.0)
    for h in range(NSA_HEADS):
        qst_ref[h * tq:(h + 1) * tq, :] = (qs[h].astype(F32) + sel_bias).astype(BF16)
    m_ref[...] = jnp.full_like(m_ref, NEG)
    acc_ref[...] = jnp.zeros_like(acc_ref)

    def online(s, v):
        chunks = [s[:, LANES * i:LANES * (i + 1)] for i in range(s.shape[1] // LANES)]
        m_prev = m_ref[...]
        m_new = jnp.maximum(m_prev, jnp.max(functools.reduce(jnp.maximum, chunks), -1, keepdims=True))
        p = jnp.concatenate([jnp.exp(ch - m_new) for ch in chunks], 1).astype(BF16)
        acc_ref[...] = jnp.exp(m_prev - m_new) * acc_ref[...] + _dot(p, v)
        m_ref[...] = m_new

    def scores(c):
        return _dot_nt(qst_ref[...], ksl_ref[0, pl.ds(pl.multiple_of(c * tq, tq), tq), 0:LANES])

    def values(c):
        return ksl_ref[0, pl.ds(pl.multiple_of(c * tq, tq), tq), LANES:2 * LANES]

    s_ref[0] = scores(0)

    def chunk_pair(i, carry):
        j = 2 * i
        s_ref[1] = scores(j + 1)
        online(s_ref[0], values(j))
        s_ref[0] = scores(j + 2)
        online(s_ref[1], values(j + 1))
        return carry

    lax.fori_loop(0, jnp.right_shift(qi, 1), chunk_pair, 0)

    @pl.when((qi & 1) == 1)
    def _():
        online(s_ref[0], values(qi - 1))
        s_ref[0] = scores(qi)

    nrow = NSA_HEADS * tq
    causal = (_iota((nrow, tq), 0) & (tq - 1)) >= _iota((nrow, tq), 1)
    online(jnp.where(causal, s_ref[0], NEG), values(qi))

    wlen = tq + NSA_WINDOW
    w0 = pl.multiple_of(jnp.maximum(t0 - NSA_WINDOW, 0), min(tq, NSA_WINDOW))
    k_w = kwin_ref[0, pl.ds(w0, wlen), 0:LANES]
    v_w = kwin_ref[0, pl.ds(w0, wlen), LANES:2 * LANES]
    d_w = (t0 - w0) + _iota((tq, wlen), 0) - _iota((tq, wlen), 1)
    keep_w = (d_w >= 0) & (d_w < NSA_WINDOW)

    gates = jax.nn.sigmoid(g_ref[...])
    for h in range(NSA_HEADS):
        s_w = jnp.where(keep_w, _dot_nt(qs[h], k_w), NEG)
        p_w = jnp.exp(s_w - jnp.max(s_w, -1, keepdims=True))
        o_win = _dot(p_w.astype(BF16), v_w)
        o_win = o_win / o_win[:, HEAD_DIM:HEAD_DIM + 1]
        o_sel = acc_ref[h * tq:(h + 1) * tq, :]
        o_sel = o_sel / o_sel[:, HEAD_DIM:HEAD_DIM + 1]
        g0 = NSAG_LANE + 3 * h
        o = (gates[:, g0:g0 + 1] * pltpu.roll(o_cmp[h], HEAD_DIM, 1) + gates[:, g0 + 1:g0 + 2] * o_sel
             + gates[:, g0 + 2:g0 + 3] * o_win)
        o_ref[0, :, LANES * h:LANES * (h + 1)] = jnp.where(lane < HEAD_DIM, o, 0.0).astype(o_ref.dtype)


def _nsa(nq, kcvc, nsl, nwin, small, tq=NSA_TQ):
    b, s, _ = nq.shape
    nq_t = s // tq
    n_cmp = (s - NSA_CMP_BLOCK) // NSA_CMP_STRIDE + 1
    return pl.pallas_call(
        functools.partial(_nsa_kernel, tq=tq, n_cmp=n_cmp),
        grid=(b, nq_t),
        in_specs=[pl.BlockSpec((1, tq, NSA_HEADS * LANES), lambda bi, qi: (bi, qi, 0)),
                  pl.BlockSpec((1, kcvc.shape[1], LANES), lambda bi, qi: (bi, 0, 0)),
                  pl.BlockSpec((1, s, 2 * LANES), lambda bi, qi: (bi, 0, 0)),
                  pl.BlockSpec((1, s, 2 * LANES), lambda bi, qi: (bi, 0, 0)),
                  pl.BlockSpec((tq, LANES), lambda bi, qi: (bi * nq_t + qi, 0))],
        out_specs=pl.BlockSpec((1, tq, NSA_HEADS * LANES), lambda bi, qi: (bi, qi, 0)),
        out_shape=jax.ShapeDtypeStruct((b, s, NSA_HEADS * LANES), BF16),
        scratch_shapes=[pltpu.VMEM((NSA_HEADS * tq, LANES), F32), pltpu.VMEM((NSA_HEADS * tq, LANES), F32),
                        pltpu.VMEM((NSA_HEADS * tq, LANES), BF16), pltpu.VMEM((2, NSA_HEADS * tq, tq), F32)],
        compiler_params=_cparams(("parallel", "arbitrary")),
        name="nsa_attn",
    )(nq, kcvc, nsl, nwin, small)


def _layer_norm(y, g, b):
    mu = jnp.mean(y, -1, keepdims=True)
    yc = y - mu
    var = jnp.mean(yc * yc, -1, keepdims=True)
    return yc * lax.rsqrt(var + NORM_EPS) * g + b


ROUTE_ROWS = ROW_TILE


def _out_proj_kernel(*refs, routed):
    if routed:
        (oa_ref, on_ref, x_ref, wa_ref, wb_ref, g_ref, b_ref, rw_ref,
         o_ref, gate_ref, pos_ref, post_ref, cnt_ref) = refs
    else:
        oa_ref, on_ref, x_ref, wa_ref, wb_ref, g_ref, b_ref, o_ref = refs
    mix = _dot(oa_ref[...], wa_ref[...]) + _dot(on_ref[...], wb_ref[...])
    x1 = _layer_norm(DEEPNORM_ALPHA * x_ref[...] + mix, g_ref[...], b_ref[...])
    o_ref[...] = x1
    if routed:
        tm = x1.shape[0]
        lane = _iota((1, LANES), 1)
        lane_f = lane.astype(F32)
        logits = jnp.where(lane < N_EXPERTS, _dot_split(x1, rw_ref[...]), NEG)
        ex = jnp.exp(logits - jnp.max(logits, -1, keepdims=True))
        probs = ex / jnp.sum(ex, -1, keepdims=True)
        p1 = jnp.max(probs, -1, keepdims=True)
        i1 = jnp.min(jnp.where(probs == p1, lane_f, float(LANES)), -1, keepdims=True)
        rest = jnp.where(lane_f == i1, -1.0, probs)
        p2 = jnp.max(rest, -1, keepdims=True)
        i2 = jnp.min(jnp.where(rest == p2, lane_f, float(LANES)), -1, keepdims=True)
        tot = p1 + p2
        gate_ref[...] = jnp.where(lane_f == i1, p1 / tot, jnp.where(lane_f == i2, p2 / tot, 0.0))
        chosen = (lane_f == i1) | (lane_f == i2)
        before = (_iota((tm, tm), 0) > _iota((tm, tm), 1)).astype(BF16)
        onehot = jnp.where(chosen, 1.0, 0.0)
        slot = _dot(before, onehot.astype(BF16))
        posm = jnp.where(chosen, slot, -1.0)
        pos_ref[...] = posm
        post_ref[...] = posm.T[0:N_EXPERTS, :]
        cnt_ref[...] = jnp.broadcast_to(jnp.sum(onehot, 0, keepdims=True), (8, LANES)).astype(jnp.int32)


def _out_proj(o_attn, o_nsa, x2, wl, router=None, tm=ROUTE_ROWS):
    n = x2.shape[0]
    routed = router is not None
    full = lambda a: pl.BlockSpec(a.shape, lambda i: (0,) * a.ndim)
    row = lambda w: pl.BlockSpec((tm, w), lambda i: (i, 0))
    consts = [wl["w_out_a"], wl["w_out_b"], wl["ln1_g"], wl["ln1_b"]] + ([router] if routed else [])
    out_specs = [row(D_MODEL)]
    out_shape = [jax.ShapeDtypeStruct((n, D_MODEL), F32)]
    if routed:
        out_specs += [row(LANES), row(LANES), pl.BlockSpec((N_EXPERTS, tm), lambda i: (i, 0)),
                      pl.BlockSpec((8, LANES), lambda i: (i, 0))]
        out_shape += [jax.ShapeDtypeStruct((n, LANES), F32), jax.ShapeDtypeStruct((n, LANES), F32),
                      jax.ShapeDtypeStruct((n // tm * N_EXPERTS, tm), F32),
                      jax.ShapeDtypeStruct((n // tm * 8, LANES), jnp.int32)]
    outs = pl.pallas_call(
        functools.partial(_out_proj_kernel, routed=routed),
        grid=(n // tm,),
        in_specs=[row(o_attn.shape[1]), row(o_nsa.shape[1]), row(D_MODEL)] + [full(a) for a in consts],
        out_specs=out_specs,
        out_shape=out_shape,
        compiler_params=_cparams(("parallel",)),
        name="out_proj_route_ln" if routed else "out_proj_ln",
    )(o_attn, o_nsa, x2, *consts)
    return outs if routed else outs[0]


def _ffn_kernel(x_ref, w1_ref, w3_ref, w2_ref, g_ref, b_ref, o_ref, acc_ref, xb_ref):
    c = pl.program_id(1)

    @pl.when(c == 0)
    def _():
        acc_ref[...] = jnp.zeros_like(acc_ref)
        xb_ref[...] = x_ref[...].astype(BF16)

    xb = xb_ref[...]
    a = jax.nn.silu(_dot(xb, w1_ref[...])) * _dot(xb, w3_ref[...])
    acc_ref[...] += _dot(a.astype(BF16), w2_ref[...])

    @pl.when(c == pl.num_programs(1) - 1)
    def _():
        o_ref[...] = _layer_norm(DEEPNORM_ALPHA * x_ref[...] + acc_ref[...], g_ref[...], b_ref[...])


def _ffn(x2, w1, w3, w2, ln_g, ln_b, tm=2 * ROW_TILE, nchunk=2):
    n = x2.shape[0]
    tf = w1.shape[1] // nchunk
    full = lambda a: pl.BlockSpec(a.shape, lambda i, c: (0,) * a.ndim)
    return pl.pallas_call(
        _ffn_kernel,
        grid=(n // tm, nchunk),
        in_specs=[pl.BlockSpec((tm, D_MODEL), lambda i, c: (i, 0)),
                  pl.BlockSpec((D_MODEL, tf), lambda i, c: (0, c)),
                  pl.BlockSpec((D_MODEL, tf), lambda i, c: (0, c)),
                  pl.BlockSpec((tf, D_MODEL), lambda i, c: (c, 0)),
                  full(ln_g), full(ln_b)],
        out_specs=pl.BlockSpec((tm, D_MODEL), lambda i, c: (i, 0)),
        out_shape=jax.ShapeDtypeStruct((n, D_MODEL), F32),
        scratch_shapes=[pltpu.VMEM((tm, D_MODEL), F32), pltpu.VMEM((tm, D_MODEL), BF16)],
        compiler_params=_cparams(("parallel", "arbitrary")),
        name="ffn_ln",
    )(x2, w1, w3, w2, ln_g, ln_b)


MOE_CAP = 160
MOE_CAP_PAD = -(-MOE_CAP // LANES) * LANES
MOE_CHUNKS = -(-ROUTE_ROWS // MOE_CAP)


def _moe_kernel(cnt_ref, x_ref, gate_ref, pos_ref, *rest, groups):
    post_refs = rest[:groups]
    w1_ref, w3_ref, w2_ref, g_ref, b_ref, o_ref, acc_ref, xb_ref = rest[groups:]
    i = pl.program_id(0)
    e = pl.program_id(1)
    lane = _iota((1, LANES), 1)

    @pl.when(e == 0)
    def _():
        acc_ref[...] = jnp.zeros_like(acc_ref)
        xb_ref[...] = x_ref[...].astype(BF16)

    for gi in range(groups):
        rows = slice(ROUTE_ROWS * gi, ROUTE_ROWS * (gi + 1))
        cnt = cnt_ref[(i * groups + gi) * N_EXPERTS + e]
        gate_e = jnp.sum(jnp.where(lane == e, gate_ref[rows, :], 0.0), -1, keepdims=True)
        slot_col = jnp.sum(jnp.where(lane == e, pos_ref[rows, :], 0.0), -1, keepdims=True)
        slot_row = post_refs[gi][0]
        for k in range(MOE_CHUNKS):
            @pl.when(cnt > k * MOE_CAP)
            def _():
                want = (_iota((MOE_CAP, 1), 0) + k * MOE_CAP).astype(F32)
                pick = jnp.where(slot_row == want, 1.0, 0.0).astype(BF16)
                xg = _dot(pick, xb_ref[rows, :]).astype(BF16)
                a = jax.nn.silu(_dot(xg, w1_ref[0])) * _dot(xg, w3_ref[0])
                y = _dot(a.astype(BF16), w2_ref[0])
                col = _iota((1, MOE_CAP_PAD), 1)
                put = jnp.where((slot_col == (col + k * MOE_CAP).astype(F32)) & (col < MOE_CAP),
                                1.0, 0.0).astype(BF16)
                yb = y.astype(BF16)
                if MOE_CAP_PAD > MOE_CAP:
                    yb = jnp.concatenate([yb, jnp.zeros((MOE_CAP_PAD - MOE_CAP, D_MODEL), BF16)], 0)
                acc_ref[rows, :] += gate_e * _dot(put, yb)

    @pl.when(e == pl.num_programs(1) - 1)
    def _():
        o_ref[...] = _layer_norm(DEEPNORM_ALPHA * x_ref[...] + acc_ref[...], g_ref[...], b_ref[...])


def _moe(x2, gates, pos, post, cnt, w1, w3, w2, ln_g, ln_b, groups=2):
    n = x2.shape[0]
    ne, _, tf = w1.shape
    tm = groups * ROUTE_ROWS
    post3 = post.reshape(n // ROUTE_ROWS * N_EXPERTS, 1, ROUTE_ROWS)
    full = lambda a: pl.BlockSpec(a.shape, lambda i, e, c: (0,) * a.ndim)
    row = lambda w: pl.BlockSpec((tm, w), lambda i, e, c: (i, 0))
    post_spec = lambda gi: pl.BlockSpec((1, 1, ROUTE_ROWS),
                                        lambda i, e, c: ((i * groups + gi) * N_EXPERTS + e, 0, 0))
    grid_spec = pltpu.PrefetchScalarGridSpec(
        num_scalar_prefetch=1,
        grid=(n // tm, ne),
        in_specs=[row(D_MODEL), row(LANES), row(LANES)] + [post_spec(gi) for gi in range(groups)]
                 + [pl.BlockSpec((1, D_MODEL, tf), lambda i, e, c: (e, 0, 0)),
                    pl.BlockSpec((1, D_MODEL, tf), lambda i, e, c: (e, 0, 0)),
                    pl.BlockSpec((1, tf, D_MODEL), lambda i, e, c: (e, 0, 0)),
                    full(ln_g), full(ln_b)],
        out_specs=row(D_MODEL),
        scratch_shapes=[pltpu.VMEM((tm, D_MODEL), F32), pltpu.VMEM((tm, D_MODEL), BF16)],
    )
    return pl.pallas_call(
        functools.partial(_moe_kernel, groups=groups),
        grid_spec=grid_spec,
        out_shape=jax.ShapeDtypeStruct((n, D_MODEL), F32),
        compiler_params=_cparams(("parallel", "arbitrary")),
        name="moe_top2_ln",
    )(cnt, x2, gates, pos, *([post3] * groups), w1, w3, w2, ln_g, ln_b)


def _in_proj_columns():
    src = np.full((IN_P,), -1, np.int64)
    scale = np.ones((IN_P,), np.float32)
    o_cq, o_ckv, o_kr = 0, MLA_Q_RANK, MLA_Q_RANK + MLA_KV_RANK
    o_fox = o_kr + MLA_ROPE
    o_foxf = o_fox + 3 * FOX_HEADS * HEAD_DIM
    o_nq = o_foxf + FOX_HEADS
    o_nkv = o_nq + NSA_HEADS * HEAD_DIM
    o_ng = o_nkv + 6 * HEAD_DIM
    src[0:o_kr] = np.arange(o_kr)
    small = o_kr
    src[small + KR_LANE:small + KR_LANE + MLA_ROPE] = o_kr + np.arange(MLA_ROPE)
    src[small + FOXF_LANE:small + FOXF_LANE + FOX_HEADS] = o_foxf + np.arange(FOX_HEADS)
    src[small + NSAG_LANE:small + NSAG_LANE + 3 * NSA_HEADS] = o_ng + np.arange(3 * NSA_HEADS)
    n_fox = FOX_HEADS * HEAD_DIM
    src[FQ_OFF:FQ_OFF + 3 * n_fox] = o_fox + np.arange(3 * n_fox)
    scale[FQ_OFF:FQ_OFF + n_fox] = HEAD_DIM ** -0.5
    src[NQ_OFF:NQ_OFF + NSA_HEADS * HEAD_DIM] = o_nq + np.arange(NSA_HEADS * HEAD_DIM)
    scale[NQ_OFF:NQ_OFF + NSA_HEADS * HEAD_DIM] = HEAD_DIM ** -0.5
    src[NC_OFF:NC_OFF + 6 * HEAD_DIM] = o_nkv + np.arange(6 * HEAD_DIM)
    return src, scale


def _gather_cols(w, src, scale=None):
    scale = np.ones(src.shape, np.float32) if scale is None else scale
    parts, i, n = [], 0, len(src)
    while i < n:
        j = i + 1
        while j < n and scale[j] == scale[i] and (src[j] == src[j - 1] + 1 if src[i] >= 0 else src[j] < 0):
            j += 1
        if src[i] < 0:
            parts.append(jnp.zeros(w.shape[:-1] + (j - i,), w.dtype))
        else:
            seg = w[..., int(src[i]):int(src[i]) + (j - i)]
            parts.append(seg if scale[i] == 1.0 else seg * float(scale[i]))
        i = j
    return jnp.concatenate(parts, axis=-1)


def _tables(s):
    half = MLA_ROPE // 2
    freqs = ROPE_THETA ** (-jnp.arange(half, dtype=F32) / half)
    ang = jnp.arange(s).astype(F32)[:, None] * freqs[None, :]
    cos, sin = jnp.cos(ang), jnp.sin(ang)
    z = lambda w: jnp.zeros((s, w), F32)
    tabs = {
        "cos": jnp.concatenate([jnp.ones((s, MLA_NOPE), F32), cos, cos, z(LANES - MLA_NOPE - MLA_ROPE)], 1),
        "sina": jnp.concatenate([z(MLA_NOPE), -sin, z(half), z(LANES - MLA_NOPE - MLA_ROPE)], 1),
        "sinb": jnp.concatenate([z(MLA_NOPE), z(half), sin, z(LANES - MLA_NOPE - MLA_ROPE)], 1),
    }
    pq = np.zeros((LANES, FOX_HEADS * LANES), np.float32)
    pk = np.zeros((LANES, FOX_HEADS * LANES), np.float32)
    oneq = np.zeros((1, FOX_HEADS * LANES), np.float32)
    onek = np.zeros((1, FOX_HEADS * LANES), np.float32)
    for h in range(FOX_HEADS):
        for t in range(3):
            pq[FOXF_LANE + FOX_HEADS * t + h, LANES * h + HEAD_DIM + t] = 1.0
            pk[FOXF_LANE + FOX_HEADS * t + h, LANES * h + HEAD_DIM + 3 + t] = -1.0
            oneq[0, LANES * h + HEAD_DIM + 3 + t] = 1.0
            onek[0, LANES * h + HEAD_DIM + t] = 1.0
    tabs.update(pq=jnp.asarray(pq, BF16), pk=jnp.asarray(pk, BF16), oneq=jnp.asarray(oneq), onek=jnp.asarray(onek))
    qaux = np.zeros((1, NSA_HEADS * LANES), np.float32)
    for h in range(NSA_HEADS):
        qaux[0, LANES * h + NSA_AUX_LANE] = ALIBI[h] * NSA_SEL_BLOCK
        qaux[0, LANES * h + NSA_AUX_LANE + 1] = ALIBI[h]
    kaux = np.zeros((s, LANES), np.float32)
    pos = np.arange(s)
    kaux[:, NSA_AUX_LANE] = pos // NSA_SEL_BLOCK
    kaux[:, NSA_AUX_LANE + 1] = pos % NSA_SEL_BLOCK
    kaux[pos, NSA_SEL_LANE + pos // NSA_SEL_BLOCK] = 1.0
    tabs.update(nsa_qaux=jnp.asarray(qaux), nsa_kaux=jnp.asarray(kaux))
    return tabs


def _layer_weights(p, l):
    qd = MLA_NOPE + MLA_ROPE
    src_q = np.full((MLA_HEADS * LANES,), -1, np.int64)
    src_k = np.full((MLA_HEADS * LANES,), -1, np.int64)
    src_v = np.full((MLA_HEADS * LANES,), -1, np.int64)
    for h in range(MLA_HEADS):
        src_q[LANES * h + np.arange(qd)] = qd * h + np.arange(qd)
        src_k[LANES * h + np.arange(MLA_NOPE)] = 2 * HEAD_DIM * h + np.arange(MLA_NOPE)
        src_v[LANES * h + np.arange(HEAD_DIM)] = 2 * HEAD_DIM * h + MLA_NOPE + np.arange(HEAD_DIM)
    w_out = p["w_out"][l]
    n_attn = (MLA_HEADS + FOX_HEADS) * HEAD_DIM
    wb = w_out[n_attn:].reshape(NSA_HEADS, HEAD_DIM, D_MODEL)
    wb = jnp.concatenate([wb, jnp.zeros_like(wb)], axis=1).reshape(NSA_HEADS * LANES, D_MODEL)
    bf = jnp.zeros((1, LANES), F32).at[0, FOXF_LANE:FOXF_LANE + FOX_HEADS].set(p["b_forget"][l])

    half = NSA_CMP_BLOCK // 2
    kpos, vpos = p["cmp_k_pos"][l], p["cmp_v_pos"][l]
    posa = jnp.concatenate([kpos[:half], vpos[:half]], -1).reshape(1, half * LANES)
    posb = jnp.concatenate([kpos[half:], vpos[half:]], -1).reshape(1, half * LANES)

    def w1_half(lo):
        k = p["cmp_k_w1"][l].reshape(NSA_CMP_BLOCK, HEAD_DIM, NSA_CMP_HIDDEN)[lo:lo + half]
        v = p["cmp_v_w1"][l].reshape(NSA_CMP_BLOCK, HEAD_DIM, NSA_CMP_HIDDEN)[lo:lo + half]
        zk = jnp.zeros_like(k)
        top = jnp.concatenate([k, zk], -1)
        bot = jnp.concatenate([zk, v], -1)
        return jnp.concatenate([top, bot], 1).reshape(half * LANES, 2 * NSA_CMP_HIDDEN).astype(BF16)

    zw2 = jnp.zeros((NSA_CMP_HIDDEN, HEAD_DIM), F32)
    w2 = jnp.concatenate([jnp.concatenate([p["cmp_k_w2"][l], zw2], 1),
                          jnp.concatenate([zw2, p["cmp_v_w2"][l]], 1)], 0).astype(BF16)
    return {
        "g_cq": p["g_cq"][l][None, :], "g_ckv": p["g_ckv"][l][None, :],
        "w_uq": _gather_cols(p["w_uq"][l], src_q).astype(BF16),
        "w_uk": _gather_cols(p["w_ukv"][l], src_k).astype(BF16),
        "w_uv": _gather_cols(p["w_ukv"][l], src_v).astype(BF16),
        "b_forget": bf,
        "cmp_posa": posa, "cmp_posb": posb, "cmp_wa": w1_half(0), "cmp_wb": w1_half(half), "cmp_w2": w2,
        "w_out_a": w_out[:n_attn].astype(BF16), "w_out_b": wb.astype(BF16),
        "ln1_g": p["ln1_g"][l][None, :], "ln1_b": p["ln1_b"][l][None, :],
        "ln2_g": p["ln2_g"][l][None, :], "ln2_b": p["ln2_b"][l][None, :],
    }


def kernel(x, w_in, b_forget, g_cq, w_uq, g_ckv, w_ukv, cmp_k_pos, cmp_k_w1, cmp_k_w2, cmp_v_pos, cmp_v_w1,
           cmp_v_w2, w_out, ln1_g, ln1_b, ln2_g, ln2_b, ffn_w1, ffn_w3, ffn_w2, router_w, moe_w1, moe_w3,
           moe_w2):
    b, s, d = x.shape
    assert d == D_MODEL and s % 512 == 0 and s // NSA_CMP_STRIDE == LANES, (b, s, d)
    p = dict(b_forget=b_forget, g_cq=g_cq, w_uq=w_uq, g_ckv=g_ckv, w_ukv=w_ukv, cmp_k_pos=cmp_k_pos,
             cmp_k_w1=cmp_k_w1, cmp_k_w2=cmp_k_w2, cmp_v_pos=cmp_v_pos, cmp_v_w1=cmp_v_w1, cmp_v_w2=cmp_v_w2,
             w_out=w_out, ln1_g=ln1_g, ln1_b=ln1_b, ln2_g=ln2_g, ln2_b=ln2_b)
    src, scale = _in_proj_columns()
    w_in_p = _gather_cols(w_in, src, scale).astype(BF16)
    tabs = _tables(s)
    n = b * s
    x2 = x.reshape(n, d)
    for l in range(DEPTH):
        wl = _layer_weights(p, l)
        small, nq, nc, nsl, nwin, q_all, k_all, v_all = _front(x2, w_in_p[l], wl, tabs, s)
        o_attn = _flash(*(t.reshape(b, s, -1) for t in (q_all, k_all, v_all)))
        kcvc = _compress(nc.reshape(b, s, LANES), wl)
        o_nsa = _nsa(nq.reshape(b, s, NSA_HEADS * LANES), kcvc, nsl.reshape(b, s, 2 * LANES),
                     nwin.reshape(b, s, 2 * LANES), small)
        j = l // 2
        if l % 2 == 0:
            x2 = _out_proj(o_attn.reshape(n, -1), o_nsa.reshape(n, -1), x2, wl)
            x2 = _ffn(x2, ffn_w1[j].astype(BF16), ffn_w3[j].astype(BF16), ffn_w2[j].astype(BF16),
                      wl["ln2_g"], wl["ln2_b"])
        else:
            rw = jnp.pad(router_w[j], ((0, 0), (0, LANES - N_EXPERTS)))
            rw_hi = rw.astype(BF16)
            rw_lo = (rw - rw_hi.astype(F32)).astype(BF16)
            rw = jnp.concatenate([jnp.concatenate([rw_hi, rw_lo], 1),
                                  jnp.concatenate([rw_hi, jnp.zeros_like(rw_lo)], 1)], 0)
            x2, gates, pos, post, cnt = _out_proj(o_attn.reshape(n, -1), o_nsa.reshape(n, -1), x2, wl, router=rw)
            cnt = cnt[::8, :N_EXPERTS].reshape(-1)
            x2 = _moe(x2, gates, pos, post, cnt, moe_w1[j].astype(BF16), moe_w3[j].astype(BF16),
                      moe_w2[j].astype(BF16), wl["ln2_g"], wl["ln2_b"])
    return x2.reshape(b, s, d)
```
